```python
import math, functools
import jax, jax.numpy as jnp
from jax import lax
import numpy as np

D_MODEL = 1024
BATCH = 16
SEQ = 256
DEPTH = 4
DEC_BATCH = 2
DEC_SEQ = 2048
PAST_LEN = 256

GRID_W = 64
D_MIX = D_MODEL
D_LRU = D_MIX // 4
LRU_HEADS = 4
LRU_HD = D_LRU // LRU_HEADS
LRU_CONV = 4
LRU_C = 8.0
D_HY = D_MIX // 4
HY_CONV = 3
HY_POS_BANDS = 16
HY_POS_DIM = 1 + 2 * HY_POS_BANDS
HY_FILT_HID = 64
HY_DECAY_TARGET = 1e-2
HY_FAST_DECAY = 0.3
HY_SLOW_DECAY = 1.5
HY_MAX_DECAY = math.log(HY_DECAY_TARGET) / HY_FAST_DECAY
HY_MIN_DECAY = math.log(HY_DECAY_TARGET) / HY_SLOW_DECAY
D_SSM = D_MIX // 2
SSM_HEADDIM = 64
SSM_HEADS = D_SSM // SSM_HEADDIM
SSM_GROUPS = 2
SSM_STATE = 64
SSM_CONV = 4
SSM_CHUNK = 128
D_XBC = D_SSM + 2 * SSM_GROUPS * SSM_STATE
D_IN = 2 * D_LRU + 3 * D_HY + D_SSM + D_XBC + 2 * SSM_HEADS
IN_SPLITS = (D_LRU, 2 * D_LRU, 2 * D_LRU + 3 * D_HY, 2 * D_LRU + 3 * D_HY + D_SSM,
             2 * D_LRU + 3 * D_HY + D_SSM + D_XBC)
D_FF = 2816
N_EXPERTS = 8
TOP_K = 2
N_DENSE = (DEPTH + 1) // 2
N_MOE = DEPTH // 2
EPS = 1e-6

kernel_name = "hybrid_lru_hyena_ssd_diffusion_step"


def rmsnorm(x, w):
    xf = x.astype(jnp.float32)
    y = xf * lax.rsqrt(jnp.mean(xf * xf, axis=-1, keepdims=True) + EPS)
    return (y * w.astype(jnp.float32)).astype(x.dtype)


def grid_pos_embed(n_tokens):
    rows = n_tokens // GRID_W
    r = jnp.repeat(jnp.arange(rows, dtype=jnp.float32), GRID_W)
    col = jnp.tile(jnp.arange(GRID_W, dtype=jnp.float32), rows)
    quarter = D_MODEL // 4
    omega = 1.0 / (10000.0 ** (jnp.arange(quarter, dtype=jnp.float32) / quarter))
    ang_r = r[:, None] * omega[None]
    ang_c = col[:, None] * omega[None]
    return jnp.concatenate([jnp.sin(ang_r), jnp.cos(ang_r), jnp.sin(ang_c), jnp.cos(ang_c)], axis=-1)


def dwconv_centred(x, w, b):
    k_w = w.shape[0]
    n = x.shape[1]
    left = k_w // 2
    xp = jnp.pad(x, ((0, 0), (left, k_w - 1 - left), (0, 0)))
    y = b
    for k in range(k_w):
        y = y + w[k] * xp[:, k:k + n]
    return y


def linear_scan(a, b, h0, reverse):
    if reverse:
        b = b.at[:, -1].add(a[:, -1] * h0)
    else:
        b = b.at[:, 0].add(a[:, 0] * h0)

    def combine(left, right):
        a1, b1 = left
        a2, b2 = right
        return a1 * a2, a2 * b1 + b2

    _, h = lax.associative_scan(combine, (a, b), reverse=reverse, axis=1)
    final = h[:, 0] if reverse else h[:, -1]
    return h, final


def rglru_direction(x, wa, ba, wi, bi, lam, h0, reverse):
    bsz, n, _ = x.shape
    xh = x.reshape(bsz, n, LRU_HEADS, LRU_HD)
    r = jax.nn.sigmoid(jnp.einsum('blhi,hij->blhj', xh, wa).reshape(bsz, n, D_LRU) + ba)
    i = jax.nn.sigmoid(jnp.einsum('blhi,hij->blhj', xh, wi).reshape(bsz, n, D_LRU) + bi)
    log_a = -LRU_C * r * jax.nn.softplus(-lam.astype(jnp.float32))
    a = jnp.exp(log_a)
    b = jnp.sqrt(-jnp.expm1(2.0 * log_a)) * (i * x)
    return linear_scan(a, b, h0, reverse)


def rglru_mixer(u_x, u_gate, h0, p):
    x = dwconv_centred(u_x, p['lru_conv_w'], p['lru_conv_b']).astype(jnp.float32)
    h0 = h0.astype(jnp.float32)
    y_f, s_f = rglru_direction(x, p['lru_wa'][0], p['lru_ba'][0], p['lru_wi'][0], p['lru_bi'][0],
                               p['lru_lambda'][0], h0[:, 0], False)
    y_b, s_b = rglru_direction(x, p['lru_wa'][1], p['lru_ba'][1], p['lru_wi'][1], p['lru_bi'][1],
                               p['lru_lambda'][1], h0[:, 1], True)
    out = (y_f + y_b) * jax.nn.gelu(u_gate.astype(jnp.float32))
    return out.astype(u_x.dtype), jnp.stack([s_f, s_b], axis=1)


def hyena_filter(n_tokens, p):
    pos = jnp.arange(n_tokens, dtype=jnp.float32)
    t = pos / (n_tokens - 1)
    bands = jnp.linspace(1e-4, HY_POS_BANDS - 1, HY_POS_BANDS, dtype=jnp.float32)
    ang = (2.0 * math.pi * pos / n_tokens)[:, None] * bands[None]
    z = jnp.concatenate([t[:, None], jnp.cos(ang), -jnp.sin(ang)], axis=-1)
    freq = p['hy_freq'].astype(jnp.float32)
    g = jnp.sin(freq * (z @ p['hy_w1'] + p['hy_b1']))
    g = jnp.sin(freq * (g @ p['hy_w2'] + p['hy_b2']))
    h = (g @ p['hy_w3']).astype(jnp.float32)
    half = n_tokens // 2
    dist = jnp.abs(pos - half) / half
    deltas = jnp.abs(jnp.linspace(HY_MIN_DECAY, HY_MAX_DECAY, D_HY, dtype=jnp.float32))
    return h * jnp.exp(-dist[:, None] * deltas[None])


def long_conv_centred(v, h):
    n = v.shape[1]
    n_fft = 2 * n
    vf = jnp.fft.rfft(v.astype(jnp.float32), n=n_fft, axis=1)
    hf = jnp.fft.rfft(h, n=n_fft, axis=0)
    y = jnp.fft.irfft(vf * hf[None], n=n_fft, axis=1)
    half = n // 2
    return y[:, half:half + n]


def hyena_mixer(u, p):
    n = u.shape[1]
    uc = dwconv_centred(u, p['hy_conv_w'], p['hy_conv_b'])
    v, x1, x2 = jnp.split(uc, 3, axis=-1)
    h = hyena_filter(n, p)
    z = (v * x1).astype(jnp.float32)
    z = long_conv_centred(z, h) + p['hy_bias'] * z
    return (x2 * z).astype(u.dtype)


def segsum(x):
    n = x.shape[-1]
    xr = jnp.broadcast_to(x[..., :, None], x.shape + (n,))
    xr = jnp.where(jnp.tril(jnp.ones((n, n), dtype=bool), -1), xr, 0.0)
    cs = jnp.cumsum(xr, axis=-2)
    return jnp.where(jnp.tril(jnp.ones((n, n), dtype=bool), 0), cs, -jnp.inf)


def ssd_chunked(x, dt, a, bm, cm, h0):
    bsz, n, nh, hd = x.shape
    nc = n // SSM_CHUNK
    xd = (x * dt[..., None]).reshape(bsz, nc, SSM_CHUNK, nh, hd)
    bc = bm.reshape(bsz, nc, SSM_CHUNK, nh, SSM_STATE)
    cc = cm.reshape(bsz, nc, SSM_CHUNK, nh, SSM_STATE)
    a_dt = (dt * a).reshape(bsz, nc, SSM_CHUNK, nh).transpose(0, 3, 1, 2)
    a_cs = jnp.cumsum(a_dt, axis=-1)
    decay_in = jnp.exp(segsum(a_dt))
    scores = jnp.einsum('bclhn,bcshn->bhcls', cc, bc) * decay_in
    y_diag = jnp.einsum('bhcls,bcshp->bclhp', scores, xd)
    decay_states = jnp.exp(a_cs[..., -1:] - a_cs)
    states = jnp.einsum('bclhn,bhcl,bclhp->bchpn', bc, decay_states, xd)
    states = jnp.concatenate([h0[:, None], states], axis=1)
    a_chunk = jnp.pad(a_cs[..., -1], ((0, 0), (0, 0), (1, 0)))
    decay_chunk = jnp.exp(segsum(a_chunk))
    states = jnp.einsum('bhzc,bchpn->bzhpn', decay_chunk, states)
    prev_states, final = states[:, :-1], states[:, -1]
    y_off = jnp.einsum('bclhn,bchpn,bhcl->bclhp', cc, prev_states, jnp.exp(a_cs))
    return (y_diag + y_off).reshape(bsz, n, nh, hd), final


def _flip(t):
    return t[:, ::-1]


def ssm_mixer(u_z, u_xbc, u_dt, h0, p):
    bsz, n, _ = u_xbc.shape
    xbc = jax.nn.silu(dwconv_centred(u_xbc, p['ssm_conv_w'], p['ssm_conv_b'])).astype(jnp.float32)
    x, bm, cm = jnp.split(xbc, [D_SSM, D_SSM + SSM_GROUPS * SSM_STATE], axis=-1)
    x = x.reshape(bsz, n, SSM_HEADS, SSM_HEADDIM)
    rep = SSM_HEADS // SSM_GROUPS
    bm = jnp.repeat(bm.reshape(bsz, n, SSM_GROUPS, SSM_STATE), rep, axis=2)
    cm = jnp.repeat(cm.reshape(bsz, n, SSM_GROUPS, SSM_STATE), rep, axis=2)
    dt_raw = u_dt.astype(jnp.float32)
    h0 = h0.astype(jnp.float32)
    dt_f = jax.nn.softplus(dt_raw[..., :SSM_HEADS] + p['ssm_dt_bias'][0])
    a_f = -jnp.exp(p['ssm_a_log'][0].astype(jnp.float32))
    y_f, s_f = ssd_chunked(x, dt_f, a_f, bm, cm, h0[:, 0])
    dt_b = jax.nn.softplus(_flip(dt_raw[..., SSM_HEADS:]) + p['ssm_dt_bias'][1])
    a_b = -jnp.exp(p['ssm_a_log'][1].astype(jnp.float32))
    y_b, s_b = ssd_chunked(_flip(x), dt_b, a_b, _flip(bm), _flip(cm), h0[:, 1])
    y = y_f + _flip(y_b) + p['ssm_d'][:, None] * x
    y = y.reshape(bsz, n, D_SSM) * jax.nn.silu(u_z.astype(jnp.float32))
    return rmsnorm(y, p['ssm_norm_w']).astype(u_z.dtype), jnp.stack([s_f, s_b], axis=1)


def swiglu(h, w_gate, w_up, w_down):
    return (jax.nn.silu(h @ w_gate) * (h @ w_up)) @ w_down


def moe_swiglu(h, router, w_gate, w_up, w_down):
    bsz, n, d = h.shape
    t = h.reshape(bsz * n, d)
    logits = (t @ router).astype(jnp.float32)
    top_logits, top_idx = lax.top_k(logits, TOP_K)
    top_w = jax.nn.softmax(top_logits, axis=-1)
    comb = jnp.sum(jax.nn.one_hot(top_idx, N_EXPERTS, dtype=jnp.float32) * top_w[..., None], axis=1)
    comb = comb.astype(h.dtype)
    out = jnp.zeros_like(t)
    for e in range(N_EXPERTS):
        out = out + comb[:, e:e + 1] * swiglu(t, w_gate[e], w_up[e], w_down[e])
    return out.reshape(bsz, n, d)


def trunk_layer(x, mod, h_lru0, h_ssm0, p, ffn):
    shift1, scale1, gate1, shift2, scale2, gate2 = jnp.split(mod[:, None, :].astype(x.dtype), 6, axis=-1)
    h = rmsnorm(x, p['norm1_w']) * (1 + scale1) + shift1
    u = h @ p['w_in']
    u_lru, u_gate, u_hy, u_z, u_xbc, u_dt = jnp.split(u, list(IN_SPLITS), axis=-1)
    o_lru, s_lru = rglru_mixer(u_lru, u_gate, h_lru0, p)
    o_hy = hyena_mixer(u_hy, p)
    o_ssm, s_ssm = ssm_mixer(u_z, u_xbc, u_dt, h_ssm0, p)
    o = jnp.concatenate([o_lru, o_hy, o_ssm], axis=-1) @ p['w_out']
    x = x + gate1 * o
    h2 = rmsnorm(x, p['norm2_w']) * (1 + scale2) + shift2
    x = x + gate2 * ffn(h2)
    return x, s_lru, s_ssm


def setup_inputs(seed: int = 0) -> dict:
    key = jax.random.key(seed)
    ks = iter(jax.random.split(key, 64))
    f32 = jnp.float32

    def nrm(shape, scale):
        return jax.random.normal(next(ks), shape, f32) * scale

    def unif(shape, lo, hi):
        return jax.random.uniform(next(ks), shape, f32, lo, hi)

    a0 = unif((DEPTH, 2, D_LRU), 0.9, 0.999)
    s0 = a0 ** (1.0 / LRU_C)
    lru_lambda = jnp.log(s0) - jnp.log1p(-s0)
    dt0 = jnp.exp(unif((DEPTH, 2, SSM_HEADS), math.log(1e-3), math.log(1e-1)))
    ssm_dt_bias = dt0 + jnp.log(-jnp.expm1(-dt0))
    return {
        'x_prompt': nrm((BATCH, SEQ, D_MODEL), 1.0),
        'x_sample': nrm((DEC_BATCH, DEC_SEQ, D_MODEL), 1.0),
        'state_lru': nrm((DEC_BATCH, DEPTH, 2, D_LRU), 0.5),
        'state_ssm': nrm((DEC_BATCH, DEPTH, 2, SSM_HEADS, SSM_HEADDIM, SSM_STATE), 0.1),
        'c': nrm((DEC_BATCH, D_MODEL), 1.0),
        'c_ctx': nrm((D_MODEL,), 1.0),
        'norm1_w': 1.0 + nrm((DEPTH, D_MODEL), 0.05),
        'norm2_w': 1.0 + nrm((DEPTH, D_MODEL), 0.05),
        'final_norm_w': 1.0 + nrm((D_MODEL,), 0.05),
        'ada_w': nrm((DEPTH, D_MODEL, 6 * D_MODEL), 0.5 * D_MODEL ** -0.5),
        'ada_b': nrm((DEPTH, 6 * D_MODEL), 0.02),
        'w_in': nrm((DEPTH, D_MODEL, D_IN), D_MODEL ** -0.5),
        'w_out': nrm((DEPTH, D_MIX, D_MODEL), D_MIX ** -0.5),
        'lru_conv_w': nrm((DEPTH, LRU_CONV, D_LRU), 0.5),
        'lru_conv_b': nrm((DEPTH, D_LRU), 0.02),
        'lru_wa': nrm((DEPTH, 2, LRU_HEADS, LRU_HD, LRU_HD), LRU_HD ** -0.5),
        'lru_ba': nrm((DEPTH, 2, D_LRU), 0.1),
        'lru_wi': nrm((DEPTH, 2, LRU_HEADS, LRU_HD, LRU_HD), LRU_HD ** -0.5),
        'lru_bi': nrm((DEPTH, 2, D_LRU), 0.1),
        'lru_lambda': lru_lambda,
        'hy_conv_w': nrm((DEPTH, HY_CONV, 3 * D_HY), HY_CONV ** -0.5),
        'hy_conv_b': nrm((DEPTH, 3 * D_HY), 0.02),
        'hy_w1': nrm((DEPTH, HY_POS_DIM, HY_FILT_HID), HY_POS_DIM ** -0.5),
        'hy_b1': nrm((DEPTH, HY_FILT_HID), 0.1),
        'hy_w2': nrm((DEPTH, HY_FILT_HID, HY_FILT_HID), HY_FILT_HID ** -0.5),
        'hy_b2': nrm((DEPTH, HY_FILT_HID), 0.1),
        'hy_freq': 1.0 + nrm((DEPTH, HY_FILT_HID), 0.05),
        'hy_w3': nrm((DEPTH, HY_FILT_HID, D_HY), 0.1 * HY_FILT_HID ** -0.5),
        'hy_bias': nrm((DEPTH, D_HY), 0.5),
        'ssm_conv_w': nrm((DEPTH, SSM_CONV, D_XBC), 0.5),
        'ssm_conv_b': nrm((DEPTH, D_XBC), 0.02),
        'ssm_dt_bias': ssm_dt_bias,
        'ssm_a_log': jnp.log(unif((DEPTH, 2, SSM_HEADS), 1.0, 16.0)),
        'ssm_d': 1.0 + nrm((DEPTH, SSM_HEADS), 0.1),
        'ssm_norm_w': 1.0 + nrm((DEPTH, D_SSM), 0.05),
        'ffn_w_gate': nrm((N_DENSE, D_MODEL, D_FF), D_MODEL ** -0.5),
        'ffn_w_up': nrm((N_DENSE, D_MODEL, D_FF), D_MODEL ** -0.5),
        'ffn_w_down': nrm((N_DENSE, D_FF, D_MODEL), D_FF ** -0.5),
        'moe_router': nrm((N_MOE, D_MODEL, N_EXPERTS), D_MODEL ** -0.5),
        'moe_w_gate': nrm((N_MOE, N_EXPERTS, D_MODEL, D_FF), D_MODEL ** -0.5),
        'moe_w_up': nrm((N_MOE, N_EXPERTS, D_MODEL, D_FF), D_MODEL ** -0.5),
        'moe_w_down': nrm((N_MOE, N_EXPERTS, D_FF, D_MODEL), D_FF ** -0.5),
    }


def reference(x_prompt, x_sample, state_lru, state_ssm, c, c_ctx, norm1_w, norm2_w, final_norm_w, ada_w, ada_b,
              w_in, w_out, lru_conv_w, lru_conv_b, lru_wa, lru_ba, lru_wi, lru_bi, lru_lambda, hy_conv_w, hy_conv_b,
              hy_w1, hy_b1, hy_w2, hy_b2, hy_freq, hy_w3, hy_bias, ssm_conv_w, ssm_conv_b, ssm_dt_bias, ssm_a_log,
              ssm_d, ssm_norm_w, ffn_w_gate, ffn_w_up, ffn_w_down, moe_router, moe_w_gate, moe_w_up, moe_w_down):
    n_ctx_batch = x_prompt.shape[0]
    xc = x_prompt
    xl = x_sample + grid_pos_embed(x_sample.shape[1]).astype(x_sample.dtype)
    zero_lru = jnp.zeros((n_ctx_batch, 2, D_LRU), jnp.float32)
    zero_ssm = jnp.zeros((n_ctx_batch, 2, SSM_HEADS, SSM_HEADDIM, SSM_STATE), jnp.float32)
    cond_ctx = jax.nn.silu(c_ctx.astype(jnp.float32))[None]
    cond_lat = jax.nn.silu(c.astype(jnp.float32))
    ctx_lru_states = []
    ctx_ssm_states = []
    for l in range(DEPTH):
        p = {
            'norm1_w': norm1_w[l], 'norm2_w': norm2_w[l], 'w_in': w_in[l], 'w_out': w_out[l],
            'lru_conv_w': lru_conv_w[l], 'lru_conv_b': lru_conv_b[l], 'lru_wa': lru_wa[l], 'lru_ba': lru_ba[l],
            'lru_wi': lru_wi[l], 'lru_bi': lru_bi[l], 'lru_lambda': lru_lambda[l],
            'hy_conv_w': hy_conv_w[l], 'hy_conv_b': hy_conv_b[l], 'hy_w1': hy_w1[l], 'hy_b1': hy_b1[l],
            'hy_w2': hy_w2[l], 'hy_b2': hy_b2[l], 'hy_freq': hy_freq[l], 'hy_w3': hy_w3[l], 'hy_bias': hy_bias[l],
            'ssm_conv_w': ssm_conv_w[l], 'ssm_conv_b': ssm_conv_b[l], 'ssm_dt_bias': ssm_dt_bias[l],
            'ssm_a_log': ssm_a_log[l], 'ssm_d': ssm_d[l], 'ssm_norm_w': ssm_norm_w[l],
        }
        if l % 2 == 0:
            j = l // 2
            ffn = functools.partial(swiglu, w_gate=ffn_w_gate[j], w_up=ffn_w_up[j], w_down=ffn_w_down[j])
        else:
            j = l // 2
            ffn = functools.partial(moe_swiglu, router=moe_router[j], w_gate=moe_w_gate[j], w_up=moe_w_up[j],
                                    w_down=moe_w_down[j])
        mod_ctx = cond_ctx @ ada_w[l] + ada_b[l]
        mod_lat = cond_lat @ ada_w[l] + ada_b[l]
        xc, s_lru, s_ssm = trunk_layer(xc, mod_ctx, zero_lru, zero_ssm, p, ffn)
        ctx_lru_states.append(s_lru)
        ctx_ssm_states.append(s_ssm)
        xl, _, _ = trunk_layer(xl, mod_lat, state_lru[:, l], state_ssm[:, l], p, ffn)
    y_prompt = rmsnorm(xc, final_norm_w)
    y_sample = rmsnorm(xl, final_norm_w)
    new_state_lru = jnp.stack(ctx_lru_states, axis=1)
    new_state_ssm = jnp.stack(ctx_ssm_states, axis=1)
    return (y_prompt, y_sample, new_state_lru, new_state_ssm)
```

```python
import functools
import math

import numpy as np
import jax
import jax.numpy as jnp
from jax import lax
from jax.experimental import pallas as pl
from jax.experimental.pallas import tpu as pltpu

F32 = jnp.float32
BF16 = jnp.bfloat16
I32 = jnp.int32
HI = lax.Precision.HIGHEST

D = 1024
N_CTX_SEQ, L_CTX = 16, 256
N_LAT_SEQ, L_LAT = 2, 2048
DEPTH = 4
GRID_W = 64
D_LRU = 256
LRU_HEADS, LRU_HD = 4, 64
LRU_C = 8.0
D_HY = 256
HY_BANDS = 16
HY_POS_DIM = 1 + 2 * HY_BANDS
HY_HID = 64
HY_MAX_DECAY = math.log(1e-2) / 0.3
HY_MIN_DECAY = math.log(1e-2) / 1.5
D_SSM = 512
SSM_P = 64
SSM_H = 8
SSM_G = 2
SSM_N = 64
SSM_Q = 128
D_XBC = D_SSM + 2 * SSM_G * SSM_N
D_MAIN = 2 * D_LRU + 3 * D_HY + D_SSM + D_XBC
D_IN = D_MAIN + 2 * SSM_H
D_FF = 2816
N_EXP = 8
EPS = 1e-6

LANE = 128
BF16_ROWS = 16
T_CTX = N_CTX_SEQ * L_CTX
T_LAT = N_LAT_SEQ * L_LAT
T = T_CTX + T_LAT
TM = 512
NT = T // TM
NT_CTX = T_CTX // TM
NT_PER_LAT = L_LAT // TM
RB = 2048
NB = T // RB
NB_CTX = T_CTX // RB
FF_CHUNK = 256
VMEM_LIMIT = 56 * 1024 * 1024

GCH = BF16_ROWS
LCAP = 2 * TM + N_EXP * GCH
NLC = LCAP // GCH
TMB = 512
RMAX = -(-(2 * T + NT * N_EXP * (GCH - 1) + N_EXP * (TMB - 1)) // TMB) * TMB
NBLK = RMAX // TMB

HY_KB_LAT = 256
N_FFT_LAT = 3 * L_LAT // 2
N_FFT_CTX = 3 * L_CTX // 2
HY_NKB = N_FFT_LAT // 2 // HY_KB_LAT


def _cparams(n_axes=1, vmem=VMEM_LIMIT):
    return pltpu.CompilerParams(dimension_semantics=("arbitrary",) * n_axes, vmem_limit_bytes=vmem)


def _mod_row(i):
    return jnp.where(i < NT_CTX, 0, 1 + (i - NT_CTX) // NT_PER_LAT)


def _bdot(a, b):
    return jnp.dot(a.astype(BF16), b.astype(BF16), preferred_element_type=F32)


def _rms(x):
    return x * lax.rsqrt(jnp.mean(x * x, axis=-1, keepdims=True) + EPS)


def _mod_kernel(cb_ref, w_ref, b_ref, o_ref):
    tn = w_ref.shape[2]

    def body(kc, accs):
        k0 = pl.multiple_of(kc * 8, 8)
        wk = w_ref[0, pl.ds(k0, 8), :]
        out = []
        for r in range(3):
            c = cb_ref[r, pl.ds(k0, 8), :]
            c = c * jax.nn.sigmoid(c)
            out.append(accs[r] + jnp.tile(c, (1, tn // LANE)) * wk)
        return tuple(out)

    accs = lax.fori_loop(0, D // 8, body, tuple(jnp.zeros((8, tn), F32) for _ in range(3)))
    rows = [jnp.sum(a, axis=0, keepdims=True) + b_ref[0] for a in accs]
    o_ref[0] = jnp.concatenate(rows + [jnp.zeros((5, tn), F32)], axis=0)


def _mod_table(cond_b, ada_w, ada_b):
    tn = 1024
    return pl.pallas_call(
        _mod_kernel,
        grid=(DEPTH, 6 * D // tn),
        in_specs=[
            pl.BlockSpec((3, D, LANE), lambda l, j: (0, 0, 0)),
            pl.BlockSpec((1, D, tn), lambda l, j: (l, 0, j)),
            pl.BlockSpec((1, 1, tn), lambda l, j: (l, 0, j)),
        ],
        out_specs=pl.BlockSpec((1, 8, tn), lambda l, j: (l, 0, j)),
        out_shape=jax.ShapeDtypeStruct((DEPTH, 8, 6 * D), F32),
        compiler_params=_cparams(2),
        name="mod_table",
    )(cond_b, ada_w, ada_b.reshape(DEPTH, 1, 6 * D))


def _k1_kernel(x_ref, mod_ref, nw_ref, w_ref, wdtT_ref,
               o_lru, o_hy, o_z, o_xbc, o_dt, o_dtT, wbf_ref):
    @pl.when(pl.program_id(0) == 0)
    def _():
        wbf_ref[...] = w_ref[0].astype(BF16)

    m = mod_ref[0]
    h = _rms(x_ref[...]) * nw_ref[0]
    h = h * (1.0 + m[1:2]) + m[0:1]
    hb = h.astype(BF16)

    def proj(lo, hi):
        return jnp.dot(hb, wbf_ref[:, lo:hi], preferred_element_type=F32)

    o_lru[...] = proj(0, 512)
    o_hy[...] = proj(512, 1280)
    o_z[...] = proj(1280, 1792)
    o_xbc[...] = proj(1792, 2560)
    o_dt[...] = proj(D_MAIN, D_IN)
    dtT = lax.dot_general(wdtT_ref[0].astype(BF16), hb, (((1,), (1,)), ((), ())),
                          preferred_element_type=F32)
    for j in range(TM // SSM_Q):
        o_dtT[j] = dtT[:, j * SSM_Q:(j + 1) * SSM_Q]


def _k1(l, x, mod_l, norm1_w, w_in, w_dtT):
    tok = lambda w: pl.BlockSpec((TM, w), lambda i: (i, 0))
    return pl.pallas_call(
        _k1_kernel,
        grid=(NT,),
        in_specs=[
            tok(D),
            pl.BlockSpec((1, 6, D), lambda i: (_mod_row(i), 0, 0)),
            pl.BlockSpec((1, 1, D), lambda i: (l, 0, 0)),
            pl.BlockSpec((1, D, D_IN), lambda i: (l, 0, 0)),
            pl.BlockSpec((1, 2 * SSM_H, D), lambda i: (l, 0, 0)),
        ],
        out_specs=[tok(512), tok(768), tok(512), tok(768), tok(2 * SSM_H),
                   pl.BlockSpec((TM // SSM_Q, 2 * SSM_H, SSM_Q), lambda i: (i, 0, 0))],
        out_shape=[jax.ShapeDtypeStruct((T, 512), F32), jax.ShapeDtypeStruct((T, 768), F32),
                   jax.ShapeDtypeStruct((T, 512), F32), jax.ShapeDtypeStruct((T, 768), F32),
                   jax.ShapeDtypeStruct((T, 2 * SSM_H), F32),
                   jax.ShapeDtypeStruct((T // SSM_Q, 2 * SSM_H, SSM_Q), F32)],
        scratch_shapes=[pltpu.VMEM((D, D_IN), BF16)],
        compiler_params=_cparams(1),
        name=f"k1_inproj_{l}",
    )(x, mod_l, norm1_w.reshape(DEPTH, 1, D), w_in, w_dtT)


def _row_in_seq(rows, lseq):
    return lax.broadcasted_iota(I32, (rows, 1), 0) & (lseq - 1)


def _shift_rows(x, s, rin, lseq):
    if s == 0:
        return x
    y = pltpu.roll(x, s % x.shape[0], axis=0)
    valid = (rin >= s) if s > 0 else (rin < lseq + s)
    return jnp.where(valid, y, 0.0)


def _dwconv(x, w_ref, b_ref, rin, lseq):
    k_w = w_ref.shape[0]
    y = b_ref[...]
    for k in range(k_w):
        y = y + w_ref[k:k + 1, :] * _shift_rows(x, k_w // 2 - k, rin, lseq)
    return y


def _lru_block(u_ref, cw_ref, cb_ref, wbig_ref, bias_ref, lam_ref, h0_ref, o_ref, st_ref, lseq):
    rows = u_ref.shape[0]
    rin = _row_in_seq(rows, lseq)
    u = u_ref[...]
    gate = u[:, D_LRU:]
    x = _dwconv(u[:, :D_LRU], cw_ref, cb_ref, rin, lseq)
    xb = x.astype(BF16)
    y = None
    finals = []
    for d in range(2):
        g = jnp.dot(xb, wbig_ref[:, 512 * d:512 * (d + 1)], preferred_element_type=F32)
        g = g + bias_ref[:, 512 * d:512 * (d + 1)]
        r = jax.nn.sigmoid(g[:, :D_LRU])
        ig = jax.nn.sigmoid(g[:, D_LRU:])
        log_a = -LRU_C * r * jax.nn.softplus(-lam_ref[d:d + 1, :])
        a = jnp.exp(log_a)
        th = jnp.tanh(log_a)
        b = jnp.sqrt(-2.0 * th / (1.0 - th)) * (ig * x)
        if h0_ref is not None:
            edge = (rin == 0) if d == 0 else (rin == lseq - 1)
            b = b + jnp.where(edge, a * h0_ref[d:d + 1, :], 0.0)
        s = 1
        while s < lseq:
            sh = (s if d == 0 else -s) % rows
            valid = (rin >= s) if d == 0 else (rin < lseq - s)
            b = b + jnp.where(valid, a * pltpu.roll(b, sh, axis=0), 0.0)
            if 2 * s < lseq:
                a = jnp.where(valid, a * pltpu.roll(a, sh, axis=0), a)
            s *= 2
        y = b if y is None else y + b
        if st_ref is not None:
            last = lseq - 1 if d == 0 else 0
            finals.append(jnp.concatenate(
                [b[j * lseq + last:j * lseq + last + 1, :] for j in range(rows // lseq)], axis=0))
    o_ref[...] = y * jax.nn.gelu(gate)
    if st_ref is not None:
        st_ref[...] = jnp.concatenate(finals, axis=1)


def _lru_kernel(u_ref, cw_ref, cb_ref, wbig_ref, bias_ref, lam_ref, h0_ref, o_ref, st_ref):
    b = pl.program_id(0)
    args = (u_ref, cw_ref.at[0], cb_ref.at[0], wbig_ref.at[0], bias_ref.at[0], lam_ref.at[0])

    @pl.when(b < NB_CTX)
    def _():
        _lru_block(*args, None, o_ref, st_ref, L_CTX)

    @pl.when(b >= NB_CTX)
    def _():
        _lru_block(*args, h0_ref.at[0, 0], o_ref, None, L_LAT)


def _lru_mixer(l, u_lru, p, state_lru):
    lsel = lambda *shape: pl.BlockSpec((1,) + shape, lambda b: (l,) + (0,) * len(shape))
    return pl.pallas_call(
        _lru_kernel,
        grid=(NB,),
        in_specs=[
            pl.BlockSpec((RB, 512), lambda b: (b, 0)),
            lsel(4, D_LRU), lsel(1, D_LRU), lsel(D_LRU, 1024), lsel(1, 1024), lsel(2, D_LRU),
            pl.BlockSpec((1, 1, 2, D_LRU), lambda b: (jnp.maximum(b - NB_CTX, 0), l, 0, 0)),
        ],
        out_specs=[pl.BlockSpec((RB, D_LRU), lambda b: (b, 0)),
                   pl.BlockSpec((RB // L_CTX, 2 * D_LRU), lambda b: (jnp.minimum(b, NB_CTX - 1), 0))],
        out_shape=[jax.ShapeDtypeStruct((T, D_LRU), F32),
                   jax.ShapeDtypeStruct((N_CTX_SEQ, 2 * D_LRU), F32)],
        compiler_params=_cparams(1),
        name=f"lru_mixer_{l}",
    )(u_lru, p["lru_conv_w"], p["lru_conv_b"], p["lru_wbig"], p["lru_bias"], p["lru_lambda"], state_lru)


def _hy_filter_kernel(z_ref, win_ref, f_ref, w1_ref, b1_ref, w2_ref, b2_ref, fr_ref, w3_ref,
                      o_ref, h_ref):
    @pl.when(pl.program_id(1) == 0)
    def _():
        fr = fr_ref[0]
        g = jnp.sin(fr * (jnp.dot(z_ref[...], w1_ref[0], precision=HI, preferred_element_type=F32)
                          + b1_ref[0]))
        g = jnp.sin(fr * (jnp.dot(g, w2_ref[0], precision=HI, preferred_element_type=F32) + b2_ref[0]))
        h = jnp.dot(g, w3_ref[0], precision=HI, preferred_element_type=F32)
        h_ref[...] = (h * win_ref[...]).astype(BF16)

    o_ref[0] = jnp.dot(f_ref[...], h_ref[...], preferred_element_type=F32)


def _hy_filter_spectrum(lseq, kb, zfeat, window, fmat, p):
    n = fmat.shape[0]
    lsel = lambda *shape: pl.BlockSpec((1,) + shape, lambda l, k: (l,) + (0,) * len(shape))
    return pl.pallas_call(
        _hy_filter_kernel,
        grid=(DEPTH, n // (2 * kb)),
        in_specs=[
            pl.BlockSpec((lseq, LANE), lambda l, k: (0, 0)),
            pl.BlockSpec((lseq, D_HY), lambda l, k: (0, 0)),
            pl.BlockSpec((2 * kb, lseq), lambda l, k: (k, 0)),
            lsel(LANE, HY_HID), lsel(1, HY_HID), lsel(HY_HID, HY_HID), lsel(1, HY_HID),
            lsel(1, HY_HID), lsel(HY_HID, D_HY),
        ],
        out_specs=pl.BlockSpec((1, 2 * kb, D_HY), lambda l, k: (l, k, 0)),
        out_shape=jax.ShapeDtypeStruct((DEPTH, n, D_HY), F32),
        scratch_shapes=[pltpu.VMEM((lseq, D_HY), BF16)],
        compiler_params=_cparams(2),
        name=f"hyena_filter_{lseq}",
    )(zfeat, window, fmat, p["hy_w1p"], p["hy_b1"], p["hy_w2"], p["hy_b2"], p["hy_freq"], p["hy_w3"])


def _hy_spectral_block(f_blk, g_blk, hf, z_bf, is_dc_block):
    kb = f_blk.shape[0] // 2
    zf = jnp.dot(f_blk, z_bf, preferred_element_type=F32)
    rz, iz = zf[:kb], zf[kb:]
    rh, ih = hf[:kb], hf[kb:]
    ii = iz * ih
    re = rz * rh - ii
    im = rz * ih + iz * rh
    if is_dc_block is not None:
        dc = jnp.logical_and(lax.broadcasted_iota(I32, (kb, 1), 0) == 0, is_dc_block)
        re = jnp.where(dc, rz * rh, re)
        im = jnp.where(dc, ii, im)
    pr = jnp.concatenate([re, im], axis=0).astype(BF16)
    return jnp.dot(g_blk, pr, preferred_element_type=F32)


def _hy_prologue(u_ref, cw_ref, cb_ref, lseq, z_ref, zbf_ref, x2_ref):
    rin = _row_in_seq(u_ref.shape[0], lseq)
    uc = _dwconv(u_ref[...], cw_ref, cb_ref, rin, lseq)
    z = uc[:, :D_HY] * uc[:, D_HY:2 * D_HY]
    z_ref[...] = z
    zbf_ref[...] = z.astype(BF16)
    x2_ref[...] = uc[:, 2 * D_HY:]


def _hy_kernel(u_ref, cw_ref, cb_ref, hb_ref, fc_ref, gc_ref, hfc_ref, fl_ref, gl_ref, hfl_ref,
               o_ref, z_ref, zbf_ref, x2_ref, acc_ref):
    b = pl.program_id(0)
    k = pl.program_id(1)
    cw, cb = cw_ref.at[0], cb_ref.at[0]

    @pl.when(jnp.logical_and(b < NB_CTX, k == 0))
    def _():
        _hy_prologue(u_ref, cw, cb, L_CTX, z_ref, zbf_ref, x2_ref)
        for s in range(RB // L_CTX):
            rows = slice(s * L_CTX, (s + 1) * L_CTX)
            acc_ref[rows, :] = _hy_spectral_block(fc_ref[...], gc_ref[...], hfc_ref[0], zbf_ref[rows, :], True)

    @pl.when(b >= NB_CTX)
    def _():
        @pl.when(k == 0)
        def _():
            _hy_prologue(u_ref, cw, cb, L_LAT, z_ref, zbf_ref, x2_ref)
            acc_ref[...] = jnp.zeros_like(acc_ref)

        acc_ref[...] += _hy_spectral_block(fl_ref[...], gl_ref[...], hfl_ref[0], zbf_ref[...], k == 0)

    @pl.when(k == HY_NKB - 1)
    def _():
        o_ref[...] = x2_ref[...] * (acc_ref[...] + hb_ref[0] * z_ref[...])


def _hy_mixer(l, u_hy, p, c):
    lat_k = lambda b, k: jnp.where(b < NB_CTX, 0, k)
    lsel = lambda *shape: pl.BlockSpec((1,) + shape, lambda b, k: (l,) + (0,) * len(shape))
    kbl = 2 * HY_KB_LAT
    return pl.pallas_call(
        _hy_kernel,
        grid=(NB, HY_NKB),
        in_specs=[
            pl.BlockSpec((RB, 3 * D_HY), lambda b, k: (b, 0)),
            lsel(3, 3 * D_HY), lsel(1, 3 * D_HY), lsel(1, D_HY),
            pl.BlockSpec((N_FFT_CTX, L_CTX), lambda b, k: (0, 0)),
            pl.BlockSpec((L_CTX, N_FFT_CTX), lambda b, k: (0, 0)),
            lsel(N_FFT_CTX, D_HY),
            pl.BlockSpec((kbl, L_LAT), lambda b, k: (lat_k(b, k), 0)),
            pl.BlockSpec((L_LAT, kbl), lambda b, k: (0, lat_k(b, k))),
            pl.BlockSpec((1, kbl, D_HY), lambda b, k: (l, lat_k(b, k), 0)),
        ],
        out_specs=pl.BlockSpec((RB, D_HY), lambda b, k: (b, 0)),
        out_shape=jax.ShapeDtypeStruct((T, D_HY), F32),
        scratch_shapes=[pltpu.VMEM((RB, D_HY), F32), pltpu.VMEM((RB, D_HY), BF16),
                        pltpu.VMEM((RB, D_HY), F32), pltpu.VMEM((RB, D_HY), F32)],
        compiler_params=_cparams(2),
        name=f"hyena_mixer_{l}",
    )(u_hy, p["hy_conv_w"], p["hy_conv_b"], p["hy_bias"],
      c["f_ctx"], c["g_ctx"], c["hf_ctx"], c["f_lat"], c["g_lat"], c["hf_lat"])


def _ssd_block(u_z, u_xbc, u_dt, u_dtT, cw, cb, dtb_row, dtb_col, alog_row, alog_col, d_exp, nw,
               e8_ref, eb_ref, mbd_ref, mdiag_ref, h0_ref, o_ref, st_ref,
               x_s, bc_s, y_s, s_s, lseq):
    rows = u_z.shape[0]
    nchunk = rows // SSM_Q
    cps = lseq // SSM_Q
    rin = _row_in_seq(rows, lseq)
    for c0 in range(0, D_XBC, LANE):
        cols = slice(c0, c0 + LANE)
        xbc = _dwconv(u_xbc[:, cols], cw.at[:, cols], cb.at[:, cols], rin, lseq)
        xbc = xbc * jax.nn.sigmoid(xbc)
        if c0 < D_SSM:
            x_s[:, cols] = xbc
        else:
            bc_s[:, c0 - D_SSM:c0 - D_SSM + LANE] = xbc

    li = lax.broadcasted_iota(I32, (SSM_Q, SSM_Q), 0)
    si = lax.broadcasted_iota(I32, (SSM_Q, SSM_Q), 1)
    e8 = e8_ref[...]
    hp = SSM_H * SSM_P

    for d in range(2):
        causal = (li >= si) if d == 0 else (li <= si)
        tri_col = causal.astype(F32)
        tri_row = ((li <= si) if d == 0 else (li >= si)).astype(F32)
        a_row = -jnp.exp(alog_row[:, SSM_H * d:SSM_H * (d + 1)])
        a_col = -jnp.exp(alog_col[SSM_H * d:SSM_H * (d + 1), :])
        b_row = dtb_row[:, SSM_H * d:SSM_H * (d + 1)]
        b_col = dtb_col[SSM_H * d:SSM_H * (d + 1), :]
        edge = SSM_Q - 1 if d == 0 else 0

        if h0_ref is not None:
            s_s[...] = jnp.tile(h0_ref[d], (1, SSM_H))
        else:
            s_s[...] = jnp.zeros_like(s_s)

        def chunk(ci, carry, d=d, causal=causal, tri_col=tri_col, tri_row=tri_row, a_row=a_row,
                  a_col=a_col, b_row=b_row, b_col=b_col, edge=edge):
            c = ci if d == 0 else nchunk - 1 - ci
            r0 = pl.multiple_of(c * SSM_Q, SSM_Q)
            rsl = pl.ds(r0, SSM_Q)
            if h0_ref is None and cps < nchunk:
                first = (c % cps == 0) if d == 0 else (c % cps == cps - 1)
                s_s[...] = s_s[...] * jnp.where(first, 0.0, 1.0)

            dt_c = jax.nn.softplus(u_dt[rsl, SSM_H * d:SSM_H * (d + 1)] + b_row)
            dt_r = jax.nn.softplus(u_dtT[c, SSM_H * d:SSM_H * (d + 1), :] + b_col)
            cs_col = jnp.dot(tri_col, dt_c * a_row, precision=HI, preferred_element_type=F32)
            cs_row = jnp.dot(dt_r * a_col, tri_row, precision=HI, preferred_element_type=F32)
            tot = cs_col[edge:edge + 1, :]

            xc = x_s[rsl, :]
            xd = xc * jnp.dot(dt_c, e8, precision=HI, preferred_element_type=F32)
            bcm = bc_s[rsl, :].astype(BF16)
            bm, cm = bcm[:, :SSM_G * SSM_N], bcm[:, SSM_G * SSM_N:]
            sc = []
            for g in range(SSM_G):
                gmat = lax.dot_general(cm[:, g * SSM_N:(g + 1) * SSM_N], bm[:, g * SSM_N:(g + 1) * SSM_N],
                                       (((1,), (1,)), ((), ())), preferred_element_type=F32)
                for h in range(g * (SSM_H // SSM_G), (g + 1) * (SSM_H // SSM_G)):
                    diff = cs_col[:, h:h + 1] - cs_row[h:h + 1, :]
                    sc.append((gmat * jnp.exp(jnp.where(causal, diff, -1e30))).astype(BF16))
            sc = jnp.concatenate(sc, axis=1)
            bd = (jnp.tile(xd, (SSM_H, 1)) * mbd_ref[...]).astype(BF16)
            y = jnp.dot(sc, bd, preferred_element_type=F32)

            bexp = jnp.dot(bm, eb_ref[...], preferred_element_type=F32)
            cexp = jnp.dot(cm, eb_ref[...], preferred_element_type=F32)
            s_prev = (s_s[...] * mdiag_ref[...]).astype(BF16)
            y_off = lax.dot_general(cexp.astype(BF16), s_prev, (((1,), (1,)), ((), ())),
                                    preferred_element_type=F32)
            y = y + y_off * jnp.dot(jnp.exp(cs_col), e8, precision=HI, preferred_element_type=F32)
            if d == 0:
                y_s[rsl, :] = y
            else:
                y_s[rsl, :] += y

            bx = bexp * jnp.dot(jnp.exp(tot - cs_col), e8, precision=HI, preferred_element_type=F32)
            s_new = jnp.dot(xd.T.astype(BF16), bx.astype(BF16), preferred_element_type=F32)
            s_s[...] = s_s[...] * jnp.dot(jnp.exp(tot), e8, precision=HI, preferred_element_type=F32) + s_new

            if st_ref is not None:
                last = (c % cps == cps - 1) if d == 0 else (c % cps == 0)

                @pl.when(last)
                def _():
                    sd = s_s[...] * mdiag_ref[...]
                    fold = sd[:, :LANE]
                    for j in range(1, hp // LANE):
                        fold = fold + sd[:, j * LANE:(j + 1) * LANE]
                    st_ref[c // cps, d] = jnp.concatenate(
                        [fold[2 * j * SSM_P:(2 * j + 1) * SSM_P] + fold[(2 * j + 1) * SSM_P:(2 * j + 2) * SSM_P]
                         for j in range(SSM_H // 2)], axis=0)
            return carry

        lax.fori_loop(0, nchunk, chunk, 0)

    for r0 in range(0, rows, 256):
        rsl = slice(r0, r0 + 256)
        y = y_s[rsl, :] + d_exp[...] * x_s[rsl, :]
        z = u_z[rsl, :]
        y = y * (z * jax.nn.sigmoid(z))
        o_ref[rsl, :] = _rms(y) * nw[...]


def _ssd_kernel(u_z, u_xbc, u_dt, u_dtT, cw, cb, dtb_row, dtb_col, alog_row, alog_col, d_exp, nw,
                e8_ref, eb_ref, mbd_ref, mdiag_ref, h0_ref, o_ref, st_ref, x_s, bc_s, y_s, s_s):
    b = pl.program_id(0)
    args = (u_z, u_xbc, u_dt, u_dtT, cw.at[0], cb.at[0], dtb_row.at[0], dtb_col.at[0], alog_row.at[0],
            alog_col.at[0], d_exp.at[0], nw.at[0], e8_ref, eb_ref, mbd_ref, mdiag_ref)
    scr = (x_s, bc_s, y_s, s_s)

    @pl.when(b < NB_CTX)
    def _():
        _ssd_block(*args, None, o_ref, st_ref, *scr, L_CTX)

    @pl.when(b >= NB_CTX)
    def _():
        _ssd_block(*args, h0_ref.at[0, 0], o_ref, None, *scr, L_LAT)


def _ssd_mixer(l, u_z, u_xbc, u_dt, u_dtT, p, c, state_ssm):
    lsel = lambda *shape: pl.BlockSpec((1,) + shape, lambda b: (l,) + (0,) * len(shape))
    full = lambda a: pl.BlockSpec(a.shape, lambda b: (0,) * a.ndim, pipeline_mode=pl.Buffered(1))
    hp = SSM_H * SSM_P
    nseq_blk = RB // L_CTX
    return pl.pallas_call(
        _ssd_kernel,
        grid=(NB,),
        in_specs=[
            pl.BlockSpec((RB, D_SSM), lambda b: (b, 0), pipeline_mode=pl.Buffered(1)),
            pl.BlockSpec((RB, D_XBC), lambda b: (b, 0)),
            pl.BlockSpec((RB, 2 * SSM_H), lambda b: (b, 0)),
            pl.BlockSpec((RB // SSM_Q, 2 * SSM_H, SSM_Q), lambda b: (b, 0, 0)),
            lsel(4, D_XBC), lsel(1, D_XBC), lsel(1, 2 * SSM_H), lsel(2 * SSM_H, 1),
            lsel(1, 2 * SSM_H), lsel(2 * SSM_H, 1), lsel(1, D_SSM), lsel(1, D_SSM),
            full(c["e8"]), full(c["eb"]), full(c["mbd"]), full(c["mdiag"]),
            pl.BlockSpec((1, 1, 2, hp, SSM_N), lambda b: (jnp.maximum(b - NB_CTX, 0), l, 0, 0, 0)),
        ],
        out_specs=[pl.BlockSpec((RB, D_SSM), lambda b: (b, 0)),
                   pl.BlockSpec((nseq_blk, 2, hp // 2, LANE), lambda b: (jnp.minimum(b, NB_CTX - 1), 0, 0, 0))],
        out_shape=[jax.ShapeDtypeStruct((T, D_SSM), F32),
                   jax.ShapeDtypeStruct((N_CTX_SEQ, 2, hp // 2, LANE), F32)],
        scratch_shapes=[pltpu.VMEM((RB, D_SSM), F32), pltpu.VMEM((RB, 2 * SSM_G * SSM_N), F32),
                        pltpu.VMEM((RB, D_SSM), F32), pltpu.VMEM((hp, hp), F32)],
        compiler_params=_cparams(1),
        name=f"ssd_mixer_{l}",
    )(u_z, u_xbc, u_dt, u_dtT, p["ssm_conv_w"], p["ssm_conv_b"], p["ssm_dtb_row"], p["ssm_dtb_col"],
      p["ssm_alog_row"], p["ssm_alog_col"], p["ssm_d_exp"], p["ssm_norm_w"],
      c["e8"], c["eb"], c["mbd"], c["mdiag"], state_ssm)


def _k2_kernel(*refs, routed):
    if routed:
        (ol_ref, oh_ref, os_ref, x_ref, mod_ref, nw_ref, w_ref, rt_ref,
         xo_ref, h2_ref, route_ref, cnt_ref, wbf_ref) = refs
    else:
        ol_ref, oh_ref, os_ref, x_ref, mod_ref, nw_ref, w_ref, xo_ref, h2_ref, wbf_ref = refs

    @pl.when(pl.program_id(0) == 0)
    def _():
        wbf_ref[...] = w_ref[0].astype(BF16)

    m = mod_ref[0]
    o = jnp.dot(ol_ref[...].astype(BF16), wbf_ref[0:256, :], preferred_element_type=F32)
    o = o + jnp.dot(oh_ref[...].astype(BF16), wbf_ref[256:512, :], preferred_element_type=F32)
    o = o + jnp.dot(os_ref[...].astype(BF16), wbf_ref[512:1024, :], preferred_element_type=F32)
    x = x_ref[...] + m[2:3] * o
    xo_ref[...] = x
    h2 = _rms(x) * nw_ref[0]
    h2 = h2 * (1.0 + m[4:5]) + m[3:4]
    h2_ref[...] = h2.astype(BF16)

    if routed:
        logits = jnp.dot(h2, rt_ref[0], precision=HI, preferred_element_type=F32)
        eid = lax.broadcasted_iota(I32, logits.shape, 1)
        m1 = jnp.max(logits, axis=1, keepdims=True)
        i1 = jnp.min(jnp.where(logits == m1, eid, N_EXP), axis=1, keepdims=True)
        rest = jnp.where(eid == i1, -jnp.inf, logits)
        m2 = jnp.max(rest, axis=1, keepdims=True)
        i2 = jnp.min(jnp.where(rest == m2, eid, N_EXP), axis=1, keepdims=True)
        w1 = 1.0 / (1.0 + jnp.exp(m2 - m1))
        w2 = 1.0 - w1
        oh1 = (eid == i1).astype(F32)
        oh2 = (eid == i2).astype(F32)
        both = oh1 + oh2
        before = (lax.broadcasted_iota(I32, (TM, TM), 0) > lax.broadcasted_iota(I32, (TM, TM), 1))
        ahead = jnp.dot(before.astype(BF16), both.astype(BF16), preferred_element_type=F32)
        r1 = jnp.sum(oh1 * ahead, axis=1, keepdims=True)
        r2 = jnp.sum(oh2 * ahead, axis=1, keepdims=True)
        zero = jnp.zeros_like(w1)
        route_ref[...] = jnp.concatenate(
            [i1.astype(F32), i2.astype(F32), r1, r2, w1, w2, zero, zero], axis=1)
        cnt_ref[0] = jnp.sum(both, axis=0, keepdims=True)


def _k2(l, o_lru, o_hy, o_ssm, x, mod_l, norm2_w, w_out, router=None, j=0):
    routed = router is not None
    tok = lambda w: pl.BlockSpec((TM, w), lambda i: (i, 0))
    in_specs = [
        tok(D_LRU), tok(D_HY), tok(D_SSM), tok(D),
        pl.BlockSpec((1, 6, D), lambda i: (_mod_row(i), 0, 0)),
        pl.BlockSpec((1, 1, D), lambda i: (l, 0, 0)),
        pl.BlockSpec((1, D, D), lambda i: (l, 0, 0)),
    ]
    args = [o_lru, o_hy, o_ssm, x, mod_l, norm2_w.reshape(DEPTH, 1, D), w_out]
    out_specs = [tok(D), tok(D)]
    out_shape = [jax.ShapeDtypeStruct((T, D), F32), jax.ShapeDtypeStruct((T, D), BF16)]
    if routed:
        in_specs.append(pl.BlockSpec((1, D, N_EXP), lambda i: (j, 0, 0)))
        args.append(router)
        out_specs += [tok(8), pl.BlockSpec((1, 1, N_EXP), lambda i: (i, 0, 0))]
        out_shape += [jax.ShapeDtypeStruct((T, 8), F32), jax.ShapeDtypeStruct((NT, 1, N_EXP), F32)]
    return pl.pallas_call(
        functools.partial(_k2_kernel, routed=routed),
        grid=(NT,),
        in_specs=in_specs,
        out_specs=out_specs,
        out_shape=out_shape,
        scratch_shapes=[pltpu.VMEM((D, D), BF16)],
        compiler_params=_cparams(1),
        name=f"k2_outproj_{l}",
    )(*args)


def _ffn_rows(x, wg_ref, wu_ref, wd_ref):
    acc = None
    for c0 in range(0, D_FF, FF_CHUNK):
        g = jnp.dot(x, wg_ref[0, 0, :, c0:c0 + FF_CHUNK], preferred_element_type=F32)
        u = jnp.dot(x, wu_ref[0, 0, :, c0:c0 + FF_CHUNK], preferred_element_type=F32)
        hmid = (g * jax.nn.sigmoid(g) * u).astype(BF16)
        part = jnp.dot(hmid, wd_ref[0, 0, c0:c0 + FF_CHUNK, :], preferred_element_type=F32)
        acc = part if acc is None else acc + part
    return acc


def _dense_ffn_kernel(h_ref, x_ref, mod_ref, wg_ref, wu_ref, wd_ref, o_ref):
    y = _ffn_rows(h_ref[...], wg_ref, wu_ref, wd_ref)
    o_ref[...] = x_ref[...] + mod_ref[0][5:6] * y


def _dense_ffn(j, h2, x, mod_l, wg, wu, wd):
    wsel = lambda r, c: pl.BlockSpec((1, 1, r, c), lambda i: (j, 0, 0, 0))
    return pl.pallas_call(
        _dense_ffn_kernel,
        grid=(NT,),
        in_specs=[
            pl.BlockSpec((TM, D), lambda i: (i, 0)),
            pl.BlockSpec((TM, D), lambda i: (i, 0)),
            pl.BlockSpec((1, 6, D), lambda i: (_mod_row(i), 0, 0)),
            wsel(D, D_FF), wsel(D, D_FF), wsel(D_FF, D),
        ],
        out_specs=pl.BlockSpec((TM, D), lambda i: (i, 0)),
        out_shape=jax.ShapeDtypeStruct((T, D), F32),
        compiler_params=_cparams(1),
        name=f"dense_ffn_{j}",
    )(h2, x, mod_l, wg, wu, wd)


def _expert_ffn_kernel(be_ref, bi_ref, na_ref, x_ref, wg_ref, wu_ref, wd_ref, o_ref):
    active = pl.program_id(0) < na_ref[0]

    @pl.when(active)
    def _():
        o_ref[...] = _ffn_rows(x_ref[...], wg_ref, wu_ref, wd_ref)

    @pl.when(jnp.logical_not(active))
    def _():
        o_ref[...] = jnp.zeros_like(o_ref)


def _expert_ffn(j, xs, blk_e, blk_i, n_active, wg, wu, wd):
    wsel = lambda r, c: pl.BlockSpec((1, 1, r, c), lambda b, be, bi, na: (j, be[b], 0, 0))
    return pl.pallas_call(
        _expert_ffn_kernel,
        grid_spec=pltpu.PrefetchScalarGridSpec(
            num_scalar_prefetch=3,
            grid=(NBLK,),
            in_specs=[
                pl.BlockSpec((TMB, D), lambda b, be, bi, na: (bi[b], 0)),
                wsel(D, D_FF), wsel(D, D_FF), wsel(D_FF, D),
            ],
            out_specs=pl.BlockSpec((TMB, D), lambda b, be, bi, na: (b, 0)),
        ),
        out_shape=jax.ShapeDtypeStruct((RMAX, D), F32),
        compiler_params=_cparams(1),
        name=f"expert_ffn_{j}",
    )(blk_e, blk_i, n_active, xs, wg, wu, wd)


def _chunk_rows(idx):
    return pl.ds(pl.multiple_of(idx * GCH, GCH), GCH)


def _sort_kernel(gch_ref, nused_ref, pad0_ref, npad_ref, h_ref, dl_ref, xs_ref, xl_ref, zero_ref, sem):
    i = pl.program_id(0)
    dl = dl_ref[0]
    r = lax.broadcasted_iota(I32, (LCAP, TM), 0).astype(F32)
    perm = jnp.logical_or(r == dl[0:1, :], r == dl[1:2, :]).astype(BF16)
    xl_ref[...] = jnp.dot(perm, h_ref[...], preferred_element_type=F32).astype(BF16)

    def copy(q):
        return pltpu.make_async_copy(xl_ref.at[_chunk_rows(q)],
                                     xs_ref.at[_chunk_rows(gch_ref[i * NLC + q])], sem.at[0])

    n = nused_ref[i]
    lax.fori_loop(0, n, lambda q, c: (copy(q).start(), c)[1], 0)
    lax.fori_loop(0, n, lambda q, c: (copy(q).wait(), c)[1], 0)

    @pl.when(i == NT - 1)
    def _():
        zero_ref[...] = jnp.zeros_like(zero_ref)
        for e in range(N_EXP + 1):
            def zcopy(q, e=e):
                return pltpu.make_async_copy(zero_ref, xs_ref.at[_chunk_rows(pad0_ref[e] + q)], sem.at[1])
            lax.fori_loop(0, npad_ref[e], lambda q, c, f=zcopy: (f(q).start(), c)[1], 0)
            lax.fori_loop(0, npad_ref[e], lambda q, c, f=zcopy: (f(q).wait(), c)[1], 0)


def _sort_tokens(h2, dl_row, gch, nused, pad0, npad):
    return pl.pallas_call(
        _sort_kernel,
        grid_spec=pltpu.PrefetchScalarGridSpec(
            num_scalar_prefetch=4,
            grid=(NT,),
            in_specs=[
                pl.BlockSpec((TM, D), lambda i, *_: (i, 0)),
                pl.BlockSpec((1, 2, TM), lambda i, *_: (i, 0, 0)),
            ],
            out_specs=pl.BlockSpec(memory_space=pl.ANY),
            scratch_shapes=[pltpu.VMEM((LCAP, D), BF16), pltpu.VMEM((GCH, D), BF16),
                            pltpu.SemaphoreType.DMA((2,))],
        ),
        out_shape=jax.ShapeDtypeStruct((RMAX, D), BF16),
        compiler_params=_cparams(1),
        name="moe_sort",
    )(gch, nused, pad0, npad, h2, dl_row)


def _combine_kernel(gch_ref, nused_ref, y_ref, info_ref, x_ref, mod_ref, fw_ref, o_ref, yl_ref, sem,
                    *, final_norm):
    i = pl.program_id(0)
    n = nused_ref[i]

    def copy(q):
        return pltpu.make_async_copy(y_ref.at[_chunk_rows(gch_ref[i * NLC + q])],
                                     yl_ref.at[_chunk_rows(q)], sem.at[0])

    lax.fori_loop(0, n, lambda q, c: (copy(q).start(), c)[1], 0)

    def clear(q, c):
        yl_ref[_chunk_rows(q), :] = jnp.zeros((GCH, D), F32)
        return c

    lax.fori_loop(n, NLC, clear, 0)
    lax.fori_loop(0, n, lambda q, c: (copy(q).wait(), c)[1], 0)

    info = info_ref[...]
    col = lax.broadcasted_iota(I32, (TM, LCAP), 1).astype(F32)
    yl = yl_ref[...].astype(BF16)
    y1 = jnp.dot((col == info[:, 0:1]).astype(BF16), yl, preferred_element_type=F32)
    y2 = jnp.dot((col == info[:, 1:2]).astype(BF16), yl, preferred_element_type=F32)
    x = x_ref[...] + mod_ref[0][5:6] * (info[:, 2:3] * y1 + info[:, 3:4] * y2)
    if final_norm:
        x = _rms(x) * fw_ref[...]
    o_ref[...] = x


def _combine(y, info, x, mod_l, final_w, gch, nused, final_norm):
    return pl.pallas_call(
        functools.partial(_combine_kernel, final_norm=final_norm),
        grid_spec=pltpu.PrefetchScalarGridSpec(
            num_scalar_prefetch=2,
            grid=(NT,),
            in_specs=[
                pl.BlockSpec(memory_space=pl.ANY),
                pl.BlockSpec((TM, 4), lambda i, *_: (i, 0)),
                pl.BlockSpec((TM, D), lambda i, *_: (i, 0)),
                pl.BlockSpec((1, 6, D), lambda i, *_: (_mod_row(i), 0, 0)),
                pl.BlockSpec((1, D), lambda i, *_: (0, 0)),
            ],
            out_specs=pl.BlockSpec((TM, D), lambda i, *_: (i, 0)),
            scratch_shapes=[pltpu.VMEM((LCAP, D), F32), pltpu.SemaphoreType.DMA((1,))],
        ),
        out_shape=jax.ShapeDtypeStruct((T, D), F32),
        compiler_params=_cparams(1),
        name="moe_combine",
    )(gch, nused, y, info, x, mod_l, final_w.reshape(1, D))


def _moe_plan(route, counts):
    cnt = counts.reshape(NT, N_EXP).astype(I32)
    cpad = (cnt + GCH - 1) // GCH * GCH
    lo = jnp.cumsum(cpad, axis=1) - cpad
    nused = (lo[:, -1] + cpad[:, -1]) // GCH
    tot = jnp.sum(cpad, axis=0)
    gpad = (tot + TMB - 1) // TMB * TMB
    goff = jnp.cumsum(gpad) - gpad
    so = goff[None, :] + jnp.cumsum(cpad, axis=0) - cpad

    e1 = route[:, 0].astype(I32)
    e2 = route[:, 1].astype(I32)
    eid = jnp.arange(N_EXP, dtype=I32)[None, :]
    lo_tok = jnp.repeat(lo, TM, axis=0)
    dl1 = jnp.sum(jnp.where(e1[:, None] == eid, lo_tok, 0), axis=1).astype(F32) + route[:, 2]
    dl2 = jnp.sum(jnp.where(e2[:, None] == eid, lo_tok, 0), axis=1).astype(F32) + route[:, 3]
    dl_row = jnp.stack([dl1.reshape(NT, TM), dl2.reshape(NT, TM)], axis=1)
    info = jnp.stack([dl1, dl2, route[:, 4], route[:, 5]], axis=1)

    q = jnp.arange(NLC, dtype=I32)[None, :, None]
    lo16 = (lo // GCH)[:, None, :]
    c16 = (cpad // GCH)[:, None, :]
    in_seg = jnp.logical_and(q >= lo16, q < lo16 + c16)
    gch = jnp.sum(jnp.where(in_seg, (so // GCH)[:, None, :] + q - lo16, 0), axis=2).reshape(NT * NLC)

    nblk = gpad // TMB
    n_active = jnp.sum(nblk)
    b = jnp.arange(NBLK, dtype=I32)
    blk_i = jnp.minimum(b, n_active - 1)
    bend = (goff + gpad) // TMB
    blk_e = jnp.minimum(jnp.sum((blk_i[:, None] >= bend[None, :]).astype(I32), axis=1), N_EXP - 1)
    used = n_active * TMB
    pad0 = jnp.concatenate([goff + tot, used.reshape(1)]) // GCH
    npad = jnp.concatenate([gpad - tot, (RMAX - used).reshape(1)]) // GCH
    return dict(dl_row=dl_row, info=info, gch=gch.astype(I32), nused=nused.astype(I32),
                blk_e=blk_e.astype(I32), blk_i=blk_i.astype(I32),
                n_active=n_active.reshape(1).astype(I32), pad0=pad0.astype(I32), npad=npad.astype(I32))


def _grid_pos_embed(n_tokens):
    rows = n_tokens // GRID_W
    r = jnp.repeat(jnp.arange(rows, dtype=F32), GRID_W)
    col = jnp.tile(jnp.arange(GRID_W, dtype=F32), rows)
    quarter = D // 4
    omega = 1.0 / (10000.0 ** (jnp.arange(quarter, dtype=F32) / quarter))
    ang_r = r[:, None] * omega[None]
    ang_c = col[:, None] * omega[None]
    return jnp.concatenate([jnp.sin(ang_r), jnp.cos(ang_r), jnp.sin(ang_c), jnp.cos(ang_c)], axis=-1)


def _hy_pos_features(n):
    pos = jnp.arange(n, dtype=F32)
    t = pos / (n - 1)
    bands = jnp.linspace(1e-4, HY_BANDS - 1, HY_BANDS, dtype=F32)
    ang = (2.0 * math.pi * pos / n)[:, None] * bands[None]
    z = jnp.concatenate([t[:, None], jnp.cos(ang), -jnp.sin(ang)], axis=-1)
    z = jnp.pad(z, ((0, 0), (0, LANE - HY_POS_DIM)))
    half = n // 2
    dist = jnp.abs(pos - half) / half
    deltas = jnp.abs(jnp.linspace(HY_MIN_DECAY, HY_MAX_DECAY, D_HY, dtype=F32))
    return z, jnp.exp(-dist[:, None] * deltas[None])


def _dft_mats(lseq, n, kb):
    half = n // 2
    k = jnp.arange(half, dtype=I32)
    t = jnp.arange(lseq, dtype=I32)
    ang = (2.0 * math.pi / n) * ((k[:, None] * t[None, :]) % n).astype(F32)
    alt = 1.0 - 2.0 * (t % 2).astype(F32)
    re = jnp.cos(ang)
    im = (-jnp.sin(ang)).at[0].set(alt)
    f = jnp.concatenate([re.reshape(half // kb, kb, lseq), im.reshape(half // kb, kb, lseq)], axis=1)
    tp = t + lseq // 2
    ang = (2.0 * math.pi / n) * ((tp[:, None] * k[None, :]) % n).astype(F32)
    wk = jnp.where(k == 0, 1.0, 2.0) / n
    alt = (1.0 - 2.0 * (tp % 2).astype(F32)) / n
    gre = wk * jnp.cos(ang)
    gim = (-wk * jnp.sin(ang)).at[:, 0].set(alt)
    g = jnp.concatenate([gre.reshape(lseq, half // kb, kb), gim.reshape(lseq, half // kb, kb)], axis=2)
    return f.reshape(n, lseq).astype(BF16), g.reshape(lseq, n).astype(BF16)


def _ssd_constants():
    hp = SSM_H * SSM_P
    head_of = np.arange(hp) // SSM_P
    e8 = (np.arange(SSM_H)[:, None] == head_of[None, :]).astype(np.float32)
    gn = np.arange(SSM_G * SSM_N)
    eb = ((gn[:, None] // SSM_N == head_of[None, :] // (SSM_H // SSM_G))
          & (gn[:, None] % SSM_N == np.arange(hp)[None, :] % SSM_P)).astype(np.float32)
    mbd = (np.arange(SSM_H * SSM_Q)[:, None] // SSM_Q == head_of[None, :]).astype(np.float32)
    mdiag = (head_of[:, None] == head_of[None, :]).astype(np.float32)
    return dict(e8=jnp.asarray(e8), eb=jnp.asarray(eb, dtype=BF16), mbd=jnp.asarray(mbd),
                mdiag=jnp.asarray(mdiag))


def _unpack_ssm_state(s):
    s = s.reshape(N_CTX_SEQ, 2, SSM_H // 2, SSM_P, 2, SSM_N)
    return jnp.transpose(s, (0, 1, 2, 4, 3, 5)).reshape(N_CTX_SEQ, 2, SSM_H, SSM_P, SSM_N)


def _block_diag_heads(w):
    eye = jnp.eye(LRU_HEADS, dtype=w.dtype)
    return jnp.einsum("ldhij,hg->ldhigj", w, eye).reshape(DEPTH, 2, D_LRU, D_LRU)


def kernel(x_prompt, x_sample, state_lru, state_ssm, c, c_ctx, norm1_w, norm2_w, final_norm_w, ada_w, ada_b,
           w_in, w_out, lru_conv_w, lru_conv_b, lru_wa, lru_ba, lru_wi, lru_bi, lru_lambda, hy_conv_w, hy_conv_b,
           hy_w1, hy_b1, hy_w2, hy_b2, hy_freq, hy_w3, hy_bias, ssm_conv_w, ssm_conv_b, ssm_dt_bias, ssm_a_log,
           ssm_d, ssm_norm_w, ffn_w_gate, ffn_w_up, ffn_w_down, moe_router, moe_w_gate, moe_w_up, moe_w_down):
    hp = SSM_H * SSM_P
    wa, wi = _block_diag_heads(lru_wa), _block_diag_heads(lru_wi)
    row = lambda a: a.reshape(DEPTH, 1, -1)
    p = {
        "lru_conv_w": lru_conv_w, "lru_conv_b": row(lru_conv_b), "lru_lambda": lru_lambda,
        "lru_wbig": jnp.concatenate([wa[:, 0], wi[:, 0], wa[:, 1], wi[:, 1]], axis=-1).astype(BF16),
        "lru_bias": jnp.concatenate([lru_ba[:, 0], lru_bi[:, 0], lru_ba[:, 1], lru_bi[:, 1]], axis=-1)[:, None],
        "hy_conv_w": hy_conv_w, "hy_conv_b": row(hy_conv_b), "hy_bias": row(hy_bias),
        "hy_w1p": jnp.pad(hy_w1, ((0, 0), (0, LANE - HY_POS_DIM), (0, 0))), "hy_b1": row(hy_b1),
        "hy_w2": hy_w2, "hy_b2": row(hy_b2), "hy_freq": row(hy_freq), "hy_w3": hy_w3,
        "ssm_conv_w": ssm_conv_w, "ssm_conv_b": row(ssm_conv_b),
        "ssm_dtb_row": row(ssm_dt_bias), "ssm_dtb_col": ssm_dt_bias.reshape(DEPTH, 2 * SSM_H, 1),
        "ssm_alog_row": row(ssm_a_log), "ssm_alog_col": ssm_a_log.reshape(DEPTH, 2 * SSM_H, 1),
        "ssm_d_exp": jnp.repeat(ssm_d, SSM_P, axis=-1)[:, None], "ssm_norm_w": row(ssm_norm_w),
    }
    w_dtT = jnp.swapaxes(w_in[:, :, D_MAIN:], 1, 2)
    ffn_w = [w.astype(BF16)[:, None] for w in (ffn_w_gate, ffn_w_up, ffn_w_down)]
    moe_w = [w.astype(BF16) for w in (moe_w_gate, moe_w_up, moe_w_down)]

    cst = _ssd_constants()
    z_ctx, win_ctx = _hy_pos_features(L_CTX)
    z_lat, win_lat = _hy_pos_features(L_LAT)
    cst["f_ctx"], cst["g_ctx"] = _dft_mats(L_CTX, N_FFT_CTX, N_FFT_CTX // 2)
    cst["f_lat"], cst["g_lat"] = _dft_mats(L_LAT, N_FFT_LAT, HY_KB_LAT)
    cst["hf_ctx"] = _hy_filter_spectrum(L_CTX, N_FFT_CTX // 2, z_ctx, win_ctx, cst["f_ctx"], p)
    cst["hf_lat"] = _hy_filter_spectrum(L_LAT, HY_KB_LAT, z_lat, win_lat, cst["f_lat"], p)

    cond = jnp.concatenate([c_ctx[None], c], axis=0)
    mod = _mod_table(jnp.broadcast_to(cond[:, :, None], (3, D, LANE)), ada_w, ada_b)
    mod = mod[:, :3].reshape(DEPTH, 3, 6, D)

    x = jnp.concatenate([x_prompt.reshape(T_CTX, D),
                         (x_sample + _grid_pos_embed(L_LAT)[None]).reshape(T_LAT, D)], axis=0)
    st_ssm_in = state_ssm.reshape(N_LAT_SEQ, DEPTH, 2, hp, SSM_N)

    lru_states, ssm_states = [], []
    for l in range(DEPTH):
        u_lru, u_hy, u_z, u_xbc, u_dt, u_dtT = _k1(l, x, mod[l], norm1_w, w_in, w_dtT)
        o_lru, s_lru = _lru_mixer(l, u_lru, p, state_lru)
        o_hy = _hy_mixer(l, u_hy, p, cst)
        o_ssm, s_ssm = _ssd_mixer(l, u_z, u_xbc, u_dt, u_dtT, p, cst, st_ssm_in)
        lru_states.append(s_lru.reshape(N_CTX_SEQ, 2, D_LRU))
        ssm_states.append(_unpack_ssm_state(s_ssm))
        j = l // 2
        if l % 2 == 0:
            x, h2 = _k2(l, o_lru, o_hy, o_ssm, x, mod[l], norm2_w, w_out)
            x = _dense_ffn(j, h2, x, mod[l], *ffn_w)
        else:
            x, h2, route, counts = _k2(l, o_lru, o_hy, o_ssm, x, mod[l], norm2_w, w_out, moe_router, j)
            plan = _moe_plan(route, counts)
            xs = _sort_tokens(h2, plan["dl_row"], plan["gch"], plan["nused"], plan["pad0"], plan["npad"])
            y = _expert_ffn(j, xs, plan["blk_e"], plan["blk_i"], plan["n_active"], *moe_w)
            x = _combine(y, plan["info"], x, mod[l], final_norm_w, plan["gch"], plan["nused"],
                         final_norm=(l == DEPTH - 1))
    y_prompt = x[:T_CTX].reshape(N_CTX_SEQ, L_CTX, D)
    y_sample = x[T_CTX:].reshape(N_LAT_SEQ, L_LAT, D)
    return (y_prompt, y_sample, jnp.stack(lru_states, axis=1), jnp.stack(ssm_states, axis=1))
```

```python
import functools
import math

import numpy as np
import jax
import jax.numpy as jnp
from jax import lax
from jax.experimental import pallas as pl
from jax.experimental.pallas import tpu as pltpu

F32 = jnp.float32
BF16 = jnp.bfloat16
I32 = jnp.int32
HI = lax.Precision.HIGHEST

D = 1024
N_CTX_SEQ, L_CTX = 16, 256
N_LAT_SEQ, L_LAT = 2, 2048
DEPTH = 4
GRID_W = 64
D_LRU = 256
LRU_HEADS, LRU_HD = 4, 64
LRU_C = 8.0
D_HY = 256
HY_BANDS = 16
HY_POS_DIM = 1 + 2 * HY_BANDS
HY_HID = 64
HY_MAX_DECAY = math.log(1e-2) / 0.3
HY_MIN_DECAY = math.log(1e-2) / 1.5
D_SSM = 512
SSM_P = 64
SSM_H = 8
SSM_G = 2
SSM_N = 64
SSM_Q = 128
D_XBC = D_SSM + 2 * SSM_G * SSM_N
D_MAIN = 2 * D_LRU + 3 * D_HY + D_SSM + D_XBC
D_IN = D_MAIN + 2 * SSM_H
D_FF = 2816
N_EXP = 8
EPS = 1e-6

LANE = 128
BF16_ROWS = 16
T_CTX = N_CTX_SEQ * L_CTX
T_LAT = N_LAT_SEQ * L_LAT
T = T_CTX + T_LAT
TM = 512
NT = T // TM
NT_CTX = T_CTX // TM
NT_PER_LAT = L_LAT // TM
RB = 2048
NB = T // RB
NB_CTX = T_CTX // RB
FF_CHUNK = 256
N_FF_CHUNK = D_FF // FF_CHUNK
FFN_STAGES = 3
VMEM_LIMIT = 56 * 1024 * 1024

GCH = BF16_ROWS
LCAP = 2 * TM + N_EXP * GCH
NLC = LCAP // GCH
TMB = 512
RMAX = -(-(2 * T + NT * N_EXP * (GCH - 1) + N_EXP * (TMB - 1)) // TMB) * TMB
NBLK = RMAX // TMB

HY_KB_LAT = 256
N_FFT_LAT = 3 * L_LAT // 2
N_FFT_CTX = 3 * L_CTX // 2
HY_NKB = N_FFT_LAT // 2 // HY_KB_LAT


def _cparams(n_axes=1, vmem=VMEM_LIMIT):
    return pltpu.CompilerParams(dimension_semantics=("arbitrary",) * n_axes, vmem_limit_bytes=vmem)


def _mod_row(i):
    return jnp.where(i < NT_CTX, 0, 1 + (i - NT_CTX) // NT_PER_LAT)


def _bdot(a, b):
    return jnp.dot(a.astype(BF16), b.astype(BF16), preferred_element_type=F32)


def _rms(x):
    return x * lax.rsqrt(jnp.mean(x * x, axis=-1, keepdims=True) + EPS)


def _split_bf16(v, parts):
    out = []
    for _ in range(parts):
        piece = v.astype(BF16)
        out.append(piece)
        v = v - piece.astype(F32)
    return out


def _mod_kernel(cb_ref, w_ref, b_ref, o_ref):
    tn = w_ref.shape[2]

    def body(kc, accs):
        k0 = pl.multiple_of(kc * 8, 8)
        wk = w_ref[0, pl.ds(k0, 8), :]
        out = []
        for r in range(3):
            c = cb_ref[r, pl.ds(k0, 8), :]
            c = c * jax.nn.sigmoid(c)
            out.append(accs[r] + jnp.tile(c, (1, tn // LANE)) * wk)
        return tuple(out)

    accs = lax.fori_loop(0, D // 8, body, tuple(jnp.zeros((8, tn), F32) for _ in range(3)), unroll=8)
    rows = [jnp.sum(a, axis=0, keepdims=True) + b_ref[0] for a in accs]
    o_ref[0] = jnp.concatenate(rows + [jnp.zeros((5, tn), F32)], axis=0)


def _mod_table(cond_b, ada_w, ada_b):
    tn = 1024
    return pl.pallas_call(
        _mod_kernel,
        grid=(DEPTH, 6 * D // tn),
        in_specs=[
            pl.BlockSpec((3, D, LANE), lambda l, j: (0, 0, 0)),
            pl.BlockSpec((1, D, tn), lambda l, j: (l, 0, j)),
            pl.BlockSpec((1, 1, tn), lambda l, j: (l, 0, j)),
        ],
        out_specs=pl.BlockSpec((1, 8, tn), lambda l, j: (l, 0, j)),
        out_shape=jax.ShapeDtypeStruct((DEPTH, 8, 6 * D), F32),
        compiler_params=_cparams(2),
        name="mod_table",
    )(cond_b, ada_w, ada_b.reshape(DEPTH, 1, 6 * D))


def _k1_kernel(x_ref, mod_ref, nw_ref, w_ref, wdtT_ref,
               o_lru, o_hy, o_z, o_xbc, o_dt, o_dtT, wbf_ref):
    @pl.when(pl.program_id(0) == 0)
    def _():
        wbf_ref[...] = w_ref[0].astype(BF16)

    m = mod_ref[0]
    h = _rms(x_ref[...]) * nw_ref[0]
    h = h * (1.0 + m[1:2]) + m[0:1]
    hb = h.astype(BF16)

    def proj(lo, hi):
        return jnp.dot(hb, wbf_ref[:, lo:hi], preferred_element_type=F32)

    o_lru[...] = proj(0, 512)
    o_hy[...] = proj(512, 1280)
    o_z[...] = proj(1280, 1792)
    o_xbc[...] = proj(1792, 2560)
    o_dt[...] = proj(D_MAIN, D_IN)
    dtT = lax.dot_general(wdtT_ref[0].astype(BF16), hb, (((1,), (1,)), ((), ())),
                          preferred_element_type=F32)
    for j in range(TM // SSM_Q):
        o_dtT[j] = dtT[:, j * SSM_Q:(j + 1) * SSM_Q]


def _k1(l, x, mod_l, norm1_w, w_in, w_dtT):
    tok = lambda w: pl.BlockSpec((TM, w), lambda i: (i, 0))
    return pl.pallas_call(
        _k1_kernel,
        grid=(NT,),
        in_specs=[
            tok(D),
            pl.BlockSpec((1, 6, D), lambda i: (_mod_row(i), 0, 0)),
            pl.BlockSpec((1, 1, D), lambda i: (l, 0, 0)),
            pl.BlockSpec((1, D, D_IN), lambda i: (l, 0, 0)),
            pl.BlockSpec((1, 2 * SSM_H, D), lambda i: (l, 0, 0)),
        ],
        out_specs=[tok(512), tok(768), tok(512), tok(768), tok(2 * SSM_H),
                   pl.BlockSpec((TM // SSM_Q, 2 * SSM_H, SSM_Q), lambda i: (i, 0, 0))],
        out_shape=[jax.ShapeDtypeStruct((T, 512), F32), jax.ShapeDtypeStruct((T, 768), F32),
                   jax.ShapeDtypeStruct((T, 512), F32), jax.ShapeDtypeStruct((T, 768), F32),
                   jax.ShapeDtypeStruct((T, 2 * SSM_H), F32),
                   jax.ShapeDtypeStruct((T // SSM_Q, 2 * SSM_H, SSM_Q), F32)],
        scratch_shapes=[pltpu.VMEM((D, D_IN), BF16)],
        compiler_params=_cparams(1),
        name=f"k1_inproj_{l}",
    )(x, mod_l, norm1_w.reshape(DEPTH, 1, D), w_in, w_dtT)


def _row_in_seq(rows, lseq):
    return lax.broadcasted_iota(I32, (rows, 1), 0) & (lseq - 1)


def _shift_rows(x, s, rin, lseq):
    if s == 0:
        return x
    y = pltpu.roll(x, s % x.shape[0], axis=0)
    valid = (rin >= s) if s > 0 else (rin < lseq + s)
    return jnp.where(valid, y, 0.0)


def _dwconv(x, w_ref, b_ref, rin, lseq):
    k_w = w_ref.shape[0]
    y = b_ref[...]
    for k in range(k_w):
        y = y + w_ref[k:k + 1, :] * _shift_rows(x, k_w // 2 - k, rin, lseq)
    return y


def _lru_block(u_ref, cw_ref, cb_ref, wbig_ref, bias_ref, lam_ref, h0_ref, o_ref, st_ref, lseq):
    rows = u_ref.shape[0]
    rin = _row_in_seq(rows, lseq)
    u = u_ref[...]
    gate = u[:, D_LRU:]
    x = _dwconv(u[:, :D_LRU], cw_ref, cb_ref, rin, lseq)
    xb = x.astype(BF16)
    y = None
    finals = []
    for d in range(2):
        g = jnp.dot(xb, wbig_ref[:, 512 * d:512 * (d + 1)], preferred_element_type=F32)
        g = g + bias_ref[:, 512 * d:512 * (d + 1)]
        r = jax.nn.sigmoid(g[:, :D_LRU])
        ig = jax.nn.sigmoid(g[:, D_LRU:])
        log_a = -LRU_C * r * jax.nn.softplus(-lam_ref[d:d + 1, :])
        a = jnp.exp(log_a)
        th = jnp.tanh(log_a)
        b = jnp.sqrt(-2.0 * th / (1.0 - th)) * (ig * x)
        if h0_ref is not None:
            edge = (rin == 0) if d == 0 else (rin == lseq - 1)
            b = b + jnp.where(edge, a * h0_ref[d:d + 1, :], 0.0)
        s = 1
        while s < lseq:
            sh = (s if d == 0 else -s) % rows
            valid = (rin >= s) if d == 0 else (rin < lseq - s)
            b = b + jnp.where(valid, a * pltpu.roll(b, sh, axis=0), 0.0)
            if 2 * s < lseq:
                a = jnp.where(valid, a * pltpu.roll(a, sh, axis=0), a)
            s *= 2
        y = b if y is None else y + b
        if st_ref is not None:
            last = lseq - 1 if d == 0 else 0
            finals.append(jnp.concatenate(
                [b[j * lseq + last:j * lseq + last + 1, :] for j in range(rows // lseq)], axis=0))
    o_ref[...] = y * jax.nn.gelu(gate)
    if st_ref is not None:
        st_ref[...] = jnp.concatenate(finals, axis=1)


def _lru_kernel(u_ref, cw_ref, cb_ref, wbig_ref, bias_ref, lam_ref, h0_ref, o_ref, st_ref):
    b = pl.program_id(0)
    args = (u_ref, cw_ref.at[0], cb_ref.at[0], wbig_ref.at[0], bias_ref.at[0], lam_ref.at[0])

    @pl.when(b < NB_CTX)
    def _():
        _lru_block(*args, None, o_ref, st_ref, L_CTX)

    @pl.when(b >= NB_CTX)
    def _():
        _lru_block(*args, h0_ref.at[0, 0], o_ref, None, L_LAT)


def _lru_mixer(l, u_lru, p, state_lru):
    lsel = lambda *shape: pl.BlockSpec((1,) + shape, lambda b: (l,) + (0,) * len(shape))
    return pl.pallas_call(
        _lru_kernel,
        grid=(NB,),
        in_specs=[
            pl.BlockSpec((RB, 512), lambda b: (b, 0)),
            lsel(4, D_LRU), lsel(1, D_LRU), lsel(D_LRU, 1024), lsel(1, 1024), lsel(2, D_LRU),
            pl.BlockSpec((1, 1, 2, D_LRU), lambda b: (jnp.maximum(b - NB_CTX, 0), l, 0, 0)),
        ],
        out_specs=[pl.BlockSpec((RB, D_LRU), lambda b: (b, 0)),
                   pl.BlockSpec((RB // L_CTX, 2 * D_LRU), lambda b: (jnp.minimum(b, NB_CTX - 1), 0))],
        out_shape=[jax.ShapeDtypeStruct((T, D_LRU), F32),
                   jax.ShapeDtypeStruct((N_CTX_SEQ, 2 * D_LRU), F32)],
        compiler_params=_cparams(1),
        name=f"lru_mixer_{l}",
    )(u_lru, p["lru_conv_w"], p["lru_conv_b"], p["lru_wbig"], p["lru_bias"], p["lru_lambda"], state_lru)


def _hy_filter_kernel(z_ref, win_ref, f_ref, w1_ref, b1_ref, w2_ref, b2_ref, fr_ref, w3_ref,
                      o_ref, h_ref):
    @pl.when(pl.program_id(1) == 0)
    def _():
        fr = fr_ref[0]
        g = jnp.sin(fr * (jnp.dot(z_ref[...], w1_ref[0], precision=HI, preferred_element_type=F32)
                          + b1_ref[0]))
        g = jnp.sin(fr * (jnp.dot(g, w2_ref[0], precision=HI, preferred_element_type=F32) + b2_ref[0]))
        h = jnp.dot(g, w3_ref[0], precision=HI, preferred_element_type=F32)
        h_ref[...] = (h * win_ref[...]).astype(BF16)

    o_ref[0] = jnp.dot(f_ref[...], h_ref[...], preferred_element_type=F32)


def _hy_filter_spectrum(lseq, kb, zfeat, window, fmat, p):
    n = fmat.shape[0]
    lsel = lambda *shape: pl.BlockSpec((1,) + shape, lambda l, k: (l,) + (0,) * len(shape))
    return pl.pallas_call(
        _hy_filter_kernel,
        grid=(DEPTH, n // (2 * kb)),
        in_specs=[
            pl.BlockSpec((lseq, LANE), lambda l, k: (0, 0)),
            pl.BlockSpec((lseq, D_HY), lambda l, k: (0, 0)),
            pl.BlockSpec((2 * kb, lseq), lambda l, k: (k, 0)),
            lsel(LANE, HY_HID), lsel(1, HY_HID), lsel(HY_HID, HY_HID), lsel(1, HY_HID),
            lsel(1, HY_HID), lsel(HY_HID, D_HY),
        ],
        out_specs=pl.BlockSpec((1, 2 * kb, D_HY), lambda l, k: (l, k, 0)),
        out_shape=jax.ShapeDtypeStruct((DEPTH, n, D_HY), F32),
        scratch_shapes=[pltpu.VMEM((lseq, D_HY), BF16)],
        compiler_params=_cparams(2),
        name=f"hyena_filter_{lseq}",
    )(zfeat, window, fmat, p["hy_w1p"], p["hy_b1"], p["hy_w2"], p["hy_b2"], p["hy_freq"], p["hy_w3"])


def _hy_spectral_block(f_blk, g_blk, hf, z_bf, is_dc_block):
    kb = f_blk.shape[0] // 2
    zf = jnp.dot(f_blk, z_bf, preferred_element_type=F32)
    rz, iz = zf[:kb], zf[kb:]
    rh, ih = hf[:kb], hf[kb:]
    ii = iz * ih
    re = rz * rh - ii
    im = rz * ih + iz * rh
    if is_dc_block is not None:
        dc = jnp.logical_and(lax.broadcasted_iota(I32, (kb, 1), 0) == 0, is_dc_block)
        re = jnp.where(dc, rz * rh, re)
        im = jnp.where(dc, ii, im)
    pr = jnp.concatenate([re, im], axis=0).astype(BF16)
    return jnp.dot(g_blk, pr, preferred_element_type=F32)


def _hy_prologue(u_ref, cw_ref, cb_ref, lseq, z_ref, zbf_ref, x2_ref):
    rin = _row_in_seq(u_ref.shape[0], lseq)
    uc = _dwconv(u_ref[...], cw_ref, cb_ref, rin, lseq)
    z = uc[:, :D_HY] * uc[:, D_HY:2 * D_HY]
    z_ref[...] = z
    zbf_ref[...] = z.astype(BF16)
    x2_ref[...] = uc[:, 2 * D_HY:]


def _hy_kernel(u_ref, cw_ref, cb_ref, hb_ref, fc_ref, gc_ref, hfc_ref, fl_ref, gl_ref, hfl_ref,
               o_ref, z_ref, zbf_ref, x2_ref, acc_ref):
    b = pl.program_id(0)
    k = pl.program_id(1)
    cw, cb = cw_ref.at[0], cb_ref.at[0]

    @pl.when(jnp.logical_and(b < NB_CTX, k == 0))
    def _():
        _hy_prologue(u_ref, cw, cb, L_CTX, z_ref, zbf_ref, x2_ref)
        for s in range(RB // L_CTX):
            rows = slice(s * L_CTX, (s + 1) * L_CTX)
            acc_ref[rows, :] = _hy_spectral_block(fc_ref[...], gc_ref[...], hfc_ref[0], zbf_ref[rows, :], True)

    @pl.when(b >= NB_CTX)
    def _():
        @pl.when(k == 0)
        def _():
            _hy_prologue(u_ref, cw, cb, L_LAT, z_ref, zbf_ref, x2_ref)
            acc_ref[...] = jnp.zeros_like(acc_ref)

        acc_ref[...] += _hy_spectral_block(fl_ref[...], gl_ref[...], hfl_ref[0], zbf_ref[...], k == 0)

    @pl.when(k == HY_NKB - 1)
    def _():
        o_ref[...] = x2_ref[...] * (acc_ref[...] + hb_ref[0] * z_ref[...])


def _hy_mixer(l, u_hy, p, c):
    lat_k = lambda b, k: jnp.where(b < NB_CTX, 0, k)
    lsel = lambda *shape: pl.BlockSpec((1,) + shape, lambda b, k: (l,) + (0,) * len(shape))
    kbl = 2 * HY_KB_LAT
    return pl.pallas_call(
        _hy_kernel,
        grid=(NB, HY_NKB),
        in_specs=[
            pl.BlockSpec((RB, 3 * D_HY), lambda b, k: (b, 0)),
            lsel(3, 3 * D_HY), lsel(1, 3 * D_HY), lsel(1, D_HY),
            pl.BlockSpec((N_FFT_CTX, L_CTX), lambda b, k: (0, 0)),
            pl.BlockSpec((L_CTX, N_FFT_CTX), lambda b, k: (0, 0)),
            lsel(N_FFT_CTX, D_HY),
            pl.BlockSpec((kbl, L_LAT), lambda b, k: (lat_k(b, k), 0)),
            pl.BlockSpec((L_LAT, kbl), lambda b, k: (0, lat_k(b, k))),
            pl.BlockSpec((1, kbl, D_HY), lambda b, k: (l, lat_k(b, k), 0)),
        ],
        out_specs=pl.BlockSpec((RB, D_HY), lambda b, k: (b, 0)),
        out_shape=jax.ShapeDtypeStruct((T, D_HY), F32),
        scratch_shapes=[pltpu.VMEM((RB, D_HY), F32), pltpu.VMEM((RB, D_HY), BF16),
                        pltpu.VMEM((RB, D_HY), F32), pltpu.VMEM((RB, D_HY), F32)],
        compiler_params=_cparams(2),
        name=f"hyena_mixer_{l}",
    )(u_hy, p["hy_conv_w"], p["hy_conv_b"], p["hy_bias"],
      c["f_ctx"], c["g_ctx"], c["hf_ctx"], c["f_lat"], c["g_lat"], c["hf_lat"])


def _ssd_block(u_z, u_xbc, u_dt, u_dtT, cw, cb, dtb_row, dtb_col, alog_row, alog_col, d_exp, nw,
               e32_ref, mbd_ref, mdiag_ref, h0_ref, o_ref, st_ref,
               x_s, bc_s, y_s, s_s, lseq):
    rows = u_z.shape[0]
    nchunk = rows // SSM_Q
    cps = lseq // SSM_Q
    rin = _row_in_seq(rows, lseq)
    for c0 in range(0, D_XBC, LANE):
        cols = slice(c0, c0 + LANE)
        xbc = _dwconv(u_xbc[:, cols], cw.at[:, cols], cb.at[:, cols], rin, lseq)
        xbc = xbc * jax.nn.sigmoid(xbc)
        if c0 < D_SSM:
            x_s[:, cols] = xbc
        else:
            bc_s[:, c0 - D_SSM:c0 - D_SSM + LANE] = xbc

    li = lax.broadcasted_iota(I32, (SSM_Q, SSM_Q), 0)
    si = lax.broadcasted_iota(I32, (SSM_Q, SSM_Q), 1)
    low_half = lax.broadcasted_iota(I32, (SSM_Q, LANE), 1) < SSM_N
    hq = SSM_H // SSM_G
    wq = hq * SSM_P
    hp = SSM_H * SSM_P

    for d in range(2):
        causal = (li >= si) if d == 0 else (li <= si)
        tri_col = causal.astype(BF16)
        tri_row = ((li <= si) if d == 0 else (li >= si)).astype(BF16)
        a_row = -jnp.exp(alog_row[:, SSM_H * d:SSM_H * (d + 1)])
        a_col = -jnp.exp(alog_col[SSM_H * d:SSM_H * (d + 1), :])
        b_row = dtb_row[:, SSM_H * d:SSM_H * (d + 1)]
        b_col = dtb_col[SSM_H * d:SSM_H * (d + 1), :]
        edge = SSM_Q - 1 if d == 0 else 0

        for g in range(SSM_G):
            if h0_ref is not None:
                s_s[g] = jnp.tile(h0_ref[d, g * wq:(g + 1) * wq, :], (1, hq))
            else:
                s_s[g] = jnp.zeros((wq, wq), F32)

        def chunk(ci, carry, d=d, causal=causal, tri_col=tri_col, tri_row=tri_row, a_row=a_row,
                  a_col=a_col, b_row=b_row, b_col=b_col, edge=edge):
            c = ci if d == 0 else nchunk - 1 - ci
            r0 = pl.multiple_of(c * SSM_Q, SSM_Q)
            rsl = pl.ds(r0, SSM_Q)
            if h0_ref is None and cps < nchunk:
                first = (c % cps == 0) if d == 0 else (c % cps == cps - 1)
                s_s[...] = s_s[...] * jnp.where(first, 0.0, 1.0)

            dt_c = jax.nn.softplus(u_dt[rsl, SSM_H * d:SSM_H * (d + 1)] + b_row)
            dt_r = jax.nn.softplus(u_dtT[c, SSM_H * d:SSM_H * (d + 1), :] + b_col)
            csc = jnp.dot(tri_col, jnp.concatenate(_split_bf16(dt_c * a_row, 3), axis=1),
                          preferred_element_type=F32)
            cs_col = csc[:, :SSM_H] + csc[:, SSM_H:2 * SSM_H] + csc[:, 2 * SSM_H:]
            cs_row = sum(jnp.dot(part, tri_row, preferred_element_type=F32)
                         for part in _split_bf16(dt_r * a_col, 3))
            tot = cs_col[edge:edge + 1, :]

            hi, lo = _split_bf16(jnp.concatenate([jnp.exp(cs_col), dt_c * jnp.exp(tot - cs_col)], axis=1), 2)
            spread = jnp.dot(jnp.concatenate([hi, lo], axis=1), e32_ref[...], preferred_element_type=F32)
            ecs_x, wdec_x = spread[:, :hp], spread[:, hp:]
            etot_x = ecs_x[edge:edge + 1, :]

            bcm = bc_s[rsl, :]
            bm, cm = bcm[:, :LANE], bcm[:, LANE:]
            bm_r, cm_r = pltpu.roll(bm, SSM_N, axis=1), pltpu.roll(cm, SSM_N, axis=1)
            bmb, cmb = bm.astype(BF16), cm.astype(BF16)
            for g in range(SSM_G):
                gl = slice(g * SSM_N, (g + 1) * SSM_N)
                ql = slice(g * wq, (g + 1) * wq)
                same = low_half if g == 0 else jnp.logical_not(low_half)
                b2 = jnp.where(same, bm, bm_r)
                c2 = jnp.where(same, cm, cm_r)
                gmat = lax.dot_general(cmb[:, gl], bmb[:, gl], (((1,), (1,)), ((), ())),
                                       preferred_element_type=F32)
                sc = []
                for h in range(g * hq, (g + 1) * hq):
                    diff = cs_col[:, h:h + 1] - cs_row[h:h + 1, :]
                    decay = jnp.exp(jnp.where(causal, diff, -1e30))
                    sc.append((gmat * decay * dt_r[h:h + 1, :]).astype(BF16))
                sc = jnp.concatenate(sc, axis=1)
                xq = x_s[rsl, ql]
                bd = jnp.tile(xq.astype(BF16), (hq, 1)) * mbd_ref[...]
                y = jnp.dot(sc, bd, preferred_element_type=F32)
                s_old = s_s[g]
                y_off = lax.dot_general(jnp.concatenate([c2, c2], axis=1).astype(BF16),
                                        (s_old * mdiag_ref[...]).astype(BF16),
                                        (((1,), (1,)), ((), ())), preferred_element_type=F32)
                y = y + y_off * ecs_x[:, ql]
                if d == 0:
                    y_s[rsl, ql] = y
                else:
                    y_s[rsl, ql] += y
                bx = (jnp.concatenate([b2, b2], axis=1) * wdec_x[:, ql]).astype(BF16)
                s_new = jnp.dot(xq.T.astype(BF16), bx, preferred_element_type=F32)
                s_s[g] = s_old * etot_x[:, ql] + s_new

            if st_ref is not None:
                last = (c % cps == cps - 1) if d == 0 else (c % cps == 0)

                @pl.when(last)
                def _():
                    pieces = []
                    for g in range(SSM_G):
                        sd = s_s[g] * mdiag_ref[...]
                        fold = sd[:, :LANE] + sd[:, LANE:]
                        for jp in range(hq // 2):
                            r = 2 * jp * SSM_P
                            pieces.append(fold[r:r + SSM_P] + fold[r + SSM_P:r + 2 * SSM_P])
                    st_ref[c // cps, d] = jnp.concatenate(pieces, axis=0)
            return carry

        lax.fori_loop(0, nchunk, chunk, 0)

    for r0 in range(0, rows, 256):
        rsl = slice(r0, r0 + 256)
        y = y_s[rsl, :] + d_exp[...] * x_s[rsl, :]
        z = u_z[rsl, :]
        y = y * (z * jax.nn.sigmoid(z))
        o_ref[rsl, :] = _rms(y) * nw[...]


def _ssd_kernel(u_z, u_xbc, u_dt, u_dtT, cw, cb, dtb_row, dtb_col, alog_row, alog_col, d_exp, nw,
                e32_ref, mbd_ref, mdiag_ref, h0_ref, o_ref, st_ref, x_s, bc_s, y_s, s_s):
    b = pl.program_id(0)
    args = (u_z, u_xbc, u_dt, u_dtT, cw.at[0], cb.at[0], dtb_row.at[0], dtb_col.at[0], alog_row.at[0],
            alog_col.at[0], d_exp.at[0], nw.at[0], e32_ref, mbd_ref, mdiag_ref)
    scr = (x_s, bc_s, y_s, s_s)

    @pl.when(b < NB_CTX)
    def _():
        _ssd_block(*args, None, o_ref, st_ref, *scr, L_CTX)

    @pl.when(b >= NB_CTX)
    def _():
        _ssd_block(*args, h0_ref.at[0, 0], o_ref, None, *scr, L_LAT)


def _ssd_mixer(l, u_z, u_xbc, u_dt, u_dtT, p, c, state_ssm):
    lsel = lambda *shape: pl.BlockSpec((1,) + shape, lambda b: (l,) + (0,) * len(shape))
    full = lambda a: pl.BlockSpec(a.shape, lambda b: (0,) * a.ndim, pipeline_mode=pl.Buffered(1))
    hp = SSM_H * SSM_P
    nseq_blk = RB // L_CTX
    return pl.pallas_call(
        _ssd_kernel,
        grid=(NB,),
        in_specs=[
            pl.BlockSpec((RB, D_SSM), lambda b: (b, 0), pipeline_mode=pl.Buffered(1)),
            pl.BlockSpec((RB, D_XBC), lambda b: (b, 0)),
            pl.BlockSpec((RB, 2 * SSM_H), lambda b: (b, 0)),
            pl.BlockSpec((RB // SSM_Q, 2 * SSM_H, SSM_Q), lambda b: (b, 0, 0)),
            lsel(4, D_XBC), lsel(1, D_XBC), lsel(1, 2 * SSM_H), lsel(2 * SSM_H, 1),
            lsel(1, 2 * SSM_H), lsel(2 * SSM_H, 1), lsel(1, D_SSM), lsel(1, D_SSM),
            full(c["e32"]), full(c["mbd"]), full(c["mdiag"]),
            pl.BlockSpec((1, 1, 2, hp, SSM_N), lambda b: (jnp.maximum(b - NB_CTX, 0), l, 0, 0, 0)),
        ],
        out_specs=[pl.BlockSpec((RB, D_SSM), lambda b: (b, 0)),
                   pl.BlockSpec((nseq_blk, 2, hp // 2, LANE), lambda b: (jnp.minimum(b, NB_CTX - 1), 0, 0, 0))],
        out_shape=[jax.ShapeDtypeStruct((T, D_SSM), F32),
                   jax.ShapeDtypeStruct((N_CTX_SEQ, 2, hp // 2, LANE), F32)],
        scratch_shapes=[pltpu.VMEM((RB, D_SSM), F32), pltpu.VMEM((RB, 2 * SSM_G * SSM_N), F32),
                        pltpu.VMEM((RB, D_SSM), F32),
                        pltpu.VMEM((SSM_G, hp // SSM_G, hp // SSM_G), F32)],
        compiler_params=_cparams(1),
        name=f"ssd_mixer_{l}",
    )(u_z, u_xbc, u_dt, u_dtT, p["ssm_conv_w"], p["ssm_conv_b"], p["ssm_dtb_row"], p["ssm_dtb_col"],
      p["ssm_alog_row"], p["ssm_alog_col"], p["ssm_d_exp"], p["ssm_norm_w"],
      c["e32"], c["mbd"], c["mdiag"], state_ssm)


def _k2_kernel(*refs, routed):
    if routed:
        (ol_ref, oh_ref, os_ref, x_ref, mod_ref, nw_ref, w_ref, rt_ref,
         xo_ref, h2_ref, route_ref, cnt_ref, wbf_ref) = refs
    else:
        ol_ref, oh_ref, os_ref, x_ref, mod_ref, nw_ref, w_ref, xo_ref, h2_ref, wbf_ref = refs

    @pl.when(pl.program_id(0) == 0)
    def _():
        wbf_ref[...] = w_ref[0].astype(BF16)

    m = mod_ref[0]
    o = jnp.dot(ol_ref[...].astype(BF16), wbf_ref[0:256, :], preferred_element_type=F32)
    o = o + jnp.dot(oh_ref[...].astype(BF16), wbf_ref[256:512, :], preferred_element_type=F32)
    o = o + jnp.dot(os_ref[...].astype(BF16), wbf_ref[512:1024, :], preferred_element_type=F32)
    x = x_ref[...] + m[2:3] * o
    xo_ref[...] = x
    h2 = _rms(x) * nw_ref[0]
    h2 = h2 * (1.0 + m[4:5]) + m[3:4]
    h2_ref[...] = h2.astype(BF16)

    if routed:
        logits = jnp.dot(h2, rt_ref[0], precision=HI, preferred_element_type=F32)
        eid = lax.broadcasted_iota(I32, logits.shape, 1)
        m1 = jnp.max(logits, axis=1, keepdims=True)
        i1 = jnp.min(jnp.where(logits == m1, eid, N_EXP), axis=1, keepdims=True)
        rest = jnp.where(eid == i1, -jnp.inf, logits)
        m2 = jnp.max(rest, axis=1, keepdims=True)
        i2 = jnp.min(jnp.where(rest == m2, eid, N_EXP), axis=1, keepdims=True)
        w1 = 1.0 / (1.0 + jnp.exp(m2 - m1))
        w2 = 1.0 - w1
        oh1 = (eid == i1).astype(F32)
        oh2 = (eid == i2).astype(F32)
        both = oh1 + oh2
        before = (lax.broadcasted_iota(I32, (TM, TM), 0) > lax.broadcasted_iota(I32, (TM, TM), 1))
        ahead = jnp.dot(before.astype(BF16), both.astype(BF16), preferred_element_type=F32)
        r1 = jnp.sum(oh1 * ahead, axis=1, keepdims=True)
        r2 = jnp.sum(oh2 * ahead, axis=1, keepdims=True)
        zero = jnp.zeros_like(w1)
        route_ref[...] = jnp.concatenate(
            [i1.astype(F32), i2.astype(F32), r1, r2, w1, w2, zero, zero], axis=1)
        cnt_ref[0] = jnp.sum(both, axis=0, keepdims=True)


def _k2(l, o_lru, o_hy, o_ssm, x, mod_l, norm2_w, w_out, router=None, j=0):
    routed = router is not None
    tok = lambda w: pl.BlockSpec((TM, w), lambda i: (i, 0))
    in_specs = [
        tok(D_LRU), tok(D_HY), tok(D_SSM), tok(D),
        pl.BlockSpec((1, 6, D), lambda i: (_mod_row(i), 0, 0)),
        pl.BlockSpec((1, 1, D), lambda i: (l, 0, 0)),
        pl.BlockSpec((1, D, D), lambda i: (l, 0, 0)),
    ]
    args = [o_lru, o_hy, o_ssm, x, mod_l, norm2_w.reshape(DEPTH, 1, D), w_out]
    out_specs = [tok(D), tok(D)]
    out_shape = [jax.ShapeDtypeStruct((T, D), F32), jax.ShapeDtypeStruct((T, D), BF16)]
    if routed:
        in_specs.append(pl.BlockSpec((1, D, N_EXP), lambda i: (j, 0, 0)))
        args.append(router)
        out_specs += [tok(8), pl.BlockSpec((1, 1, N_EXP), lambda i: (i, 0, 0))]
        out_shape += [jax.ShapeDtypeStruct((T, 8), F32), jax.ShapeDtypeStruct((NT, 1, N_EXP), F32)]
    return pl.pallas_call(
        functools.partial(_k2_kernel, routed=routed),
        grid=(NT,),
        in_specs=in_specs,
        out_specs=out_specs,
        out_shape=out_shape,
        scratch_shapes=[pltpu.VMEM((D, D), BF16)],
        compiler_params=_cparams(1),
        name=f"k2_outproj_{l}",
    )(*args)


def _ffn_stream_kernel(be_ref, bi_ref, na_ref, x_ref, *rest, j, dense):
    if dense:
        xres_ref, mod_ref, wg_hbm, wu_hbm, wd_hbm, o_ref, wg_s, wu_s, wd_s, stg_g, stg_u, stg_d, sem = rest
    else:
        wg_hbm, wu_hbm, wd_hbm, o_ref, wg_s, wu_s, wd_s, stg_g, stg_u, stg_d, sem = rest
    del bi_ref
    b = pl.program_id(0)
    n_act = na_ref[0]
    e = be_ref[b]
    active = b < n_act
    load = jnp.logical_and(active, jnp.logical_or(b == 0, be_ref[jnp.maximum(b - 1, 0)] != e))
    e_next = be_ref[jnp.minimum(b + 1, pl.num_programs(0) - 1)]
    feed_next = jnp.logical_and(b + 1 < n_act, e_next != e)

    def copies(ee, c):
        slot = c % FFN_STAGES
        cols = slice(c * FF_CHUNK, (c + 1) * FF_CHUNK)
        return (pltpu.make_async_copy(wg_hbm.at[j, ee, :, cols], stg_g.at[slot], sem.at[0, slot]),
                pltpu.make_async_copy(wu_hbm.at[j, ee, :, cols], stg_u.at[slot], sem.at[1, slot]),
                pltpu.make_async_copy(wd_hbm.at[j, ee, cols, :], stg_d.at[slot], sem.at[2, slot]))

    def start(ee, c):
        for cp in copies(ee, c):
            cp.start()

    def chunk_out(c, x, acc):
        g = jnp.dot(x, wg_s[c], preferred_element_type=F32)
        u = jnp.dot(x, wu_s[c], preferred_element_type=F32)
        hmid = (g * jax.nn.sigmoid(g) * u).astype(BF16)
        part = jnp.dot(hmid, wd_s[c], preferred_element_type=F32)
        return part if acc is None else acc + part

    def finish(acc):
        if dense:
            o_ref[...] = xres_ref[...] + mod_ref[0][5:6] * acc
        else:
            o_ref[...] = acc.astype(o_ref.dtype)

    @pl.when(load)
    def _():
        @pl.when(b == 0)
        def _():
            for c in range(FFN_STAGES):
                start(e, c)

        x = x_ref[...]
        acc = None
        for c in range(N_FF_CHUNK):
            slot = c % FFN_STAGES
            for cp in copies(e, c):
                cp.wait()
            wg_s[c] = stg_g[slot].astype(BF16)
            wu_s[c] = stg_u[slot].astype(BF16)
            wd_s[c] = stg_d[slot].astype(BF16)
            if c + FFN_STAGES < N_FF_CHUNK:
                start(e, c + FFN_STAGES)
            acc = chunk_out(c, x, acc)
        finish(acc)

    @pl.when(jnp.logical_and(active, jnp.logical_not(load)))
    def _():
        x = x_ref[...]
        acc = None
        for c in range(N_FF_CHUNK):
            acc = chunk_out(c, x, acc)
        finish(acc)

    if not dense:
        @pl.when(jnp.logical_not(active))
        def _():
            o_ref[...] = jnp.zeros_like(o_ref)

    @pl.when(feed_next)
    def _():
        for c in range(FFN_STAGES):
            start(e_next, c)


def _ffn_scratch():
    return [pltpu.VMEM((N_FF_CHUNK, D, FF_CHUNK), BF16), pltpu.VMEM((N_FF_CHUNK, D, FF_CHUNK), BF16),
            pltpu.VMEM((N_FF_CHUNK, FF_CHUNK, D), BF16),
            pltpu.VMEM((FFN_STAGES, D, FF_CHUNK), F32), pltpu.VMEM((FFN_STAGES, D, FF_CHUNK), F32),
            pltpu.VMEM((FFN_STAGES, FF_CHUNK, D), F32), pltpu.SemaphoreType.DMA((3, FFN_STAGES))]


def _dense_ffn(j, h2, x, mod_l, wg, wu, wd):
    hbm = pl.BlockSpec(memory_space=pl.ANY)
    zeros = jnp.zeros((NT,), I32)
    return pl.pallas_call(
        functools.partial(_ffn_stream_kernel, j=j, dense=True),
        grid_spec=pltpu.PrefetchScalarGridSpec(
            num_scalar_prefetch=3,
            grid=(NT,),
            in_specs=[
                pl.BlockSpec((TM, D), lambda i, *_: (i, 0)),
                pl.BlockSpec((TM, D), lambda i, *_: (i, 0)),
                pl.BlockSpec((1, 6, D), lambda i, *_: (_mod_row(i), 0, 0)),
                hbm, hbm, hbm,
            ],
            out_specs=pl.BlockSpec((TM, D), lambda i, *_: (i, 0)),
            scratch_shapes=_ffn_scratch(),
        ),
        out_shape=jax.ShapeDtypeStruct((T, D), F32),
        compiler_params=_cparams(1),
        name=f"dense_ffn_{j}",
    )(zeros, zeros, jnp.full((1,), NT, I32), h2, x, mod_l, wg[:, None], wu[:, None], wd[:, None])


def _expert_ffn(j, xs, blk_e, blk_i, n_active, wg, wu, wd):
    hbm = pl.BlockSpec(memory_space=pl.ANY)
    return pl.pallas_call(
        functools.partial(_ffn_stream_kernel, j=j, dense=False),
        grid_spec=pltpu.PrefetchScalarGridSpec(
            num_scalar_prefetch=3,
            grid=(NBLK,),
            in_specs=[pl.BlockSpec((TMB, D), lambda b, be, bi, na: (bi[b], 0)), hbm, hbm, hbm],
            out_specs=pl.BlockSpec((TMB, D), lambda b, be, bi, na: (b, 0)),
            scratch_shapes=_ffn_scratch(),
        ),
        out_shape=jax.ShapeDtypeStruct((RMAX, D), BF16),
        compiler_params=_cparams(1),
        name=f"expert_ffn_{j}",
    )(blk_e, blk_i, n_active, xs, wg, wu, wd)


def _chunk_rows(idx):
    return pl.ds(pl.multiple_of(idx * GCH, GCH), GCH)


def _sort_kernel(gch_ref, nused_ref, pad0_ref, npad_ref, h_ref, dl_ref, xs_ref, xl_ref, zero_ref, sem):
    i = pl.program_id(0)
    slot = i % 2

    def copy(tile, q):
        s = tile % 2
        return pltpu.make_async_copy(xl_ref.at[s, _chunk_rows(q)],
                                     xs_ref.at[_chunk_rows(gch_ref[tile * NLC + q])], sem.at[s])

    def wait_tile(tile):
        lax.fori_loop(0, nused_ref[tile], lambda q, c: (copy(tile, q).wait(), c)[1], 0)

    @pl.when(i >= 2)
    def _():
        wait_tile(i - 2)

    dl = dl_ref[0]
    r = lax.broadcasted_iota(I32, (LCAP, TM), 0).astype(F32)
    perm = jnp.logical_or(r == dl[0:1, :], r == dl[1:2, :]).astype(BF16)
    xl_ref[slot] = jnp.dot(perm, h_ref[...], preferred_element_type=F32).astype(BF16)
    lax.fori_loop(0, nused_ref[i], lambda q, c: (copy(i, q).start(), c)[1], 0)

    @pl.when(i == NT - 1)
    def _():
        zero_ref[...] = jnp.zeros_like(zero_ref)
        for e in range(N_EXP + 1):
            def zcopy(q, e=e):
                return pltpu.make_async_copy(zero_ref, xs_ref.at[_chunk_rows(pad0_ref[e] + q)], sem.at[2])
            lax.fori_loop(0, npad_ref[e], lambda q, c, f=zcopy: (f(q).start(), c)[1], 0)
            lax.fori_loop(0, npad_ref[e], lambda q, c, f=zcopy: (f(q).wait(), c)[1], 0)
        wait_tile(i - 1)
        wait_tile(i)


def _sort_tokens(h2, dl_row, gch, nused, pad0, npad):
    return pl.pallas_call(
        _sort_kernel,
        grid_spec=pltpu.PrefetchScalarGridSpec(
            num_scalar_prefetch=4,
            grid=(NT,),
            in_specs=[
                pl.BlockSpec((TM, D), lambda i, *_: (i, 0)),
                pl.BlockSpec((1, 2, TM), lambda i, *_: (i, 0, 0)),
            ],
            out_specs=pl.BlockSpec(memory_space=pl.ANY),
            scratch_shapes=[pltpu.VMEM((2, LCAP, D), BF16), pltpu.VMEM((GCH, D), BF16),
                            pltpu.SemaphoreType.DMA((3,))],
        ),
        out_shape=jax.ShapeDtypeStruct((RMAX, D), BF16),
        compiler_params=_cparams(1),
        name="moe_sort",
    )(gch, nused, pad0, npad, h2, dl_row)


def _combine_kernel(gch_ref, nused_ref, y_ref, info_ref, x_ref, mod_ref, fw_ref, *rest, final):
    if final:
        oc_ref, ol_ref, yl_ref, sem = rest
    else:
        o_ref, yl_ref, sem = rest
    i = pl.program_id(0)
    slot = i % 2

    def copy(tile, q):
        s = tile % 2
        return pltpu.make_async_copy(y_ref.at[_chunk_rows(gch_ref[tile * NLC + q])],
                                     yl_ref.at[s, _chunk_rows(q)], sem.at[s])

    def fetch(tile):
        lax.fori_loop(0, nused_ref[tile], lambda q, c: (copy(tile, q).start(), c)[1], 0)

    @pl.when(i == 0)
    def _():
        fetch(i)

    @pl.when(i + 1 < NT)
    def _():
        fetch(i + 1)

    n = nused_ref[i]

    def clear(q, c):
        yl_ref[slot, _chunk_rows(q), :] = jnp.zeros((GCH, D), BF16)
        return c

    lax.fori_loop(n, NLC, clear, 0)
    lax.fori_loop(0, n, lambda q, c: (copy(i, q).wait(), c)[1], 0)

    info = info_ref[...]
    col = lax.broadcasted_iota(I32, (TM, LCAP), 1).astype(F32)
    yl = yl_ref[slot]
    y1 = jnp.dot((col == info[:, 0:1]).astype(BF16), yl, preferred_element_type=F32)
    y2 = jnp.dot((col == info[:, 1:2]).astype(BF16), yl, preferred_element_type=F32)
    x = x_ref[...] + mod_ref[0][5:6] * (info[:, 2:3] * y1 + info[:, 3:4] * y2)
    if final:
        x = _rms(x) * fw_ref[...]

        @pl.when(i < NT_CTX)
        def _():
            oc_ref[...] = x

        @pl.when(i >= NT_CTX)
        def _():
            ol_ref[...] = x
    else:
        o_ref[...] = x


def _combine(y, info, x, mod_l, final_w, gch, nused, final):
    tile = pl.BlockSpec((TM, D), lambda i, *_: (i, 0))
    if final:
        out_specs = [pl.BlockSpec((TM, D), lambda i, *_: (jnp.minimum(i, NT_CTX - 1), 0)),
                     pl.BlockSpec((TM, D), lambda i, *_: (jnp.maximum(i - NT_CTX, 0), 0))]
        out_shape = [jax.ShapeDtypeStruct((T_CTX, D), F32), jax.ShapeDtypeStruct((T_LAT, D), F32)]
    else:
        out_specs, out_shape = tile, jax.ShapeDtypeStruct((T, D), F32)
    return pl.pallas_call(
        functools.partial(_combine_kernel, final=final),
        grid_spec=pltpu.PrefetchScalarGridSpec(
            num_scalar_prefetch=2,
            grid=(NT,),
            in_specs=[
                pl.BlockSpec(memory_space=pl.ANY),
                pl.BlockSpec((TM, 4), lambda i, *_: (i, 0)),
                tile,
                pl.BlockSpec((1, 6, D), lambda i, *_: (_mod_row(i), 0, 0)),
                pl.BlockSpec((1, D), lambda i, *_: (0, 0)),
            ],
            out_specs=out_specs,
            scratch_shapes=[pltpu.VMEM((2, LCAP, D), BF16), pltpu.SemaphoreType.DMA((2,))],
        ),
        out_shape=out_shape,
        compiler_params=_cparams(1),
        name="moe_combine",
    )(gch, nused, y, info, x, mod_l, final_w.reshape(1, D))


def _moe_plan(route, counts):
    cnt = counts.reshape(NT, N_EXP).astype(I32)
    cpad = (cnt + GCH - 1) // GCH * GCH
    lo = jnp.cumsum(cpad, axis=1) - cpad
    nused = (lo[:, -1] + cpad[:, -1]) // GCH
    tot = jnp.sum(cpad, axis=0)
    gpad = (tot + TMB - 1) // TMB * TMB
    goff = jnp.cumsum(gpad) - gpad
    so = goff[None, :] + jnp.cumsum(cpad, axis=0) - cpad

    e1 = route[:, 0].astype(I32)
    e2 = route[:, 1].astype(I32)
    eid = jnp.arange(N_EXP, dtype=I32)[None, :]
    lo_tok = jnp.repeat(lo, TM, axis=0)
    dl1 = jnp.sum(jnp.where(e1[:, None] == eid, lo_tok, 0), axis=1).astype(F32) + route[:, 2]
    dl2 = jnp.sum(jnp.where(e2[:, None] == eid, lo_tok, 0), axis=1).astype(F32) + route[:, 3]
    dl_row = jnp.stack([dl1.reshape(NT, TM), dl2.reshape(NT, TM)], axis=1)
    info = jnp.stack([dl1, dl2, route[:, 4], route[:, 5]], axis=1)

    q = jnp.arange(NLC, dtype=I32)[None, :, None]
    lo16 = (lo // GCH)[:, None, :]
    c16 = (cpad // GCH)[:, None, :]
    in_seg = jnp.logical_and(q >= lo16, q < lo16 + c16)
    gch = jnp.sum(jnp.where(in_seg, (so // GCH)[:, None, :] + q - lo16, 0), axis=2).reshape(NT * NLC)

    nblk = gpad // TMB
    n_active = jnp.sum(nblk)
    b = jnp.arange(NBLK, dtype=I32)
    blk_i = jnp.minimum(b, n_active - 1)
    bend = (goff + gpad) // TMB
    blk_e = jnp.minimum(jnp.sum((blk_i[:, None] >= bend[None, :]).astype(I32), axis=1), N_EXP - 1)
    used = n_active * TMB
    pad0 = jnp.concatenate([goff + tot, used.reshape(1)]) // GCH
    npad = jnp.concatenate([gpad - tot, (RMAX - used).reshape(1)]) // GCH
    return dict(dl_row=dl_row, info=info, gch=gch.astype(I32), nused=nused.astype(I32),
                blk_e=blk_e.astype(I32), blk_i=blk_i.astype(I32),
                n_active=n_active.reshape(1).astype(I32), pad0=pad0.astype(I32), npad=npad.astype(I32))


def _grid_pos_embed(n_tokens):
    rows = n_tokens // GRID_W
    r = jnp.repeat(jnp.arange(rows, dtype=F32), GRID_W)
    col = jnp.tile(jnp.arange(GRID_W, dtype=F32), rows)
    quarter = D // 4
    omega = 1.0 / (10000.0 ** (jnp.arange(quarter, dtype=F32) / quarter))
    ang_r = r[:, None] * omega[None]
    ang_c = col[:, None] * omega[None]
    return jnp.concatenate([jnp.sin(ang_r), jnp.cos(ang_r), jnp.sin(ang_c), jnp.cos(ang_c)], axis=-1)


def _hy_pos_features(n):
    pos = jnp.arange(n, dtype=F32)
    t = pos / (n - 1)
    bands = jnp.linspace(1e-4, HY_BANDS - 1, HY_BANDS, dtype=F32)
    ang = (2.0 * math.pi * pos / n)[:, None] * bands[None]
    z = jnp.concatenate([t[:, None], jnp.cos(ang), -jnp.sin(ang)], axis=-1)
    z = jnp.pad(z, ((0, 0), (0, LANE - HY_POS_DIM)))
    half = n // 2
    dist = jnp.abs(pos - half) / half
    deltas = jnp.abs(jnp.linspace(HY_MIN_DECAY, HY_MAX_DECAY, D_HY, dtype=F32))
    return z, jnp.exp(-dist[:, None] * deltas[None])


def _dft_mats(lseq, n, kb):
    nkb = n // 2 // kb
    t = jnp.arange(lseq, dtype=I32)

    def tables(tt):
        def cs(freq):
            ang = (2.0 * math.pi / n) * ((freq[:, None] * tt[None, :]) % n).astype(F32)
            return jnp.cos(ang), jnp.sin(ang)
        (ca, sa), (cb, sb) = cs(jnp.arange(nkb, dtype=I32) * kb), cs(jnp.arange(kb, dtype=I32))
        ca, sa, cb, sb = ca[:, None, :], sa[:, None, :], cb[None], sb[None]
        re = ca * cb - sa * sb
        im = -(sa * cb + ca * sb)
        dc = jnp.logical_and(jnp.arange(nkb)[:, None, None] == 0, jnp.arange(kb)[None, :, None] == 0)
        alt = (1.0 - 2.0 * (tt % 2).astype(F32))[None, None, :]
        return re, jnp.where(dc, alt, im), dc

    re, im, _ = tables(t)
    f = jnp.concatenate([re, im], axis=1).reshape(n, lseq)
    re, im, dc = tables(t + lseq // 2)
    wk = jnp.where(dc, 1.0, 2.0) / n
    gt = jnp.concatenate([wk * re, jnp.where(dc, 1.0 / n, wk) * im], axis=1).reshape(n, lseq)
    return f.astype(BF16), gt.T.astype(BF16)


def _ssd_constants():
    hp = SSM_H * SSM_P
    hq = SSM_H // SSM_G
    head_of = np.arange(hp) // SSM_P
    spread = (np.arange(SSM_H)[:, None] == head_of[None, :]).astype(np.float32)
    zero = np.zeros_like(spread)
    e32 = np.block([[spread, zero], [zero, spread], [spread, zero], [zero, spread]])
    hq_of = np.arange(hq * SSM_P) // SSM_P
    mbd = (np.arange(hq * SSM_Q)[:, None] // SSM_Q == hq_of[None, :]).astype(np.float32)
    mdiag = (hq_of[:, None] == hq_of[None, :]).astype(np.float32)
    return dict(e32=jnp.asarray(e32, dtype=BF16), mbd=jnp.asarray(mbd, dtype=BF16), mdiag=jnp.asarray(mdiag))


def _unpack_ssm_state(s):
    s = s.reshape(N_CTX_SEQ, 2, SSM_H // 2, SSM_P, 2, SSM_N)
    return jnp.transpose(s, (0, 1, 2, 4, 3, 5)).reshape(N_CTX_SEQ, 2, SSM_H, SSM_P, SSM_N)


def _block_diag_heads(w):
    eye = jnp.eye(LRU_HEADS, dtype=w.dtype)
    return jnp.einsum("ldhij,hg->ldhigj", w, eye).reshape(DEPTH, 2, D_LRU, D_LRU)


def kernel(x_prompt, x_sample, state_lru, state_ssm, c, c_ctx, norm1_w, norm2_w, final_norm_w, ada_w, ada_b,
           w_in, w_out, lru_conv_w, lru_conv_b, lru_wa, lru_ba, lru_wi, lru_bi, lru_lambda, hy_conv_w, hy_conv_b,
           hy_w1, hy_b1, hy_w2, hy_b2, hy_freq, hy_w3, hy_bias, ssm_conv_w, ssm_conv_b, ssm_dt_bias, ssm_a_log,
           ssm_d, ssm_norm_w, ffn_w_gate, ffn_w_up, ffn_w_down, moe_router, moe_w_gate, moe_w_up, moe_w_down):
    hp = SSM_H * SSM_P
    wa, wi = _block_diag_heads(lru_wa), _block_diag_heads(lru_wi)
    row = lambda a: a.reshape(DEPTH, 1, -1)
    p = {
        "lru_conv_w": lru_conv_w, "lru_conv_b": row(lru_conv_b), "lru_lambda": lru_lambda,
        "lru_wbig": jnp.concatenate([wa[:, 0], wi[:, 0], wa[:, 1], wi[:, 1]], axis=-1).astype(BF16),
        "lru_bias": jnp.concatenate([lru_ba[:, 0], lru_bi[:, 0], lru_ba[:, 1], lru_bi[:, 1]], axis=-1)[:, None],
        "hy_conv_w": hy_conv_w, "hy_conv_b": row(hy_conv_b), "hy_bias": row(hy_bias),
        "hy_w1p": jnp.pad(hy_w1, ((0, 0), (0, LANE - HY_POS_DIM), (0, 0))), "hy_b1": row(hy_b1),
        "hy_w2": hy_w2, "hy_b2": row(hy_b2), "hy_freq": row(hy_freq), "hy_w3": hy_w3,
        "ssm_conv_w": ssm_conv_w, "ssm_conv_b": row(ssm_conv_b),
        "ssm_dtb_row": row(ssm_dt_bias), "ssm_dtb_col": ssm_dt_bias.reshape(DEPTH, 2 * SSM_H, 1),
        "ssm_alog_row": row(ssm_a_log), "ssm_alog_col": ssm_a_log.reshape(DEPTH, 2 * SSM_H, 1),
        "ssm_d_exp": jnp.repeat(ssm_d, SSM_P, axis=-1)[:, None], "ssm_norm_w": row(ssm_norm_w),
    }
    w_dtT = jnp.swapaxes(w_in[:, :, D_MAIN:], 1, 2)
    ffn_w = (ffn_w_gate, ffn_w_up, ffn_w_down)
    moe_w = (moe_w_gate, moe_w_up, moe_w_down)

    cst = _ssd_constants()
    z_ctx, win_ctx = _hy_pos_features(L_CTX)
    z_lat, win_lat = _hy_pos_features(L_LAT)
    cst["f_ctx"], cst["g_ctx"] = _dft_mats(L_CTX, N_FFT_CTX, N_FFT_CTX // 2)
    cst["f_lat"], cst["g_lat"] = _dft_mats(L_LAT, N_FFT_LAT, HY_KB_LAT)
    cst["hf_ctx"] = _hy_filter_spectrum(L_CTX, N_FFT_CTX // 2, z_ctx, win_ctx, cst["f_ctx"], p)
    cst["hf_lat"] = _hy_filter_spectrum(L_LAT, HY_KB_LAT, z_lat, win_lat, cst["f_lat"], p)

    cond = jnp.concatenate([c_ctx[None], c], axis=0)
    mod = _mod_table(jnp.broadcast_to(cond[:, :, None], (3, D, LANE)), ada_w, ada_b)
    mod = mod[:, :3].reshape(DEPTH, 3, 6, D)

    x = jnp.concatenate([x_prompt.reshape(T_CTX, D),
                         (x_sample + _grid_pos_embed(L_LAT)[None]).reshape(T_LAT, D)], axis=0)
    st_ssm_in = state_ssm.reshape(N_LAT_SEQ, DEPTH, 2, hp, SSM_N)

    lru_states, ssm_states = [], []
    for l in range(DEPTH):
        u_lru, u_hy, u_z, u_xbc, u_dt, u_dtT = _k1(l, x, mod[l], norm1_w, w_in, w_dtT)
        o_lru, s_lru = _lru_mixer(l, u_lru, p, state_lru)
        o_hy = _hy_mixer(l, u_hy, p, cst)
        o_ssm, s_ssm = _ssd_mixer(l, u_z, u_xbc, u_dt, u_dtT, p, cst, st_ssm_in)
        lru_states.append(s_lru.reshape(N_CTX_SEQ, 2, D_LRU))
        ssm_states.append(_unpack_ssm_state(s_ssm))
        j = l // 2
        if l % 2 == 0:
            x, h2 = _k2(l, o_lru, o_hy, o_ssm, x, mod[l], norm2_w, w_out)
            x = _dense_ffn(j, h2, x, mod[l], *ffn_w)
        else:
            x, h2, route, counts = _k2(l, o_lru, o_hy, o_ssm, x, mod[l], norm2_w, w_out, moe_router, j)
            plan = _moe_plan(route, counts)
            xs = _sort_tokens(h2, plan["dl_row"], plan["gch"], plan["nused"], plan["pad0"], plan["npad"])
            y = _expert_ffn(j, xs, plan["blk_e"], plan["blk_i"], plan["n_active"], *moe_w)
            x = _combine(y, plan["info"], x, mod[l], final_norm_w, plan["gch"], plan["nused"],
                         final=(l == DEPTH - 1))
    y_prompt = x[0].reshape(N_CTX_SEQ, L_CTX, D)
    y_sample = x[1].reshape(N_LAT_SEQ, L_LAT, D)
    return (y_prompt, y_sample, jnp.stack(lru_states, axis=1), jnp.stack(ssm_states, axis=1))
```

```python
import functools
import math

import numpy as np
import jax
import jax.numpy as jnp
from jax import lax
from jax.experimental import pallas as pl
from jax.experimental.pallas import tpu as pltpu

F32 = jnp.float32
BF16 = jnp.bfloat16
I32 = jnp.int32
HI = lax.Precision.HIGHEST

D = 1024
N_CTX_SEQ, L_CTX = 16, 256
N_LAT_SEQ, L_LAT = 2, 2048
DEPTH = 4
GRID_W = 64
D_LRU = 256
LRU_HEADS, LRU_HD = 4, 64
LRU_C = 8.0
D_HY = 256
HY_BANDS = 16
HY_POS_DIM = 1 + 2 * HY_BANDS
HY_HID = 64
HY_MAX_DECAY = math.log(1e-2) / 0.3
HY_MIN_DECAY = math.log(1e-2) / 1.5
D_SSM = 512
SSM_P = 64
SSM_H = 8
SSM_G = 2
SSM_N = 64
SSM_Q = 128
D_XBC = D_SSM + 2 * SSM_G * SSM_N
D_MAIN = 2 * D_LRU + 3 * D_HY + D_SSM + D_XBC
D_IN = D_MAIN + 2 * SSM_H
D_FF = 2816
N_EXP = 8
EPS = 1e-6

LANE = 128
BF16_ROWS = 16
T_CTX = N_CTX_SEQ * L_CTX
T_LAT = N_LAT_SEQ * L_LAT
T = T_CTX + T_LAT
TM = 512
NT = T // TM
NT_CTX = T_CTX // TM
NT_PER_LAT = L_LAT // TM
RB = 2048
NB = T // RB
NB_CTX = T_CTX // RB
FF_CHUNK = 256
N_FF_CHUNK = D_FF // FF_CHUNK
FFN_STAGES = 3
VMEM_LIMIT = 56 * 1024 * 1024

GCH = BF16_ROWS
LCAP = 2 * TM + N_EXP * GCH
NLC = LCAP // GCH
TMB = 512
RMAX = -(-(2 * T + NT * N_EXP * (GCH - 1) + N_EXP * (TMB - 1)) // TMB) * TMB
NBLK = RMAX // TMB

HY_KB_LAT = 256
N_FFT_LAT = 3 * L_LAT // 2
N_FFT_CTX = 3 * L_CTX // 2
HY_NKB = N_FFT_LAT // 2 // HY_KB_LAT


def _cparams(n_axes=1, vmem=VMEM_LIMIT):
    return pltpu.CompilerParams(dimension_semantics=("arbitrary",) * n_axes, vmem_limit_bytes=vmem)


def _mod_row(i):
    return jnp.where(i < NT_CTX, 0, 1 + (i - NT_CTX) // NT_PER_LAT)


def _bdot(a, b):
    return jnp.dot(a.astype(BF16), b.astype(BF16), preferred_element_type=F32)


def _rms(x):
    return x * lax.rsqrt(jnp.mean(x * x, axis=-1, keepdims=True) + EPS)


def _split_bf16(v, parts):
    out = []
    for _ in range(parts):
        piece = v.astype(BF16)
        out.append(piece)
        v = v - piece.astype(F32)
    return out


def _mod_kernel(cb_ref, w_ref, b_ref, o_ref):
    tn = w_ref.shape[2]

    def body(kc, accs):
        k0 = pl.multiple_of(kc * 8, 8)
        wk = w_ref[0, pl.ds(k0, 8), :]
        out = []
        for r in range(3):
            c = cb_ref[r, pl.ds(k0, 8), :]
            c = c * jax.nn.sigmoid(c)
            out.append(accs[r] + jnp.tile(c, (1, tn // LANE)) * wk)
        return tuple(out)

    accs = lax.fori_loop(0, D // 8, body, tuple(jnp.zeros((8, tn), F32) for _ in range(3)), unroll=8)
    rows = [jnp.sum(a, axis=0, keepdims=True) + b_ref[0] for a in accs]
    o_ref[0] = jnp.concatenate(rows + [jnp.zeros((5, tn), F32)], axis=0)


def _mod_table(cond_b, ada_w, ada_b):
    tn = 1024
    return pl.pallas_call(
        _mod_kernel,
        grid=(DEPTH, 6 * D // tn),
        in_specs=[
            pl.BlockSpec((3, D, LANE), lambda l, j: (0, 0, 0)),
            pl.BlockSpec((1, D, tn), lambda l, j: (l, 0, j)),
            pl.BlockSpec((1, 1, tn), lambda l, j: (l, 0, j)),
        ],
        out_specs=pl.BlockSpec((1, 8, tn), lambda l, j: (l, 0, j)),
        out_shape=jax.ShapeDtypeStruct((DEPTH, 8, 6 * D), F32),
        compiler_params=_cparams(2),
        name="mod_table",
    )(cond_b, ada_w, ada_b.reshape(DEPTH, 1, 6 * D))


def _k1_kernel(x_ref, mod_ref, nw_ref, w_ref, wdtT_ref,
               o_lru, o_hy, o_z, o_xbc, o_dt, o_dtT, wbf_ref):
    @pl.when(pl.program_id(0) == 0)
    def _():
        wbf_ref[...] = w_ref[0].astype(BF16)

    m = mod_ref[0]
    h = _rms(x_ref[...]) * nw_ref[0]
    h = h * (1.0 + m[1:2]) + m[0:1]
    hb = h.astype(BF16)

    def proj(lo, hi):
        return jnp.dot(hb, wbf_ref[:, lo:hi], preferred_element_type=F32)

    o_lru[...] = proj(0, 512)
    o_hy[...] = proj(512, 1280)
    o_z[...] = proj(1280, 1792)
    o_xbc[...] = proj(1792, 2560)
    o_dt[...] = proj(D_MAIN, D_IN)
    dtT = lax.dot_general(wdtT_ref[0].astype(BF16), hb, (((1,), (1,)), ((), ())),
                          preferred_element_type=F32)
    for j in range(TM // SSM_Q):
        o_dtT[j] = dtT[:, j * SSM_Q:(j + 1) * SSM_Q]


def _k1(l, x, mod_l, norm1_w, w_in, w_dtT):
    tok = lambda w: pl.BlockSpec((TM, w), lambda i: (i, 0))
    return pl.pallas_call(
        _k1_kernel,
        grid=(NT,),
        in_specs=[
            tok(D),
            pl.BlockSpec((1, 6, D), lambda i: (_mod_row(i), 0, 0)),
            pl.BlockSpec((1, 1, D), lambda i: (l, 0, 0)),
            pl.BlockSpec((1, D, D_IN), lambda i: (l, 0, 0)),
            pl.BlockSpec((1, 2 * SSM_H, D), lambda i: (l, 0, 0)),
        ],
        out_specs=[tok(512), tok(768), tok(512), tok(768), tok(2 * SSM_H),
                   pl.BlockSpec((TM // SSM_Q, 2 * SSM_H, SSM_Q), lambda i: (i, 0, 0))],
        out_shape=[jax.ShapeDtypeStruct((T, 512), F32), jax.ShapeDtypeStruct((T, 768), F32),
                   jax.ShapeDtypeStruct((T, 512), F32), jax.ShapeDtypeStruct((T, 768), F32),
                   jax.ShapeDtypeStruct((T, 2 * SSM_H), F32),
                   jax.ShapeDtypeStruct((T // SSM_Q, 2 * SSM_H, SSM_Q), F32)],
        scratch_shapes=[pltpu.VMEM((D, D_IN), BF16)],
        compiler_params=_cparams(1),
        name=f"k1_inproj_{l}",
    )(x, mod_l, norm1_w.reshape(DEPTH, 1, D), w_in, w_dtT)


def _row_in_seq(rows, lseq):
    return lax.broadcasted_iota(I32, (rows, 1), 0) & (lseq - 1)


def _shift_rows(x, s, rin, lseq):
    if s == 0:
        return x
    y = pltpu.roll(x, s % x.shape[0], axis=0)
    valid = (rin >= s) if s > 0 else (rin < lseq + s)
    return jnp.where(valid, y, 0.0)


def _dwconv(x, w_ref, b_ref, rin, lseq):
    k_w = w_ref.shape[0]
    y = b_ref[...]
    for k in range(k_w):
        y = y + w_ref[k:k + 1, :] * _shift_rows(x, k_w // 2 - k, rin, lseq)
    return y


def _lru_block(u_ref, cw_ref, cb_ref, wbig_ref, bias_ref, lam_ref, h0_ref, o_ref, st_ref, lseq):
    rows = u_ref.shape[0]
    rin = _row_in_seq(rows, lseq)
    u = u_ref[...]
    gate = u[:, D_LRU:]
    x = _dwconv(u[:, :D_LRU], cw_ref, cb_ref, rin, lseq)
    xb = x.astype(BF16)
    y = None
    finals = []
    for d in range(2):
        g = jnp.dot(xb, wbig_ref[:, 512 * d:512 * (d + 1)], preferred_element_type=F32)
        g = g + bias_ref[:, 512 * d:512 * (d + 1)]
        r = jax.nn.sigmoid(g[:, :D_LRU])
        ig = jax.nn.sigmoid(g[:, D_LRU:])
        log_a = -LRU_C * r * jax.nn.softplus(-lam_ref[d:d + 1, :])
        a = jnp.exp(log_a)
        th = jnp.tanh(log_a)
        b = jnp.sqrt(-2.0 * th / (1.0 - th)) * (ig * x)
        if h0_ref is not None:
            edge = (rin == 0) if d == 0 else (rin == lseq - 1)
            b = b + jnp.where(edge, a * h0_ref[d:d + 1, :], 0.0)
        s = 1
        while s < lseq:
            sh = (s if d == 0 else -s) % rows
            valid = (rin >= s) if d == 0 else (rin < lseq - s)
            b = b + jnp.where(valid, a * pltpu.roll(b, sh, axis=0), 0.0)
            if 2 * s < lseq:
                a = jnp.where(valid, a * pltpu.roll(a, sh, axis=0), a)
            s *= 2
        y = b if y is None else y + b
        if st_ref is not None:
            last = lseq - 1 if d == 0 else 0
            finals.append(jnp.concatenate(
                [b[j * lseq + last:j * lseq + last + 1, :] for j in range(rows // lseq)], axis=0))
    o_ref[...] = y * jax.nn.gelu(gate)
    if st_ref is not None:
        st_ref[...] = jnp.concatenate(finals, axis=1)


def _lru_kernel(u_ref, cw_ref, cb_ref, wbig_ref, bias_ref, lam_ref, h0_ref, o_ref, st_ref):
    b = pl.program_id(0)
    args = (u_ref, cw_ref.at[0], cb_ref.at[0], wbig_ref.at[0], bias_ref.at[0], lam_ref.at[0])

    @pl.when(b < NB_CTX)
    def _():
        _lru_block(*args, None, o_ref, st_ref, L_CTX)

    @pl.when(b >= NB_CTX)
    def _():
        _lru_block(*args, h0_ref.at[0, 0], o_ref, None, L_LAT)


def _lru_mixer(l, u_lru, p, state_lru):
    lsel = lambda *shape: pl.BlockSpec((1,) + shape, lambda b: (l,) + (0,) * len(shape))
    return pl.pallas_call(
        _lru_kernel,
        grid=(NB,),
        in_specs=[
            pl.BlockSpec((RB, 512), lambda b: (b, 0)),
            lsel(4, D_LRU), lsel(1, D_LRU), lsel(D_LRU, 1024), lsel(1, 1024), lsel(2, D_LRU),
            pl.BlockSpec((1, 1, 2, D_LRU), lambda b: (jnp.maximum(b - NB_CTX, 0), l, 0, 0)),
        ],
        out_specs=[pl.BlockSpec((RB, D_LRU), lambda b: (b, 0)),
                   pl.BlockSpec((RB // L_CTX, 2 * D_LRU), lambda b: (jnp.minimum(b, NB_CTX - 1), 0))],
        out_shape=[jax.ShapeDtypeStruct((T, D_LRU), F32),
                   jax.ShapeDtypeStruct((N_CTX_SEQ, 2 * D_LRU), F32)],
        compiler_params=_cparams(1),
        name=f"lru_mixer_{l}",
    )(u_lru, p["lru_conv_w"], p["lru_conv_b"], p["lru_wbig"], p["lru_bias"], p["lru_lambda"], state_lru)


def _hy_filter_kernel(z_ref, win_ref, f_ref, w1_ref, b1_ref, w2_ref, b2_ref, fr_ref, w3_ref,
                      o_ref, h_ref):
    l = pl.program_id(1)

    @pl.when(pl.program_id(0) == 0)
    def _():
        fr = fr_ref[0]
        g = jnp.sin(fr * (jnp.dot(z_ref[...], w1_ref[0], precision=HI, preferred_element_type=F32)
                          + b1_ref[0]))
        g = jnp.sin(fr * (jnp.dot(g, w2_ref[0], precision=HI, preferred_element_type=F32) + b2_ref[0]))
        h = jnp.dot(g, w3_ref[0], precision=HI, preferred_element_type=F32)
        h_ref[l] = (h * win_ref[...]).astype(BF16)

    o_ref[0] = jnp.dot(f_ref[...], h_ref[l], preferred_element_type=F32)


def _hy_filter_spectrum(lseq, kb, zfeat, window, fmat, p):
    n = fmat.shape[0]
    lsel = lambda *shape: pl.BlockSpec((1,) + shape, lambda k, l: (l,) + (0,) * len(shape))
    return pl.pallas_call(
        _hy_filter_kernel,
        grid=(n // (2 * kb), DEPTH),
        in_specs=[
            pl.BlockSpec((lseq, LANE), lambda k, l: (0, 0)),
            pl.BlockSpec((lseq, D_HY), lambda k, l: (0, 0)),
            pl.BlockSpec((2 * kb, lseq), lambda k, l: (k, 0)),
            lsel(LANE, HY_HID), lsel(1, HY_HID), lsel(HY_HID, HY_HID), lsel(1, HY_HID),
            lsel(1, HY_HID), lsel(HY_HID, D_HY),
        ],
        out_specs=pl.BlockSpec((1, 2 * kb, D_HY), lambda k, l: (l, k, 0)),
        out_shape=jax.ShapeDtypeStruct((DEPTH, n, D_HY), F32),
        scratch_shapes=[pltpu.VMEM((DEPTH, lseq, D_HY), BF16)],
        compiler_params=_cparams(2),
        name=f"hyena_filter_{lseq}",
    )(zfeat, window, fmat, p["hy_w1p"], p["hy_b1"], p["hy_w2"], p["hy_b2"], p["hy_freq"], p["hy_w3"])


def _hy_spectral_block(f_blk, g_blk, hf, z_bf, is_dc_block):
    kb = f_blk.shape[0] // 2
    zf = jnp.dot(f_blk, z_bf, preferred_element_type=F32)
    rz, iz = zf[:kb], zf[kb:]
    rh, ih = hf[:kb], hf[kb:]
    ii = iz * ih
    re = rz * rh - ii
    im = rz * ih + iz * rh
    if is_dc_block is not None:
        dc = jnp.logical_and(lax.broadcasted_iota(I32, (kb, 1), 0) == 0, is_dc_block)
        re = jnp.where(dc, rz * rh, re)
        im = jnp.where(dc, ii, im)
    pr = jnp.concatenate([re, im], axis=0).astype(BF16)
    return jnp.dot(g_blk, pr, preferred_element_type=F32)


def _hy_prologue(u_ref, cw_ref, cb_ref, lseq, z_ref, zbf_ref, x2_ref):
    rin = _row_in_seq(u_ref.shape[0], lseq)
    uc = _dwconv(u_ref[...], cw_ref, cb_ref, rin, lseq)
    z = uc[:, :D_HY] * uc[:, D_HY:2 * D_HY]
    z_ref[...] = z
    zbf_ref[...] = z.astype(BF16)
    x2_ref[...] = uc[:, 2 * D_HY:]


def _hy_kernel(u_ref, cw_ref, cb_ref, hb_ref, fc_ref, gc_ref, hfc_ref, fl_ref, gl_ref, hfl_ref,
               o_ref, z_ref, zbf_ref, x2_ref, acc_ref):
    b = pl.program_id(0)
    k = pl.program_id(1)
    cw, cb = cw_ref.at[0], cb_ref.at[0]

    @pl.when(jnp.logical_and(b < NB_CTX, k == 0))
    def _():
        _hy_prologue(u_ref, cw, cb, L_CTX, z_ref, zbf_ref, x2_ref)
        for s in range(RB // L_CTX):
            rows = slice(s * L_CTX, (s + 1) * L_CTX)
            acc_ref[rows, :] = _hy_spectral_block(fc_ref[...], gc_ref[...], hfc_ref[0], zbf_ref[rows, :], True)

    @pl.when(b >= NB_CTX)
    def _():
        @pl.when(k == 0)
        def _():
            _hy_prologue(u_ref, cw, cb, L_LAT, z_ref, zbf_ref, x2_ref)
            acc_ref[...] = jnp.zeros_like(acc_ref)

        acc_ref[...] += _hy_spectral_block(fl_ref[...], gl_ref[...], hfl_ref[0], zbf_ref[...], k == 0)

    @pl.when(k == HY_NKB - 1)
    def _():
        o_ref[...] = x2_ref[...] * (acc_ref[...] + hb_ref[0] * z_ref[...])


def _hy_mixer(l, u_hy, p, c):
    lat_k = lambda b, k: jnp.where(b < NB_CTX, 0, k)
    lsel = lambda *shape: pl.BlockSpec((1,) + shape, lambda b, k: (l,) + (0,) * len(shape))
    kbl = 2 * HY_KB_LAT
    return pl.pallas_call(
        _hy_kernel,
        grid=(NB, HY_NKB),
        in_specs=[
            pl.BlockSpec((RB, 3 * D_HY), lambda b, k: (b, 0)),
            lsel(3, 3 * D_HY), lsel(1, 3 * D_HY), lsel(1, D_HY),
            pl.BlockSpec((N_FFT_CTX, L_CTX), lambda b, k: (0, 0)),
            pl.BlockSpec((L_CTX, N_FFT_CTX), lambda b, k: (0, 0)),
            lsel(N_FFT_CTX, D_HY),
            pl.BlockSpec((kbl, L_LAT), lambda b, k: (lat_k(b, k), 0)),
            pl.BlockSpec((L_LAT, kbl), lambda b, k: (0, lat_k(b, k))),
            pl.BlockSpec((1, kbl, D_HY), lambda b, k: (l, lat_k(b, k), 0)),
        ],
        out_specs=pl.BlockSpec((RB, D_HY), lambda b, k: (b, 0)),
        out_shape=jax.ShapeDtypeStruct((T, D_HY), F32),
        scratch_shapes=[pltpu.VMEM((RB, D_HY), F32), pltpu.VMEM((RB, D_HY), BF16),
                        pltpu.VMEM((RB, D_HY), F32), pltpu.VMEM((RB, D_HY), F32)],
        compiler_params=_cparams(2),
        name=f"hyena_mixer_{l}",
    )(u_hy, p["hy_conv_w"], p["hy_conv_b"], p["hy_bias"],
      c["f_ctx"], c["g_ctx"], c["hf_ctx"], c["f_lat"], c["g_lat"], c["hf_lat"])


def _ssd_block(u_z, u_xbc, u_dt, u_dtT, cw, cb, dtb_row, dtb_col, alog_row, alog_col, d_exp, nw,
               e32_ref, mbd_ref, mdiag_ref, h0_ref, o_ref, st_ref,
               x_s, bc_s, y_s, s_s, lseq):
    rows = u_z.shape[0]
    nchunk = rows // SSM_Q
    cps = lseq // SSM_Q
    rin = _row_in_seq(rows, lseq)
    for c0 in range(0, D_XBC, LANE):
        cols = slice(c0, c0 + LANE)
        xbc = _dwconv(u_xbc[:, cols], cw.at[:, cols], cb.at[:, cols], rin, lseq)
        xbc = xbc * jax.nn.sigmoid(xbc)
        if c0 < D_SSM:
            x_s[:, cols] = xbc
        else:
            bc_s[:, c0 - D_SSM:c0 - D_SSM + LANE] = xbc

    li = lax.broadcasted_iota(I32, (SSM_Q, SSM_Q), 0)
    si = lax.broadcasted_iota(I32, (SSM_Q, SSM_Q), 1)
    low_half = lax.broadcasted_iota(I32, (SSM_Q, LANE), 1) < SSM_N
    hq = SSM_H // SSM_G
    wq = hq * SSM_P
    hp = SSM_H * SSM_P

    y_s[...] = jnp.zeros_like(y_s)
    for d in range(2):
        for g in range(SSM_G):
            if h0_ref is not None:
                s_s[d, g] = jnp.tile(h0_ref[d, g * wq:(g + 1) * wq, :], (1, hq))
            else:
                s_s[d, g] = jnp.zeros((wq, wq), F32)

    def chunk_pair(ci, carry):
        for d in range(2):
            causal = (li >= si) if d == 0 else (li <= si)
            tri_col = causal.astype(BF16)
            tri_row = ((li <= si) if d == 0 else (li >= si)).astype(BF16)
            a_row = -jnp.exp(alog_row[:, SSM_H * d:SSM_H * (d + 1)])
            a_col = -jnp.exp(alog_col[SSM_H * d:SSM_H * (d + 1), :])
            b_row = dtb_row[:, SSM_H * d:SSM_H * (d + 1)]
            b_col = dtb_col[SSM_H * d:SSM_H * (d + 1), :]
            edge = SSM_Q - 1 if d == 0 else 0

            c = ci if d == 0 else nchunk - 1 - ci
            r0 = pl.multiple_of(c * SSM_Q, SSM_Q)
            rsl = pl.ds(r0, SSM_Q)
            if h0_ref is None and cps < nchunk:
                first = (c % cps == 0) if d == 0 else (c % cps == cps - 1)
                s_s[d] = s_s[d] * jnp.where(first, 0.0, 1.0)

            dt_c = jax.nn.softplus(u_dt[rsl, SSM_H * d:SSM_H * (d + 1)] + b_row)
            dt_r = jax.nn.softplus(u_dtT[c, SSM_H * d:SSM_H * (d + 1), :] + b_col)
            csc = jnp.dot(tri_col, jnp.concatenate(_split_bf16(dt_c * a_row, 3), axis=1),
                          preferred_element_type=F32)
            cs_col = csc[:, :SSM_H] + csc[:, SSM_H:2 * SSM_H] + csc[:, 2 * SSM_H:]
            cs_row = sum(jnp.dot(part, tri_row, preferred_element_type=F32)
                         for part in _split_bf16(dt_r * a_col, 3))
            tot = cs_col[edge:edge + 1, :]

            hi, lo = _split_bf16(jnp.concatenate([jnp.exp(cs_col), dt_c * jnp.exp(tot - cs_col)], axis=1), 2)
            spread = jnp.dot(jnp.concatenate([hi, lo], axis=1), e32_ref[...], preferred_element_type=F32)
            ecs_x, wdec_x = spread[:, :hp], spread[:, hp:]
            etot_x = ecs_x[edge:edge + 1, :]

            bcm = bc_s[rsl, :]
            bm, cm = bcm[:, :LANE], bcm[:, LANE:]
            bm_r, cm_r = pltpu.roll(bm, SSM_N, axis=1), pltpu.roll(cm, SSM_N, axis=1)
            bmb, cmb = bm.astype(BF16), cm.astype(BF16)
            for g in range(SSM_G):
                gl = slice(g * SSM_N, (g + 1) * SSM_N)
                ql = slice(g * wq, (g + 1) * wq)
                same = low_half if g == 0 else jnp.logical_not(low_half)
                b2 = jnp.where(same, bm, bm_r)
                c2 = jnp.where(same, cm, cm_r)
                gmat = lax.dot_general(cmb[:, gl], bmb[:, gl], (((1,), (1,)), ((), ())),
                                       preferred_element_type=F32)
                sc = []
                for h in range(g * hq, (g + 1) * hq):
                    diff = cs_col[:, h:h + 1] - cs_row[h:h + 1, :]
                    decay = jnp.exp(jnp.where(causal, diff, -1e30))
                    sc.append((gmat * decay * dt_r[h:h + 1, :]).astype(BF16))
                sc = jnp.concatenate(sc, axis=1)
                xq = x_s[rsl, ql]
                bd = jnp.tile(xq.astype(BF16), (hq, 1)) * mbd_ref[...]
                y = jnp.dot(sc, bd, preferred_element_type=F32)
                s_old = s_s[d, g]
                y_off = lax.dot_general(jnp.concatenate([c2, c2], axis=1).astype(BF16),
                                        (s_old * mdiag_ref[...]).astype(BF16),
                                        (((1,), (1,)), ((), ())), preferred_element_type=F32)
                y_s[rsl, ql] += y + y_off * ecs_x[:, ql]
                bx = (jnp.concatenate([b2, b2], axis=1) * wdec_x[:, ql]).astype(BF16)
                s_new = jnp.dot(xq.T.astype(BF16), bx, preferred_element_type=F32)
                s_s[d, g] = s_old * etot_x[:, ql] + s_new

            if st_ref is not None:
                last = (c % cps == cps - 1) if d == 0 else (c % cps == 0)

                @pl.when(last)
                def _(c=c, d=d):
                    pieces = []
                    for g in range(SSM_G):
                        sd = s_s[d, g] * mdiag_ref[...]
                        fold = sd[:, :LANE] + sd[:, LANE:]
                        for jp in range(hq // 2):
                            r = 2 * jp * SSM_P
                            pieces.append(fold[r:r + SSM_P] + fold[r + SSM_P:r + 2 * SSM_P])
                    st_ref[c // cps, d] = jnp.concatenate(pieces, axis=0)
        return carry

    lax.fori_loop(0, nchunk, chunk_pair, 0)

    for r0 in range(0, rows, 256):
        rsl = slice(r0, r0 + 256)
        y = y_s[rsl, :] + d_exp[...] * x_s[rsl, :]
        z = u_z[rsl, :]
        y = y * (z * jax.nn.sigmoid(z))
        o_ref[rsl, :] = _rms(y) * nw[...]


def _ssd_kernel(u_z, u_xbc, u_dt, u_dtT, cw, cb, dtb_row, dtb_col, alog_row, alog_col, d_exp, nw,
                e32_ref, mbd_ref, mdiag_ref, h0_ref, o_ref, st_ref, x_s, bc_s, y_s, s_s):
    b = pl.program_id(0)
    args = (u_z, u_xbc, u_dt, u_dtT, cw.at[0], cb.at[0], dtb_row.at[0], dtb_col.at[0], alog_row.at[0],
            alog_col.at[0], d_exp.at[0], nw.at[0], e32_ref, mbd_ref, mdiag_ref)
    scr = (x_s, bc_s, y_s, s_s)

    @pl.when(b < NB_CTX)
    def _():
        _ssd_block(*args, None, o_ref, st_ref, *scr, L_CTX)

    @pl.when(b >= NB_CTX)
    def _():
        _ssd_block(*args, h0_ref.at[0, 0], o_ref, None, *scr, L_LAT)


def _ssd_mixer(l, u_z, u_xbc, u_dt, u_dtT, p, c, state_ssm):
    lsel = lambda *shape: pl.BlockSpec((1,) + shape, lambda b: (l,) + (0,) * len(shape))
    full = lambda a: pl.BlockSpec(a.shape, lambda b: (0,) * a.ndim, pipeline_mode=pl.Buffered(1))
    hp = SSM_H * SSM_P
    nseq_blk = RB // L_CTX
    return pl.pallas_call(
        _ssd_kernel,
        grid=(NB,),
        in_specs=[
            pl.BlockSpec((RB, D_SSM), lambda b: (b, 0), pipeline_mode=pl.Buffered(1)),
            pl.BlockSpec((RB, D_XBC), lambda b: (b, 0)),
            pl.BlockSpec((RB, 2 * SSM_H), lambda b: (b, 0)),
            pl.BlockSpec((RB // SSM_Q, 2 * SSM_H, SSM_Q), lambda b: (b, 0, 0)),
            lsel(4, D_XBC), lsel(1, D_XBC), lsel(1, 2 * SSM_H), lsel(2 * SSM_H, 1),
            lsel(1, 2 * SSM_H), lsel(2 * SSM_H, 1), lsel(1, D_SSM), lsel(1, D_SSM),
            full(c["e32"]), full(c["mbd"]), full(c["mdiag"]),
            pl.BlockSpec((1, 1, 2, hp, SSM_N), lambda b: (jnp.maximum(b - NB_CTX, 0), l, 0, 0, 0)),
        ],
        out_specs=[pl.BlockSpec((RB, D_SSM), lambda b: (b, 0)),
                   pl.BlockSpec((nseq_blk, 2, hp // 2, LANE), lambda b: (jnp.minimum(b, NB_CTX - 1), 0, 0, 0))],
        out_shape=[jax.ShapeDtypeStruct((T, D_SSM), F32),
                   jax.ShapeDtypeStruct((N_CTX_SEQ, 2, hp // 2, LANE), F32)],
        scratch_shapes=[pltpu.VMEM((RB, D_SSM), F32), pltpu.VMEM((RB, 2 * SSM_G * SSM_N), F32),
                        pltpu.VMEM((RB, D_SSM), F32),
                        pltpu.VMEM((2, SSM_G, hp // SSM_G, hp // SSM_G), F32)],
        compiler_params=_cparams(1),
        name=f"ssd_mixer_{l}",
    )(u_z, u_xbc, u_dt, u_dtT, p["ssm_conv_w"], p["ssm_conv_b"], p["ssm_dtb_row"], p["ssm_dtb_col"],
      p["ssm_alog_row"], p["ssm_alog_col"], p["ssm_d_exp"], p["ssm_norm_w"],
      c["e32"], c["mbd"], c["mdiag"], state_ssm)


def _k2_kernel(*refs, routed):
    if routed:
        (ol_ref, oh_ref, os_ref, x_ref, mod_ref, nw_ref, w_ref, rt_ref,
         xo_ref, h2_ref, route_ref, cnt_ref, wbf_ref) = refs
    else:
        ol_ref, oh_ref, os_ref, x_ref, mod_ref, nw_ref, w_ref, xo_ref, h2_ref, wbf_ref = refs

    @pl.when(pl.program_id(0) == 0)
    def _():
        wbf_ref[...] = w_ref[0].astype(BF16)

    m = mod_ref[0]
    o = jnp.dot(ol_ref[...].astype(BF16), wbf_ref[0:256, :], preferred_element_type=F32)
    o = o + jnp.dot(oh_ref[...].astype(BF16), wbf_ref[256:512, :], preferred_element_type=F32)
    o = o + jnp.dot(os_ref[...].astype(BF16), wbf_ref[512:1024, :], preferred_element_type=F32)
    x = x_ref[...] + m[2:3] * o
    xo_ref[...] = x
    h2 = _rms(x) * nw_ref[0]
    h2 = h2 * (1.0 + m[4:5]) + m[3:4]
    h2_hi = h2.astype(BF16)
    h2_ref[...] = h2_hi

    if routed:
        h2_lo = (h2 - h2_hi.astype(F32)).astype(BF16)
        r_hi, r_lo = _split_bf16(rt_ref[0], 2)
        logits = (jnp.dot(h2_hi, r_hi, preferred_element_type=F32)
                  + jnp.dot(h2_lo, r_hi, preferred_element_type=F32)
                  + jnp.dot(h2_hi, r_lo, preferred_element_type=F32))
        eid = lax.broadcasted_iota(I32, logits.shape, 1)
        m1 = jnp.max(logits, axis=1, keepdims=True)
        i1 = jnp.min(jnp.where(logits == m1, eid, N_EXP), axis=1, keepdims=True)
        rest = jnp.where(eid == i1, -jnp.inf, logits)
        m2 = jnp.max(rest, axis=1, keepdims=True)
        i2 = jnp.min(jnp.where(rest == m2, eid, N_EXP), axis=1, keepdims=True)
        w1 = 1.0 / (1.0 + jnp.exp(m2 - m1))
        w2 = 1.0 - w1
        oh1 = (eid == i1).astype(F32)
        oh2 = (eid == i2).astype(F32)
        both = oh1 + oh2
        before = (lax.broadcasted_iota(I32, (TM, TM), 0) > lax.broadcasted_iota(I32, (TM, TM), 1))
        ahead = jnp.dot(before.astype(BF16), both.astype(BF16), preferred_element_type=F32)
        r1 = jnp.sum(oh1 * ahead, axis=1, keepdims=True)
        r2 = jnp.sum(oh2 * ahead, axis=1, keepdims=True)
        zero = jnp.zeros_like(w1)
        route_ref[...] = jnp.concatenate(
            [i1.astype(F32), i2.astype(F32), r1, r2, w1, w2, zero, zero], axis=1)
        cnt_ref[0] = jnp.sum(both, axis=0, keepdims=True)


def _k2(l, o_lru, o_hy, o_ssm, x, mod_l, norm2_w, w_out, router=None, j=0):
    routed = router is not None
    tok = lambda w: pl.BlockSpec((TM, w), lambda i: (i, 0))
    in_specs = [
        tok(D_LRU), tok(D_HY), tok(D_SSM), tok(D),
        pl.BlockSpec((1, 6, D), lambda i: (_mod_row(i), 0, 0)),
        pl.BlockSpec((1, 1, D), lambda i: (l, 0, 0)),
        pl.BlockSpec((1, D, D), lambda i: (l, 0, 0)),
    ]
    args = [o_lru, o_hy, o_ssm, x, mod_l, norm2_w.reshape(DEPTH, 1, D), w_out]
    out_specs = [tok(D), tok(D)]
    out_shape = [jax.ShapeDtypeStruct((T, D), F32), jax.ShapeDtypeStruct((T, D), BF16)]
    if routed:
        in_specs.append(pl.BlockSpec((1, D, N_EXP), lambda i: (j, 0, 0)))
        args.append(router)
        out_specs += [tok(8), pl.BlockSpec((1, 1, N_EXP), lambda i: (i, 0, 0))]
        out_shape += [jax.ShapeDtypeStruct((T, 8), F32), jax.ShapeDtypeStruct((NT, 1, N_EXP), F32)]
    return pl.pallas_call(
        functools.partial(_k2_kernel, routed=routed),
        grid=(NT,),
        in_specs=in_specs,
        out_specs=out_specs,
        out_shape=out_shape,
        scratch_shapes=[pltpu.VMEM((D, D), BF16)],
        compiler_params=_cparams(1),
        name=f"k2_outproj_{l}",
    )(*args)


def _ffn_stream_kernel(be_ref, bi_ref, na_ref, x_ref, *rest, j, dense):
    if dense:
        xres_ref, mod_ref, wg_hbm, wu_hbm, wd_hbm, o_ref, wg_s, wu_s, wd_s, stg_g, stg_u, stg_d, sem = rest
    else:
        wg_hbm, wu_hbm, wd_hbm, o_ref, wg_s, wu_s, wd_s, stg_g, stg_u, stg_d, sem = rest
    del bi_ref
    b = pl.program_id(0)
    n_act = na_ref[0]
    e = be_ref[b]
    active = b < n_act
    load = jnp.logical_and(active, jnp.logical_or(b == 0, be_ref[jnp.maximum(b - 1, 0)] != e))
    e_next = be_ref[jnp.minimum(b + 1, pl.num_programs(0) - 1)]
    feed_next = jnp.logical_and(b + 1 < n_act, e_next != e)

    def copies(ee, c):
        slot = c % FFN_STAGES
        cols = slice(c * FF_CHUNK, (c + 1) * FF_CHUNK)
        return (pltpu.make_async_copy(wg_hbm.at[j, ee, :, cols], stg_g.at[slot], sem.at[0, slot]),
                pltpu.make_async_copy(wu_hbm.at[j, ee, :, cols], stg_u.at[slot], sem.at[1, slot]),
                pltpu.make_async_copy(wd_hbm.at[j, ee, cols, :], stg_d.at[slot], sem.at[2, slot]))

    def start(ee, c):
        for cp in copies(ee, c):
            cp.start()

    def chunk_out(c, x, acc):
        g = jnp.dot(x, wg_s[c], preferred_element_type=F32)
        u = jnp.dot(x, wu_s[c], preferred_element_type=F32)
        hmid = (g * jax.nn.sigmoid(g) * u).astype(BF16)
        part = jnp.dot(hmid, wd_s[c], preferred_element_type=F32)
        return part if acc is None else acc + part

    def finish(acc):
        if dense:
            o_ref[...] = xres_ref[...] + mod_ref[0][5:6] * acc
        else:
            o_ref[...] = acc.astype(o_ref.dtype)

    @pl.when(load)
    def _():
        @pl.when(b == 0)
        def _():
            for c in range(FFN_STAGES):
                start(e, c)

        x = x_ref[...]
        acc = None
        for c in range(N_FF_CHUNK):
            slot = c % FFN_STAGES
            for cp in copies(e, c):
                cp.wait()
            wg_s[c] = stg_g[slot].astype(BF16)
            wu_s[c] = stg_u[slot].astype(BF16)
            wd_s[c] = stg_d[slot].astype(BF16)
            if c + FFN_STAGES < N_FF_CHUNK:
                start(e, c + FFN_STAGES)
            acc = chunk_out(c, x, acc)
        finish(acc)

    @pl.when(jnp.logical_and(active, jnp.logical_not(load)))
    def _():
        x = x_ref[...]
        acc = None
        for c in range(N_FF_CHUNK):
            acc = chunk_out(c, x, acc)
        finish(acc)

    if not dense:
        @pl.when(jnp.logical_not(active))
        def _():
            o_ref[...] = jnp.zeros_like(o_ref)

    @pl.when(feed_next)
    def _():
        for c in range(FFN_STAGES):
            start(e_next, c)


def _ffn_scratch():
    return [pltpu.VMEM((N_FF_CHUNK, D, FF_CHUNK), BF16), pltpu.VMEM((N_FF_CHUNK, D, FF_CHUNK), BF16),
            pltpu.VMEM((N_FF_CHUNK, FF_CHUNK, D), BF16),
            pltpu.VMEM((FFN_STAGES, D, FF_CHUNK), F32), pltpu.VMEM((FFN_STAGES, D, FF_CHUNK), F32),
            pltpu.VMEM((FFN_STAGES, FF_CHUNK, D), F32), pltpu.SemaphoreType.DMA((3, FFN_STAGES))]


def _dense_ffn(j, h2, x, mod_l, wg, wu, wd):
    hbm = pl.BlockSpec(memory_space=pl.ANY)
    zeros = jnp.zeros((NT,), I32)
    return pl.pallas_call(
        functools.partial(_ffn_stream_kernel, j=j, dense=True),
        grid_spec=pltpu.PrefetchScalarGridSpec(
            num_scalar_prefetch=3,
            grid=(NT,),
            in_specs=[
                pl.BlockSpec((TM, D), lambda i, *_: (i, 0)),
                pl.BlockSpec((TM, D), lambda i, *_: (i, 0)),
                pl.BlockSpec((1, 6, D), lambda i, *_: (_mod_row(i), 0, 0)),
                hbm, hbm, hbm,
            ],
            out_specs=pl.BlockSpec((TM, D), lambda i, *_: (i, 0)),
            scratch_shapes=_ffn_scratch(),
        ),
        out_shape=jax.ShapeDtypeStruct((T, D), F32),
        compiler_params=_cparams(1),
        name=f"dense_ffn_{j}",
    )(zeros, zeros, jnp.full((1,), NT, I32), h2, x, mod_l, wg[:, None], wu[:, None], wd[:, None])


def _expert_ffn(j, xs, blk_e, blk_i, n_active, wg, wu, wd):
    hbm = pl.BlockSpec(memory_space=pl.ANY)
    return pl.pallas_call(
        functools.partial(_ffn_stream_kernel, j=j, dense=False),
        grid_spec=pltpu.PrefetchScalarGridSpec(
            num_scalar_prefetch=3,
            grid=(NBLK,),
            in_specs=[pl.BlockSpec((TMB, D), lambda b, be, bi, na: (bi[b], 0)), hbm, hbm, hbm],
            out_specs=pl.BlockSpec((TMB, D), lambda b, be, bi, na: (b, 0)),
            scratch_shapes=_ffn_scratch(),
        ),
        out_shape=jax.ShapeDtypeStruct((RMAX, D), BF16),
        compiler_params=_cparams(1),
        name=f"expert_ffn_{j}",
    )(blk_e, blk_i, n_active, xs, wg, wu, wd)


def _chunk_rows(idx):
    return pl.ds(pl.multiple_of(idx * GCH, GCH), GCH)


def _sort_kernel(gch_ref, nused_ref, pad0_ref, npad_ref, h_ref, dl_ref, xs_ref, xl_ref, zero_ref, sem):
    i = pl.program_id(0)
    slot = i % 2

    def copy(tile, q):
        s = tile % 2
        return pltpu.make_async_copy(xl_ref.at[s, _chunk_rows(q)],
                                     xs_ref.at[_chunk_rows(gch_ref[tile * NLC + q])], sem.at[s])

    def wait_tile(tile):
        lax.fori_loop(0, nused_ref[tile], lambda q, c: (copy(tile, q).wait(), c)[1], 0)

    @pl.when(i >= 2)
    def _():
        wait_tile(i - 2)

    dl = dl_ref[0]
    r = lax.broadcasted_iota(I32, (LCAP, TM), 0).astype(F32)
    perm = jnp.logical_or(r == dl[0:1, :], r == dl[1:2, :]).astype(BF16)
    xl_ref[slot] = jnp.dot(perm, h_ref[...], preferred_element_type=F32).astype(BF16)
    lax.fori_loop(0, nused_ref[i], lambda q, c: (copy(i, q).start(), c)[1], 0)

    @pl.when(i == NT - 1)
    def _():
        zero_ref[...] = jnp.zeros_like(zero_ref)
        for e in range(N_EXP + 1):
            def zcopy(q, e=e):
                return pltpu.make_async_copy(zero_ref, xs_ref.at[_chunk_rows(pad0_ref[e] + q)], sem.at[2])
            lax.fori_loop(0, npad_ref[e], lambda q, c, f=zcopy: (f(q).start(), c)[1], 0)
            lax.fori_loop(0, npad_ref[e], lambda q, c, f=zcopy: (f(q).wait(), c)[1], 0)
        wait_tile(i - 1)
        wait_tile(i)


def _sort_tokens(h2, dl_row, gch, nused, pad0, npad):
    return pl.pallas_call(
        _sort_kernel,
        grid_spec=pltpu.PrefetchScalarGridSpec(
            num_scalar_prefetch=4,
            grid=(NT,),
            in_specs=[
                pl.BlockSpec((TM, D), lambda i, *_: (i, 0)),
                pl.BlockSpec((1, 2, TM), lambda i, *_: (i, 0, 0)),
            ],
            out_specs=pl.BlockSpec(memory_space=pl.ANY),
            scratch_shapes=[pltpu.VMEM((2, LCAP, D), BF16), pltpu.VMEM((GCH, D), BF16),
                            pltpu.SemaphoreType.DMA((3,))],
        ),
        out_shape=jax.ShapeDtypeStruct((RMAX, D), BF16),
        compiler_params=_cparams(1),
        name="moe_sort",
    )(gch, nused, pad0, npad, h2, dl_row)


def _combine_kernel(gch_ref, nused_ref, y_ref, info_ref, x_ref, mod_ref, fw_ref, *rest, final):
    if final:
        oc_ref, ol_ref, yl_ref, sem = rest
    else:
        o_ref, yl_ref, sem = rest
    i = pl.program_id(0)
    slot = i % 2

    def copy(tile, q):
        s = tile % 2
        return pltpu.make_async_copy(y_ref.at[_chunk_rows(gch_ref[tile * NLC + q])],
                                     yl_ref.at[s, _chunk_rows(q)], sem.at[s])

    def fetch(tile):
        lax.fori_loop(0, nused_ref[tile], lambda q, c: (copy(tile, q).start(), c)[1], 0)

    @pl.when(i == 0)
    def _():
        fetch(i)

    @pl.when(i + 1 < NT)
    def _():
        fetch(i + 1)

    n = nused_ref[i]

    def clear(q, c):
        yl_ref[slot, _chunk_rows(q), :] = jnp.zeros((GCH, D), BF16)
        return c

    lax.fori_loop(n, NLC, clear, 0)
    lax.fori_loop(0, n, lambda q, c: (copy(i, q).wait(), c)[1], 0)

    info = info_ref[...]
    col = lax.broadcasted_iota(I32, (TM, LCAP), 1).astype(F32)
    yl = yl_ref[slot]
    y1 = jnp.dot((col == info[:, 0:1]).astype(BF16), yl, preferred_element_type=F32)
    y2 = jnp.dot((col == info[:, 1:2]).astype(BF16), yl, preferred_element_type=F32)
    x = x_ref[...] + mod_ref[0][5:6] * (info[:, 2:3] * y1 + info[:, 3:4] * y2)
    if final:
        x = _rms(x) * fw_ref[...]

        @pl.when(i < NT_CTX)
        def _():
            oc_ref[...] = x

        @pl.when(i >= NT_CTX)
        def _():
            ol_ref[...] = x
    else:
        o_ref[...] = x


def _combine(y, info, x, mod_l, final_w, gch, nused, final):
    tile = pl.BlockSpec((TM, D), lambda i, *_: (i, 0))
    if final:
        out_specs = [pl.BlockSpec((TM, D), lambda i, *_: (jnp.minimum(i, NT_CTX - 1), 0)),
                     pl.BlockSpec((TM, D), lambda i, *_: (jnp.maximum(i - NT_CTX, 0), 0))]
        out_shape = [jax.ShapeDtypeStruct((T_CTX, D), F32), jax.ShapeDtypeStruct((T_LAT, D), F32)]
    else:
        out_specs, out_shape = tile, jax.ShapeDtypeStruct((T, D), F32)
    return pl.pallas_call(
        functools.partial(_combine_kernel, final=final),
        grid_spec=pltpu.PrefetchScalarGridSpec(
            num_scalar_prefetch=2,
            grid=(NT,),
            in_specs=[
                pl.BlockSpec(memory_space=pl.ANY),
                pl.BlockSpec((TM, 4), lambda i, *_: (i, 0)),
                tile,
                pl.BlockSpec((1, 6, D), lambda i, *_: (_mod_row(i), 0, 0)),
                pl.BlockSpec((1, D), lambda i, *_: (0, 0)),
            ],
            out_specs=out_specs,
            scratch_shapes=[pltpu.VMEM((2, LCAP, D), BF16), pltpu.SemaphoreType.DMA((2,))],
        ),
        out_shape=out_shape,
        compiler_params=_cparams(1),
        name="moe_combine",
    )(gch, nused, y, info, x, mod_l, final_w.reshape(1, D))


def _moe_plan(route, counts):
    cnt = counts.reshape(NT, N_EXP).astype(I32)
    cpad = (cnt + GCH - 1) // GCH * GCH
    lo = jnp.cumsum(cpad, axis=1) - cpad
    nused = (lo[:, -1] + cpad[:, -1]) // GCH
    tot = jnp.sum(cpad, axis=0)
    gpad = (tot + TMB - 1) // TMB * TMB
    goff = jnp.cumsum(gpad) - gpad
    so = goff[None, :] + jnp.cumsum(cpad, axis=0) - cpad

    e1 = route[:, 0].astype(I32)
    e2 = route[:, 1].astype(I32)
    eid = jnp.arange(N_EXP, dtype=I32)[None, :]
    lo_tok = jnp.repeat(lo, TM, axis=0)
    dl1 = jnp.sum(jnp.where(e1[:, None] == eid, lo_tok, 0), axis=1).astype(F32) + route[:, 2]
    dl2 = jnp.sum(jnp.where(e2[:, None] == eid, lo_tok, 0), axis=1).astype(F32) + route[:, 3]
    dl_row = jnp.stack([dl1.reshape(NT, TM), dl2.reshape(NT, TM)], axis=1)
    info = jnp.stack([dl1, dl2, route[:, 4], route[:, 5]], axis=1)

    q = jnp.arange(NLC, dtype=I32)[None, :, None]
    lo16 = (lo // GCH)[:, None, :]
    c16 = (cpad // GCH)[:, None, :]
    in_seg = jnp.logical_and(q >= lo16, q < lo16 + c16)
    gch = jnp.sum(jnp.where(in_seg, (so // GCH)[:, None, :] + q - lo16, 0), axis=2).reshape(NT * NLC)

    nblk = gpad // TMB
    n_active = jnp.sum(nblk)
    b = jnp.arange(NBLK, dtype=I32)
    blk_i = jnp.maximum(jnp.minimum(b, n_active - 1), 0)
    bend = (goff + gpad) // TMB
    blk_e = jnp.minimum(jnp.sum((blk_i[:, None] >= bend[None, :]).astype(I32), axis=1), N_EXP - 1)
    used = n_active * TMB
    pad0 = jnp.concatenate([goff + tot, used.reshape(1)]) // GCH
    npad = jnp.concatenate([gpad - tot, (RMAX - used).reshape(1)]) // GCH
    return dict(dl_row=dl_row, info=info, gch=gch.astype(I32), nused=nused.astype(I32),
                blk_e=blk_e.astype(I32), blk_i=blk_i.astype(I32),
                n_active=n_active.reshape(1).astype(I32), pad0=pad0.astype(I32), npad=npad.astype(I32))


def _grid_pos_embed(n_tokens):
    rows = n_tokens // GRID_W
    r = jnp.repeat(jnp.arange(rows, dtype=F32), GRID_W)
    col = jnp.tile(jnp.arange(GRID_W, dtype=F32), rows)
    quarter = D // 4
    omega = 1.0 / (10000.0 ** (jnp.arange(quarter, dtype=F32) / quarter))
    ang_r = r[:, None] * omega[None]
    ang_c = col[:, None] * omega[None]
    return jnp.concatenate([jnp.sin(ang_r), jnp.cos(ang_r), jnp.sin(ang_c), jnp.cos(ang_c)], axis=-1)


def _hy_pos_features(n):
    pos = jnp.arange(n, dtype=F32)
    t = pos / (n - 1)
    bands = jnp.linspace(1e-4, HY_BANDS - 1, HY_BANDS, dtype=F32)
    ang = (2.0 * math.pi * pos / n)[:, None] * bands[None]
    z = jnp.concatenate([t[:, None], jnp.cos(ang), -jnp.sin(ang)], axis=-1)
    z = jnp.pad(z, ((0, 0), (0, LANE - HY_POS_DIM)))
    half = n // 2
    dist = jnp.abs(pos - half) / half
    deltas = jnp.abs(jnp.linspace(HY_MIN_DECAY, HY_MAX_DECAY, D_HY, dtype=F32))
    return z, jnp.exp(-dist[:, None] * deltas[None])


def _dft_mats(lseq, n, kb):
    nkb = n // 2 // kb
    t = jnp.arange(lseq, dtype=I32)

    def tables(tt):
        def cs(freq):
            ang = (2.0 * math.pi / n) * ((freq[:, None] * tt[None, :]) % n).astype(F32)
            return jnp.cos(ang), jnp.sin(ang)
        (ca, sa), (cb, sb) = cs(jnp.arange(nkb, dtype=I32) * kb), cs(jnp.arange(kb, dtype=I32))
        ca, sa, cb, sb = ca[:, None, :], sa[:, None, :], cb[None], sb[None]
        re = ca * cb - sa * sb
        im = -(sa * cb + ca * sb)
        dc = jnp.logical_and(jnp.arange(nkb)[:, None, None] == 0, jnp.arange(kb)[None, :, None] == 0)
        alt = (1.0 - 2.0 * (tt % 2).astype(F32))[None, None, :]
        return re, jnp.where(dc, alt, im), dc

    re, im, _ = tables(t)
    f = jnp.concatenate([re, im], axis=1).reshape(n, lseq)
    re, im, dc = tables(t + lseq // 2)
    wk = jnp.where(dc, 1.0, 2.0) / n
    gt = jnp.concatenate([wk * re, jnp.where(dc, 1.0 / n, wk) * im], axis=1).reshape(n, lseq)
    return f.astype(BF16), gt.T.astype(BF16)


def _ssd_constants():
    hp = SSM_H * SSM_P
    hq = SSM_H // SSM_G
    head_of = np.arange(hp) // SSM_P
    spread = (np.arange(SSM_H)[:, None] == head_of[None, :]).astype(np.float32)
    zero = np.zeros_like(spread)
    e32 = np.block([[spread, zero], [zero, spread], [spread, zero], [zero, spread]])
    hq_of = np.arange(hq * SSM_P) // SSM_P
    mbd = (np.arange(hq * SSM_Q)[:, None] // SSM_Q == hq_of[None, :]).astype(np.float32)
    mdiag = (hq_of[:, None] == hq_of[None, :]).astype(np.float32)
    return dict(e32=jnp.asarray(e32, dtype=BF16), mbd=jnp.asarray(mbd, dtype=BF16), mdiag=jnp.asarray(mdiag))


def _unpack_ssm_state(s):
    lead = s.shape[:-2]
    s = s.reshape(lead + (SSM_H // 2, SSM_P, 2, SSM_N))
    return jnp.swapaxes(s, -3, -2).reshape(lead + (SSM_H, SSM_P, SSM_N))


def _block_diag_heads(w):
    eye = jnp.eye(LRU_HEADS, dtype=w.dtype)
    return jnp.einsum("ldhij,hg->ldhigj", w, eye).reshape(DEPTH, 2, D_LRU, D_LRU)


def kernel(x_prompt, x_sample, state_lru, state_ssm, c, c_ctx, norm1_w, norm2_w, final_norm_w, ada_w, ada_b,
           w_in, w_out, lru_conv_w, lru_conv_b, lru_wa, lru_ba, lru_wi, lru_bi, lru_lambda, hy_conv_w, hy_conv_b,
           hy_w1, hy_b1, hy_w2, hy_b2, hy_freq, hy_w3, hy_bias, ssm_conv_w, ssm_conv_b, ssm_dt_bias, ssm_a_log,
           ssm_d, ssm_norm_w, ffn_w_gate, ffn_w_up, ffn_w_down, moe_router, moe_w_gate, moe_w_up, moe_w_down):
    hp = SSM_H * SSM_P
    wa, wi = _block_diag_heads(lru_wa), _block_diag_heads(lru_wi)
    row = lambda a: a.reshape(DEPTH, 1, -1)
    p = {
        "lru_conv_w": lru_conv_w, "lru_conv_b": row(lru_conv_b), "lru_lambda": lru_lambda,
        "lru_wbig": jnp.concatenate([wa[:, 0], wi[:, 0], wa[:, 1], wi[:, 1]], axis=-1).astype(BF16),
        "lru_bias": jnp.concatenate([lru_ba[:, 0], lru_bi[:, 0], lru_ba[:, 1], lru_bi[:, 1]], axis=-1)[:, None],
        "hy_conv_w": hy_conv_w, "hy_conv_b": row(hy_conv_b), "hy_bias": row(hy_bias),
        "hy_w1p": jnp.pad(hy_w1, ((0, 0), (0, LANE - HY_POS_DIM), (0, 0))), "hy_b1": row(hy_b1),
        "hy_w2": hy_w2, "hy_b2": row(hy_b2), "hy_freq": row(hy_freq), "hy_w3": hy_w3,
        "ssm_conv_w": ssm_conv_w, "ssm_conv_b": row(ssm_conv_b),
        "ssm_dtb_row": row(ssm_dt_bias), "ssm_dtb_col": ssm_dt_bias.reshape(DEPTH, 2 * SSM_H, 1),
        "ssm_alog_row": row(ssm_a_log), "ssm_alog_col": ssm_a_log.reshape(DEPTH, 2 * SSM_H, 1),
        "ssm_d_exp": jnp.repeat(ssm_d, SSM_P, axis=-1)[:, None], "ssm_norm_w": row(ssm_norm_w),
    }
    w_dtT = jnp.swapaxes(w_in[:, :, D_MAIN:], 1, 2)
    ffn_w = (ffn_w_gate, ffn_w_up, ffn_w_down)
    moe_w = (moe_w_gate, moe_w_up, moe_w_down)

    cst = _ssd_constants()
    z_ctx, win_ctx = _hy_pos_features(L_CTX)
    z_lat, win_lat = _hy_pos_features(L_LAT)
    cst["f_ctx"], cst["g_ctx"] = _dft_mats(L_CTX, N_FFT_CTX, N_FFT_CTX // 2)
    cst["f_lat"], cst["g_lat"] = _dft_mats(L_LAT, N_FFT_LAT, HY_KB_LAT)
    cst["hf_ctx"] = _hy_filter_spectrum(L_CTX, N_FFT_CTX // 2, z_ctx, win_ctx, cst["f_ctx"], p)
    cst["hf_lat"] = _hy_filter_spectrum(L_LAT, HY_KB_LAT, z_lat, win_lat, cst["f_lat"], p)

    cond = jnp.concatenate([c_ctx[None], c], axis=0)
    mod = _mod_table(jnp.broadcast_to(cond[:, :, None], (3, D, LANE)), ada_w, ada_b)
    mod = mod[:, :3].reshape(DEPTH, 3, 6, D)

    x = jnp.concatenate([x_prompt.reshape(T_CTX, D),
                         (x_sample + _grid_pos_embed(L_LAT)[None]).reshape(T_LAT, D)], axis=0)
    st_ssm_in = state_ssm.reshape(N_LAT_SEQ, DEPTH, 2, hp, SSM_N)

    lru_states, ssm_states = [], []
    for l in range(DEPTH):
        u_lru, u_hy, u_z, u_xbc, u_dt, u_dtT = _k1(l, x, mod[l], norm1_w, w_in, w_dtT)
        o_lru, s_lru = _lru_mixer(l, u_lru, p, state_lru)
        o_hy = _hy_mixer(l, u_hy, p, cst)
        o_ssm, s_ssm = _ssd_mixer(l, u_z, u_xbc, u_dt, u_dtT, p, cst, st_ssm_in)
        lru_states.append(s_lru.reshape(N_CTX_SEQ, 2, D_LRU))
        ssm_states.append(s_ssm)
        j = l // 2
        if l % 2 == 0:
            x, h2 = _k2(l, o_lru, o_hy, o_ssm, x, mod[l], norm2_w, w_out)
            x = _dense_ffn(j, h2, x, mod[l], *ffn_w)
        else:
            x, h2, route, counts = _k2(l, o_lru, o_hy, o_ssm, x, mod[l], norm2_w, w_out, moe_router, j)
            plan = _moe_plan(route, counts)
            xs = _sort_tokens(h2, plan["dl_row"], plan["gch"], plan["nused"], plan["pad0"], plan["npad"])
            y = _expert_ffn(j, xs, plan["blk_e"], plan["blk_i"], plan["n_active"], *moe_w)
            x = _combine(y, plan["info"], x, mod[l], final_norm_w, plan["gch"], plan["nused"],
                         final=(l == DEPTH - 1))
    y_prompt = x[0].reshape(N_CTX_SEQ, L_CTX, D)
    y_sample = x[1].reshape(N_LAT_SEQ, L_LAT, D)
    return (y_prompt, y_sample, jnp.stack(lru_states, axis=1), _unpack_ssm_state(jnp.stack(ssm_states, axis=1)))
```

```python
import functools
import math

import numpy as np
import jax
import jax.numpy as jnp
from jax import lax
from jax.experimental import pallas as pl
from jax.experimental.pallas import tpu as pltpu

F32 = jnp.float32
BF16 = jnp.bfloat16
I32 = jnp.int32
HI = lax.Precision.HIGHEST

D = 1024
N_CTX_SEQ, L_CTX = 16, 256
N_LAT_SEQ, L_LAT = 2, 2048
DEPTH = 4
GRID_W = 64
D_LRU = 256
LRU_HEADS, LRU_HD = 4, 64
LRU_C = 8.0
D_HY = 256
HY_BANDS = 16
HY_POS_DIM = 1 + 2 * HY_BANDS
HY_HID = 64
HY_MAX_DECAY = math.log(1e-2) / 0.3
HY_MIN_DECAY = math.log(1e-2) / 1.5
D_SSM = 512
SSM_P = 64
SSM_H = 8
SSM_G = 2
SSM_N = 64
SSM_Q = 128
D_XBC = D_SSM + 2 * SSM_G * SSM_N
D_MAIN = 2 * D_LRU + 3 * D_HY + D_SSM + D_XBC
D_IN = D_MAIN + 2 * SSM_H
D_FF = 2816
N_EXP = 8
EPS = 1e-6

LANE = 128
BF16_ROWS = 16
T_CTX = N_CTX_SEQ * L_CTX
T_LAT = N_LAT_SEQ * L_LAT
T = T_CTX + T_LAT
TM = 512
NT = T // TM
NT_CTX = T_CTX // TM
NT_PER_LAT = L_LAT // TM
RB = 2048
NB = T // RB
NB_CTX = T_CTX // RB
FF_CHUNK = 256
N_FF_CHUNK = D_FF // FF_CHUNK
FFN_STAGES = 3
VMEM_LIMIT = 56 * 1024 * 1024

GCH = BF16_ROWS
LCAP = 2 * TM + N_EXP * GCH
NLC = LCAP // GCH
TMB = 512
RMAX = -(-(2 * T + NT * N_EXP * (GCH - 1) + N_EXP * (TMB - 1)) // TMB) * TMB
NBLK = RMAX // TMB

HY_KB_LAT = 256
N_FFT_LAT = 3 * L_LAT // 2
N_FFT_CTX = 3 * L_CTX // 2
HY_NKB = N_FFT_LAT // 2 // HY_KB_LAT


def _cparams(n_axes=1, vmem=VMEM_LIMIT):
    return pltpu.CompilerParams(dimension_semantics=("arbitrary",) * n_axes, vmem_limit_bytes=vmem)


def _mod_row(i):
    return jnp.where(i < NT_CTX, 0, 1 + (i - NT_CTX) // NT_PER_LAT)


def _bdot(a, b):
    return jnp.dot(a.astype(BF16), b.astype(BF16), preferred_element_type=F32)


def _rms(x):
    return x * lax.rsqrt(jnp.mean(x * x, axis=-1, keepdims=True) + EPS)


def _split_bf16(v, parts):
    out = []
    for _ in range(parts):
        piece = v.astype(BF16)
        out.append(piece)
        v = v - piece.astype(F32)
    return out


def _mod_kernel(cb_ref, w_ref, b_ref, o_ref):
    tn = w_ref.shape[2]

    def body(kc, accs):
        k0 = pl.multiple_of(kc * 8, 8)
        wk = w_ref[0, pl.ds(k0, 8), :]
        out = []
        for r in range(3):
            c = cb_ref[r, pl.ds(k0, 8), :]
            c = c * jax.nn.sigmoid(c)
            out.append(accs[r] + jnp.tile(c, (1, tn // LANE)) * wk)
        return tuple(out)

    accs = lax.fori_loop(0, D // 8, body, tuple(jnp.zeros((8, tn), F32) for _ in range(3)), unroll=8)
    rows = [jnp.sum(a, axis=0, keepdims=True) + b_ref[0] for a in accs]
    o_ref[0] = jnp.concatenate(rows + [jnp.zeros((5, tn), F32)], axis=0)


def _mod_table(cond_b, ada_w, ada_b):
    tn = 1024
    return pl.pallas_call(
        _mod_kernel,
        grid=(DEPTH, 6 * D // tn),
        in_specs=[
            pl.BlockSpec((3, D, LANE), lambda l, j: (0, 0, 0)),
            pl.BlockSpec((1, D, tn), lambda l, j: (l, 0, j)),
            pl.BlockSpec((1, 1, tn), lambda l, j: (l, 0, j)),
        ],
        out_specs=pl.BlockSpec((1, 8, tn), lambda l, j: (l, 0, j)),
        out_shape=jax.ShapeDtypeStruct((DEPTH, 8, 6 * D), F32),
        compiler_params=_cparams(2),
        name="mod_table",
    )(cond_b, ada_w, ada_b.reshape(DEPTH, 1, 6 * D))


def _k1_kernel(x_ref, mod_ref, nw_ref, w_ref, wdtT_ref,
               o_lru, o_hy, o_z, o_xbc, o_dt, o_dtT, wbf_ref):
    @pl.when(pl.program_id(0) == 0)
    def _():
        wbf_ref[...] = w_ref[0].astype(BF16)

    m = mod_ref[0]
    h = _rms(x_ref[...]) * nw_ref[0]
    h = h * (1.0 + m[1:2]) + m[0:1]
    hb = h.astype(BF16)

    def proj(lo, hi):
        return jnp.dot(hb, wbf_ref[:, lo:hi], preferred_element_type=F32)

    o_lru[...] = proj(0, 512)
    o_hy[...] = proj(512, 1280)
    o_z[...] = proj(1280, 1792)
    o_xbc[...] = proj(1792, 2560)
    o_dt[...] = proj(D_MAIN, D_IN)
    dtT = lax.dot_general(wdtT_ref[0].astype(BF16), hb, (((1,), (1,)), ((), ())),
                          preferred_element_type=F32)
    for j in range(TM // SSM_Q):
        o_dtT[j] = dtT[:, j * SSM_Q:(j + 1) * SSM_Q]


def _k1(l, x, mod_l, norm1_w, w_in, w_dtT):
    tok = lambda w: pl.BlockSpec((TM, w), lambda i: (i, 0))
    return pl.pallas_call(
        _k1_kernel,
        grid=(NT,),
        in_specs=[
            tok(D),
            pl.BlockSpec((1, 6, D), lambda i: (_mod_row(i), 0, 0)),
            pl.BlockSpec((1, 1, D), lambda i: (l, 0, 0)),
            pl.BlockSpec((1, D, D_IN), lambda i: (l, 0, 0)),
            pl.BlockSpec((1, 2 * SSM_H, D), lambda i: (l, 0, 0)),
        ],
        out_specs=[tok(512), tok(768), tok(512), tok(768), tok(2 * SSM_H),
                   pl.BlockSpec((TM // SSM_Q, 2 * SSM_H, SSM_Q), lambda i: (i, 0, 0))],
        out_shape=[jax.ShapeDtypeStruct((T, 512), F32), jax.ShapeDtypeStruct((T, 768), F32),
                   jax.ShapeDtypeStruct((T, 512), F32), jax.ShapeDtypeStruct((T, 768), F32),
                   jax.ShapeDtypeStruct((T, 2 * SSM_H), F32),
                   jax.ShapeDtypeStruct((T // SSM_Q, 2 * SSM_H, SSM_Q), F32)],
        scratch_shapes=[pltpu.VMEM((D, D_IN), BF16)],
        compiler_params=_cparams(1),
        name=f"k1_inproj_{l}",
    )(x, mod_l, norm1_w.reshape(DEPTH, 1, D), w_in, w_dtT)


def _row_in_seq(rows, lseq):
    return lax.broadcasted_iota(I32, (rows, 1), 0) & (lseq - 1)


def _shift_rows(x, s, rin, lseq):
    if s == 0:
        return x
    y = pltpu.roll(x, s % x.shape[0], axis=0)
    valid = (rin >= s) if s > 0 else (rin < lseq + s)
    return jnp.where(valid, y, 0.0)


def _dwconv(x, w_ref, b_ref, rin, lseq):
    k_w = w_ref.shape[0]
    y = b_ref[...]
    for k in range(k_w):
        y = y + w_ref[k:k + 1, :] * _shift_rows(x, k_w // 2 - k, rin, lseq)
    return y


def _lru_block(u_ref, cw_ref, cb_ref, wbig_ref, bias_ref, lam_ref, h0_ref, o_ref, st_ref, lseq):
    rows = u_ref.shape[0]
    rin = _row_in_seq(rows, lseq)
    u = u_ref[...]
    gate = u[:, D_LRU:]
    x = _dwconv(u[:, :D_LRU], cw_ref, cb_ref, rin, lseq)
    xb = x.astype(BF16)
    y = None
    finals = []
    for d in range(2):
        g = jnp.dot(xb, wbig_ref[:, 512 * d:512 * (d + 1)], preferred_element_type=F32)
        g = g + bias_ref[:, 512 * d:512 * (d + 1)]
        r = jax.nn.sigmoid(g[:, :D_LRU])
        ig = jax.nn.sigmoid(g[:, D_LRU:])
        log_a = -LRU_C * r * jax.nn.softplus(-lam_ref[d:d + 1, :])
        a = jnp.exp(log_a)
        th = jnp.tanh(log_a)
        b = jnp.sqrt(-2.0 * th / (1.0 - th)) * (ig * x)
        if h0_ref is not None:
            edge = (rin == 0) if d == 0 else (rin == lseq - 1)
            b = b + jnp.where(edge, a * h0_ref[d:d + 1, :], 0.0)
        s = 1
        while s < lseq:
            sh = (s if d == 0 else -s) % rows
            valid = (rin >= s) if d == 0 else (rin < lseq - s)
            b = b + jnp.where(valid, a * pltpu.roll(b, sh, axis=0), 0.0)
            if 2 * s < lseq:
                a = jnp.where(valid, a * pltpu.roll(a, sh, axis=0), a)
            s *= 2
        y = b if y is None else y + b
        if st_ref is not None:
            last = lseq - 1 if d == 0 else 0
            finals.append(jnp.concatenate(
                [b[j * lseq + last:j * lseq + last + 1, :] for j in range(rows // lseq)], axis=0))
    o_ref[...] = y * jax.nn.gelu(gate)
    if st_ref is not None:
        st_ref[...] = jnp.concatenate(finals, axis=1)


def _lru_kernel(u_ref, cw_ref, cb_ref, wbig_ref, bias_ref, lam_ref, h0_ref, o_ref, st_ref):
    b = pl.program_id(0)
    args = (u_ref, cw_ref.at[0], cb_ref.at[0], wbig_ref.at[0], bias_ref.at[0], lam_ref.at[0])

    @pl.when(b < NB_CTX)
    def _():
        _lru_block(*args, None, o_ref, st_ref, L_CTX)

    @pl.when(b >= NB_CTX)
    def _():
        _lru_block(*args, h0_ref.at[0, 0], o_ref, None, L_LAT)


def _lru_mixer(l, u_lru, p, state_lru):
    lsel = lambda *shape: pl.BlockSpec((1,) + shape, lambda b: (l,) + (0,) * len(shape))
    return pl.pallas_call(
        _lru_kernel,
        grid=(NB,),
        in_specs=[
            pl.BlockSpec((RB, 512), lambda b: (b, 0)),
            lsel(4, D_LRU), lsel(1, D_LRU), lsel(D_LRU, 1024), lsel(1, 1024), lsel(2, D_LRU),
            pl.BlockSpec((1, 1, 2, D_LRU), lambda b: (jnp.maximum(b - NB_CTX, 0), l, 0, 0)),
        ],
        out_specs=[pl.BlockSpec((RB, D_LRU), lambda b: (b, 0)),
                   pl.BlockSpec((RB // L_CTX, 2 * D_LRU), lambda b: (jnp.minimum(b, NB_CTX - 1), 0))],
        out_shape=[jax.ShapeDtypeStruct((T, D_LRU), F32),
                   jax.ShapeDtypeStruct((N_CTX_SEQ, 2 * D_LRU), F32)],
        compiler_params=_cparams(1),
        name=f"lru_mixer_{l}",
    )(u_lru, p["lru_conv_w"], p["lru_conv_b"], p["lru_wbig"], p["lru_bias"], p["lru_lambda"], state_lru)


def _hy_filter_kernel(z_ref, win_ref, f_ref, w1_ref, b1_ref, w2_ref, b2_ref, fr_ref, w3_ref,
                      o_ref, h_ref):
    l = pl.program_id(1)

    @pl.when(pl.program_id(0) == 0)
    def _():
        fr = fr_ref[0]
        g = jnp.sin(fr * (jnp.dot(z_ref[...], w1_ref[0], precision=HI, preferred_element_type=F32)
                          + b1_ref[0]))
        g = jnp.sin(fr * (jnp.dot(g, w2_ref[0], precision=HI, preferred_element_type=F32) + b2_ref[0]))
        h = jnp.dot(g, w3_ref[0], precision=HI, preferred_element_type=F32)
        h_ref[l] = (h * win_ref[...]).astype(BF16)

    o_ref[0] = jnp.dot(f_ref[...], h_ref[l], preferred_element_type=F32)


def _hy_filter_spectrum(lseq, kb, zfeat, window, fmat, p):
    n = fmat.shape[0]
    lsel = lambda *shape: pl.BlockSpec((1,) + shape, lambda k, l: (l,) + (0,) * len(shape))
    return pl.pallas_call(
        _hy_filter_kernel,
        grid=(n // (2 * kb), DEPTH),
        in_specs=[
            pl.BlockSpec((lseq, LANE), lambda k, l: (0, 0)),
            pl.BlockSpec((lseq, D_HY), lambda k, l: (0, 0)),
            pl.BlockSpec((2 * kb, lseq), lambda k, l: (k, 0)),
            lsel(LANE, HY_HID), lsel(1, HY_HID), lsel(HY_HID, HY_HID), lsel(1, HY_HID),
            lsel(1, HY_HID), lsel(HY_HID, D_HY),
        ],
        out_specs=pl.BlockSpec((1, 2 * kb, D_HY), lambda k, l: (l, k, 0)),
        out_shape=jax.ShapeDtypeStruct((DEPTH, n, D_HY), F32),
        scratch_shapes=[pltpu.VMEM((DEPTH, lseq, D_HY), BF16)],
        compiler_params=_cparams(2),
        name=f"hyena_filter_{lseq}",
    )(zfeat, window, fmat, p["hy_w1p"], p["hy_b1"], p["hy_w2"], p["hy_b2"], p["hy_freq"], p["hy_w3"])


def _hy_spectral_block(f_blk, g_blk, hf, z_bf, is_dc_block):
    kb = f_blk.shape[0] // 2
    zf = jnp.dot(f_blk, z_bf, preferred_element_type=F32)
    rz, iz = zf[:kb], zf[kb:]
    rh, ih = hf[:kb], hf[kb:]
    ii = iz * ih
    re = rz * rh - ii
    im = rz * ih + iz * rh
    if is_dc_block is not None:
        dc = jnp.logical_and(lax.broadcasted_iota(I32, (kb, 1), 0) == 0, is_dc_block)
        re = jnp.where(dc, rz * rh, re)
        im = jnp.where(dc, ii, im)
    pr = jnp.concatenate([re, im], axis=0).astype(BF16)
    return jnp.dot(g_blk, pr, preferred_element_type=F32)


def _hy_prologue(u_ref, cw_ref, cb_ref, lseq, z_ref, zbf_ref, x2_ref):
    rin = _row_in_seq(u_ref.shape[0], lseq)
    uc = _dwconv(u_ref[...], cw_ref, cb_ref, rin, lseq)
    z = uc[:, :D_HY] * uc[:, D_HY:2 * D_HY]
    z_ref[...] = z
    zbf_ref[...] = z.astype(BF16)
    x2_ref[...] = uc[:, 2 * D_HY:]


def _hy_kernel(u_ref, cw_ref, cb_ref, hb_ref, fc_ref, gc_ref, hfc_ref, fl_ref, gl_ref, hfl_ref,
               o_ref, z_ref, zbf_ref, x2_ref, acc_ref):
    b = pl.program_id(0)
    k = pl.program_id(1)
    cw, cb = cw_ref.at[0], cb_ref.at[0]

    @pl.when(jnp.logical_and(b < NB_CTX, k == 0))
    def _():
        _hy_prologue(u_ref, cw, cb, L_CTX, z_ref, zbf_ref, x2_ref)
        for s in range(RB // L_CTX):
            rows = slice(s * L_CTX, (s + 1) * L_CTX)
            acc_ref[rows, :] = _hy_spectral_block(fc_ref[...], gc_ref[...], hfc_ref[0], zbf_ref[rows, :], True)

    @pl.when(b >= NB_CTX)
    def _():
        @pl.when(k == 0)
        def _():
            _hy_prologue(u_ref, cw, cb, L_LAT, z_ref, zbf_ref, x2_ref)
            acc_ref[...] = jnp.zeros_like(acc_ref)

        acc_ref[...] += _hy_spectral_block(fl_ref[...], gl_ref[...], hfl_ref[0], zbf_ref[...], k == 0)

    @pl.when(k == HY_NKB - 1)
    def _():
        o_ref[...] = x2_ref[...] * (acc_ref[...] + hb_ref[0] * z_ref[...])


def _hy_mixer(l, u_hy, p, c):
    lat_k = lambda b, k: jnp.where(b < NB_CTX, 0, k)
    lsel = lambda *shape: pl.BlockSpec((1,) + shape, lambda b, k: (l,) + (0,) * len(shape))
    kbl = 2 * HY_KB_LAT
    return pl.pallas_call(
        _hy_kernel,
        grid=(NB, HY_NKB),
        in_specs=[
            pl.BlockSpec((RB, 3 * D_HY), lambda b, k: (b, 0)),
            lsel(3, 3 * D_HY), lsel(1, 3 * D_HY), lsel(1, D_HY),
            pl.BlockSpec((N_FFT_CTX, L_CTX), lambda b, k: (0, 0)),
            pl.BlockSpec((L_CTX, N_FFT_CTX), lambda b, k: (0, 0)),
            lsel(N_FFT_CTX, D_HY),
            pl.BlockSpec((kbl, L_LAT), lambda b, k: (lat_k(b, k), 0)),
            pl.BlockSpec((L_LAT, kbl), lambda b, k: (0, lat_k(b, k))),
            pl.BlockSpec((1, kbl, D_HY), lambda b, k: (l, lat_k(b, k), 0)),
        ],
        out_specs=pl.BlockSpec((RB, D_HY), lambda b, k: (b, 0)),
        out_shape=jax.ShapeDtypeStruct((T, D_HY), F32),
        scratch_shapes=[pltpu.VMEM((RB, D_HY), F32), pltpu.VMEM((RB, D_HY), BF16),
                        pltpu.VMEM((RB, D_HY), F32), pltpu.VMEM((RB, D_HY), F32)],
        compiler_params=_cparams(2),
        name=f"hyena_mixer_{l}",
    )(u_hy, p["hy_conv_w"], p["hy_conv_b"], p["hy_bias"],
      c["f_ctx"], c["g_ctx"], c["hf_ctx"], c["f_lat"], c["g_lat"], c["hf_lat"])


def _ssd_block(u_z, u_xbc, u_dt, u_dtT, cw, cb, dtb_row, dtb_col, alog_row, alog_col, d_exp, nw,
               e64_ref, mbd_ref, mdiag_ref, h0_ref, o_ref, st_ref,
               x_s, bc_s, yb_s, sf_s, sb_s, cs_s, col_s, row_s, lseq):
    y_acc = (o_ref, yb_s)
    s_dir = (sf_s, sb_s)
    rows = u_z.shape[0]
    nchunk = rows // SSM_Q
    cps = lseq // SSM_Q
    rin = _row_in_seq(rows, lseq)
    for c0 in range(0, D_XBC, LANE):
        cols = slice(c0, c0 + LANE)
        xbc = _dwconv(u_xbc[:, cols], cw.at[:, cols], cb.at[:, cols], rin, lseq)
        xbc = xbc * jax.nn.sigmoid(xbc)
        if c0 < D_SSM:
            x_s[:, cols] = xbc
        else:
            bc_s[:, c0 - D_SSM:c0 - D_SSM + LANE] = xbc

    li = lax.broadcasted_iota(I32, (SSM_Q, SSM_Q), 0)
    si = lax.broadcasted_iota(I32, (SSM_Q, SSM_Q), 1)
    low_half = lax.broadcasted_iota(I32, (SSM_Q, LANE), 1) < SSM_N
    hq = SSM_H // SSM_G
    wq = hq * SSM_P
    hp = SSM_H * SSM_P

    for d in range(2):
        for g in range(SSM_G):
            if h0_ref is not None:
                s_dir[d][g] = jnp.tile(h0_ref[d, g * wq:(g + 1) * wq, :], (1, hq))
            else:
                s_dir[d][g] = jnp.zeros((wq, wq), F32)

    nh2 = 2 * SSM_H
    tri_f = (li >= si).astype(BF16)
    tri_b = (li <= si).astype(BF16)
    fwd_lane = lax.broadcasted_iota(I32, (1, nh2), 1) < SSM_H
    fwd_sub = lax.broadcasted_iota(I32, (nh2, 1), 0) < SSM_H
    a_row = -jnp.exp(alog_row[...])
    a_col = -jnp.exp(alog_col[...])
    for c in range(nchunk):
        rs = slice(c * SSM_Q, (c + 1) * SSM_Q)
        dt_c = jax.nn.softplus(u_dt[rs, :] + dtb_row[...])
        hi, lo = _split_bf16(dt_c * a_row, 2)
        cs_f = jnp.dot(tri_f, hi, preferred_element_type=F32) + jnp.dot(tri_f, lo, preferred_element_type=F32)
        cs_b = jnp.dot(tri_b, hi, preferred_element_type=F32) + jnp.dot(tri_b, lo, preferred_element_type=F32)
        cs_col = jnp.where(fwd_lane, cs_f, cs_b)
        tot = jnp.where(fwd_lane, cs_f[SSM_Q - 1:SSM_Q, :], cs_b[0:1, :])
        cs_s[rs, :] = cs_col
        for k, v in enumerate((jnp.exp(cs_col), dt_c * jnp.exp(tot - cs_col))):
            for m, piece in enumerate(_split_bf16(v, 2)):
                col_s[rs, (2 * k + m) * nh2:(2 * k + m + 1) * nh2] = piece
        dt_r = jax.nn.softplus(u_dtT[c] + dtb_col[...])
        hi, lo = _split_bf16(dt_r * a_col, 2)
        csr_f = jnp.dot(hi, tri_b, preferred_element_type=F32) + jnp.dot(lo, tri_b, preferred_element_type=F32)
        csr_b = jnp.dot(hi, tri_f, preferred_element_type=F32) + jnp.dot(lo, tri_f, preferred_element_type=F32)
        row_s[c, :nh2, :] = jnp.where(fwd_sub, csr_f, csr_b)
        row_s[c, nh2:, :] = dt_r

    def chunk_pair(ci, carry):
        for d in range(2):
            causal = (li >= si) if d == 0 else (li <= si)
            edge = SSM_Q - 1 if d == 0 else 0

            c = ci if d == 0 else nchunk - 1 - ci
            r0 = pl.multiple_of(c * SSM_Q, SSM_Q)
            rsl = pl.ds(r0, SSM_Q)
            if h0_ref is None and cps < nchunk:
                first = (c % cps == 0) if d == 0 else (c % cps == cps - 1)
                s_dir[d][...] = s_dir[d][...] * jnp.where(first, 0.0, 1.0)

            cs_col = cs_s[rsl, :]
            rows_c = row_s[c]
            spread = jnp.dot(col_s[rsl, :], e64_ref[d], preferred_element_type=F32)
            ecs_x, wdec_x = spread[:, :hp], spread[:, hp:]
            etot_x = ecs_x[edge:edge + 1, :]

            bcm = bc_s[rsl, :]
            bm, cm = bcm[:, :LANE], bcm[:, LANE:]
            bm_r, cm_r = pltpu.roll(bm, SSM_N, axis=1), pltpu.roll(cm, SSM_N, axis=1)
            bmb, cmb = bm.astype(BF16), cm.astype(BF16)
            for g in range(SSM_G):
                gl = slice(g * SSM_N, (g + 1) * SSM_N)
                ql = slice(g * wq, (g + 1) * wq)
                same = low_half if g == 0 else jnp.logical_not(low_half)
                b2 = jnp.where(same, bm, bm_r)
                c2 = jnp.where(same, cm, cm_r)
                gmat = lax.dot_general(cmb[:, gl], bmb[:, gl], (((1,), (1,)), ((), ())),
                                       preferred_element_type=F32)
                sc = []
                for h in range(SSM_H * d + g * hq, SSM_H * d + (g + 1) * hq):
                    diff = cs_col[:, h:h + 1] - rows_c[h:h + 1, :]
                    decay = jnp.exp(jnp.where(causal, diff, -1e30))
                    sc.append((gmat * decay * rows_c[nh2 + h:nh2 + h + 1, :]).astype(BF16))
                sc = jnp.concatenate(sc, axis=1)
                xq = x_s[rsl, ql]
                bd = jnp.tile(xq.astype(BF16), (hq, 1)) * mbd_ref[...]
                y = jnp.dot(sc, bd, preferred_element_type=F32)
                s_old = s_dir[d][g]
                y_off = lax.dot_general(jnp.concatenate([c2, c2], axis=1).astype(BF16),
                                        (s_old * mdiag_ref[...]).astype(BF16),
                                        (((1,), (1,)), ((), ())), preferred_element_type=F32)
                y_acc[d][rsl, ql] = y + y_off * ecs_x[:, ql]
                bx = (jnp.concatenate([b2, b2], axis=1) * wdec_x[:, ql]).astype(BF16)
                s_new = jnp.dot(xq.T.astype(BF16), bx, preferred_element_type=F32)
                s_dir[d][g] = s_old * etot_x[:, ql] + s_new

            if st_ref is not None:
                last = (c % cps == cps - 1) if d == 0 else (c % cps == 0)

                @pl.when(last)
                def _(c=c, d=d):
                    for g in range(SSM_G):
                        for hl in range(hq):
                            blk = slice(hl * SSM_P, (hl + 1) * SSM_P)
                            st_ref[c // cps, 0, d, g * hq + hl] = s_dir[d][g, blk, blk]
        return carry

    lax.fori_loop(0, nchunk, chunk_pair, 0, unroll=2)

    for r0 in range(0, rows, 256):
        rsl = slice(r0, r0 + 256)
        y = o_ref[rsl, :] + yb_s[rsl, :] + d_exp[...] * x_s[rsl, :]
        z = u_z[rsl, :]
        y = y * (z * jax.nn.sigmoid(z))
        o_ref[rsl, :] = _rms(y) * nw[...]


def _ssd_kernel(u_z, u_xbc, u_dt, u_dtT, cw, cb, dtb_row, dtb_col, alog_row, alog_col, d_exp, nw,
                e64_ref, mbd_ref, mdiag_ref, h0_ref, st_in_ref, o_ref, st_ref,
                x_s, bc_s, yb_s, sf_s, sb_s, cs_s, col_s, row_s):
    del st_in_ref
    b = pl.program_id(0)
    args = (u_z, u_xbc, u_dt, u_dtT, cw.at[0], cb.at[0], dtb_row.at[0], dtb_col.at[0], alog_row.at[0],
            alog_col.at[0], d_exp.at[0], nw.at[0], e64_ref, mbd_ref, mdiag_ref)
    scr = (x_s, bc_s, yb_s, sf_s, sb_s, cs_s, col_s, row_s)

    @pl.when(b < NB_CTX)
    def _():
        _ssd_block(*args, None, o_ref, st_ref, *scr, L_CTX)

    @pl.when(b >= NB_CTX)
    def _():
        _ssd_block(*args, h0_ref.at[0, 0], o_ref, None, *scr, L_LAT)


def _ssd_mixer(l, u_z, u_xbc, u_dt, u_dtT, p, c, state_ssm, new_states):
    lsel = lambda *shape: pl.BlockSpec((1,) + shape, lambda b: (l,) + (0,) * len(shape))
    full = lambda a: pl.BlockSpec(a.shape, lambda b: (0,) * a.ndim, pipeline_mode=pl.Buffered(1))
    hp = SSM_H * SSM_P
    nseq_blk = RB // L_CTX
    return pl.pallas_call(
        _ssd_kernel,
        grid=(NB,),
        in_specs=[
            pl.BlockSpec((RB, D_SSM), lambda b: (b, 0), pipeline_mode=pl.Buffered(1)),
            pl.BlockSpec((RB, D_XBC), lambda b: (b, 0)),
            pl.BlockSpec((RB, 2 * SSM_H), lambda b: (b, 0)),
            pl.BlockSpec((RB // SSM_Q, 2 * SSM_H, SSM_Q), lambda b: (b, 0, 0)),
            lsel(4, D_XBC), lsel(1, D_XBC), lsel(1, 2 * SSM_H), lsel(2 * SSM_H, 1),
            lsel(1, 2 * SSM_H), lsel(2 * SSM_H, 1), lsel(1, D_SSM), lsel(1, D_SSM),
            full(c["e64"]), full(c["mbd"]), full(c["mdiag"]),
            pl.BlockSpec((1, 1, 2, hp, SSM_N), lambda b: (jnp.maximum(b - NB_CTX, 0), l, 0, 0, 0)),
            pl.BlockSpec(memory_space=pl.ANY),
        ],
        out_specs=[pl.BlockSpec((RB, D_SSM), lambda b: (b, 0)),
                   pl.BlockSpec((nseq_blk, 1, 2, SSM_H, SSM_P, SSM_N),
                                lambda b: (jnp.minimum(b, NB_CTX - 1), l, 0, 0, 0, 0))],
        out_shape=[jax.ShapeDtypeStruct((T, D_SSM), F32),
                   jax.ShapeDtypeStruct(new_states.shape, F32)],
        input_output_aliases={16: 1},
        scratch_shapes=[pltpu.VMEM((RB, D_SSM), F32), pltpu.VMEM((RB, 2 * SSM_G * SSM_N), F32),
                        pltpu.VMEM((RB, D_SSM), F32),
                        pltpu.VMEM((SSM_G, hp // SSM_G, hp // SSM_G), F32),
                        pltpu.VMEM((SSM_G, hp // SSM_G, hp // SSM_G), F32),
                        pltpu.VMEM((RB, 2 * SSM_H), F32), pltpu.VMEM((RB, 8 * SSM_H), BF16),
                        pltpu.VMEM((RB // SSM_Q, 4 * SSM_H, SSM_Q), F32)],
        compiler_params=_cparams(1),
        name=f"ssd_mixer_{l}",
    )(u_z, u_xbc, u_dt, u_dtT, p["ssm_conv_w"], p["ssm_conv_b"], p["ssm_dtb_row"], p["ssm_dtb_col"],
      p["ssm_alog_row"], p["ssm_alog_col"], p["ssm_d_exp"], p["ssm_norm_w"],
      c["e64"], c["mbd"], c["mdiag"], state_ssm, new_states)


def _k2_kernel(*refs, routed):
    if routed:
        (ol_ref, oh_ref, os_ref, x_ref, mod_ref, nw_ref, w_ref, rt_ref,
         xo_ref, h2_ref, route_ref, cnt_ref, wbf_ref) = refs
    else:
        ol_ref, oh_ref, os_ref, x_ref, mod_ref, nw_ref, w_ref, xo_ref, h2_ref, wbf_ref = refs

    @pl.when(pl.program_id(0) == 0)
    def _():
        wbf_ref[...] = w_ref[0].astype(BF16)

    m = mod_ref[0]
    o = jnp.dot(ol_ref[...].astype(BF16), wbf_ref[0:256, :], preferred_element_type=F32)
    o = o + jnp.dot(oh_ref[...].astype(BF16), wbf_ref[256:512, :], preferred_element_type=F32)
    o = o + jnp.dot(os_ref[...].astype(BF16), wbf_ref[512:1024, :], preferred_element_type=F32)
    x = x_ref[...] + m[2:3] * o
    xo_ref[...] = x
    h2 = _rms(x) * nw_ref[0]
    h2 = h2 * (1.0 + m[4:5]) + m[3:4]
    h2_hi = h2.astype(BF16)
    h2_ref[...] = h2_hi

    if routed:
        h2_lo = (h2 - h2_hi.astype(F32)).astype(BF16)
        r_hi, r_lo = _split_bf16(rt_ref[0], 2)
        logits = (jnp.dot(h2_hi, r_hi, preferred_element_type=F32)
                  + jnp.dot(h2_lo, r_hi, preferred_element_type=F32)
                  + jnp.dot(h2_hi, r_lo, preferred_element_type=F32))
        eid = lax.broadcasted_iota(I32, logits.shape, 1)
        m1 = jnp.max(logits, axis=1, keepdims=True)
        i1 = jnp.min(jnp.where(logits == m1, eid, N_EXP), axis=1, keepdims=True)
        rest = jnp.where(eid == i1, -jnp.inf, logits)
        m2 = jnp.max(rest, axis=1, keepdims=True)
        i2 = jnp.min(jnp.where(rest == m2, eid, N_EXP), axis=1, keepdims=True)
        w1 = 1.0 / (1.0 + jnp.exp(m2 - m1))
        w2 = 1.0 - w1
        oh1 = (eid == i1).astype(F32)
        oh2 = (eid == i2).astype(F32)
        both = oh1 + oh2
        before = (lax.broadcasted_iota(I32, (TM, TM), 0) > lax.broadcasted_iota(I32, (TM, TM), 1))
        ahead = jnp.dot(before.astype(BF16), both.astype(BF16), preferred_element_type=F32)
        r1 = jnp.sum(oh1 * ahead, axis=1, keepdims=True)
        r2 = jnp.sum(oh2 * ahead, axis=1, keepdims=True)
        zero = jnp.zeros_like(w1)
        route_ref[...] = jnp.concatenate(
            [i1.astype(F32), i2.astype(F32), r1, r2, w1, w2, zero, zero], axis=1)
        cnt_ref[0] = jnp.sum(both, axis=0, keepdims=True)


def _k2(l, o_lru, o_hy, o_ssm, x, mod_l, norm2_w, w_out, router=None, j=0):
    routed = router is not None
    tok = lambda w: pl.BlockSpec((TM, w), lambda i: (i, 0))
    in_specs = [
        tok(D_LRU), tok(D_HY), tok(D_SSM), tok(D),
        pl.BlockSpec((1, 6, D), lambda i: (_mod_row(i), 0, 0)),
        pl.BlockSpec((1, 1, D), lambda i: (l, 0, 0)),
        pl.BlockSpec((1, D, D), lambda i: (l, 0, 0)),
    ]
    args = [o_lru, o_hy, o_ssm, x, mod_l, norm2_w.reshape(DEPTH, 1, D), w_out]
    out_specs = [tok(D), tok(D)]
    out_shape = [jax.ShapeDtypeStruct((T, D), F32), jax.ShapeDtypeStruct((T, D), BF16)]
    if routed:
        in_specs.append(pl.BlockSpec((1, D, N_EXP), lambda i: (j, 0, 0)))
        args.append(router)
        out_specs += [tok(8), pl.BlockSpec((1, 1, N_EXP), lambda i: (i, 0, 0))]
        out_shape += [jax.ShapeDtypeStruct((T, 8), F32), jax.ShapeDtypeStruct((NT, 1, N_EXP), F32)]
    return pl.pallas_call(
        functools.partial(_k2_kernel, routed=routed),
        grid=(NT,),
        in_specs=in_specs,
        out_specs=out_specs,
        out_shape=out_shape,
        scratch_shapes=[pltpu.VMEM((D, D), BF16)],
        compiler_params=_cparams(1),
        name=f"k2_outproj_{l}",
    )(*args)


def _ffn_stream_kernel(be_ref, bi_ref, na_ref, x_ref, *rest, j, dense):
    if dense:
        xres_ref, mod_ref, wg_hbm, wu_hbm, wd_hbm, o_ref, wg_s, wu_s, wd_s, stg_g, stg_u, stg_d, sem = rest
    else:
        wg_hbm, wu_hbm, wd_hbm, o_ref, wg_s, wu_s, wd_s, stg_g, stg_u, stg_d, sem = rest
    del bi_ref
    b = pl.program_id(0)
    n_act = na_ref[0]
    e = be_ref[b]
    active = b < n_act
    load = jnp.logical_and(active, jnp.logical_or(b == 0, be_ref[jnp.maximum(b - 1, 0)] != e))
    e_next = be_ref[jnp.minimum(b + 1, pl.num_programs(0) - 1)]
    feed_next = jnp.logical_and(b + 1 < n_act, e_next != e)

    def copies(ee, c):
        slot = c % FFN_STAGES
        cols = slice(c * FF_CHUNK, (c + 1) * FF_CHUNK)
        return (pltpu.make_async_copy(wg_hbm.at[j, ee, :, cols], stg_g.at[slot], sem.at[0, slot]),
                pltpu.make_async_copy(wu_hbm.at[j, ee, :, cols], stg_u.at[slot], sem.at[1, slot]),
                pltpu.make_async_copy(wd_hbm.at[j, ee, cols, :], stg_d.at[slot], sem.at[2, slot]))

    def start(ee, c):
        for cp in copies(ee, c):
            cp.start()

    def chunk_out(c, x, acc):
        g = jnp.dot(x, wg_s[c], preferred_element_type=F32)
        u = jnp.dot(x, wu_s[c], preferred_element_type=F32)
        hmid = (g * jax.nn.sigmoid(g) * u).astype(BF16)
        part = jnp.dot(hmid, wd_s[c], preferred_element_type=F32)
        return part if acc is None else acc + part

    def finish(acc):
        if dense:
            o_ref[...] = xres_ref[...] + mod_ref[0][5:6] * acc
        else:
            o_ref[...] = acc.astype(o_ref.dtype)

    @pl.when(load)
    def _():
        @pl.when(b == 0)
        def _():
            for c in range(FFN_STAGES):
                start(e, c)

        x = x_ref[...]
        acc = None
        for c in range(N_FF_CHUNK):
            slot = c % FFN_STAGES
            for cp in copies(e, c):
                cp.wait()
            wg_s[c] = stg_g[slot].astype(BF16)
            wu_s[c] = stg_u[slot].astype(BF16)
            wd_s[c] = stg_d[slot].astype(BF16)
            if c + FFN_STAGES < N_FF_CHUNK:
                start(e, c + FFN_STAGES)
            acc = chunk_out(c, x, acc)
        finish(acc)

    @pl.when(jnp.logical_and(active, jnp.logical_not(load)))
    def _():
        x = x_ref[...]
        acc = None
        for c in range(N_FF_CHUNK):
            acc = chunk_out(c, x, acc)
        finish(acc)

    if not dense:
        @pl.when(jnp.logical_not(active))
        def _():
            o_ref[...] = jnp.zeros_like(o_ref)

    @pl.when(feed_next)
    def _():
        for c in range(FFN_STAGES):
            start(e_next, c)


def _ffn_scratch():
    return [pltpu.VMEM((N_FF_CHUNK, D, FF_CHUNK), BF16), pltpu.VMEM((N_FF_CHUNK, D, FF_CHUNK), BF16),
            pltpu.VMEM((N_FF_CHUNK, FF_CHUNK, D), BF16),
            pltpu.VMEM((FFN_STAGES, D, FF_CHUNK), F32), pltpu.VMEM((FFN_STAGES, D, FF_CHUNK), F32),
            pltpu.VMEM((FFN_STAGES, FF_CHUNK, D), F32), pltpu.SemaphoreType.DMA((3, FFN_STAGES))]


def _dense_ffn(j, h2, x, mod_l, wg, wu, wd):
    hbm = pl.BlockSpec(memory_space=pl.ANY)
    zeros = jnp.zeros((NT,), I32)
    return pl.pallas_call(
        functools.partial(_ffn_stream_kernel, j=j, dense=True),
        grid_spec=pltpu.PrefetchScalarGridSpec(
            num_scalar_prefetch=3,
            grid=(NT,),
            in_specs=[
                pl.BlockSpec((TM, D), lambda i, *_: (i, 0)),
                pl.BlockSpec((TM, D), lambda i, *_: (i, 0)),
                pl.BlockSpec((1, 6, D), lambda i, *_: (_mod_row(i), 0, 0)),
                hbm, hbm, hbm,
            ],
            out_specs=pl.BlockSpec((TM, D), lambda i, *_: (i, 0)),
            scratch_shapes=_ffn_scratch(),
        ),
        out_shape=jax.ShapeDtypeStruct((T, D), F32),
        compiler_params=_cparams(1),
        name=f"dense_ffn_{j}",
    )(zeros, zeros, jnp.full((1,), NT, I32), h2, x, mod_l, wg[:, None], wu[:, None], wd[:, None])


def _expert_ffn(j, xs, blk_e, blk_i, n_active, wg, wu, wd):
    hbm = pl.BlockSpec(memory_space=pl.ANY)
    return pl.pallas_call(
        functools.partial(_ffn_stream_kernel, j=j, dense=False),
        grid_spec=pltpu.PrefetchScalarGridSpec(
            num_scalar_prefetch=3,
            grid=(NBLK,),
            in_specs=[pl.BlockSpec((TMB, D), lambda b, be, bi, na: (bi[b], 0)), hbm, hbm, hbm],
            out_specs=pl.BlockSpec((TMB, D), lambda b, be, bi, na: (b, 0)),
            scratch_shapes=_ffn_scratch(),
        ),
        out_shape=jax.ShapeDtypeStruct((RMAX, D), BF16),
        compiler_params=_cparams(1),
        name=f"expert_ffn_{j}",
    )(blk_e, blk_i, n_active, xs, wg, wu, wd)


def _chunk_rows(idx):
    return pl.ds(pl.multiple_of(idx * GCH, GCH), GCH)


def _sort_kernel(gch_ref, nused_ref, pad0_ref, npad_ref, h_ref, dl_ref, xs_ref, xl_ref, zero_ref, sem):
    i = pl.program_id(0)
    slot = i % 2

    def copy(tile, q):
        s = tile % 2
        return pltpu.make_async_copy(xl_ref.at[s, _chunk_rows(q)],
                                     xs_ref.at[_chunk_rows(gch_ref[tile * NLC + q])], sem.at[s])

    def wait_tile(tile):
        lax.fori_loop(0, nused_ref[tile], lambda q, c: (copy(tile, q).wait(), c)[1], 0)

    @pl.when(i >= 2)
    def _():
        wait_tile(i - 2)

    dl = dl_ref[0]
    r = lax.broadcasted_iota(I32, (LCAP, TM), 0).astype(F32)
    perm = jnp.logical_or(r == dl[0:1, :], r == dl[1:2, :]).astype(BF16)
    xl_ref[slot] = jnp.dot(perm, h_ref[...], preferred_element_type=F32).astype(BF16)
    lax.fori_loop(0, nused_ref[i], lambda q, c: (copy(i, q).start(), c)[1], 0)

    @pl.when(i == NT - 1)
    def _():
        zero_ref[...] = jnp.zeros_like(zero_ref)
        for e in range(N_EXP + 1):
            def zcopy(q, e=e):
                return pltpu.make_async_copy(zero_ref, xs_ref.at[_chunk_rows(pad0_ref[e] + q)], sem.at[2])
            lax.fori_loop(0, npad_ref[e], lambda q, c, f=zcopy: (f(q).start(), c)[1], 0)
            lax.fori_loop(0, npad_ref[e], lambda q, c, f=zcopy: (f(q).wait(), c)[1], 0)
        wait_tile(i - 1)
        wait_tile(i)


def _sort_tokens(h2, dl_row, gch, nused, pad0, npad):
    return pl.pallas_call(
        _sort_kernel,
        grid_spec=pltpu.PrefetchScalarGridSpec(
            num_scalar_prefetch=4,
            grid=(NT,),
            in_specs=[
                pl.BlockSpec((TM, D), lambda i, *_: (i, 0)),
                pl.BlockSpec((1, 2, TM), lambda i, *_: (i, 0, 0)),
            ],
            out_specs=pl.BlockSpec(memory_space=pl.ANY),
            scratch_shapes=[pltpu.VMEM((2, LCAP, D), BF16), pltpu.VMEM((GCH, D), BF16),
                            pltpu.SemaphoreType.DMA((3,))],
        ),
        out_shape=jax.ShapeDtypeStruct((RMAX, D), BF16),
        compiler_params=_cparams(1),
        name="moe_sort",
    )(gch, nused, pad0, npad, h2, dl_row)


def _combine_kernel(gch_ref, nused_ref, y_ref, info_ref, x_ref, mod_ref, fw_ref, *rest, final):
    if final:
        oc_ref, ol_ref, yl_ref, sem = rest
    else:
        o_ref, yl_ref, sem = rest
    i = pl.program_id(0)
    slot = i % 2

    def copy(tile, q):
        s = tile % 2
        return pltpu.make_async_copy(y_ref.at[_chunk_rows(gch_ref[tile * NLC + q])],
                                     yl_ref.at[s, _chunk_rows(q)], sem.at[s])

    def fetch(tile):
        lax.fori_loop(0, nused_ref[tile], lambda q, c: (copy(tile, q).start(), c)[1], 0)

    @pl.when(i == 0)
    def _():
        fetch(i)

    @pl.when(i + 1 < NT)
    def _():
        fetch(i + 1)

    n = nused_ref[i]

    def clear(q, c):
        yl_ref[slot, _chunk_rows(q), :] = jnp.zeros((GCH, D), BF16)
        return c

    lax.fori_loop(n, NLC, clear, 0)
    lax.fori_loop(0, n, lambda q, c: (copy(i, q).wait(), c)[1], 0)

    info = info_ref[...]
    col = lax.broadcasted_iota(I32, (TM, LCAP), 1).astype(F32)
    yl = yl_ref[slot]
    y1 = jnp.dot((col == info[:, 0:1]).astype(BF16), yl, preferred_element_type=F32)
    y2 = jnp.dot((col == info[:, 1:2]).astype(BF16), yl, preferred_element_type=F32)
    x = x_ref[...] + mod_ref[0][5:6] * (info[:, 2:3] * y1 + info[:, 3:4] * y2)
    if final:
        x = _rms(x) * fw_ref[...]

        @pl.when(i < NT_CTX)
        def _():
            oc_ref[...] = x

        @pl.when(i >= NT_CTX)
        def _():
            ol_ref[...] = x
    else:
        o_ref[...] = x


def _combine(y, info, x, mod_l, final_w, gch, nused, final):
    tile = pl.BlockSpec((TM, D), lambda i, *_: (i, 0))
    if final:
        out_specs = [pl.BlockSpec((TM, D), lambda i, *_: (jnp.minimum(i, NT_CTX - 1), 0)),
                     pl.BlockSpec((TM, D), lambda i, *_: (jnp.maximum(i - NT_CTX, 0), 0))]
        out_shape = [jax.ShapeDtypeStruct((T_CTX, D), F32), jax.ShapeDtypeStruct((T_LAT, D), F32)]
    else:
        out_specs, out_shape = tile, jax.ShapeDtypeStruct((T, D), F32)
    return pl.pallas_call(
        functools.partial(_combine_kernel, final=final),
        grid_spec=pltpu.PrefetchScalarGridSpec(
            num_scalar_prefetch=2,
            grid=(NT,),
            in_specs=[
                pl.BlockSpec(memory_space=pl.ANY),
                pl.BlockSpec((TM, 4), lambda i, *_: (i, 0)),
                tile,
                pl.BlockSpec((1, 6, D), lambda i, *_: (_mod_row(i), 0, 0)),
                pl.BlockSpec((1, D), lambda i, *_: (0, 0)),
            ],
            out_specs=out_specs,
            scratch_shapes=[pltpu.VMEM((2, LCAP, D), BF16), pltpu.SemaphoreType.DMA((2,))],
        ),
        out_shape=out_shape,
        compiler_params=_cparams(1),
        name="moe_combine",
    )(gch, nused, y, info, x, mod_l, final_w.reshape(1, D))


def _moe_plan(route, counts):
    cnt = counts.reshape(NT, N_EXP).astype(I32)
    cpad = (cnt + GCH - 1) // GCH * GCH
    lo = jnp.cumsum(cpad, axis=1) - cpad
    nused = (lo[:, -1] + cpad[:, -1]) // GCH
    tot = jnp.sum(cpad, axis=0)
    gpad = (tot + TMB - 1) // TMB * TMB
    goff = jnp.cumsum(gpad) - gpad
    so = goff[None, :] + jnp.cumsum(cpad, axis=0) - cpad

    e1 = route[:, 0].astype(I32)
    e2 = route[:, 1].astype(I32)
    eid = jnp.arange(N_EXP, dtype=I32)[None, :]
    lo_tok = jnp.repeat(lo, TM, axis=0)
    dl1 = jnp.sum(jnp.where(e1[:, None] == eid, lo_tok, 0), axis=1).astype(F32) + route[:, 2]
    dl2 = jnp.sum(jnp.where(e2[:, None] == eid, lo_tok, 0), axis=1).astype(F32) + route[:, 3]
    dl_row = jnp.stack([dl1.reshape(NT, TM), dl2.reshape(NT, TM)], axis=1)
    info = jnp.stack([dl1, dl2, route[:, 4], route[:, 5]], axis=1)

    q = jnp.arange(NLC, dtype=I32)[None, :, None]
    lo16 = (lo // GCH)[:, None, :]
    c16 = (cpad // GCH)[:, None, :]
    in_seg = jnp.logical_and(q >= lo16, q < lo16 + c16)
    gch = jnp.sum(jnp.where(in_seg, (so // GCH)[:, None, :] + q - lo16, 0), axis=2).reshape(NT * NLC)

    nblk = gpad // TMB
    n_active = jnp.sum(nblk)
    b = jnp.arange(NBLK, dtype=I32)
    blk_i = jnp.maximum(jnp.minimum(b, n_active - 1), 0)
    bend = (goff + gpad) // TMB
    blk_e = jnp.minimum(jnp.sum((blk_i[:, None] >= bend[None, :]).astype(I32), axis=1), N_EXP - 1)
    used = n_active * TMB
    pad0 = jnp.concatenate([goff + tot, used.reshape(1)]) // GCH
    npad = jnp.concatenate([gpad - tot, (RMAX - used).reshape(1)]) // GCH
    return dict(dl_row=dl_row, info=info, gch=gch.astype(I32), nused=nused.astype(I32),
                blk_e=blk_e.astype(I32), blk_i=blk_i.astype(I32),
                n_active=n_active.reshape(1).astype(I32), pad0=pad0.astype(I32), npad=npad.astype(I32))


def _grid_pos_embed(n_tokens):
    rows = n_tokens // GRID_W
    r = jnp.repeat(jnp.arange(rows, dtype=F32), GRID_W)
    col = jnp.tile(jnp.arange(GRID_W, dtype=F32), rows)
    quarter = D // 4
    omega = 1.0 / (10000.0 ** (jnp.arange(quarter, dtype=F32) / quarter))
    ang_r = r[:, None] * omega[None]
    ang_c = col[:, None] * omega[None]
    return jnp.concatenate([jnp.sin(ang_r), jnp.cos(ang_r), jnp.sin(ang_c), jnp.cos(ang_c)], axis=-1)


def _hy_pos_features(n):
    pos = jnp.arange(n, dtype=F32)
    t = pos / (n - 1)
    bands = jnp.linspace(1e-4, HY_BANDS - 1, HY_BANDS, dtype=F32)
    ang = (2.0 * math.pi * pos / n)[:, None] * bands[None]
    z = jnp.concatenate([t[:, None], jnp.cos(ang), -jnp.sin(ang)], axis=-1)
    z = jnp.pad(z, ((0, 0), (0, LANE - HY_POS_DIM)))
    half = n // 2
    dist = jnp.abs(pos - half) / half
    deltas = jnp.abs(jnp.linspace(HY_MIN_DECAY, HY_MAX_DECAY, D_HY, dtype=F32))
    return z, jnp.exp(-dist[:, None] * deltas[None])


def _dft_mats(lseq, n, kb):
    nkb = n // 2 // kb
    t = jnp.arange(lseq, dtype=I32)

    def tables(tt):
        def cs(freq):
            ang = (2.0 * math.pi / n) * ((freq[:, None] * tt[None, :]) % n).astype(F32)
            return jnp.cos(ang), jnp.sin(ang)
        (ca, sa), (cb, sb) = cs(jnp.arange(nkb, dtype=I32) * kb), cs(jnp.arange(kb, dtype=I32))
        ca, sa, cb, sb = ca[:, None, :], sa[:, None, :], cb[None], sb[None]
        re = ca * cb - sa * sb
        im = -(sa * cb + ca * sb)
        dc = jnp.logical_and(jnp.arange(nkb)[:, None, None] == 0, jnp.arange(kb)[None, :, None] == 0)
        alt = (1.0 - 2.0 * (tt % 2).astype(F32))[None, None, :]
        return re, jnp.where(dc, alt, im), dc

    re, im, _ = tables(t)
    f = jnp.concatenate([re, im], axis=1).reshape(n, lseq)
    re, im, dc = tables(t + lseq // 2)
    wk = jnp.where(dc, 1.0, 2.0) / n
    gt = jnp.concatenate([wk * re, jnp.where(dc, 1.0 / n, wk) * im], axis=1).reshape(n, lseq)
    return f.astype(BF16), gt.T.astype(BF16)


def _ssd_constants():
    hp = SSM_H * SSM_P
    hq = SSM_H // SSM_G
    e64 = np.zeros((2, 8 * SSM_H, 2 * hp), np.float32)
    for d in range(2):
        for q in range(4):
            for hh in range(SSM_H):
                e64[d, q * 2 * SSM_H + d * SSM_H + hh, (q // 2) * hp + hh * SSM_P:(q // 2) * hp + (hh + 1) * SSM_P] = 1.0
    hq_of = np.arange(hq * SSM_P) // SSM_P
    mbd = (np.arange(hq * SSM_Q)[:, None] // SSM_Q == hq_of[None, :]).astype(np.float32)
    mdiag = (hq_of[:, None] == hq_of[None, :]).astype(np.float32)
    return dict(e64=jnp.asarray(e64, dtype=BF16), mbd=jnp.asarray(mbd, dtype=BF16), mdiag=jnp.asarray(mdiag))


def _block_diag_heads(w):
    eye = jnp.eye(LRU_HEADS, dtype=w.dtype)
    return jnp.einsum("ldhij,hg->ldhigj", w, eye).reshape(DEPTH, 2, D_LRU, D_LRU)


def kernel(x_prompt, x_sample, state_lru, state_ssm, c, c_ctx, norm1_w, norm2_w, final_norm_w, ada_w, ada_b,
           w_in, w_out, lru_conv_w, lru_conv_b, lru_wa, lru_ba, lru_wi, lru_bi, lru_lambda, hy_conv_w, hy_conv_b,
           hy_w1, hy_b1, hy_w2, hy_b2, hy_freq, hy_w3, hy_bias, ssm_conv_w, ssm_conv_b, ssm_dt_bias, ssm_a_log,
           ssm_d, ssm_norm_w, ffn_w_gate, ffn_w_up, ffn_w_down, moe_router, moe_w_gate, moe_w_up, moe_w_down):
    hp = SSM_H * SSM_P
    wa, wi = _block_diag_heads(lru_wa), _block_diag_heads(lru_wi)
    row = lambda a: a.reshape(DEPTH, 1, -1)
    p = {
        "lru_conv_w": lru_conv_w, "lru_conv_b": row(lru_conv_b), "lru_lambda": lru_lambda,
        "lru_wbig": jnp.concatenate([wa[:, 0], wi[:, 0], wa[:, 1], wi[:, 1]], axis=-1).astype(BF16),
        "lru_bias": jnp.concatenate([lru_ba[:, 0], lru_bi[:, 0], lru_ba[:, 1], lru_bi[:, 1]], axis=-1)[:, None],
        "hy_conv_w": hy_conv_w, "hy_conv_b": row(hy_conv_b), "hy_bias": row(hy_bias),
        "hy_w1p": jnp.pad(hy_w1, ((0, 0), (0, LANE - HY_POS_DIM), (0, 0))), "hy_b1": row(hy_b1),
        "hy_w2": hy_w2, "hy_b2": row(hy_b2), "hy_freq": row(hy_freq), "hy_w3": hy_w3,
        "ssm_conv_w": ssm_conv_w, "ssm_conv_b": row(ssm_conv_b),
        "ssm_dtb_row": row(ssm_dt_bias), "ssm_dtb_col": ssm_dt_bias.reshape(DEPTH, 2 * SSM_H, 1),
        "ssm_alog_row": row(ssm_a_log), "ssm_alog_col": ssm_a_log.reshape(DEPTH, 2 * SSM_H, 1),
        "ssm_d_exp": jnp.repeat(ssm_d, SSM_P, axis=-1)[:, None], "ssm_norm_w": row(ssm_norm_w),
    }
    w_dtT = jnp.swapaxes(w_in[:, :, D_MAIN:], 1, 2)
    ffn_w = (ffn_w_gate, ffn_w_up, ffn_w_down)
    moe_w = (moe_w_gate, moe_w_up, moe_w_down)

    cst = _ssd_constants()
    z_ctx, win_ctx = _hy_pos_features(L_CTX)
    z_lat, win_lat = _hy_pos_features(L_LAT)
    cst["f_ctx"], cst["g_ctx"] = _dft_mats(L_CTX, N_FFT_CTX, N_FFT_CTX // 2)
    cst["f_lat"], cst["g_lat"] = _dft_mats(L_LAT, N_FFT_LAT, HY_KB_LAT)
    cst["hf_ctx"] = _hy_filter_spectrum(L_CTX, N_FFT_CTX // 2, z_ctx, win_ctx, cst["f_ctx"], p)
    cst["hf_lat"] = _hy_filter_spectrum(L_LAT, HY_KB_LAT, z_lat, win_lat, cst["f_lat"], p)

    cond = jnp.concatenate([c_ctx[None], c], axis=0)
    mod = _mod_table(jnp.broadcast_to(cond[:, :, None], (3, D, LANE)), ada_w, ada_b)
    mod = mod[:, :3].reshape(DEPTH, 3, 6, D)

    x = jnp.concatenate([x_prompt.reshape(T_CTX, D),
                         (x_sample + _grid_pos_embed(L_LAT)[None]).reshape(T_LAT, D)], axis=0)
    st_ssm_in = state_ssm.reshape(N_LAT_SEQ, DEPTH, 2, hp, SSM_N)

    lru_states = []
    new_ssm = jnp.zeros((N_CTX_SEQ, DEPTH, 2, SSM_H, SSM_P, SSM_N), F32)
    for l in range(DEPTH):
        u_lru, u_hy, u_z, u_xbc, u_dt, u_dtT = _k1(l, x, mod[l], norm1_w, w_in, w_dtT)
        o_lru, s_lru = _lru_mixer(l, u_lru, p, state_lru)
        o_hy = _hy_mixer(l, u_hy, p, cst)
        o_ssm, new_ssm = _ssd_mixer(l, u_z, u_xbc, u_dt, u_dtT, p, cst, st_ssm_in, new_ssm)
        lru_states.append(s_lru.reshape(N_CTX_SEQ, 2, D_LRU))
        j = l // 2
        if l % 2 == 0:
            x, h2 = _k2(l, o_lru, o_hy, o_ssm, x, mod[l], norm2_w, w_out)
            x = _dense_ffn(j, h2, x, mod[l], *ffn_w)
        else:
            x, h2, route, counts = _k2(l, o_lru, o_hy, o_ssm, x, mod[l], norm2_w, w_out, moe_router, j)
            plan = _moe_plan(route, counts)
            xs = _sort_tokens(h2, plan["dl_row"], plan["gch"], plan["nused"], plan["pad0"], plan["npad"])
            y = _expert_ffn(j, xs, plan["blk_e"], plan["blk_i"], plan["n_active"], *moe_w)
            x = _combine(y, plan["info"], x, mod[l], final_norm_w, plan["gch"], plan["nused"],
                         final=(l == DEPTH - 1))
    y_prompt = x[0].reshape(N_CTX_SEQ, L_CTX, D)
    y_sample = x[1].reshape(N_LAT_SEQ, L_LAT, D)
    return (y_prompt, y_sample, jnp.stack(lru_states, axis=1), new_ssm)
```

```python
import functools
import math

import numpy as np
import jax
import jax.numpy as jnp
from jax import lax
from jax.experimental import pallas as pl
from jax.experimental.pallas import tpu as pltpu

F32 = jnp.float32
BF16 = jnp.bfloat16
I32 = jnp.int32
HI = lax.Precision.HIGHEST

D = 1024
N_CTX_SEQ, L_CTX = 16, 256
N_LAT_SEQ, L_LAT = 2, 2048
DEPTH = 4
GRID_W = 64
D_LRU = 256
LRU_HEADS, LRU_HD = 4, 64
LRU_C = 8.0
D_HY = 256
HY_BANDS = 16
HY_POS_DIM = 1 + 2 * HY_BANDS
HY_HID = 64
HY_MAX_DECAY = math.log(1e-2) / 0.3
HY_MIN_DECAY = math.log(1e-2) / 1.5
D_SSM = 512
SSM_P = 64
SSM_H = 8
SSM_G = 2
SSM_N = 64
SSM_Q = 128
D_XBC = D_SSM + 2 * SSM_G * SSM_N
D_MAIN = 2 * D_LRU + 3 * D_HY + D_SSM + D_XBC
D_IN = D_MAIN + 2 * SSM_H
D_FF = 2816
N_EXP = 8
EPS = 1e-6

LANE = 128
BF16_ROWS = 16
T_CTX = N_CTX_SEQ * L_CTX
T_LAT = N_LAT_SEQ * L_LAT
T = T_CTX + T_LAT
TM = 512
NT = T // TM
NT_CTX = T_CTX // TM
NT_PER_LAT = L_LAT // TM
RB = 2048
NB = T // RB
NB_CTX = T_CTX // RB
FF_CHUNK = 256
N_FF_CHUNK = D_FF // FF_CHUNK
FFN_STAGES = 3
VMEM_LIMIT = 56 * 1024 * 1024

GCH = BF16_ROWS
LCAP = 2 * TM + N_EXP * GCH
NLC = LCAP // GCH
TMB = 512
RMAX = -(-(2 * T + NT * N_EXP * (GCH - 1) + N_EXP * (TMB - 1)) // TMB) * TMB
NBLK = RMAX // TMB

HY_KB_LAT = 256
N_FFT_LAT = 3 * L_LAT // 2
N_FFT_CTX = 3 * L_CTX // 2
HY_NKB = N_FFT_LAT // 2 // HY_KB_LAT


def _cparams(n_axes=1, vmem=VMEM_LIMIT):
    return pltpu.CompilerParams(dimension_semantics=("arbitrary",) * n_axes, vmem_limit_bytes=vmem)


def _mod_row(i):
    return jnp.where(i < NT_CTX, 0, 1 + (i - NT_CTX) // NT_PER_LAT)


def _bdot(a, b):
    return jnp.dot(a.astype(BF16), b.astype(BF16), preferred_element_type=F32)


def _rms(x):
    return x * lax.rsqrt(jnp.mean(x * x, axis=-1, keepdims=True) + EPS)


def _split_bf16(v, parts):
    out = []
    for _ in range(parts):
        piece = v.astype(BF16)
        out.append(piece)
        v = v - piece.astype(F32)
    return out


def _mod_kernel(cb_ref, w_ref, b_ref, o_ref):
    tn = w_ref.shape[2]

    def body(kc, accs):
        k0 = pl.multiple_of(kc * 8, 8)
        wk = w_ref[0, pl.ds(k0, 8), :]
        out = []
        for r in range(3):
            c = cb_ref[r, pl.ds(k0, 8), :]
            c = c * jax.nn.sigmoid(c)
            out.append(accs[r] + jnp.tile(c, (1, tn // LANE)) * wk)
        return tuple(out)

    accs = lax.fori_loop(0, D // 8, body, tuple(jnp.zeros((8, tn), F32) for _ in range(3)), unroll=8)
    rows = [jnp.sum(a, axis=0, keepdims=True) + b_ref[0] for a in accs]
    o_ref[0] = jnp.concatenate(rows + [jnp.zeros((5, tn), F32)], axis=0)


def _mod_table(cond_b, ada_w, ada_b):
    tn = 1024
    return pl.pallas_call(
        _mod_kernel,
        grid=(DEPTH, 6 * D // tn),
        in_specs=[
            pl.BlockSpec((3, D, LANE), lambda l, j: (0, 0, 0)),
            pl.BlockSpec((1, D, tn), lambda l, j: (l, 0, j)),
            pl.BlockSpec((1, 1, tn), lambda l, j: (l, 0, j)),
        ],
        out_specs=pl.BlockSpec((1, 8, tn), lambda l, j: (l, 0, j)),
        out_shape=jax.ShapeDtypeStruct((DEPTH, 8, 6 * D), F32),
        compiler_params=_cparams(2),
        name="mod_table",
    )(cond_b, ada_w, ada_b.reshape(DEPTH, 1, 6 * D))


def _k1_kernel(x_ref, mod_ref, nw_ref, w_ref, wdtT_ref,
               o_lru, o_hy, o_z, o_xbc, o_dt, o_dtT, wbf_ref):
    @pl.when(pl.program_id(0) == 0)
    def _():
        wbf_ref[...] = w_ref[0].astype(BF16)

    m = mod_ref[0]
    h = _rms(x_ref[...]) * nw_ref[0]
    h = h * (1.0 + m[1:2]) + m[0:1]
    hb = h.astype(BF16)

    def proj(lo, hi):
        return jnp.dot(hb, wbf_ref[:, lo:hi], preferred_element_type=F32)

    o_lru[...] = proj(0, 512)
    o_hy[...] = proj(512, 1280)
    o_z[...] = proj(1280, 1792)
    o_xbc[...] = proj(1792, 2560)
    o_dt[...] = proj(D_MAIN, D_IN)
    dtT = lax.dot_general(wdtT_ref[0].astype(BF16), hb, (((1,), (1,)), ((), ())),
                          preferred_element_type=F32)
    for j in range(TM // SSM_Q):
        o_dtT[j] = dtT[:, j * SSM_Q:(j + 1) * SSM_Q]


def _k1(l, x, mod_l, norm1_w, w_in, w_dtT):
    tok = lambda w: pl.BlockSpec((TM, w), lambda i: (i, 0))
    return pl.pallas_call(
        _k1_kernel,
        grid=(NT,),
        in_specs=[
            tok(D),
            pl.BlockSpec((1, 6, D), lambda i: (_mod_row(i), 0, 0)),
            pl.BlockSpec((1, 1, D), lambda i: (l, 0, 0)),
            pl.BlockSpec((1, D, D_IN), lambda i: (l, 0, 0)),
            pl.BlockSpec((1, 2 * SSM_H, D), lambda i: (l, 0, 0)),
        ],
        out_specs=[tok(512), tok(768), tok(512), tok(768), tok(2 * SSM_H),
                   pl.BlockSpec((TM // SSM_Q, 2 * SSM_H, SSM_Q), lambda i: (i, 0, 0))],
        out_shape=[jax.ShapeDtypeStruct((T, 512), F32), jax.ShapeDtypeStruct((T, 768), F32),
                   jax.ShapeDtypeStruct((T, 512), F32), jax.ShapeDtypeStruct((T, 768), F32),
                   jax.ShapeDtypeStruct((T, 2 * SSM_H), F32),
                   jax.ShapeDtypeStruct((T // SSM_Q, 2 * SSM_H, SSM_Q), F32)],
        scratch_shapes=[pltpu.VMEM((D, D_IN), BF16)],
        compiler_params=_cparams(1),
        name=f"k1_inproj_{l}",
    )(x, mod_l, norm1_w.reshape(DEPTH, 1, D), w_in, w_dtT)


def _row_in_seq(rows, lseq):
    return lax.broadcasted_iota(I32, (rows, 1), 0) & (lseq - 1)


def _shift_rows(x, s, rin, lseq):
    if s == 0:
        return x
    y = pltpu.roll(x, s % x.shape[0], axis=0)
    valid = (rin >= s) if s > 0 else (rin < lseq + s)
    return jnp.where(valid, y, 0.0)


def _dwconv(x, w_ref, b_ref, rin, lseq):
    k_w = w_ref.shape[0]
    y = b_ref[...]
    for k in range(k_w):
        y = y + w_ref[k:k + 1, :] * _shift_rows(x, k_w // 2 - k, rin, lseq)
    return y


def _lru_block(u_ref, cw_ref, cb_ref, wbig_ref, bias_ref, lam_ref, h0_ref, o_ref, st_ref, lseq):
    rows = u_ref.shape[0]
    rin = _row_in_seq(rows, lseq)
    u = u_ref[...]
    gate = u[:, D_LRU:]
    x = _dwconv(u[:, :D_LRU], cw_ref, cb_ref, rin, lseq)
    xb = x.astype(BF16)
    y = None
    finals = []
    for d in range(2):
        g = jnp.dot(xb, wbig_ref[:, 512 * d:512 * (d + 1)], preferred_element_type=F32)
        g = g + bias_ref[:, 512 * d:512 * (d + 1)]
        r = jax.nn.sigmoid(g[:, :D_LRU])
        ig = jax.nn.sigmoid(g[:, D_LRU:])
        log_a = -LRU_C * r * jax.nn.softplus(-lam_ref[d:d + 1, :])
        a = jnp.exp(log_a)
        th = jnp.tanh(log_a)
        b = jnp.sqrt(-2.0 * th / (1.0 - th)) * (ig * x)
        if h0_ref is not None:
            edge = (rin == 0) if d == 0 else (rin == lseq - 1)
            b = b + jnp.where(edge, a * h0_ref[d:d + 1, :], 0.0)
        s = 1
        while s < lseq:
            sh = (s if d == 0 else -s) % rows
            valid = (rin >= s) if d == 0 else (rin < lseq - s)
            b = b + jnp.where(valid, a * pltpu.roll(b, sh, axis=0), 0.0)
            if 2 * s < lseq:
                a = jnp.where(valid, a * pltpu.roll(a, sh, axis=0), a)
            s *= 2
        y = b if y is None else y + b
        if st_ref is not None:
            last = lseq - 1 if d == 0 else 0
            finals.append(jnp.concatenate(
                [b[j * lseq + last:j * lseq + last + 1, :] for j in range(rows // lseq)], axis=0))
    o_ref[...] = y * jax.nn.gelu(gate)
    if st_ref is not None:
        st_ref[...] = jnp.concatenate(finals, axis=1)


def _lru_kernel(u_ref, cw_ref, cb_ref, wbig_ref, bias_ref, lam_ref, h0_ref, o_ref, st_ref):
    b = pl.program_id(0)
    args = (u_ref, cw_ref.at[0], cb_ref.at[0], wbig_ref.at[0], bias_ref.at[0], lam_ref.at[0])

    @pl.when(b < NB_CTX)
    def _():
        _lru_block(*args, None, o_ref, st_ref, L_CTX)

    @pl.when(b >= NB_CTX)
    def _():
        _lru_block(*args, h0_ref.at[0, 0], o_ref, None, L_LAT)


def _lru_mixer(l, u_lru, p, state_lru):
    lsel = lambda *shape: pl.BlockSpec((1,) + shape, lambda b: (l,) + (0,) * len(shape))
    return pl.pallas_call(
        _lru_kernel,
        grid=(NB,),
        in_specs=[
            pl.BlockSpec((RB, 512), lambda b: (b, 0)),
            lsel(4, D_LRU), lsel(1, D_LRU), lsel(D_LRU, 1024), lsel(1, 1024), lsel(2, D_LRU),
            pl.BlockSpec((1, 1, 2, D_LRU), lambda b: (jnp.maximum(b - NB_CTX, 0), l, 0, 0)),
        ],
        out_specs=[pl.BlockSpec((RB, D_LRU), lambda b: (b, 0)),
                   pl.BlockSpec((RB // L_CTX, 2 * D_LRU), lambda b: (jnp.minimum(b, NB_CTX - 1), 0))],
        out_shape=[jax.ShapeDtypeStruct((T, D_LRU), F32),
                   jax.ShapeDtypeStruct((N_CTX_SEQ, 2 * D_LRU), F32)],
        compiler_params=_cparams(1),
        name=f"lru_mixer_{l}",
    )(u_lru, p["lru_conv_w"], p["lru_conv_b"], p["lru_wbig"], p["lru_bias"], p["lru_lambda"], state_lru)


def _hy_filter_kernel(z_ref, win_ref, f_ref, w1_ref, b1_ref, w2_ref, b2_ref, fr_ref, w3_ref,
                      o_ref, h_ref):
    l = pl.program_id(1)

    @pl.when(pl.program_id(0) == 0)
    def _():
        fr = fr_ref[0]
        g = jnp.sin(fr * (jnp.dot(w1_ref[0], z_ref[...], precision=HI, preferred_element_type=F32)
                          + b1_ref[0]))
        g = jnp.sin(fr * (jnp.dot(w2_ref[0], g, precision=HI, preferred_element_type=F32) + b2_ref[0]))
        h = jnp.dot(w3_ref[0], g, precision=HI, preferred_element_type=F32)
        h_ref[l] = (h * win_ref[...]).astype(BF16)

    o_ref[0] = lax.dot_general(f_ref[...], h_ref[l], (((1,), (1,)), ((), ())), preferred_element_type=F32)


def _hy_filter_spectrum(lseq, kb, zfeat_t, window_t, fmat, p):
    n = fmat.shape[0]
    lsel = lambda *shape: pl.BlockSpec((1,) + shape, lambda k, l: (l,) + (0,) * len(shape))
    return pl.pallas_call(
        _hy_filter_kernel,
        grid=(n // (2 * kb), DEPTH),
        in_specs=[
            pl.BlockSpec((LANE, lseq), lambda k, l: (0, 0)),
            pl.BlockSpec((D_HY, lseq), lambda k, l: (0, 0)),
            pl.BlockSpec((2 * kb, lseq), lambda k, l: (k, 0)),
            lsel(HY_HID, LANE), lsel(HY_HID, 1), lsel(HY_HID, HY_HID), lsel(HY_HID, 1),
            lsel(HY_HID, 1), lsel(D_HY, HY_HID),
        ],
        out_specs=pl.BlockSpec((1, 2 * kb, D_HY), lambda k, l: (l, k, 0)),
        out_shape=jax.ShapeDtypeStruct((DEPTH, n, D_HY), F32),
        scratch_shapes=[pltpu.VMEM((DEPTH, D_HY, lseq), BF16)],
        compiler_params=_cparams(2),
        name=f"hyena_filter_{lseq}",
    )(zfeat_t, window_t, fmat, p["hy_w1t"], p["hy_b1"], p["hy_w2t"], p["hy_b2"], p["hy_freq"], p["hy_w3t"])


def _hy_spectral_block(f_blk, g_blk, hf, z_bf, is_dc_block):
    kb = f_blk.shape[0] // 2
    zf = jnp.dot(f_blk, z_bf, preferred_element_type=F32)
    rz, iz = zf[:kb], zf[kb:]
    rh, ih = hf[:kb], hf[kb:]
    ii = iz * ih
    re = rz * rh - ii
    im = rz * ih + iz * rh
    if is_dc_block is not None:
        dc = jnp.logical_and(lax.broadcasted_iota(I32, (kb, 1), 0) == 0, is_dc_block)
        re = jnp.where(dc, rz * rh, re)
        im = jnp.where(dc, ii, im)
    pr = jnp.concatenate([re, im], axis=0).astype(BF16)
    return jnp.dot(g_blk, pr, preferred_element_type=F32)


def _hy_prologue(u_ref, cw_ref, cb_ref, lseq, z_ref, zbf_ref, x2_ref):
    rin = _row_in_seq(u_ref.shape[0], lseq)
    uc = _dwconv(u_ref[...], cw_ref, cb_ref, rin, lseq)
    z = uc[:, :D_HY] * uc[:, D_HY:2 * D_HY]
    z_ref[...] = z
    zbf_ref[...] = z.astype(BF16)
    x2_ref[...] = uc[:, 2 * D_HY:]


def _hy_kernel(u_ref, cw_ref, cb_ref, hb_ref, fc_ref, gc_ref, hfc_ref, fl_ref, gl_ref, hfl_ref,
               o_ref, z_ref, zbf_ref, x2_ref, acc_ref):
    b = pl.program_id(0)
    k = pl.program_id(1)
    cw, cb = cw_ref.at[0], cb_ref.at[0]

    @pl.when(jnp.logical_and(b < NB_CTX, k == 0))
    def _():
        _hy_prologue(u_ref, cw, cb, L_CTX, z_ref, zbf_ref, x2_ref)
        for s in range(RB // L_CTX):
            rows = slice(s * L_CTX, (s + 1) * L_CTX)
            acc_ref[rows, :] = _hy_spectral_block(fc_ref[...], gc_ref[...], hfc_ref[0], zbf_ref[rows, :], True)

    @pl.when(b >= NB_CTX)
    def _():
        @pl.when(k == 0)
        def _():
            _hy_prologue(u_ref, cw, cb, L_LAT, z_ref, zbf_ref, x2_ref)
            acc_ref[...] = jnp.zeros_like(acc_ref)

        acc_ref[...] += _hy_spectral_block(fl_ref[...], gl_ref[...], hfl_ref[0], zbf_ref[...], k == 0)

    @pl.when(k == HY_NKB - 1)
    def _():
        o_ref[...] = x2_ref[...] * (acc_ref[...] + hb_ref[0] * z_ref[...])


def _hy_mixer(l, u_hy, p, c):
    lat_k = lambda b, k: jnp.where(b < NB_CTX, 0, k)
    lsel = lambda *shape: pl.BlockSpec((1,) + shape, lambda b, k: (l,) + (0,) * len(shape))
    kbl = 2 * HY_KB_LAT
    return pl.pallas_call(
        _hy_kernel,
        grid=(NB, HY_NKB),
        in_specs=[
            pl.BlockSpec((RB, 3 * D_HY), lambda b, k: (b, 0)),
            lsel(3, 3 * D_HY), lsel(1, 3 * D_HY), lsel(1, D_HY),
            pl.BlockSpec((N_FFT_CTX, L_CTX), lambda b, k: (0, 0)),
            pl.BlockSpec((L_CTX, N_FFT_CTX), lambda b, k: (0, 0)),
            lsel(N_FFT_CTX, D_HY),
            pl.BlockSpec((kbl, L_LAT), lambda b, k: (lat_k(b, k), 0)),
            pl.BlockSpec((L_LAT, kbl), lambda b, k: (0, lat_k(b, k))),
            pl.BlockSpec((1, kbl, D_HY), lambda b, k: (l, lat_k(b, k), 0)),
        ],
        out_specs=pl.BlockSpec((RB, D_HY), lambda b, k: (b, 0)),
        out_shape=jax.ShapeDtypeStruct((T, D_HY), F32),
        scratch_shapes=[pltpu.VMEM((RB, D_HY), F32), pltpu.VMEM((RB, D_HY), BF16),
                        pltpu.VMEM((RB, D_HY), F32), pltpu.VMEM((RB, D_HY), F32)],
        compiler_params=_cparams(2),
        name=f"hyena_mixer_{l}",
    )(u_hy, p["hy_conv_w"], p["hy_conv_b"], p["hy_bias"],
      c["f_ctx"], c["g_ctx"], c["hf_ctx"], c["f_lat"], c["g_lat"], c["hf_lat"])


def _ssd_block(u_z, u_xbc, u_dt, u_dtT, cw, cb, dtb_row, dtb_col, alog_row, alog_col, d_exp, nw,
               e64_ref, mbd_ref, mdiag_ref, h0_ref, o_ref, st_ref,
               x_s, bc_s, yb_s, sf_s, sb_s, cs_s, col_s, row_s, lseq):
    y_acc = (o_ref, yb_s)
    s_dir = (sf_s, sb_s)
    rows = u_z.shape[0]
    nchunk = rows // SSM_Q
    cps = lseq // SSM_Q
    rin = _row_in_seq(rows, lseq)
    for c0 in range(0, D_XBC, LANE):
        cols = slice(c0, c0 + LANE)
        xbc = _dwconv(u_xbc[:, cols], cw.at[:, cols], cb.at[:, cols], rin, lseq)
        xbc = xbc * jax.nn.sigmoid(xbc)
        if c0 < D_SSM:
            x_s[:, cols] = xbc
        else:
            bc_s[:, c0 - D_SSM:c0 - D_SSM + LANE] = xbc

    li = lax.broadcasted_iota(I32, (SSM_Q, SSM_Q), 0)
    si = lax.broadcasted_iota(I32, (SSM_Q, SSM_Q), 1)
    low_half = lax.broadcasted_iota(I32, (SSM_Q, LANE), 1) < SSM_N
    hq = SSM_H // SSM_G
    wq = hq * SSM_P
    hp = SSM_H * SSM_P

    for d in range(2):
        for g in range(SSM_G):
            if h0_ref is not None:
                s_dir[d][g] = jnp.tile(h0_ref[d, g * wq:(g + 1) * wq, :], (1, hq))
            else:
                s_dir[d][g] = jnp.zeros((wq, wq), F32)

    nh2 = 2 * SSM_H
    tri_f = (li >= si).astype(BF16)
    tri_b = (li <= si).astype(BF16)
    fwd_lane = lax.broadcasted_iota(I32, (1, nh2), 1) < SSM_H
    fwd_sub = lax.broadcasted_iota(I32, (nh2, 1), 0) < SSM_H
    a_row = -jnp.exp(alog_row[...])
    a_col = -jnp.exp(alog_col[...])
    def chunk_stats(c, carry):
        rs = pl.ds(pl.multiple_of(c * SSM_Q, SSM_Q), SSM_Q)
        dt_c = jax.nn.softplus(u_dt[rs, :] + dtb_row[...])
        hi, lo = _split_bf16(dt_c * a_row, 2)
        cs_f = jnp.dot(tri_f, hi, preferred_element_type=F32) + jnp.dot(tri_f, lo, preferred_element_type=F32)
        cs_b = jnp.dot(tri_b, hi, preferred_element_type=F32) + jnp.dot(tri_b, lo, preferred_element_type=F32)
        cs_col = jnp.where(fwd_lane, cs_f, cs_b)
        tot = jnp.where(fwd_lane, cs_f[SSM_Q - 1:SSM_Q, :], cs_b[0:1, :])
        cs_s[rs, :] = cs_col
        for k, v in enumerate((jnp.exp(cs_col), dt_c * jnp.exp(tot - cs_col))):
            for m, piece in enumerate(_split_bf16(v, 2)):
                col_s[rs, (2 * k + m) * nh2:(2 * k + m + 1) * nh2] = piece
        dt_r = jax.nn.softplus(u_dtT[c] + dtb_col[...])
        hi, lo = _split_bf16(dt_r * a_col, 2)
        csr_f = jnp.dot(hi, tri_b, preferred_element_type=F32) + jnp.dot(lo, tri_b, preferred_element_type=F32)
        csr_b = jnp.dot(hi, tri_f, preferred_element_type=F32) + jnp.dot(lo, tri_f, preferred_element_type=F32)
        row_s[c, :nh2, :] = jnp.where(fwd_sub, csr_f, csr_b)
        row_s[c, nh2:, :] = dt_r
        return carry

    lax.fori_loop(0, nchunk, chunk_stats, 0)

    def chunk_step(ci, carry, d):
        if True:
            causal = (li >= si) if d == 0 else (li <= si)
            edge = SSM_Q - 1 if d == 0 else 0

            c = ci if d == 0 else nchunk - 1 - ci
            r0 = pl.multiple_of(c * SSM_Q, SSM_Q)
            rsl = pl.ds(r0, SSM_Q)
            if h0_ref is None and cps < nchunk:
                first = (c % cps == 0) if d == 0 else (c % cps == cps - 1)
                s_dir[d][...] = s_dir[d][...] * jnp.where(first, 0.0, 1.0)

            cs_col = cs_s[rsl, :]
            rows_c = row_s[c]
            spread = jnp.dot(col_s[rsl, :], e64_ref[d], preferred_element_type=F32)
            ecs_x, wdec_x = spread[:, :hp], spread[:, hp:]
            etot_x = ecs_x[edge:edge + 1, :]

            bcm = bc_s[rsl, :]
            bm, cm = bcm[:, :LANE], bcm[:, LANE:]
            bm_r, cm_r = pltpu.roll(bm, SSM_N, axis=1), pltpu.roll(cm, SSM_N, axis=1)
            bmb, cmb = bm.astype(BF16), cm.astype(BF16)
            for g in range(SSM_G):
                gl = slice(g * SSM_N, (g + 1) * SSM_N)
                ql = slice(g * wq, (g + 1) * wq)
                same = low_half if g == 0 else jnp.logical_not(low_half)
                b2 = jnp.where(same, bm, bm_r)
                c2 = jnp.where(same, cm, cm_r)
                gmat = lax.dot_general(cmb[:, gl], bmb[:, gl], (((1,), (1,)), ((), ())),
                                       preferred_element_type=F32)
                sc = []
                for h in range(SSM_H * d + g * hq, SSM_H * d + (g + 1) * hq):
                    diff = cs_col[:, h:h + 1] - rows_c[h:h + 1, :]
                    decay = jnp.exp(jnp.where(causal, diff, -1e30))
                    sc.append((gmat * decay * rows_c[nh2 + h:nh2 + h + 1, :]).astype(BF16))
                sc = jnp.concatenate(sc, axis=1)
                xq = x_s[rsl, ql]
                bd = jnp.tile(xq.astype(BF16), (hq, 1)) * mbd_ref[...]
                y = jnp.dot(sc, bd, preferred_element_type=F32)
                s_old = s_dir[d][g]
                y_off = lax.dot_general(jnp.concatenate([c2, c2], axis=1).astype(BF16),
                                        (s_old * mdiag_ref[...]).astype(BF16),
                                        (((1,), (1,)), ((), ())), preferred_element_type=F32)
                y_acc[d][rsl, ql] = y + y_off * ecs_x[:, ql]
                bx = (jnp.concatenate([b2, b2], axis=1) * wdec_x[:, ql]).astype(BF16)
                s_new = jnp.dot(xq.T.astype(BF16), bx, preferred_element_type=F32)
                s_dir[d][g] = s_old * etot_x[:, ql] + s_new

            if st_ref is not None:
                last = (c % cps == cps - 1) if d == 0 else (c % cps == 0)

                @pl.when(last)
                def _(c=c, d=d):
                    for g in range(SSM_G):
                        for hl in range(hq):
                            blk = slice(hl * SSM_P, (hl + 1) * SSM_P)
                            st_ref[c // cps, 0, d, g * hq + hl] = s_dir[d][g, blk, blk]
        return carry

    for d in range(2):
        lax.fori_loop(0, nchunk, functools.partial(chunk_step, d=d), 0)

    for r0 in range(0, rows, 256):
        rsl = slice(r0, r0 + 256)
        y = o_ref[rsl, :] + yb_s[rsl, :] + d_exp[...] * x_s[rsl, :]
        z = u_z[rsl, :]
        y = y * (z * jax.nn.sigmoid(z))
        o_ref[rsl, :] = _rms(y) * nw[...]


def _ssd_kernel(u_z, u_xbc, u_dt, u_dtT, cw, cb, dtb_row, dtb_col, alog_row, alog_col, d_exp, nw,
                e64_ref, mbd_ref, mdiag_ref, h0_ref, st_in_ref, o_ref, st_ref,
                x_s, bc_s, yb_s, sf_s, sb_s, cs_s, col_s, row_s):
    del st_in_ref
    b = pl.program_id(0)
    args = (u_z, u_xbc, u_dt, u_dtT, cw.at[0], cb.at[0], dtb_row.at[0], dtb_col.at[0], alog_row.at[0],
            alog_col.at[0], d_exp.at[0], nw.at[0], e64_ref, mbd_ref, mdiag_ref)
    scr = (x_s, bc_s, yb_s, sf_s, sb_s, cs_s, col_s, row_s)

    @pl.when(b < NB_CTX)
    def _():
        _ssd_block(*args, None, o_ref, st_ref, *scr, L_CTX)

    @pl.when(b >= NB_CTX)
    def _():
        _ssd_block(*args, h0_ref.at[0, 0], o_ref, None, *scr, L_LAT)


def _ssd_mixer(l, u_z, u_xbc, u_dt, u_dtT, p, c, state_ssm, new_states):
    lsel = lambda *shape: pl.BlockSpec((1,) + shape, lambda b: (l,) + (0,) * len(shape))
    full = lambda a: pl.BlockSpec(a.shape, lambda b: (0,) * a.ndim, pipeline_mode=pl.Buffered(1))
    hp = SSM_H * SSM_P
    nseq_blk = RB // L_CTX
    return pl.pallas_call(
        _ssd_kernel,
        grid=(NB,),
        in_specs=[
            pl.BlockSpec((RB, D_SSM), lambda b: (b, 0), pipeline_mode=pl.Buffered(1)),
            pl.BlockSpec((RB, D_XBC), lambda b: (b, 0)),
            pl.BlockSpec((RB, 2 * SSM_H), lambda b: (b, 0)),
            pl.BlockSpec((RB // SSM_Q, 2 * SSM_H, SSM_Q), lambda b: (b, 0, 0)),
            lsel(4, D_XBC), lsel(1, D_XBC), lsel(1, 2 * SSM_H), lsel(2 * SSM_H, 1),
            lsel(1, 2 * SSM_H), lsel(2 * SSM_H, 1), lsel(1, D_SSM), lsel(1, D_SSM),
            full(c["e64"]), full(c["mbd"]), full(c["mdiag"]),
            pl.BlockSpec((1, 1, 2, hp, SSM_N), lambda b: (jnp.maximum(b - NB_CTX, 0), l, 0, 0, 0)),
            pl.BlockSpec(memory_space=pl.ANY),
        ],
        out_specs=[pl.BlockSpec((RB, D_SSM), lambda b: (b, 0)),
                   pl.BlockSpec((nseq_blk, 1, 2, SSM_H, SSM_P, SSM_N),
                                lambda b: (jnp.minimum(b, NB_CTX - 1), l, 0, 0, 0, 0))],
        out_shape=[jax.ShapeDtypeStruct((T, D_SSM), F32),
                   jax.ShapeDtypeStruct(new_states.shape, F32)],
        input_output_aliases={16: 1},
        scratch_shapes=[pltpu.VMEM((RB, D_SSM), F32), pltpu.VMEM((RB, 2 * SSM_G * SSM_N), F32),
                        pltpu.VMEM((RB, D_SSM), F32),
                        pltpu.VMEM((SSM_G, hp // SSM_G, hp // SSM_G), F32),
                        pltpu.VMEM((SSM_G, hp // SSM_G, hp // SSM_G), F32),
                        pltpu.VMEM((RB, 2 * SSM_H), F32), pltpu.VMEM((RB, 8 * SSM_H), BF16),
                        pltpu.VMEM((RB // SSM_Q, 4 * SSM_H, SSM_Q), F32)],
        compiler_params=_cparams(1),
        name=f"ssd_mixer_{l}",
    )(u_z, u_xbc, u_dt, u_dtT, p["ssm_conv_w"], p["ssm_conv_b"], p["ssm_dtb_row"], p["ssm_dtb_col"],
      p["ssm_alog_row"], p["ssm_alog_col"], p["ssm_d_exp"], p["ssm_norm_w"],
      c["e64"], c["mbd"], c["mdiag"], state_ssm, new_states)


def _k2_kernel(*refs, routed):
    if routed:
        (ol_ref, oh_ref, os_ref, x_ref, mod_ref, nw_ref, w_ref, rt_ref,
         xo_ref, h2_ref, route_ref, cnt_ref, wbf_ref) = refs
    else:
        ol_ref, oh_ref, os_ref, x_ref, mod_ref, nw_ref, w_ref, xo_ref, h2_ref, wbf_ref = refs

    @pl.when(pl.program_id(0) == 0)
    def _():
        wbf_ref[...] = w_ref[0].astype(BF16)

    m = mod_ref[0]
    o = jnp.dot(ol_ref[...].astype(BF16), wbf_ref[0:256, :], preferred_element_type=F32)
    o = o + jnp.dot(oh_ref[...].astype(BF16), wbf_ref[256:512, :], preferred_element_type=F32)
    o = o + jnp.dot(os_ref[...].astype(BF16), wbf_ref[512:1024, :], preferred_element_type=F32)
    x = x_ref[...] + m[2:3] * o
    xo_ref[...] = x
    h2 = _rms(x) * nw_ref[0]
    h2 = h2 * (1.0 + m[4:5]) + m[3:4]
    h2_hi = h2.astype(BF16)
    h2_ref[...] = h2_hi

    if routed:
        h2_lo = (h2 - h2_hi.astype(F32)).astype(BF16)
        r_hi, r_lo = _split_bf16(rt_ref[0], 2)
        logits = (jnp.dot(h2_hi, r_hi, preferred_element_type=F32)
                  + jnp.dot(h2_lo, r_hi, preferred_element_type=F32)
                  + jnp.dot(h2_hi, r_lo, preferred_element_type=F32))
        eid = lax.broadcasted_iota(I32, logits.shape, 1)
        m1 = jnp.max(logits, axis=1, keepdims=True)
        i1 = jnp.min(jnp.where(logits == m1, eid, N_EXP), axis=1, keepdims=True)
        rest = jnp.where(eid == i1, -jnp.inf, logits)
        m2 = jnp.max(rest, axis=1, keepdims=True)
        i2 = jnp.min(jnp.where(rest == m2, eid, N_EXP), axis=1, keepdims=True)
        w1 = 1.0 / (1.0 + jnp.exp(m2 - m1))
        w2 = 1.0 - w1
        oh1 = (eid == i1).astype(F32)
        oh2 = (eid == i2).astype(F32)
        both = oh1 + oh2
        before = (lax.broadcasted_iota(I32, (TM, TM), 0) > lax.broadcasted_iota(I32, (TM, TM), 1))
        ahead = jnp.dot(before.astype(BF16), both.astype(BF16), preferred_element_type=F32)
        r1 = jnp.sum(oh1 * ahead, axis=1, keepdims=True)
        r2 = jnp.sum(oh2 * ahead, axis=1, keepdims=True)
        zero = jnp.zeros_like(w1)
        route_ref[...] = jnp.concatenate(
            [i1.astype(F32), i2.astype(F32), r1, r2, w1, w2, zero, zero], axis=1)
        cnt_ref[0] = jnp.sum(both, axis=0, keepdims=True)


def _k2(l, o_lru, o_hy, o_ssm, x, mod_l, norm2_w, w_out, router=None, j=0):
    routed = router is not None
    tok = lambda w: pl.BlockSpec((TM, w), lambda i: (i, 0))
    in_specs = [
        tok(D_LRU), tok(D_HY), tok(D_SSM), tok(D),
        pl.BlockSpec((1, 6, D), lambda i: (_mod_row(i), 0, 0)),
        pl.BlockSpec((1, 1, D), lambda i: (l, 0, 0)),
        pl.BlockSpec((1, D, D), lambda i: (l, 0, 0)),
    ]
    args = [o_lru, o_hy, o_ssm, x, mod_l, norm2_w.reshape(DEPTH, 1, D), w_out]
    out_specs = [tok(D), tok(D)]
    out_shape = [jax.ShapeDtypeStruct((T, D), F32), jax.ShapeDtypeStruct((T, D), BF16)]
    if routed:
        in_specs.append(pl.BlockSpec((1, D, N_EXP), lambda i: (j, 0, 0)))
        args.append(router)
        out_specs += [tok(8), pl.BlockSpec((1, 1, N_EXP), lambda i: (i, 0, 0))]
        out_shape += [jax.ShapeDtypeStruct((T, 8), F32), jax.ShapeDtypeStruct((NT, 1, N_EXP), F32)]
    return pl.pallas_call(
        functools.partial(_k2_kernel, routed=routed),
        grid=(NT,),
        in_specs=in_specs,
        out_specs=out_specs,
        out_shape=out_shape,
        scratch_shapes=[pltpu.VMEM((D, D), BF16)],
        compiler_params=_cparams(1),
        name=f"k2_outproj_{l}",
    )(*args)


def _ffn_stream_kernel(be_ref, bi_ref, na_ref, x_ref, *rest, j, dense):
    if dense:
        xres_ref, mod_ref, wg_hbm, wu_hbm, wd_hbm, o_ref, wg_s, wu_s, wd_s, stg_g, stg_u, stg_d, sem = rest
    else:
        wg_hbm, wu_hbm, wd_hbm, o_ref, wg_s, wu_s, wd_s, stg_g, stg_u, stg_d, sem = rest
    del bi_ref
    b = pl.program_id(0)
    n_act = na_ref[0]
    e = be_ref[b]
    active = b < n_act
    load = jnp.logical_and(active, jnp.logical_or(b == 0, be_ref[jnp.maximum(b - 1, 0)] != e))
    e_next = be_ref[jnp.minimum(b + 1, pl.num_programs(0) - 1)]
    feed_next = jnp.logical_and(b + 1 < n_act, e_next != e)

    def copies(ee, c):
        slot = c % FFN_STAGES
        cols = slice(c * FF_CHUNK, (c + 1) * FF_CHUNK)
        return (pltpu.make_async_copy(wg_hbm.at[j, ee, :, cols], stg_g.at[slot], sem.at[0, slot]),
                pltpu.make_async_copy(wu_hbm.at[j, ee, :, cols], stg_u.at[slot], sem.at[1, slot]),
                pltpu.make_async_copy(wd_hbm.at[j, ee, cols, :], stg_d.at[slot], sem.at[2, slot]))

    def start(ee, c):
        for cp in copies(ee, c):
            cp.start()

    def chunk_out(c, x, acc):
        g = jnp.dot(x, wg_s[c], preferred_element_type=F32)
        u = jnp.dot(x, wu_s[c], preferred_element_type=F32)
        hmid = (g * jax.nn.sigmoid(g) * u).astype(BF16)
        part = jnp.dot(hmid, wd_s[c], preferred_element_type=F32)
        return part if acc is None else acc + part

    def finish(acc):
        if dense:
            o_ref[...] = xres_ref[...] + mod_ref[0][5:6] * acc
        else:
            o_ref[...] = acc.astype(o_ref.dtype)

    @pl.when(load)
    def _():
        @pl.when(b == 0)
        def _():
            for c in range(FFN_STAGES):
                start(e, c)

        x = x_ref[...]
        acc = None
        for c in range(N_FF_CHUNK):
            slot = c % FFN_STAGES
            for cp in copies(e, c):
                cp.wait()
            wg_s[c] = stg_g[slot].astype(BF16)
            wu_s[c] = stg_u[slot].astype(BF16)
            wd_s[c] = stg_d[slot].astype(BF16)
            if c + FFN_STAGES < N_FF_CHUNK:
                start(e, c + FFN_STAGES)
            acc = chunk_out(c, x, acc)
        finish(acc)

    @pl.when(jnp.logical_and(active, jnp.logical_not(load)))
    def _():
        x = x_ref[...]
        acc = None
        for c in range(N_FF_CHUNK):
            acc = chunk_out(c, x, acc)
        finish(acc)

    if not dense:
        @pl.when(jnp.logical_not(active))
        def _():
            o_ref[...] = jnp.zeros_like(o_ref)

    @pl.when(feed_next)
    def _():
        for c in range(FFN_STAGES):
            start(e_next, c)


def _ffn_scratch():
    return [pltpu.VMEM((N_FF_CHUNK, D, FF_CHUNK), BF16), pltpu.VMEM((N_FF_CHUNK, D, FF_CHUNK), BF16),
            pltpu.VMEM((N_FF_CHUNK, FF_CHUNK, D), BF16),
            pltpu.VMEM((FFN_STAGES, D, FF_CHUNK), F32), pltpu.VMEM((FFN_STAGES, D, FF_CHUNK), F32),
            pltpu.VMEM((FFN_STAGES, FF_CHUNK, D), F32), pltpu.SemaphoreType.DMA((3, FFN_STAGES))]


def _dense_ffn(j, h2, x, mod_l, wg, wu, wd):
    hbm = pl.BlockSpec(memory_space=pl.ANY)
    zeros = jnp.zeros((NT,), I32)
    return pl.pallas_call(
        functools.partial(_ffn_stream_kernel, j=j, dense=True),
        grid_spec=pltpu.PrefetchScalarGridSpec(
            num_scalar_prefetch=3,
            grid=(NT,),
            in_specs=[
                pl.BlockSpec((TM, D), lambda i, *_: (i, 0)),
                pl.BlockSpec((TM, D), lambda i, *_: (i, 0)),
                pl.BlockSpec((1, 6, D), lambda i, *_: (_mod_row(i), 0, 0)),
                hbm, hbm, hbm,
            ],
            out_specs=pl.BlockSpec((TM, D), lambda i, *_: (i, 0)),
            scratch_shapes=_ffn_scratch(),
        ),
        out_shape=jax.ShapeDtypeStruct((T, D), F32),
        compiler_params=_cparams(1),
        name=f"dense_ffn_{j}",
    )(zeros, zeros, jnp.full((1,), NT, I32), h2, x, mod_l, wg[:, None], wu[:, None], wd[:, None])


def _expert_ffn(j, xs, blk_e, blk_i, n_active, wg, wu, wd):
    hbm = pl.BlockSpec(memory_space=pl.ANY)
    return pl.pallas_call(
        functools.partial(_ffn_stream_kernel, j=j, dense=False),
        grid_spec=pltpu.PrefetchScalarGridSpec(
            num_scalar_prefetch=3,
            grid=(NBLK,),
            in_specs=[pl.BlockSpec((TMB, D), lambda b, be, bi, na: (bi[b], 0)), hbm, hbm, hbm],
            out_specs=pl.BlockSpec((TMB, D), lambda b, be, bi, na: (b, 0)),
            scratch_shapes=_ffn_scratch(),
        ),
        out_shape=jax.ShapeDtypeStruct((RMAX, D), BF16),
        compiler_params=_cparams(1),
        name=f"expert_ffn_{j}",
    )(blk_e, blk_i, n_active, xs, wg, wu, wd)


def _chunk_rows(idx):
    return pl.ds(pl.multiple_of(idx * GCH, GCH), GCH)


def _sort_kernel(gch_ref, nused_ref, pad0_ref, npad_ref, h_ref, dl_ref, xs_ref, xl_ref, zero_ref, sem):
    i = pl.program_id(0)
    slot = i % 2

    def copy(tile, q):
        s = tile % 2
        return pltpu.make_async_copy(xl_ref.at[s, _chunk_rows(q)],
                                     xs_ref.at[_chunk_rows(gch_ref[tile * NLC + q])], sem.at[s])

    def wait_tile(tile):
        lax.fori_loop(0, nused_ref[tile], lambda q, c: (copy(tile, q).wait(), c)[1], 0)

    @pl.when(i >= 2)
    def _():
        wait_tile(i - 2)

    dl = dl_ref[0]
    r = lax.broadcasted_iota(I32, (LCAP, TM), 0).astype(F32)
    perm = jnp.logical_or(r == dl[0:1, :], r == dl[1:2, :]).astype(BF16)
    xl_ref[slot] = jnp.dot(perm, h_ref[...], preferred_element_type=F32).astype(BF16)
    lax.fori_loop(0, nused_ref[i], lambda q, c: (copy(i, q).start(), c)[1], 0)

    @pl.when(i == NT - 1)
    def _():
        zero_ref[...] = jnp.zeros_like(zero_ref)
        for e in range(N_EXP + 1):
            def zcopy(q, e=e):
                return pltpu.make_async_copy(zero_ref, xs_ref.at[_chunk_rows(pad0_ref[e] + q)], sem.at[2])
            lax.fori_loop(0, npad_ref[e], lambda q, c, f=zcopy: (f(q).start(), c)[1], 0)
            lax.fori_loop(0, npad_ref[e], lambda q, c, f=zcopy: (f(q).wait(), c)[1], 0)
        wait_tile(i - 1)
        wait_tile(i)


def _sort_tokens(h2, dl_row, gch, nused, pad0, npad):
    return pl.pallas_call(
        _sort_kernel,
        grid_spec=pltpu.PrefetchScalarGridSpec(
            num_scalar_prefetch=4,
            grid=(NT,),
            in_specs=[
                pl.BlockSpec((TM, D), lambda i, *_: (i, 0)),
                pl.BlockSpec((1, 2, TM), lambda i, *_: (i, 0, 0)),
            ],
            out_specs=pl.BlockSpec(memory_space=pl.ANY),
            scratch_shapes=[pltpu.VMEM((2, LCAP, D), BF16), pltpu.VMEM((GCH, D), BF16),
                            pltpu.SemaphoreType.DMA((3,))],
        ),
        out_shape=jax.ShapeDtypeStruct((RMAX, D), BF16),
        compiler_params=_cparams(1),
        name="moe_sort",
    )(gch, nused, pad0, npad, h2, dl_row)


def _combine_kernel(gch_ref, nused_ref, y_ref, info_ref, x_ref, mod_ref, fw_ref, *rest, final):
    if final:
        oc_ref, ol_ref, yl_ref, sem = rest
    else:
        o_ref, yl_ref, sem = rest
    i = pl.program_id(0)
    slot = i % 2

    def copy(tile, q):
        s = tile % 2
        return pltpu.make_async_copy(y_ref.at[_chunk_rows(gch_ref[tile * NLC + q])],
                                     yl_ref.at[s, _chunk_rows(q)], sem.at[s])

    def fetch(tile):
        lax.fori_loop(0, nused_ref[tile], lambda q, c: (copy(tile, q).start(), c)[1], 0)

    @pl.when(i == 0)
    def _():
        fetch(i)

    @pl.when(i + 1 < NT)
    def _():
        fetch(i + 1)

    n = nused_ref[i]

    def clear(q, c):
        yl_ref[slot, _chunk_rows(q), :] = jnp.zeros((GCH, D), BF16)
        return c

    lax.fori_loop(n, NLC, clear, 0)
    lax.fori_loop(0, n, lambda q, c: (copy(i, q).wait(), c)[1], 0)

    info = info_ref[...]
    col = lax.broadcasted_iota(I32, (TM, LCAP), 1).astype(F32)
    yl = yl_ref[slot]
    y1 = jnp.dot((col == info[:, 0:1]).astype(BF16), yl, preferred_element_type=F32)
    y2 = jnp.dot((col == info[:, 1:2]).astype(BF16), yl, preferred_element_type=F32)
    x = x_ref[...] + mod_ref[0][5:6] * (info[:, 2:3] * y1 + info[:, 3:4] * y2)
    if final:
        x = _rms(x) * fw_ref[...]

        @pl.when(i < NT_CTX)
        def _():
            oc_ref[...] = x

        @pl.when(i >= NT_CTX)
        def _():
            ol_ref[...] = x
    else:
        o_ref[...] = x


def _combine(y, info, x, mod_l, final_w, gch, nused, final):
    tile = pl.BlockSpec((TM, D), lambda i, *_: (i, 0))
    if final:
        out_specs = [pl.BlockSpec((TM, D), lambda i, *_: (jnp.minimum(i, NT_CTX - 1), 0)),
                     pl.BlockSpec((TM, D), lambda i, *_: (jnp.maximum(i - NT_CTX, 0), 0))]
        out_shape = [jax.ShapeDtypeStruct((T_CTX, D), F32), jax.ShapeDtypeStruct((T_LAT, D), F32)]
    else:
        out_specs, out_shape = tile, jax.ShapeDtypeStruct((T, D), F32)
    return pl.pallas_call(
        functools.partial(_combine_kernel, final=final),
        grid_spec=pltpu.PrefetchScalarGridSpec(
            num_scalar_prefetch=2,
            grid=(NT,),
            in_specs=[
                pl.BlockSpec(memory_space=pl.ANY),
                pl.BlockSpec((TM, 4), lambda i, *_: (i, 0)),
                tile,
                pl.BlockSpec((1, 6, D), lambda i, *_: (_mod_row(i), 0, 0)),
                pl.BlockSpec((1, D), lambda i, *_: (0, 0)),
            ],
            out_specs=out_specs,
            scratch_shapes=[pltpu.VMEM((2, LCAP, D), BF16), pltpu.SemaphoreType.DMA((2,))],
        ),
        out_shape=out_shape,
        compiler_params=_cparams(1),
        name="moe_combine",
    )(gch, nused, y, info, x, mod_l, final_w.reshape(1, D))


def _moe_plan(route, counts):
    cnt = counts.reshape(NT, N_EXP).astype(I32)
    cpad = (cnt + GCH - 1) // GCH * GCH
    lo = jnp.cumsum(cpad, axis=1) - cpad
    nused = (lo[:, -1] + cpad[:, -1]) // GCH
    tot = jnp.sum(cpad, axis=0)
    gpad = (tot + TMB - 1) // TMB * TMB
    goff = jnp.cumsum(gpad) - gpad
    so = goff[None, :] + jnp.cumsum(cpad, axis=0) - cpad

    e1 = route[:, 0].astype(I32)
    e2 = route[:, 1].astype(I32)
    eid = jnp.arange(N_EXP, dtype=I32)[None, :]
    lo_tok = jnp.repeat(lo, TM, axis=0)
    dl1 = jnp.sum(jnp.where(e1[:, None] == eid, lo_tok, 0), axis=1).astype(F32) + route[:, 2]
    dl2 = jnp.sum(jnp.where(e2[:, None] == eid, lo_tok, 0), axis=1).astype(F32) + route[:, 3]
    dl_row = jnp.stack([dl1.reshape(NT, TM), dl2.reshape(NT, TM)], axis=1)
    info = jnp.stack([dl1, dl2, route[:, 4], route[:, 5]], axis=1)

    q = jnp.arange(NLC, dtype=I32)[None, :, None]
    lo16 = (lo // GCH)[:, None, :]
    c16 = (cpad // GCH)[:, None, :]
    in_seg = jnp.logical_and(q >= lo16, q < lo16 + c16)
    gch = jnp.sum(jnp.where(in_seg, (so // GCH)[:, None, :] + q - lo16, 0), axis=2).reshape(NT * NLC)

    nblk = gpad // TMB
    n_active = jnp.sum(nblk)
    b = jnp.arange(NBLK, dtype=I32)
    blk_i = jnp.maximum(jnp.minimum(b, n_active - 1), 0)
    bend = (goff + gpad) // TMB
    blk_e = jnp.minimum(jnp.sum((blk_i[:, None] >= bend[None, :]).astype(I32), axis=1), N_EXP - 1)
    used = n_active * TMB
    pad0 = jnp.concatenate([goff + tot, used.reshape(1)]) // GCH
    npad = jnp.concatenate([gpad - tot, (RMAX - used).reshape(1)]) // GCH
    return dict(dl_row=dl_row, info=info, gch=gch.astype(I32), nused=nused.astype(I32),
                blk_e=blk_e.astype(I32), blk_i=blk_i.astype(I32),
                n_active=n_active.reshape(1).astype(I32), pad0=pad0.astype(I32), npad=npad.astype(I32))


def _grid_pos_embed(n_tokens):
    rows = n_tokens // GRID_W
    r = jnp.repeat(jnp.arange(rows, dtype=F32), GRID_W)
    col = jnp.tile(jnp.arange(GRID_W, dtype=F32), rows)
    quarter = D // 4
    omega = 1.0 / (10000.0 ** (jnp.arange(quarter, dtype=F32) / quarter))
    ang_r = r[:, None] * omega[None]
    ang_c = col[:, None] * omega[None]
    return jnp.concatenate([jnp.sin(ang_r), jnp.cos(ang_r), jnp.sin(ang_c), jnp.cos(ang_c)], axis=-1)


def _hy_pos_features(n):
    pos = jnp.arange(n, dtype=F32)
    t = pos / (n - 1)
    bands = jnp.linspace(1e-4, HY_BANDS - 1, HY_BANDS, dtype=F32)
    ang = (2.0 * math.pi * pos / n)[:, None] * bands[None]
    z = jnp.concatenate([t[:, None], jnp.cos(ang), -jnp.sin(ang)], axis=-1)
    z = jnp.pad(z, ((0, 0), (0, LANE - HY_POS_DIM)))
    half = n // 2
    dist = jnp.abs(pos - half) / half
    deltas = jnp.abs(jnp.linspace(HY_MIN_DECAY, HY_MAX_DECAY, D_HY, dtype=F32))
    return z, jnp.exp(-dist[:, None] * deltas[None])


def _dft_mats(lseq, n, kb):
    nkb = n // 2 // kb
    t = jnp.arange(lseq, dtype=I32)

    def tables(tt):
        def cs(freq):
            ang = (2.0 * math.pi / n) * ((freq[:, None] * tt[None, :]) % n).astype(F32)
            return jnp.cos(ang), jnp.sin(ang)
        (ca, sa), (cb, sb) = cs(jnp.arange(nkb, dtype=I32) * kb), cs(jnp.arange(kb, dtype=I32))
        ca, sa, cb, sb = ca[:, None, :], sa[:, None, :], cb[None], sb[None]
        re = ca * cb - sa * sb
        im = -(sa * cb + ca * sb)
        dc = jnp.logical_and(jnp.arange(nkb)[:, None, None] == 0, jnp.arange(kb)[None, :, None] == 0)
        alt = (1.0 - 2.0 * (tt % 2).astype(F32))[None, None, :]
        return re, jnp.where(dc, alt, im), dc

    re, im, _ = tables(t)
    f = jnp.concatenate([re, im], axis=1).reshape(n, lseq)
    re, im, dc = tables(t + lseq // 2)
    wk = jnp.where(dc, 1.0, 2.0) / n
    gt = jnp.concatenate([wk * re, jnp.where(dc, 1.0 / n, wk) * im], axis=1).reshape(n, lseq)
    return f.astype(BF16), gt.T.astype(BF16)


def _ssd_constants():
    hp = SSM_H * SSM_P
    hq = SSM_H // SSM_G
    e64 = np.zeros((2, 8 * SSM_H, 2 * hp), np.float32)
    for d in range(2):
        for q in range(4):
            for hh in range(SSM_H):
                e64[d, q * 2 * SSM_H + d * SSM_H + hh, (q // 2) * hp + hh * SSM_P:(q // 2) * hp + (hh + 1) * SSM_P] = 1.0
    hq_of = np.arange(hq * SSM_P) // SSM_P
    mbd = (np.arange(hq * SSM_Q)[:, None] // SSM_Q == hq_of[None, :]).astype(np.float32)
    mdiag = (hq_of[:, None] == hq_of[None, :]).astype(np.float32)
    return dict(e64=jnp.asarray(e64, dtype=BF16), mbd=jnp.asarray(mbd, dtype=BF16), mdiag=jnp.asarray(mdiag))


def _block_diag_heads(w):
    eye = jnp.eye(LRU_HEADS, dtype=w.dtype)
    return jnp.einsum("ldhij,hg->ldhigj", w, eye).reshape(DEPTH, 2, D_LRU, D_LRU)


def kernel(x_prompt, x_sample, state_lru, state_ssm, c, c_ctx, norm1_w, norm2_w, final_norm_w, ada_w, ada_b,
           w_in, w_out, lru_conv_w, lru_conv_b, lru_wa, lru_ba, lru_wi, lru_bi, lru_lambda, hy_conv_w, hy_conv_b,
           hy_w1, hy_b1, hy_w2, hy_b2, hy_freq, hy_w3, hy_bias, ssm_conv_w, ssm_conv_b, ssm_dt_bias, ssm_a_log,
           ssm_d, ssm_norm_w, ffn_w_gate, ffn_w_up, ffn_w_down, moe_router, moe_w_gate, moe_w_up, moe_w_down):
    hp = SSM_H * SSM_P
    wa, wi = _block_diag_heads(lru_wa), _block_diag_heads(lru_wi)
    row = lambda a: a.reshape(DEPTH, 1, -1)
    p = {
        "lru_conv_w": lru_conv_w, "lru_conv_b": row(lru_conv_b), "lru_lambda": lru_lambda,
        "lru_wbig": jnp.concatenate([wa[:, 0], wi[:, 0], wa[:, 1], wi[:, 1]], axis=-1).astype(BF16),
        "lru_bias": jnp.concatenate([lru_ba[:, 0], lru_bi[:, 0], lru_ba[:, 1], lru_bi[:, 1]], axis=-1)[:, None],
        "hy_conv_w": hy_conv_w, "hy_conv_b": row(hy_conv_b), "hy_bias": row(hy_bias),
        "hy_w1t": jnp.swapaxes(jnp.pad(hy_w1, ((0, 0), (0, LANE - HY_POS_DIM), (0, 0))), 1, 2),
        "hy_b1": hy_b1[:, :, None], "hy_w2t": jnp.swapaxes(hy_w2, 1, 2), "hy_b2": hy_b2[:, :, None],
        "hy_freq": hy_freq[:, :, None], "hy_w3t": jnp.swapaxes(hy_w3, 1, 2),
        "ssm_conv_w": ssm_conv_w, "ssm_conv_b": row(ssm_conv_b),
        "ssm_dtb_row": row(ssm_dt_bias), "ssm_dtb_col": ssm_dt_bias.reshape(DEPTH, 2 * SSM_H, 1),
        "ssm_alog_row": row(ssm_a_log), "ssm_alog_col": ssm_a_log.reshape(DEPTH, 2 * SSM_H, 1),
        "ssm_d_exp": jnp.repeat(ssm_d, SSM_P, axis=-1)[:, None], "ssm_norm_w": row(ssm_norm_w),
    }
    w_dtT = jnp.swapaxes(w_in[:, :, D_MAIN:], 1, 2)
    ffn_w = (ffn_w_gate, ffn_w_up, ffn_w_down)
    moe_w = (moe_w_gate, moe_w_up, moe_w_down)

    cst = _ssd_constants()
    z_ctx, win_ctx = _hy_pos_features(L_CTX)
    z_lat, win_lat = _hy_pos_features(L_LAT)
    cst["f_ctx"], cst["g_ctx"] = _dft_mats(L_CTX, N_FFT_CTX, N_FFT_CTX // 2)
    cst["f_lat"], cst["g_lat"] = _dft_mats(L_LAT, N_FFT_LAT, HY_KB_LAT)
    cst["hf_ctx"] = _hy_filter_spectrum(L_CTX, N_FFT_CTX // 2, z_ctx.T, win_ctx.T, cst["f_ctx"], p)
    cst["hf_lat"] = _hy_filter_spectrum(L_LAT, HY_KB_LAT, z_lat.T, win_lat.T, cst["f_lat"], p)

    cond = jnp.concatenate([c_ctx[None], c], axis=0)
    mod = _mod_table(jnp.broadcast_to(cond[:, :, None], (3, D, LANE)), ada_w, ada_b)
    mod = mod[:, :3].reshape(DEPTH, 3, 6, D)

    x = jnp.concatenate([x_prompt.reshape(T_CTX, D),
                         (x_sample + _grid_pos_embed(L_LAT)[None]).reshape(T_LAT, D)], axis=0)
    st_ssm_in = state_ssm.reshape(N_LAT_SEQ, DEPTH, 2, hp, SSM_N)

    lru_states = []
    new_ssm = jnp.zeros((N_CTX_SEQ, DEPTH, 2, SSM_H, SSM_P, SSM_N), F32)
    for l in range(DEPTH):
        u_lru, u_hy, u_z, u_xbc, u_dt, u_dtT = _k1(l, x, mod[l], norm1_w, w_in, w_dtT)
        o_lru, s_lru = _lru_mixer(l, u_lru, p, state_lru)
        o_hy = _hy_mixer(l, u_hy, p, cst)
        o_ssm, new_ssm = _ssd_mixer(l, u_z, u_xbc, u_dt, u_dtT, p, cst, st_ssm_in, new_ssm)
        lru_states.append(s_lru.reshape(N_CTX_SEQ, 2, D_LRU))
        j = l // 2
        if l % 2 == 0:
            x, h2 = _k2(l, o_lru, o_hy, o_ssm, x, mod[l], norm2_w, w_out)
            x = _dense_ffn(j, h2, x, mod[l], *ffn_w)
        else:
            x, h2, route, counts = _k2(l, o_lru, o_hy, o_ssm, x, mod[l], norm2_w, w_out, moe_router, j)
            plan = _moe_plan(route, counts)
            xs = _sort_tokens(h2, plan["dl_row"], plan["gch"], plan["nused"], plan["pad0"], plan["npad"])
            y = _expert_ffn(j, xs, plan["blk_e"], plan["blk_i"], plan["n_active"], *moe_w)
            x = _combine(y, plan["info"], x, mod[l], final_norm_w, plan["gch"], plan["nused"],
                         final=(l == DEPTH - 1))
    y_prompt = x[0].reshape(N_CTX_SEQ, L_CTX, D)
    y_sample = x[1].reshape(N_LAT_SEQ, L_LAT, D)
    return (y_prompt, y_sample, jnp.stack(lru_states, axis=1), new_ssm)
```

```python
import functools
import math

import numpy as np
import jax
import jax.numpy as jnp
from jax import lax
from jax.experimental import pallas as pl
from jax.experimental.pallas import tpu as pltpu

F32 = jnp.float32
BF16 = jnp.bfloat16
I32 = jnp.int32
HI = lax.Precision.HIGHEST

D = 1024
N_CTX_SEQ, L_CTX = 16, 256
N_LAT_SEQ, L_LAT = 2, 2048
DEPTH = 4
GRID_W = 64
D_LRU = 256
LRU_HEADS, LRU_HD = 4, 64
LRU_C = 8.0
D_HY = 256
HY_BANDS = 16
HY_POS_DIM = 1 + 2 * HY_BANDS
HY_HID = 64
HY_MAX_DECAY = math.log(1e-2) / 0.3
HY_MIN_DECAY = math.log(1e-2) / 1.5
D_SSM = 512
SSM_P = 64
SSM_H = 8
SSM_G = 2
SSM_N = 64
SSM_Q = 128
D_XBC = D_SSM + 2 * SSM_G * SSM_N
D_MAIN = 2 * D_LRU + 3 * D_HY + D_SSM + D_XBC
D_IN = D_MAIN + 2 * SSM_H
D_FF = 2816
N_EXP = 8
EPS = 1e-6

LANE = 128
SUBLANES = 8
BF16_ROWS = 16
T_CTX = N_CTX_SEQ * L_CTX
T_LAT = N_LAT_SEQ * L_LAT
T = T_CTX + T_LAT
TM = 512
NT = T // TM
NT_CTX = T_CTX // TM
NT_PER_LAT = L_LAT // TM
RB = 2048
NB = T // RB
NB_CTX = T_CTX // RB
FF_CHUNK = 256
N_FF_CHUNK = D_FF // FF_CHUNK
FFN_STAGES = 3
VMEM_LIMIT = 56 * 1024 * 1024

GCH = BF16_ROWS
LCAP = 2 * TM + N_EXP * GCH
NLC = LCAP // GCH
TMB = 512
RMAX = -(-(2 * T + NT * N_EXP * (GCH - 1) + N_EXP * (TMB - 1)) // TMB) * TMB
NBLK = RMAX // TMB

HY_KB_LAT = 256
N_FFT_LAT = 3 * L_LAT // 2
N_FFT_CTX = 3 * L_CTX // 2
HY_NKB = N_FFT_LAT // 2 // HY_KB_LAT


def _cparams(n_axes=1, vmem=VMEM_LIMIT):
    return pltpu.CompilerParams(dimension_semantics=("arbitrary",) * n_axes, vmem_limit_bytes=vmem)


def _mod_row(i):
    return jnp.where(i < NT_CTX, 0, 1 + (i - NT_CTX) // NT_PER_LAT)


def _bdot(a, b):
    return jnp.dot(a.astype(BF16), b.astype(BF16), preferred_element_type=F32)


def _rms(x):
    return x * lax.rsqrt(jnp.mean(x * x, axis=-1, keepdims=True) + EPS)


def _split_bf16(v, parts):
    out = []
    for _ in range(parts):
        piece = v.astype(BF16)
        out.append(piece)
        v = v - piece.astype(F32)
    return out


def _mod_kernel(cb_ref, w_ref, b_ref, o_ref):
    tn = w_ref.shape[2]

    def body(kc, accs):
        k0 = pl.multiple_of(kc * 8, 8)
        wk = w_ref[0, pl.ds(k0, 8), :]
        out = []
        for r in range(3):
            c = cb_ref[r, pl.ds(k0, 8), :]
            c = c * jax.nn.sigmoid(c)
            out.append(accs[r] + jnp.tile(c, (1, tn // LANE)) * wk)
        return tuple(out)

    accs = lax.fori_loop(0, D // 8, body, tuple(jnp.zeros((8, tn), F32) for _ in range(3)), unroll=8)
    rows = [jnp.sum(a, axis=0, keepdims=True) + b_ref[0] for a in accs]
    o_ref[0] = jnp.concatenate(rows + [jnp.zeros((5, tn), F32)], axis=0)


def _mod_table(cond_b, ada_w, ada_b):
    tn = 1024
    return pl.pallas_call(
        _mod_kernel,
        grid=(DEPTH, 6 * D // tn),
        in_specs=[
            pl.BlockSpec((3, D, LANE), lambda l, j: (0, 0, 0)),
            pl.BlockSpec((1, D, tn), lambda l, j: (l, 0, j)),
            pl.BlockSpec((1, 1, tn), lambda l, j: (l, 0, j)),
        ],
        out_specs=pl.BlockSpec((1, 8, tn), lambda l, j: (l, 0, j)),
        out_shape=jax.ShapeDtypeStruct((DEPTH, 8, 6 * D), F32),
        compiler_params=_cparams(2),
        name="mod_table",
    )(cond_b, ada_w, ada_b.reshape(DEPTH, 1, 6 * D))


def _k1_kernel(x_ref, mod_ref, nw_ref, w_ref, wdtT_ref,
               o_lru, o_hy, o_z, o_xbc, o_dt, o_dtT, wbf_ref):
    @pl.when(pl.program_id(0) == 0)
    def _():
        wbf_ref[...] = w_ref[0].astype(BF16)

    m = mod_ref[0]
    h = _rms(x_ref[...]) * nw_ref[0]
    h = h * (1.0 + m[1:2]) + m[0:1]
    hb = h.astype(BF16)

    def proj(lo, hi):
        return jnp.dot(hb, wbf_ref[:, lo:hi], preferred_element_type=F32)

    o_lru[...] = proj(0, 512)
    o_hy[...] = proj(512, 1280)
    o_z[...] = proj(1280, 1792)
    o_xbc[...] = proj(1792, 2560)
    o_dt[...] = proj(D_MAIN, D_IN)
    dtT = lax.dot_general(wdtT_ref[0].astype(BF16), hb, (((1,), (1,)), ((), ())),
                          preferred_element_type=F32)
    for j in range(TM // SSM_Q):
        o_dtT[j] = dtT[:, j * SSM_Q:(j + 1) * SSM_Q]


def _k1(l, x, mod_l, norm1_w, w_in, w_dtT):
    tok = lambda w: pl.BlockSpec((TM, w), lambda i: (i, 0))
    return pl.pallas_call(
        _k1_kernel,
        grid=(NT,),
        in_specs=[
            tok(D),
            pl.BlockSpec((1, 6, D), lambda i: (_mod_row(i), 0, 0)),
            pl.BlockSpec((1, 1, D), lambda i: (l, 0, 0)),
            pl.BlockSpec((1, D, D_IN), lambda i: (l, 0, 0)),
            pl.BlockSpec((1, 2 * SSM_H, D), lambda i: (l, 0, 0)),
        ],
        out_specs=[tok(512), tok(768), tok(512), tok(768), tok(2 * SSM_H),
                   pl.BlockSpec((TM // SSM_Q, 2 * SSM_H, SSM_Q), lambda i: (i, 0, 0))],
        out_shape=[jax.ShapeDtypeStruct((T, 512), F32), jax.ShapeDtypeStruct((T, 768), F32),
                   jax.ShapeDtypeStruct((T, 512), F32), jax.ShapeDtypeStruct((T, 768), F32),
                   jax.ShapeDtypeStruct((T, 2 * SSM_H), F32),
                   jax.ShapeDtypeStruct((T // SSM_Q, 2 * SSM_H, SSM_Q), F32)],
        scratch_shapes=[pltpu.VMEM((D, D_IN), BF16)],
        compiler_params=_cparams(1),
        name=f"k1_inproj_{l}",
    )(x, mod_l, norm1_w.reshape(DEPTH, 1, D), w_in, w_dtT)


def _row_in_seq(rows, lseq):
    return lax.broadcasted_iota(I32, (rows, 1), 0) & (lseq - 1)


def _shift_rows(x, s, rin, lseq):
    if s == 0:
        return x
    y = pltpu.roll(x, s % x.shape[0], axis=0)
    valid = (rin >= s) if s > 0 else (rin < lseq + s)
    return jnp.where(valid, y, 0.0)


def _dwconv(x, w_ref, b_ref, rin, lseq):
    k_w = w_ref.shape[0]
    y = b_ref[...]
    for k in range(k_w):
        y = y + w_ref[k:k + 1, :] * _shift_rows(x, k_w // 2 - k, rin, lseq)
    return y


def _lru_scan(a, b, d, rin, lseq, rep_ref, ab_s):
    rows = a.shape[0]
    ngrp, gps = rows // SUBLANES, lseq // SUBLANES
    sub = rin & (SUBLANES - 1)
    for s in (1, 2, 4):
        sh = (s if d == 0 else -s) % rows
        valid = (sub >= s) if d == 0 else (sub < SUBLANES - s)
        b = b + jnp.where(valid, a * pltpu.roll(b, sh, axis=0), 0.0)
        a = jnp.where(valid, a * pltpu.roll(a, sh, axis=0), a)
    edge = SUBLANES - 1 if d == 0 else 0

    def group_edges(k, v):
        for j in range(D_LRU // LANE):
            ab_s[k, j] = v[:, j * LANE:(j + 1) * LANE]
        return jnp.concatenate([ab_s[k, j, pl.ds(edge, ngrp, stride=SUBLANES), :]
                                for j in range(D_LRU // LANE)], axis=1)

    ga, gb = group_edges(0, a), group_edges(1, b)
    gin = lax.broadcasted_iota(I32, (ngrp, 1), 0) & (gps - 1)
    s = 1
    while s < gps:
        sh = (s if d == 0 else -s) % ngrp
        valid = (gin >= s) if d == 0 else (gin < gps - s)
        gb = gb + jnp.where(valid, ga * pltpu.roll(gb, sh, axis=0), 0.0)
        if 2 * s < gps:
            ga = jnp.where(valid, ga * pltpu.roll(ga, sh, axis=0), ga)
        s *= 2
    valid = (gin >= 1) if d == 0 else (gin < gps - 1)
    cin = jnp.where(valid, pltpu.roll(gb, (1 if d == 0 else -1) % ngrp, axis=0), 0.0)
    hi, lo = _split_bf16(cin, 2)
    cin_x = (jnp.dot(rep_ref[...], hi, preferred_element_type=F32)
             + jnp.dot(rep_ref[...], lo, preferred_element_type=F32))
    return a * cin_x + b


def _lru_block(u_ref, cw_ref, cb_ref, wbig_ref, bias_ref, lam_ref, rep_ref, h0_ref, o_ref, st_ref, ab_s, lseq):
    rows = u_ref.shape[0]
    rin = _row_in_seq(rows, lseq)
    u = u_ref[...]
    gate = u[:, D_LRU:]
    x = _dwconv(u[:, :D_LRU], cw_ref, cb_ref, rin, lseq)
    xb = x.astype(BF16)
    y = None
    finals = []
    for d in range(2):
        g = jnp.dot(xb, wbig_ref[:, 512 * d:512 * (d + 1)], preferred_element_type=F32)
        g = g + bias_ref[:, 512 * d:512 * (d + 1)]
        r = jax.nn.sigmoid(g[:, :D_LRU])
        ig = jax.nn.sigmoid(g[:, D_LRU:])
        log_a = -LRU_C * r * jax.nn.softplus(-lam_ref[d:d + 1, :])
        a = jnp.exp(log_a)
        th = jnp.tanh(log_a)
        b = jnp.sqrt(-2.0 * th / (1.0 - th)) * (ig * x)
        if h0_ref is not None:
            edge = (rin == 0) if d == 0 else (rin == lseq - 1)
            b = b + jnp.where(edge, a * h0_ref[d:d + 1, :], 0.0)
        b = _lru_scan(a, b, d, rin, lseq, rep_ref, ab_s)
        y = b if y is None else y + b
        if st_ref is not None:
            last = lseq - 1 if d == 0 else 0
            finals.append(jnp.concatenate(
                [b[j * lseq + last:j * lseq + last + 1, :] for j in range(rows // lseq)], axis=0))
    o_ref[...] = y * jax.nn.gelu(gate)
    if st_ref is not None:
        st_ref[...] = jnp.concatenate(finals, axis=1)


def _lru_kernel(u_ref, cw_ref, cb_ref, wbig_ref, bias_ref, lam_ref, rep_ref, h0_ref, o_ref, st_ref, ab_s):
    b = pl.program_id(0)
    args = (u_ref, cw_ref.at[0], cb_ref.at[0], wbig_ref.at[0], bias_ref.at[0], lam_ref.at[0], rep_ref)

    @pl.when(b < NB_CTX)
    def _():
        _lru_block(*args, None, o_ref, st_ref, ab_s, L_CTX)

    @pl.when(b >= NB_CTX)
    def _():
        _lru_block(*args, h0_ref.at[0, 0], o_ref, None, ab_s, L_LAT)


def _lru_mixer(l, u_lru, p, c, state_lru):
    lsel = lambda *shape: pl.BlockSpec((1,) + shape, lambda b: (l,) + (0,) * len(shape))
    return pl.pallas_call(
        _lru_kernel,
        grid=(NB,),
        in_specs=[
            pl.BlockSpec((RB, 512), lambda b: (b, 0)),
            lsel(4, D_LRU), lsel(1, D_LRU), lsel(D_LRU, 1024), lsel(1, 1024), lsel(2, D_LRU),
            pl.BlockSpec((RB, RB // SUBLANES), lambda b: (0, 0), pipeline_mode=pl.Buffered(1)),
            pl.BlockSpec((1, 1, 2, D_LRU), lambda b: (jnp.maximum(b - NB_CTX, 0), l, 0, 0)),
        ],
        out_specs=[pl.BlockSpec((RB, D_LRU), lambda b: (b, 0)),
                   pl.BlockSpec((RB // L_CTX, 2 * D_LRU), lambda b: (jnp.minimum(b, NB_CTX - 1), 0))],
        out_shape=[jax.ShapeDtypeStruct((T, D_LRU), F32),
                   jax.ShapeDtypeStruct((N_CTX_SEQ, 2 * D_LRU), F32)],
        scratch_shapes=[pltpu.VMEM((2, D_LRU // LANE, RB, LANE), F32)],
        compiler_params=_cparams(1),
        name=f"lru_mixer_{l}",
    )(u_lru, p["lru_conv_w"], p["lru_conv_b"], p["lru_wbig"], p["lru_bias"], p["lru_lambda"],
      c["lru_rep"], state_lru)


def _hy_filter_kernel(z_ref, win_ref, f_ref, w1_ref, b1_ref, w2_ref, b2_ref, fr_ref, w3_ref,
                      o_ref, h_ref):
    l = pl.program_id(1)

    @pl.when(pl.program_id(0) == 0)
    def _():
        fr = fr_ref[0]
        g = jnp.sin(fr * (jnp.dot(w1_ref[0], z_ref[...], precision=HI, preferred_element_type=F32)
                          + b1_ref[0]))
        g = jnp.sin(fr * (jnp.dot(w2_ref[0], g, precision=HI, preferred_element_type=F32) + b2_ref[0]))
        h = jnp.dot(w3_ref[0], g, precision=HI, preferred_element_type=F32)
        h_ref[l] = (h * win_ref[...]).astype(BF16)

    o_ref[0] = lax.dot_general(f_ref[...], h_ref[l], (((1,), (1,)), ((), ())), preferred_element_type=F32)


def _hy_filter_spectrum(lseq, kb, zfeat_t, window_t, fmat, p):
    n = fmat.shape[0]
    lsel = lambda *shape: pl.BlockSpec((1,) + shape, lambda k, l: (l,) + (0,) * len(shape))
    return pl.pallas_call(
        _hy_filter_kernel,
        grid=(n // (2 * kb), DEPTH),
        in_specs=[
            pl.BlockSpec((LANE, lseq), lambda k, l: (0, 0)),
            pl.BlockSpec((D_HY, lseq), lambda k, l: (0, 0)),
            pl.BlockSpec((2 * kb, lseq), lambda k, l: (k, 0)),
            lsel(HY_HID, LANE), lsel(HY_HID, 1), lsel(HY_HID, HY_HID), lsel(HY_HID, 1),
            lsel(HY_HID, 1), lsel(D_HY, HY_HID),
        ],
        out_specs=pl.BlockSpec((1, 2 * kb, D_HY), lambda k, l: (l, k, 0)),
        out_shape=jax.ShapeDtypeStruct((DEPTH, n, D_HY), F32),
        scratch_shapes=[pltpu.VMEM((DEPTH, D_HY, lseq), BF16)],
        compiler_params=_cparams(2),
        name=f"hyena_filter_{lseq}",
    )(zfeat_t, window_t, fmat, p["hy_w1t"], p["hy_b1"], p["hy_w2t"], p["hy_b2"], p["hy_freq"], p["hy_w3t"])


def _hy_spectral_block(f_blk, g_blk, hf, z_bf, is_dc_block):
    kb = f_blk.shape[0] // 2
    zf = jnp.dot(f_blk, z_bf, preferred_element_type=F32)
    rz, iz = zf[:kb], zf[kb:]
    rh, ih = hf[:kb], hf[kb:]
    ii = iz * ih
    re = rz * rh - ii
    im = rz * ih + iz * rh
    if is_dc_block is not None:
        dc = jnp.logical_and(lax.broadcasted_iota(I32, (kb, 1), 0) == 0, is_dc_block)
        re = jnp.where(dc, rz * rh, re)
        im = jnp.where(dc, ii, im)
    pr = jnp.concatenate([re, im], axis=0).astype(BF16)
    return jnp.dot(g_blk, pr, preferred_element_type=F32)


def _hy_prologue(u_ref, cw_ref, cb_ref, lseq, z_ref, zbf_ref, x2_ref):
    rin = _row_in_seq(u_ref.shape[0], lseq)
    uc = _dwconv(u_ref[...], cw_ref, cb_ref, rin, lseq)
    z = uc[:, :D_HY] * uc[:, D_HY:2 * D_HY]
    z_ref[...] = z
    zbf_ref[...] = z.astype(BF16)
    x2_ref[...] = uc[:, 2 * D_HY:]


def _hy_kernel(u_ref, cw_ref, cb_ref, hb_ref, fc_ref, gc_ref, hfc_ref, fl_ref, gl_ref, hfl_ref,
               o_ref, z_ref, zbf_ref, x2_ref, acc_ref):
    b = pl.program_id(0)
    k = pl.program_id(1)
    cw, cb = cw_ref.at[0], cb_ref.at[0]

    @pl.when(jnp.logical_and(b < NB_CTX, k == 0))
    def _():
        _hy_prologue(u_ref, cw, cb, L_CTX, z_ref, zbf_ref, x2_ref)
        for s in range(RB // L_CTX):
            rows = slice(s * L_CTX, (s + 1) * L_CTX)
            acc_ref[rows, :] = _hy_spectral_block(fc_ref[...], gc_ref[...], hfc_ref[0], zbf_ref[rows, :], True)

    @pl.when(b >= NB_CTX)
    def _():
        @pl.when(k == 0)
        def _():
            _hy_prologue(u_ref, cw, cb, L_LAT, z_ref, zbf_ref, x2_ref)
            acc_ref[...] = jnp.zeros_like(acc_ref)

        acc_ref[...] += _hy_spectral_block(fl_ref[...], gl_ref[...], hfl_ref[0], zbf_ref[...], k == 0)

    @pl.when(k == HY_NKB - 1)
    def _():
        o_ref[...] = x2_ref[...] * (acc_ref[...] + hb_ref[0] * z_ref[...])


def _hy_mixer(l, u_hy, p, c):
    lat_k = lambda b, k: jnp.where(b < NB_CTX, 0, k)
    lsel = lambda *shape: pl.BlockSpec((1,) + shape, lambda b, k: (l,) + (0,) * len(shape))
    kbl = 2 * HY_KB_LAT
    return pl.pallas_call(
        _hy_kernel,
        grid=(NB, HY_NKB),
        in_specs=[
            pl.BlockSpec((RB, 3 * D_HY), lambda b, k: (b, 0)),
            lsel(3, 3 * D_HY), lsel(1, 3 * D_HY), lsel(1, D_HY),
            pl.BlockSpec((N_FFT_CTX, L_CTX), lambda b, k: (0, 0)),
            pl.BlockSpec((L_CTX, N_FFT_CTX), lambda b, k: (0, 0)),
            lsel(N_FFT_CTX, D_HY),
            pl.BlockSpec((kbl, L_LAT), lambda b, k: (lat_k(b, k), 0)),
            pl.BlockSpec((L_LAT, kbl), lambda b, k: (0, lat_k(b, k))),
            pl.BlockSpec((1, kbl, D_HY), lambda b, k: (l, lat_k(b, k), 0)),
        ],
        out_specs=pl.BlockSpec((RB, D_HY), lambda b, k: (b, 0)),
        out_shape=jax.ShapeDtypeStruct((T, D_HY), F32),
        scratch_shapes=[pltpu.VMEM((RB, D_HY), F32), pltpu.VMEM((RB, D_HY), BF16),
                        pltpu.VMEM((RB, D_HY), F32), pltpu.VMEM((RB, D_HY), F32)],
        compiler_params=_cparams(2),
        name=f"hyena_mixer_{l}",
    )(u_hy, p["hy_conv_w"], p["hy_conv_b"], p["hy_bias"],
      c["f_ctx"], c["g_ctx"], c["hf_ctx"], c["f_lat"], c["g_lat"], c["hf_lat"])


def _ssd_block(u_z, u_xbc, u_dt, u_dtT, cw, cb, dtb_row, dtb_col, alog_row, alog_col, d_exp, nw,
               e64_ref, mbd_ref, mdiag_ref, h0_ref, o_ref, st_ref,
               x_s, bc_s, yb_s, sf_s, sb_s, cs_s, col_s, row_s, lseq):
    y_acc = (o_ref, yb_s)
    s_dir = (sf_s, sb_s)
    rows = u_z.shape[0]
    nchunk = rows // SSM_Q
    cps = lseq // SSM_Q
    rin = _row_in_seq(rows, lseq)
    for c0 in range(0, D_XBC, LANE):
        cols = slice(c0, c0 + LANE)
        xbc = _dwconv(u_xbc[:, cols], cw.at[:, cols], cb.at[:, cols], rin, lseq)
        xbc = xbc * jax.nn.sigmoid(xbc)
        if c0 < D_SSM:
            x_s[:, cols] = xbc
        else:
            bc_s[:, c0 - D_SSM:c0 - D_SSM + LANE] = xbc

    li = lax.broadcasted_iota(I32, (SSM_Q, SSM_Q), 0)
    si = lax.broadcasted_iota(I32, (SSM_Q, SSM_Q), 1)
    low_half = lax.broadcasted_iota(I32, (SSM_Q, LANE), 1) < SSM_N
    hq = SSM_H // SSM_G
    wq = hq * SSM_P
    hp = SSM_H * SSM_P

    for d in range(2):
        for g in range(SSM_G):
            if h0_ref is not None:
                s_dir[d][g] = jnp.tile(h0_ref[d, g * wq:(g + 1) * wq, :], (1, hq))
            else:
                s_dir[d][g] = jnp.zeros((wq, wq), F32)

    nh2 = 2 * SSM_H
    tri_f = (li >= si).astype(BF16)
    tri_b = (li <= si).astype(BF16)
    fwd_lane = lax.broadcasted_iota(I32, (1, nh2), 1) < SSM_H
    fwd_sub = lax.broadcasted_iota(I32, (nh2, 1), 0) < SSM_H
    a_row = -jnp.exp(alog_row[...])
    a_col = -jnp.exp(alog_col[...])
    def chunk_stats(c, carry):
        rs = pl.ds(pl.multiple_of(c * SSM_Q, SSM_Q), SSM_Q)
        dt_c = jax.nn.softplus(u_dt[rs, :] + dtb_row[...])
        hi, lo = _split_bf16(dt_c * a_row, 2)
        cs_f = jnp.dot(tri_f, hi, preferred_element_type=F32) + jnp.dot(tri_f, lo, preferred_element_type=F32)
        cs_b = jnp.dot(tri_b, hi, preferred_element_type=F32) + jnp.dot(tri_b, lo, preferred_element_type=F32)
        cs_col = jnp.where(fwd_lane, cs_f, cs_b)
        tot = jnp.where(fwd_lane, cs_f[SSM_Q - 1:SSM_Q, :], cs_b[0:1, :])
        cs_s[rs, :] = cs_col
        for k, v in enumerate((jnp.exp(cs_col), dt_c * jnp.exp(tot - cs_col))):
            for m, piece in enumerate(_split_bf16(v, 2)):
                col_s[rs, (2 * k + m) * nh2:(2 * k + m + 1) * nh2] = piece
        dt_r = jax.nn.softplus(u_dtT[c] + dtb_col[...])
        hi, lo = _split_bf16(dt_r * a_col, 2)
        csr_f = jnp.dot(hi, tri_b, preferred_element_type=F32) + jnp.dot(lo, tri_b, preferred_element_type=F32)
        csr_b = jnp.dot(hi, tri_f, preferred_element_type=F32) + jnp.dot(lo, tri_f, preferred_element_type=F32)
        row_s[c, :nh2, :] = jnp.where(fwd_sub, csr_f, csr_b)
        row_s[c, nh2:, :] = dt_r
        return carry

    lax.fori_loop(0, nchunk, chunk_stats, 0)

    def chunk_step(ci, carry, d):
        causal = (li >= si) if d == 0 else (li <= si)
        edge = SSM_Q - 1 if d == 0 else 0
        c = ci if d == 0 else nchunk - 1 - ci
        rsl = pl.ds(pl.multiple_of(c * SSM_Q, SSM_Q), SSM_Q)
        if h0_ref is None and cps < nchunk:
            first = (c % cps == 0) if d == 0 else (c % cps == cps - 1)
            s_dir[d][...] = s_dir[d][...] * jnp.where(first, 0.0, 1.0)

        cs_col = cs_s[rsl, :]
        rows_c = row_s[c]
        spread = jnp.dot(col_s[rsl, :], e64_ref[d], preferred_element_type=F32)
        ecs_x, wdec_x = spread[:, :hp], spread[:, hp:]
        etot_x = ecs_x[edge:edge + 1, :]

        bcm = bc_s[rsl, :]
        bm, cm = bcm[:, :LANE], bcm[:, LANE:]
        bm_r, cm_r = pltpu.roll(bm, SSM_N, axis=1), pltpu.roll(cm, SSM_N, axis=1)
        bmb, cmb = bm.astype(BF16), cm.astype(BF16)
        for g in range(SSM_G):
            gl = slice(g * SSM_N, (g + 1) * SSM_N)
            ql = slice(g * wq, (g + 1) * wq)
            same = low_half if g == 0 else jnp.logical_not(low_half)
            b2 = jnp.where(same, bm, bm_r)
            c2 = jnp.where(same, cm, cm_r)
            gmat = lax.dot_general(cmb[:, gl], bmb[:, gl], (((1,), (1,)), ((), ())),
                                   preferred_element_type=F32)
            sc = []
            for h in range(SSM_H * d + g * hq, SSM_H * d + (g + 1) * hq):
                diff = cs_col[:, h:h + 1] - rows_c[h:h + 1, :]
                decay = jnp.exp(jnp.where(causal, diff, -1e30))
                sc.append((gmat * decay * rows_c[nh2 + h:nh2 + h + 1, :]).astype(BF16))
            sc = jnp.concatenate(sc, axis=1)
            xq = x_s[rsl, ql]
            bd = jnp.tile(xq.astype(BF16), (hq, 1)) * mbd_ref[...]
            y = jnp.dot(sc, bd, preferred_element_type=F32)
            s_old = s_dir[d][g]
            y_off = lax.dot_general(jnp.concatenate([c2, c2], axis=1).astype(BF16),
                                    (s_old * mdiag_ref[...]).astype(BF16),
                                    (((1,), (1,)), ((), ())), preferred_element_type=F32)
            y_acc[d][rsl, ql] = y + y_off * ecs_x[:, ql]
            bx = (jnp.concatenate([b2, b2], axis=1) * wdec_x[:, ql]).astype(BF16)
            s_new = jnp.dot(xq.T.astype(BF16), bx, preferred_element_type=F32)
            s_dir[d][g] = s_old * etot_x[:, ql] + s_new

        if st_ref is not None:
            last = (c % cps == cps - 1) if d == 0 else (c % cps == 0)

            @pl.when(last)
            def _():
                for g in range(SSM_G):
                    for hl in range(hq):
                        blk = slice(hl * SSM_P, (hl + 1) * SSM_P)
                        st_ref[c // cps, 0, d, g * hq + hl] = s_dir[d][g, blk, blk]
        return carry

    for d in range(2):
        lax.fori_loop(0, nchunk, functools.partial(chunk_step, d=d), 0)

    for r0 in range(0, rows, 256):
        rsl = slice(r0, r0 + 256)
        y = o_ref[rsl, :] + yb_s[rsl, :] + d_exp[...] * x_s[rsl, :]
        z = u_z[rsl, :]
        y = y * (z * jax.nn.sigmoid(z))
        o_ref[rsl, :] = _rms(y) * nw[...]


def _ssd_kernel(u_z, u_xbc, u_dt, u_dtT, cw, cb, dtb_row, dtb_col, alog_row, alog_col, d_exp, nw,
                e64_ref, mbd_ref, mdiag_ref, h0_ref, st_in_ref, o_ref, st_ref,
                x_s, bc_s, yb_s, sf_s, sb_s, cs_s, col_s, row_s):
    del st_in_ref
    b = pl.program_id(0)
    args = (u_z, u_xbc, u_dt, u_dtT, cw.at[0], cb.at[0], dtb_row.at[0], dtb_col.at[0], alog_row.at[0],
            alog_col.at[0], d_exp.at[0], nw.at[0], e64_ref, mbd_ref, mdiag_ref)
    scr = (x_s, bc_s, yb_s, sf_s, sb_s, cs_s, col_s, row_s)

    @pl.when(b < NB_CTX)
    def _():
        _ssd_block(*args, None, o_ref, st_ref, *scr, L_CTX)

    @pl.when(b >= NB_CTX)
    def _():
        _ssd_block(*args, h0_ref.at[0, 0], o_ref, None, *scr, L_LAT)


def _ssd_mixer(l, u_z, u_xbc, u_dt, u_dtT, p, c, state_ssm, new_states):
    lsel = lambda *shape: pl.BlockSpec((1,) + shape, lambda b: (l,) + (0,) * len(shape))
    full = lambda a: pl.BlockSpec(a.shape, lambda b: (0,) * a.ndim, pipeline_mode=pl.Buffered(1))
    hp = SSM_H * SSM_P
    nseq_blk = RB // L_CTX
    return pl.pallas_call(
        _ssd_kernel,
        grid=(NB,),
        in_specs=[
            pl.BlockSpec((RB, D_SSM), lambda b: (b, 0), pipeline_mode=pl.Buffered(1)),
            pl.BlockSpec((RB, D_XBC), lambda b: (b, 0)),
            pl.BlockSpec((RB, 2 * SSM_H), lambda b: (b, 0)),
            pl.BlockSpec((RB // SSM_Q, 2 * SSM_H, SSM_Q), lambda b: (b, 0, 0)),
            lsel(4, D_XBC), lsel(1, D_XBC), lsel(1, 2 * SSM_H), lsel(2 * SSM_H, 1),
            lsel(1, 2 * SSM_H), lsel(2 * SSM_H, 1), lsel(1, D_SSM), lsel(1, D_SSM),
            full(c["e64"]), full(c["mbd"]), full(c["mdiag"]),
            pl.BlockSpec((1, 1, 2, hp, SSM_N), lambda b: (jnp.maximum(b - NB_CTX, 0), l, 0, 0, 0)),
            pl.BlockSpec(memory_space=pl.ANY),
        ],
        out_specs=[pl.BlockSpec((RB, D_SSM), lambda b: (b, 0)),
                   pl.BlockSpec((nseq_blk, 1, 2, SSM_H, SSM_P, SSM_N),
                                lambda b: (jnp.minimum(b, NB_CTX - 1), l, 0, 0, 0, 0))],
        out_shape=[jax.ShapeDtypeStruct((T, D_SSM), F32),
                   jax.ShapeDtypeStruct(new_states.shape, F32)],
        input_output_aliases={16: 1},
        scratch_shapes=[pltpu.VMEM((RB, D_SSM), F32), pltpu.VMEM((RB, 2 * SSM_G * SSM_N), F32),
                        pltpu.VMEM((RB, D_SSM), F32),
                        pltpu.VMEM((SSM_G, hp // SSM_G, hp // SSM_G), F32),
                        pltpu.VMEM((SSM_G, hp // SSM_G, hp // SSM_G), F32),
                        pltpu.VMEM((RB, 2 * SSM_H), F32), pltpu.VMEM((RB, 8 * SSM_H), BF16),
                        pltpu.VMEM((RB // SSM_Q, 4 * SSM_H, SSM_Q), F32)],
        compiler_params=_cparams(1),
        name=f"ssd_mixer_{l}",
    )(u_z, u_xbc, u_dt, u_dtT, p["ssm_conv_w"], p["ssm_conv_b"], p["ssm_dtb_row"], p["ssm_dtb_col"],
      p["ssm_alog_row"], p["ssm_alog_col"], p["ssm_d_exp"], p["ssm_norm_w"],
      c["e64"], c["mbd"], c["mdiag"], state_ssm, new_states)


def _k2_kernel(*refs, routed):
    if routed:
        (ol_ref, oh_ref, os_ref, x_ref, mod_ref, nw_ref, w_ref, rt_ref,
         xo_ref, h2_ref, route_ref, cnt_ref, wbf_ref) = refs
    else:
        ol_ref, oh_ref, os_ref, x_ref, mod_ref, nw_ref, w_ref, xo_ref, h2_ref, wbf_ref = refs

    @pl.when(pl.program_id(0) == 0)
    def _():
        wbf_ref[...] = w_ref[0].astype(BF16)

    m = mod_ref[0]
    o = jnp.dot(ol_ref[...].astype(BF16), wbf_ref[0:256, :], preferred_element_type=F32)
    o = o + jnp.dot(oh_ref[...].astype(BF16), wbf_ref[256:512, :], preferred_element_type=F32)
    o = o + jnp.dot(os_ref[...].astype(BF16), wbf_ref[512:1024, :], preferred_element_type=F32)
    x = x_ref[...] + m[2:3] * o
    xo_ref[...] = x
    h2 = _rms(x) * nw_ref[0]
    h2 = h2 * (1.0 + m[4:5]) + m[3:4]
    h2_hi = h2.astype(BF16)
    h2_ref[...] = h2_hi

    if routed:
        h2_lo = (h2 - h2_hi.astype(F32)).astype(BF16)
        r_hi, r_lo = _split_bf16(rt_ref[0], 2)
        logits = (jnp.dot(h2_hi, r_hi, preferred_element_type=F32)
                  + jnp.dot(h2_lo, r_hi, preferred_element_type=F32)
                  + jnp.dot(h2_hi, r_lo, preferred_element_type=F32))
        eid = lax.broadcasted_iota(I32, logits.shape, 1)
        m1 = jnp.max(logits, axis=1, keepdims=True)
        i1 = jnp.min(jnp.where(logits == m1, eid, N_EXP), axis=1, keepdims=True)
        rest = jnp.where(eid == i1, -jnp.inf, logits)
        m2 = jnp.max(rest, axis=1, keepdims=True)
        i2 = jnp.min(jnp.where(rest == m2, eid, N_EXP), axis=1, keepdims=True)
        w1 = 1.0 / (1.0 + jnp.exp(m2 - m1))
        w2 = 1.0 - w1
        oh1 = (eid == i1).astype(F32)
        oh2 = (eid == i2).astype(F32)
        both = oh1 + oh2
        before = (lax.broadcasted_iota(I32, (TM, TM), 0) > lax.broadcasted_iota(I32, (TM, TM), 1))
        ahead = jnp.dot(before.astype(BF16), both.astype(BF16), preferred_element_type=F32)
        r1 = jnp.sum(oh1 * ahead, axis=1, keepdims=True)
        r2 = jnp.sum(oh2 * ahead, axis=1, keepdims=True)
        zero = jnp.zeros_like(w1)
        route_ref[...] = jnp.concatenate(
            [i1.astype(F32), i2.astype(F32), r1, r2, w1, w2, zero, zero], axis=1)
        cnt_ref[0] = jnp.sum(both, axis=0, keepdims=True)


def _k2(l, o_lru, o_hy, o_ssm, x, mod_l, norm2_w, w_out, router=None, j=0):
    routed = router is not None
    tok = lambda w: pl.BlockSpec((TM, w), lambda i: (i, 0))
    in_specs = [
        tok(D_LRU), tok(D_HY), tok(D_SSM), tok(D),
        pl.BlockSpec((1, 6, D), lambda i: (_mod_row(i), 0, 0)),
        pl.BlockSpec((1, 1, D), lambda i: (l, 0, 0)),
        pl.BlockSpec((1, D, D), lambda i: (l, 0, 0)),
    ]
    args = [o_lru, o_hy, o_ssm, x, mod_l, norm2_w.reshape(DEPTH, 1, D), w_out]
    out_specs = [tok(D), tok(D)]
    out_shape = [jax.ShapeDtypeStruct((T, D), F32), jax.ShapeDtypeStruct((T, D), BF16)]
    if routed:
        in_specs.append(pl.BlockSpec((1, D, N_EXP), lambda i: (j, 0, 0)))
        args.append(router)
        out_specs += [tok(8), pl.BlockSpec((1, 1, N_EXP), lambda i: (i, 0, 0))]
        out_shape += [jax.ShapeDtypeStruct((T, 8), F32), jax.ShapeDtypeStruct((NT, 1, N_EXP), F32)]
    return pl.pallas_call(
        functools.partial(_k2_kernel, routed=routed),
        grid=(NT,),
        in_specs=in_specs,
        out_specs=out_specs,
        out_shape=out_shape,
        scratch_shapes=[pltpu.VMEM((D, D), BF16)],
        compiler_params=_cparams(1),
        name=f"k2_outproj_{l}",
    )(*args)


def _ffn_stream_kernel(be_ref, bi_ref, na_ref, x_ref, *rest, j, dense):
    if dense:
        xres_ref, mod_ref, wg_hbm, wu_hbm, wd_hbm, o_ref, wg_s, wu_s, wd_s, stg_g, stg_u, stg_d, sem = rest
    else:
        wg_hbm, wu_hbm, wd_hbm, o_ref, wg_s, wu_s, wd_s, stg_g, stg_u, stg_d, sem = rest
    del bi_ref
    b = pl.program_id(0)
    n_act = na_ref[0]
    e = be_ref[b]
    active = b < n_act
    load = jnp.logical_and(active, jnp.logical_or(b == 0, be_ref[jnp.maximum(b - 1, 0)] != e))
    e_next = be_ref[jnp.minimum(b + 1, pl.num_programs(0) - 1)]
    feed_next = jnp.logical_and(b + 1 < n_act, e_next != e)

    def copies(ee, c):
        slot = c % FFN_STAGES
        cols = slice(c * FF_CHUNK, (c + 1) * FF_CHUNK)
        return (pltpu.make_async_copy(wg_hbm.at[j, ee, :, cols], stg_g.at[slot], sem.at[0, slot]),
                pltpu.make_async_copy(wu_hbm.at[j, ee, :, cols], stg_u.at[slot], sem.at[1, slot]),
                pltpu.make_async_copy(wd_hbm.at[j, ee, cols, :], stg_d.at[slot], sem.at[2, slot]))

    def start(ee, c):
        for cp in copies(ee, c):
            cp.start()

    def chunk_out(c, x, acc):
        g = jnp.dot(x, wg_s[c], preferred_element_type=F32)
        u = jnp.dot(x, wu_s[c], preferred_element_type=F32)
        hmid = (g * jax.nn.sigmoid(g) * u).astype(BF16)
        part = jnp.dot(hmid, wd_s[c], preferred_element_type=F32)
        return part if acc is None else acc + part

    def finish(acc):
        if dense:
            o_ref[...] = xres_ref[...] + mod_ref[0][5:6] * acc
        else:
            o_ref[...] = acc.astype(o_ref.dtype)

    @pl.when(load)
    def _():
        @pl.when(b == 0)
        def _():
            for c in range(FFN_STAGES):
                start(e, c)

        x = x_ref[...]
        acc = None
        for c in range(N_FF_CHUNK):
            slot = c % FFN_STAGES
            for cp in copies(e, c):
                cp.wait()
            wg_s[c] = stg_g[slot].astype(BF16)
            wu_s[c] = stg_u[slot].astype(BF16)
            wd_s[c] = stg_d[slot].astype(BF16)
            if c + FFN_STAGES < N_FF_CHUNK:
                start(e, c + FFN_STAGES)
            acc = chunk_out(c, x, acc)
        finish(acc)

    @pl.when(jnp.logical_and(active, jnp.logical_not(load)))
    def _():
        x = x_ref[...]
        acc = None
        for c in range(N_FF_CHUNK):
            acc = chunk_out(c, x, acc)
        finish(acc)

    if not dense:
        @pl.when(jnp.logical_not(active))
        def _():
            o_ref[...] = jnp.zeros_like(o_ref)

    @pl.when(feed_next)
    def _():
        for c in range(FFN_STAGES):
            start(e_next, c)


def _ffn_scratch():
    return [pltpu.VMEM((N_FF_CHUNK, D, FF_CHUNK), BF16), pltpu.VMEM((N_FF_CHUNK, D, FF_CHUNK), BF16),
            pltpu.VMEM((N_FF_CHUNK, FF_CHUNK, D), BF16),
            pltpu.VMEM((FFN_STAGES, D, FF_CHUNK), F32), pltpu.VMEM((FFN_STAGES, D, FF_CHUNK), F32),
            pltpu.VMEM((FFN_STAGES, FF_CHUNK, D), F32), pltpu.SemaphoreType.DMA((3, FFN_STAGES))]


def _dense_ffn(j, h2, x, mod_l, wg, wu, wd):
    hbm = pl.BlockSpec(memory_space=pl.ANY)
    zeros = jnp.zeros((NT,), I32)
    return pl.pallas_call(
        functools.partial(_ffn_stream_kernel, j=j, dense=True),
        grid_spec=pltpu.PrefetchScalarGridSpec(
            num_scalar_prefetch=3,
            grid=(NT,),
            in_specs=[
                pl.BlockSpec((TM, D), lambda i, *_: (i, 0)),
                pl.BlockSpec((TM, D), lambda i, *_: (i, 0)),
                pl.BlockSpec((1, 6, D), lambda i, *_: (_mod_row(i), 0, 0)),
                hbm, hbm, hbm,
            ],
            out_specs=pl.BlockSpec((TM, D), lambda i, *_: (i, 0)),
            scratch_shapes=_ffn_scratch(),
        ),
        out_shape=jax.ShapeDtypeStruct((T, D), F32),
        compiler_params=_cparams(1),
        name=f"dense_ffn_{j}",
    )(zeros, zeros, jnp.full((1,), NT, I32), h2, x, mod_l, wg[:, None], wu[:, None], wd[:, None])


def _expert_ffn(j, xs, blk_e, blk_i, n_active, wg, wu, wd):
    hbm = pl.BlockSpec(memory_space=pl.ANY)
    return pl.pallas_call(
        functools.partial(_ffn_stream_kernel, j=j, dense=False),
        grid_spec=pltpu.PrefetchScalarGridSpec(
            num_scalar_prefetch=3,
            grid=(NBLK,),
            in_specs=[pl.BlockSpec((TMB, D), lambda b, be, bi, na: (bi[b], 0)), hbm, hbm, hbm],
            out_specs=pl.BlockSpec((TMB, D), lambda b, be, bi, na: (b, 0)),
            scratch_shapes=_ffn_scratch(),
        ),
        out_shape=jax.ShapeDtypeStruct((RMAX, D), BF16),
        compiler_params=_cparams(1),
        name=f"expert_ffn_{j}",
    )(blk_e, blk_i, n_active, xs, wg, wu, wd)


def _chunk_rows(idx):
    return pl.ds(pl.multiple_of(idx * GCH, GCH), GCH)


def _sort_kernel(gch_ref, nused_ref, pad0_ref, npad_ref, h_ref, dl_ref, xs_ref, xl_ref, zero_ref, sem):
    i = pl.program_id(0)
    slot = i % 2

    def copy(tile, q):
        s = tile % 2
        return pltpu.make_async_copy(xl_ref.at[s, _chunk_rows(q)],
                                     xs_ref.at[_chunk_rows(gch_ref[tile * NLC + q])], sem.at[s])

    def wait_tile(tile):
        lax.fori_loop(0, nused_ref[tile], lambda q, c: (copy(tile, q).wait(), c)[1], 0)

    @pl.when(i >= 2)
    def _():
        wait_tile(i - 2)

    dl = dl_ref[0]
    r = lax.broadcasted_iota(I32, (LCAP, TM), 0).astype(F32)
    perm = jnp.logical_or(r == dl[0:1, :], r == dl[1:2, :]).astype(BF16)
    xl_ref[slot] = jnp.dot(perm, h_ref[...], preferred_element_type=F32).astype(BF16)
    lax.fori_loop(0, nused_ref[i], lambda q, c: (copy(i, q).start(), c)[1], 0)

    @pl.when(i == NT - 1)
    def _():
        zero_ref[...] = jnp.zeros_like(zero_ref)
        for e in range(N_EXP + 1):
            def zcopy(q, e=e):
                return pltpu.make_async_copy(zero_ref, xs_ref.at[_chunk_rows(pad0_ref[e] + q)], sem.at[2])
            lax.fori_loop(0, npad_ref[e], lambda q, c, f=zcopy: (f(q).start(), c)[1], 0)
            lax.fori_loop(0, npad_ref[e], lambda q, c, f=zcopy: (f(q).wait(), c)[1], 0)
        wait_tile(i - 1)
        wait_tile(i)


def _sort_tokens(h2, dl_row, gch, nused, pad0, npad):
    return pl.pallas_call(
        _sort_kernel,
        grid_spec=pltpu.PrefetchScalarGridSpec(
            num_scalar_prefetch=4,
            grid=(NT,),
            in_specs=[
                pl.BlockSpec((TM, D), lambda i, *_: (i, 0)),
                pl.BlockSpec((1, 2, TM), lambda i, *_: (i, 0, 0)),
            ],
            out_specs=pl.BlockSpec(memory_space=pl.ANY),
            scratch_shapes=[pltpu.VMEM((2, LCAP, D), BF16), pltpu.VMEM((GCH, D), BF16),
                            pltpu.SemaphoreType.DMA((3,))],
        ),
        out_shape=jax.ShapeDtypeStruct((RMAX, D), BF16),
        compiler_params=_cparams(1),
        name="moe_sort",
    )(gch, nused, pad0, npad, h2, dl_row)


def _combine_kernel(gch_ref, nused_ref, y_ref, info_ref, x_ref, mod_ref, fw_ref, *rest, final):
    if final:
        oc_ref, ol_ref, yl_ref, sem = rest
    else:
        o_ref, yl_ref, sem = rest
    i = pl.program_id(0)
    slot = i % 2

    def copy(tile, q):
        s = tile % 2
        return pltpu.make_async_copy(y_ref.at[_chunk_rows(gch_ref[tile * NLC + q])],
                                     yl_ref.at[s, _chunk_rows(q)], sem.at[s])

    def fetch(tile):
        lax.fori_loop(0, nused_ref[tile], lambda q, c: (copy(tile, q).start(), c)[1], 0)

    @pl.when(i == 0)
    def _():
        fetch(i)

    @pl.when(i + 1 < NT)
    def _():
        fetch(i + 1)

    n = nused_ref[i]

    def clear(q, c):
        yl_ref[slot, _chunk_rows(q), :] = jnp.zeros((GCH, D), BF16)
        return c

    lax.fori_loop(n, NLC, clear, 0)
    lax.fori_loop(0, n, lambda q, c: (copy(i, q).wait(), c)[1], 0)

    info = info_ref[...]
    col = lax.broadcasted_iota(I32, (TM, LCAP), 1).astype(F32)
    yl = yl_ref[slot]
    y1 = jnp.dot((col == info[:, 0:1]).astype(BF16), yl, preferred_element_type=F32)
    y2 = jnp.dot((col == info[:, 1:2]).astype(BF16), yl, preferred_element_type=F32)
    x = x_ref[...] + mod_ref[0][5:6] * (info[:, 2:3] * y1 + info[:, 3:4] * y2)
    if final:
        x = _rms(x) * fw_ref[...]

        @pl.when(i < NT_CTX)
        def _():
            oc_ref[...] = x

        @pl.when(i >= NT_CTX)
        def _():
            ol_ref[...] = x
    else:
        o_ref[...] = x


def _combine(y, info, x, mod_l, final_w, gch, nused, final):
    tile = pl.BlockSpec((TM, D), lambda i, *_: (i, 0))
    if final:
        out_specs = [pl.BlockSpec((TM, D), lambda i, *_: (jnp.minimum(i, NT_CTX - 1), 0)),
                     pl.BlockSpec((TM, D), lambda i, *_: (jnp.maximum(i - NT_CTX, 0), 0))]
        out_shape = [jax.ShapeDtypeStruct((T_CTX, D), F32), jax.ShapeDtypeStruct((T_LAT, D), F32)]
    else:
        out_specs, out_shape = tile, jax.ShapeDtypeStruct((T, D), F32)
    return pl.pallas_call(
        functools.partial(_combine_kernel, final=final),
        grid_spec=pltpu.PrefetchScalarGridSpec(
            num_scalar_prefetch=2,
            grid=(NT,),
            in_specs=[
                pl.BlockSpec(memory_space=pl.ANY),
                pl.BlockSpec((TM, 4), lambda i, *_: (i, 0)),
                tile,
                pl.BlockSpec((1, 6, D), lambda i, *_: (_mod_row(i), 0, 0)),
                pl.BlockSpec((1, D), lambda i, *_: (0, 0)),
            ],
            out_specs=out_specs,
            scratch_shapes=[pltpu.VMEM((2, LCAP, D), BF16), pltpu.SemaphoreType.DMA((2,))],
        ),
        out_shape=out_shape,
        compiler_params=_cparams(1),
        name="moe_combine",
    )(gch, nused, y, info, x, mod_l, final_w.reshape(1, D))


def _moe_plan(route, counts):
    cnt = counts.reshape(NT, N_EXP).astype(I32)
    cpad = (cnt + GCH - 1) // GCH * GCH
    lo = jnp.cumsum(cpad, axis=1) - cpad
    nused = (lo[:, -1] + cpad[:, -1]) // GCH
    tot = jnp.sum(cpad, axis=0)
    gpad = (tot + TMB - 1) // TMB * TMB
    goff = jnp.cumsum(gpad) - gpad
    so = goff[None, :] + jnp.cumsum(cpad, axis=0) - cpad

    e1 = route[:, 0].astype(I32)
    e2 = route[:, 1].astype(I32)
    eid = jnp.arange(N_EXP, dtype=I32)[None, :]
    lo_tok = jnp.repeat(lo, TM, axis=0)
    dl1 = jnp.sum(jnp.where(e1[:, None] == eid, lo_tok, 0), axis=1).astype(F32) + route[:, 2]
    dl2 = jnp.sum(jnp.where(e2[:, None] == eid, lo_tok, 0), axis=1).astype(F32) + route[:, 3]
    dl_row = jnp.stack([dl1.reshape(NT, TM), dl2.reshape(NT, TM)], axis=1)
    info = jnp.stack([dl1, dl2, route[:, 4], route[:, 5]], axis=1)

    q = jnp.arange(NLC, dtype=I32)[None, :, None]
    lo16 = (lo // GCH)[:, None, :]
    c16 = (cpad // GCH)[:, None, :]
    in_seg = jnp.logical_and(q >= lo16, q < lo16 + c16)
    gch = jnp.sum(jnp.where(in_seg, (so // GCH)[:, None, :] + q - lo16, 0), axis=2).reshape(NT * NLC)

    nblk = gpad // TMB
    n_active = jnp.sum(nblk)
    b = jnp.arange(NBLK, dtype=I32)
    blk_i = jnp.maximum(jnp.minimum(b, n_active - 1), 0)
    bend = (goff + gpad) // TMB
    blk_e = jnp.minimum(jnp.sum((blk_i[:, None] >= bend[None, :]).astype(I32), axis=1), N_EXP - 1)
    used = n_active * TMB
    pad0 = jnp.concatenate([goff + tot, used.reshape(1)]) // GCH
    npad = jnp.concatenate([gpad - tot, (RMAX - used).reshape(1)]) // GCH
    return dict(dl_row=dl_row, info=info, gch=gch.astype(I32), nused=nused.astype(I32),
                blk_e=blk_e.astype(I32), blk_i=blk_i.astype(I32),
                n_active=n_active.reshape(1).astype(I32), pad0=pad0.astype(I32), npad=npad.astype(I32))


def _grid_pos_embed(n_tokens):
    rows = n_tokens // GRID_W
    r = jnp.repeat(jnp.arange(rows, dtype=F32), GRID_W)
    col = jnp.tile(jnp.arange(GRID_W, dtype=F32), rows)
    quarter = D // 4
    omega = 1.0 / (10000.0 ** (jnp.arange(quarter, dtype=F32) / quarter))
    ang_r = r[:, None] * omega[None]
    ang_c = col[:, None] * omega[None]
    return jnp.concatenate([jnp.sin(ang_r), jnp.cos(ang_r), jnp.sin(ang_c), jnp.cos(ang_c)], axis=-1)


def _hy_pos_features(n):
    pos = jnp.arange(n, dtype=F32)
    t = pos / (n - 1)
    bands = jnp.linspace(1e-4, HY_BANDS - 1, HY_BANDS, dtype=F32)
    ang = (2.0 * math.pi * pos / n)[:, None] * bands[None]
    z = jnp.concatenate([t[:, None], jnp.cos(ang), -jnp.sin(ang)], axis=-1)
    z = jnp.pad(z, ((0, 0), (0, LANE - HY_POS_DIM)))
    half = n // 2
    dist = jnp.abs(pos - half) / half
    deltas = jnp.abs(jnp.linspace(HY_MIN_DECAY, HY_MAX_DECAY, D_HY, dtype=F32))
    return z, jnp.exp(-dist[:, None] * deltas[None])


def _dft_mats(lseq, n, kb):
    nkb = n // 2 // kb
    t = jnp.arange(lseq, dtype=I32)

    def tables(tt):
        def cs(freq):
            ang = (2.0 * math.pi / n) * ((freq[:, None] * tt[None, :]) % n).astype(F32)
            return jnp.cos(ang), jnp.sin(ang)
        (ca, sa), (cb, sb) = cs(jnp.arange(nkb, dtype=I32) * kb), cs(jnp.arange(kb, dtype=I32))
        ca, sa, cb, sb = ca[:, None, :], sa[:, None, :], cb[None], sb[None]
        re = ca * cb - sa * sb
        im = -(sa * cb + ca * sb)
        dc = jnp.logical_and(jnp.arange(nkb)[:, None, None] == 0, jnp.arange(kb)[None, :, None] == 0)
        alt = (1.0 - 2.0 * (tt % 2).astype(F32))[None, None, :]
        return re, jnp.where(dc, alt, im), dc

    re, im, _ = tables(t)
    f = jnp.concatenate([re, im], axis=1).reshape(n, lseq)
    re, im, dc = tables(t + lseq // 2)
    wk = jnp.where(dc, 1.0, 2.0) / n
    gt = jnp.concatenate([wk * re, jnp.where(dc, 1.0 / n, wk) * im], axis=1).reshape(n, lseq)
    return f.astype(BF16), gt.T.astype(BF16)


def _ssd_constants():
    hp = SSM_H * SSM_P
    hq = SSM_H // SSM_G
    e64 = np.zeros((2, 8 * SSM_H, 2 * hp), np.float32)
    for d in range(2):
        for q in range(4):
            for hh in range(SSM_H):
                e64[d, q * 2 * SSM_H + d * SSM_H + hh, (q // 2) * hp + hh * SSM_P:(q // 2) * hp + (hh + 1) * SSM_P] = 1.0
    hq_of = np.arange(hq * SSM_P) // SSM_P
    mbd = (np.arange(hq * SSM_Q)[:, None] // SSM_Q == hq_of[None, :]).astype(np.float32)
    mdiag = (hq_of[:, None] == hq_of[None, :]).astype(np.float32)
    return dict(e64=jnp.asarray(e64, dtype=BF16), mbd=jnp.asarray(mbd, dtype=BF16), mdiag=jnp.asarray(mdiag))


def _block_diag_heads(w):
    eye = jnp.eye(LRU_HEADS, dtype=w.dtype)
    return jnp.einsum("ldhij,hg->ldhigj", w, eye).reshape(DEPTH, 2, D_LRU, D_LRU)


def kernel(x_prompt, x_sample, state_lru, state_ssm, c, c_ctx, norm1_w, norm2_w, final_norm_w, ada_w, ada_b,
           w_in, w_out, lru_conv_w, lru_conv_b, lru_wa, lru_ba, lru_wi, lru_bi, lru_lambda, hy_conv_w, hy_conv_b,
           hy_w1, hy_b1, hy_w2, hy_b2, hy_freq, hy_w3, hy_bias, ssm_conv_w, ssm_conv_b, ssm_dt_bias, ssm_a_log,
           ssm_d, ssm_norm_w, ffn_w_gate, ffn_w_up, ffn_w_down, moe_router, moe_w_gate, moe_w_up, moe_w_down):
    hp = SSM_H * SSM_P
    wa, wi = _block_diag_heads(lru_wa), _block_diag_heads(lru_wi)
    row = lambda a: a.reshape(DEPTH, 1, -1)
    p = {
        "lru_conv_w": lru_conv_w, "lru_conv_b": row(lru_conv_b), "lru_lambda": lru_lambda,
        "lru_wbig": jnp.concatenate([wa[:, 0], wi[:, 0], wa[:, 1], wi[:, 1]], axis=-1).astype(BF16),
        "lru_bias": jnp.concatenate([lru_ba[:, 0], lru_bi[:, 0], lru_ba[:, 1], lru_bi[:, 1]], axis=-1)[:, None],
        "hy_conv_w": hy_conv_w, "hy_conv_b": row(hy_conv_b), "hy_bias": row(hy_bias),
        "hy_w1t": jnp.swapaxes(jnp.pad(hy_w1, ((0, 0), (0, LANE - HY_POS_DIM), (0, 0))), 1, 2),
        "hy_b1": hy_b1[:, :, None], "hy_w2t": jnp.swapaxes(hy_w2, 1, 2), "hy_b2": hy_b2[:, :, None],
        "hy_freq": hy_freq[:, :, None], "hy_w3t": jnp.swapaxes(hy_w3, 1, 2),
        "ssm_conv_w": ssm_conv_w, "ssm_conv_b": row(ssm_conv_b),
        "ssm_dtb_row": row(ssm_dt_bias), "ssm_dtb_col": ssm_dt_bias.reshape(DEPTH, 2 * SSM_H, 1),
        "ssm_alog_row": row(ssm_a_log), "ssm_alog_col": ssm_a_log.reshape(DEPTH, 2 * SSM_H, 1),
        "ssm_d_exp": jnp.repeat(ssm_d, SSM_P, axis=-1)[:, None], "ssm_norm_w": row(ssm_norm_w),
    }
    w_dtT = jnp.swapaxes(w_in[:, :, D_MAIN:], 1, 2)
    ffn_w = (ffn_w_gate, ffn_w_up, ffn_w_down)
    moe_w = (moe_w_gate, moe_w_up, moe_w_down)

    cst = _ssd_constants()
    cst["lru_rep"] = jnp.asarray(np.arange(RB)[:, None] // SUBLANES == np.arange(RB // SUBLANES)[None, :], dtype=BF16)
    z_ctx, win_ctx = _hy_pos_features(L_CTX)
    z_lat, win_lat = _hy_pos_features(L_LAT)
    cst["f_ctx"], cst["g_ctx"] = _dft_mats(L_CTX, N_FFT_CTX, N_FFT_CTX // 2)
    cst["f_lat"], cst["g_lat"] = _dft_mats(L_LAT, N_FFT_LAT, HY_KB_LAT)
    cst["hf_ctx"] = _hy_filter_spectrum(L_CTX, N_FFT_CTX // 2, z_ctx.T, win_ctx.T, cst["f_ctx"], p)
    cst["hf_lat"] = _hy_filter_spectrum(L_LAT, HY_KB_LAT, z_lat.T, win_lat.T, cst["f_lat"], p)

    cond = jnp.concatenate([c_ctx[None], c], axis=0)
    mod = _mod_table(jnp.broadcast_to(cond[:, :, None], (3, D, LANE)), ada_w, ada_b)
    mod = mod[:, :3].reshape(DEPTH, 3, 6, D)

    x = jnp.concatenate([x_prompt.reshape(T_CTX, D),
                         (x_sample + _grid_pos_embed(L_LAT)[None]).reshape(T_LAT, D)], axis=0)
    st_ssm_in = state_ssm.reshape(N_LAT_SEQ, DEPTH, 2, hp, SSM_N)

    lru_states = []
    new_ssm = jnp.zeros((N_CTX_SEQ, DEPTH, 2, SSM_H, SSM_P, SSM_N), F32)
    for l in range(DEPTH):
        u_lru, u_hy, u_z, u_xbc, u_dt, u_dtT = _k1(l, x, mod[l], norm1_w, w_in, w_dtT)
        o_lru, s_lru = _lru_mixer(l, u_lru, p, cst, state_lru)
        o_hy = _hy_mixer(l, u_hy, p, cst)
        o_ssm, new_ssm = _ssd_mixer(l, u_z, u_xbc, u_dt, u_dtT, p, cst, st_ssm_in, new_ssm)
        lru_states.append(s_lru.reshape(N_CTX_SEQ, 2, D_LRU))
        j = l // 2
        if l % 2 == 0:
            x, h2 = _k2(l, o_lru, o_hy, o_ssm, x, mod[l], norm2_w, w_out)
            x = _dense_ffn(j, h2, x, mod[l], *ffn_w)
        else:
            x, h2, route, counts = _k2(l, o_lru, o_hy, o_ssm, x, mod[l], norm2_w, w_out, moe_router, j)
            plan = _moe_plan(route, counts)
            xs = _sort_tokens(h2, plan["dl_row"], plan["gch"], plan["nused"], plan["pad0"], plan["npad"])
            y = _expert_ffn(j, xs, plan["blk_e"], plan["blk_i"], plan["n_active"], *moe_w)
            x = _combine(y, plan["info"], x, mod[l], final_norm_w, plan["gch"], plan["nused"],
                         final=(l == DEPTH - 1))
    y_prompt = x[0].reshape(N_CTX_SEQ, L_CTX, D)
    y_sample = x[1].reshape(N_LAT_SEQ, L_LAT, D)
    return (y_prompt, y_sample, jnp.stack(lru_states, axis=1), new_ssm)
```

```python
import functools
import math

import numpy as np
import jax
import jax.numpy as jnp
from jax import lax
from jax.experimental import pallas as pl
from jax.experimental.pallas import tpu as pltpu

F32 = jnp.float32
BF16 = jnp.bfloat16
I32 = jnp.int32
HI = lax.Precision.HIGHEST

D = 1024
N_CTX_SEQ, L_CTX = 16, 256
N_LAT_SEQ, L_LAT = 2, 2048
DEPTH = 4
GRID_W = 64
D_LRU = 256
LRU_HEADS, LRU_HD = 4, 64
LRU_C = 8.0
D_HY = 256
HY_BANDS = 16
HY_POS_DIM = 1 + 2 * HY_BANDS
HY_HID = 64
HY_MAX_DECAY = math.log(1e-2) / 0.3
HY_MIN_DECAY = math.log(1e-2) / 1.5
D_SSM = 512
SSM_P = 64
SSM_H = 8
SSM_G = 2
SSM_N = 64
SSM_Q = 128
D_XBC = D_SSM + 2 * SSM_G * SSM_N
D_MAIN = 2 * D_LRU + 3 * D_HY + D_SSM + D_XBC
D_IN = D_MAIN + 2 * SSM_H
D_FF = 2816
N_EXP = 8
EPS = 1e-6

LANE = 128
SUBLANES = 8
BF16_ROWS = 16
T_CTX = N_CTX_SEQ * L_CTX
T_LAT = N_LAT_SEQ * L_LAT
T = T_CTX + T_LAT
TM = 512
NT = T // TM
NT_CTX = T_CTX // TM
NT_PER_LAT = L_LAT // TM
RB = 2048
NB = T // RB
NB_CTX = T_CTX // RB
FF_CHUNK = 256
N_FF_CHUNK = D_FF // FF_CHUNK
FFN_STAGES = 3
VMEM_LIMIT = 56 * 1024 * 1024

GCH = BF16_ROWS
LCAP = 2 * TM + N_EXP * GCH
NLC = LCAP // GCH
TMB = 512
RMAX = -(-(2 * T + NT * N_EXP * (GCH - 1) + N_EXP * (TMB - 1)) // TMB) * TMB
NBLK = RMAX // TMB

HY_KB_LAT = 512
N_FFT_LAT = 3 * L_LAT // 2
N_FFT_CTX = 3 * L_CTX // 2
HY_NKB = N_FFT_LAT // 2 // HY_KB_LAT


def _cparams(n_axes=1, vmem=VMEM_LIMIT):
    return pltpu.CompilerParams(dimension_semantics=("arbitrary",) * n_axes, vmem_limit_bytes=vmem)


def _mod_row(i):
    return jnp.where(i < NT_CTX, 0, 1 + (i - NT_CTX) // NT_PER_LAT)


def _bdot(a, b):
    return jnp.dot(a.astype(BF16), b.astype(BF16), preferred_element_type=F32)


def _rms(x):
    return x * lax.rsqrt(jnp.mean(x * x, axis=-1, keepdims=True) + EPS)


def _split_bf16(v, parts):
    out = []
    for _ in range(parts):
        piece = v.astype(BF16)
        out.append(piece)
        v = v - piece.astype(F32)
    return out


def _mod_kernel(cb_ref, w_ref, b_ref, o_ref):
    tn = w_ref.shape[2]

    def body(kc, accs):
        k0 = pl.multiple_of(kc * 8, 8)
        wk = w_ref[0, pl.ds(k0, 8), :]
        out = []
        for r in range(3):
            c = cb_ref[r, pl.ds(k0, 8), :]
            c = c * jax.nn.sigmoid(c)
            out.append(accs[r] + jnp.tile(c, (1, tn // LANE)) * wk)
        return tuple(out)

    accs = lax.fori_loop(0, D // 8, body, tuple(jnp.zeros((8, tn), F32) for _ in range(3)), unroll=8)
    rows = [jnp.sum(a, axis=0, keepdims=True) + b_ref[0] for a in accs]
    o_ref[0] = jnp.concatenate(rows + [jnp.zeros((5, tn), F32)], axis=0)


def _mod_table(cond_b, ada_w, ada_b):
    tn = 1024
    return pl.pallas_call(
        _mod_kernel,
        grid=(DEPTH, 6 * D // tn),
        in_specs=[
            pl.BlockSpec((3, D, LANE), lambda l, j: (0, 0, 0)),
            pl.BlockSpec((1, D, tn), lambda l, j: (l, 0, j)),
            pl.BlockSpec((1, 1, tn), lambda l, j: (l, 0, j)),
        ],
        out_specs=pl.BlockSpec((1, 8, tn), lambda l, j: (l, 0, j)),
        out_shape=jax.ShapeDtypeStruct((DEPTH, 8, 6 * D), F32),
        compiler_params=_cparams(2),
        name="mod_table",
    )(cond_b, ada_w, ada_b.reshape(DEPTH, 1, 6 * D))


def _k1_kernel(x_ref, mod_ref, nw_ref, w_hbm, wdtT_ref,
               o_lru, o_hy, o_z, o_xbc, o_dt, o_dtT, wbf_ref, stage_ref, sem, *, l):
    @pl.when(pl.program_id(0) == 0)
    def _():
        cp = pltpu.make_async_copy(w_hbm.at[l], stage_ref, sem.at[0])
        cp.start()
        cp.wait()
        wbf_ref[...] = stage_ref[...].astype(BF16)

    m = mod_ref[0]
    h = _rms(x_ref[...]) * nw_ref[0]
    h = h * (1.0 + m[1:2]) + m[0:1]
    hb = h.astype(BF16)

    def proj(lo, hi):
        return jnp.dot(hb, wbf_ref[:, lo:hi], preferred_element_type=F32)

    o_lru[...] = proj(0, 512)
    o_hy[...] = proj(512, 1280)
    o_z[...] = proj(1280, 1792)
    o_xbc[...] = proj(1792, 2560)
    o_dt[...] = proj(D_MAIN, D_IN)
    dtT = lax.dot_general(wdtT_ref[0].astype(BF16), hb, (((1,), (1,)), ((), ())),
                          preferred_element_type=F32)
    for j in range(TM // SSM_Q):
        o_dtT[j] = dtT[:, j * SSM_Q:(j + 1) * SSM_Q]


def _k1(l, x, mod_l, norm1_w, w_in, w_dtT):
    tok = lambda w: pl.BlockSpec((TM, w), lambda i: (i, 0))
    return pl.pallas_call(
        functools.partial(_k1_kernel, l=l),
        grid=(NT,),
        in_specs=[
            tok(D),
            pl.BlockSpec((1, 6, D), lambda i: (_mod_row(i), 0, 0)),
            pl.BlockSpec((1, 1, D), lambda i: (l, 0, 0)),
            pl.BlockSpec(memory_space=pl.ANY),
            pl.BlockSpec((1, 2 * SSM_H, D), lambda i: (l, 0, 0)),
        ],
        out_specs=[tok(512), tok(768), tok(512), tok(768), tok(2 * SSM_H),
                   pl.BlockSpec((TM // SSM_Q, 2 * SSM_H, SSM_Q), lambda i: (i, 0, 0))],
        out_shape=[jax.ShapeDtypeStruct((T, 512), F32), jax.ShapeDtypeStruct((T, 768), F32),
                   jax.ShapeDtypeStruct((T, 512), F32), jax.ShapeDtypeStruct((T, 768), F32),
                   jax.ShapeDtypeStruct((T, 2 * SSM_H), F32),
                   jax.ShapeDtypeStruct((T // SSM_Q, 2 * SSM_H, SSM_Q), F32)],
        scratch_shapes=[pltpu.VMEM((D, D_IN), BF16), pltpu.VMEM((D, D_IN), F32),
                        pltpu.SemaphoreType.DMA((1,))],
        compiler_params=_cparams(1),
        name=f"k1_inproj_{l}",
    )(x, mod_l, norm1_w.reshape(DEPTH, 1, D), w_in, w_dtT)


def _row_in_seq(rows, lseq):
    return lax.broadcasted_iota(I32, (rows, 1), 0) & (lseq - 1)


def _shift_rows(x, s, rin, lseq):
    if s == 0:
        return x
    y = pltpu.roll(x, s % x.shape[0], axis=0)
    valid = (rin >= s) if s > 0 else (rin < lseq + s)
    return jnp.where(valid, y, 0.0)


def _dwconv(x, w_ref, b_ref, rin, lseq):
    k_w = w_ref.shape[0]
    y = b_ref[...]
    for k in range(k_w):
        y = y + w_ref[k:k + 1, :] * _shift_rows(x, k_w // 2 - k, rin, lseq)
    return y


def _lru_scan(a, b, d, rin, lseq, rep_ref, ab_s):
    rows = a.shape[0]
    ngrp, gps = rows // SUBLANES, lseq // SUBLANES
    sub = rin & (SUBLANES - 1)
    for s in (1, 2, 4):
        sh = (s if d == 0 else -s) % rows
        valid = (sub >= s) if d == 0 else (sub < SUBLANES - s)
        b = b + jnp.where(valid, a * pltpu.roll(b, sh, axis=0), 0.0)
        a = jnp.where(valid, a * pltpu.roll(a, sh, axis=0), a)
    edge = SUBLANES - 1 if d == 0 else 0

    def group_edges(k, v):
        for j in range(D_LRU // LANE):
            ab_s[k, j] = v[:, j * LANE:(j + 1) * LANE]
        return jnp.concatenate([ab_s[k, j, pl.ds(edge, ngrp, stride=SUBLANES), :]
                                for j in range(D_LRU // LANE)], axis=1)

    ga, gb = group_edges(0, a), group_edges(1, b)
    gin = lax.broadcasted_iota(I32, (ngrp, 1), 0) & (gps - 1)
    s = 1
    while s < gps:
        sh = (s if d == 0 else -s) % ngrp
        valid = (gin >= s) if d == 0 else (gin < gps - s)
        gb = gb + jnp.where(valid, ga * pltpu.roll(gb, sh, axis=0), 0.0)
        if 2 * s < gps:
            ga = jnp.where(valid, ga * pltpu.roll(ga, sh, axis=0), ga)
        s *= 2
    valid = (gin >= 1) if d == 0 else (gin < gps - 1)
    cin = jnp.where(valid, pltpu.roll(gb, (1 if d == 0 else -1) % ngrp, axis=0), 0.0)
    hi, lo = _split_bf16(cin, 2)
    cin_x = (jnp.dot(rep_ref[...], hi, preferred_element_type=F32)
             + jnp.dot(rep_ref[...], lo, preferred_element_type=F32))
    return a * cin_x + b


def _lru_block(u_ref, cw_ref, cb_ref, wbig_ref, bias_ref, lam_ref, rep_ref, h0_ref, o_ref, st_ref, ab_s, lseq):
    rows = u_ref.shape[0]
    rin = _row_in_seq(rows, lseq)
    u = u_ref[...]
    gate = u[:, D_LRU:]
    x = _dwconv(u[:, :D_LRU], cw_ref, cb_ref, rin, lseq)
    xb = x.astype(BF16)
    y = None
    finals = []
    for d in range(2):
        g = jnp.dot(xb, wbig_ref[:, 512 * d:512 * (d + 1)], preferred_element_type=F32)
        g = g + bias_ref[:, 512 * d:512 * (d + 1)]
        r = jax.nn.sigmoid(g[:, :D_LRU])
        ig = jax.nn.sigmoid(g[:, D_LRU:])
        log_a = -LRU_C * r * jax.nn.softplus(-lam_ref[d:d + 1, :])
        a = jnp.exp(log_a)
        th = jnp.tanh(log_a)
        b = jnp.sqrt(-2.0 * th / (1.0 - th)) * (ig * x)
        if h0_ref is not None:
            edge = (rin == 0) if d == 0 else (rin == lseq - 1)
            b = b + jnp.where(edge, a * h0_ref[d:d + 1, :], 0.0)
        b = _lru_scan(a, b, d, rin, lseq, rep_ref, ab_s)
        y = b if y is None else y + b
        if st_ref is not None:
            last = lseq - 1 if d == 0 else 0
            finals.append(jnp.concatenate(
                [b[j * lseq + last:j * lseq + last + 1, :] for j in range(rows // lseq)], axis=0))
    o_ref[...] = y * jax.nn.gelu(gate)
    if st_ref is not None:
        st_ref[...] = jnp.concatenate(finals, axis=1)


def _lru_kernel(u_ref, cw_ref, cb_ref, wbig_ref, bias_ref, lam_ref, rep_ref, h0_ref, o_ref, st_ref, ab_s):
    b = pl.program_id(0)
    args = (u_ref, cw_ref.at[0], cb_ref.at[0], wbig_ref.at[0], bias_ref.at[0], lam_ref.at[0], rep_ref)

    @pl.when(b < NB_CTX)
    def _():
        _lru_block(*args, None, o_ref, st_ref, ab_s, L_CTX)

    @pl.when(b >= NB_CTX)
    def _():
        _lru_block(*args, h0_ref.at[0, 0], o_ref, None, ab_s, L_LAT)


def _lru_mixer(l, u_lru, p, c, state_lru):
    lsel = lambda *shape: pl.BlockSpec((1,) + shape, lambda b: (l,) + (0,) * len(shape))
    return pl.pallas_call(
        _lru_kernel,
        grid=(NB,),
        in_specs=[
            pl.BlockSpec((RB, 512), lambda b: (b, 0)),
            lsel(4, D_LRU), lsel(1, D_LRU), lsel(D_LRU, 1024), lsel(1, 1024), lsel(2, D_LRU),
            pl.BlockSpec((RB, RB // SUBLANES), lambda b: (0, 0), pipeline_mode=pl.Buffered(1)),
            pl.BlockSpec((1, 1, 2, D_LRU), lambda b: (jnp.maximum(b - NB_CTX, 0), l, 0, 0)),
        ],
        out_specs=[pl.BlockSpec((RB, D_LRU), lambda b: (b, 0)),
                   pl.BlockSpec((RB // L_CTX, 2 * D_LRU), lambda b: (jnp.minimum(b, NB_CTX - 1), 0))],
        out_shape=[jax.ShapeDtypeStruct((T, D_LRU), F32),
                   jax.ShapeDtypeStruct((N_CTX_SEQ, 2 * D_LRU), F32)],
        scratch_shapes=[pltpu.VMEM((2, D_LRU // LANE, RB, LANE), F32)],
        compiler_params=_cparams(1),
        name=f"lru_mixer_{l}",
    )(u_lru, p["lru_conv_w"], p["lru_conv_b"], p["lru_wbig"], p["lru_bias"], p["lru_lambda"],
      c["lru_rep"], state_lru)


def _hy_filter_kernel(z_ref, win_ref, f_ref, w1_ref, b1_ref, w2_ref, b2_ref, fr_ref, w3_ref,
                      o_ref, h_ref):
    l = pl.program_id(1)

    @pl.when(pl.program_id(0) == 0)
    def _():
        fr = fr_ref[0]
        g = jnp.sin(fr * (jnp.dot(w1_ref[0], z_ref[...], precision=HI, preferred_element_type=F32)
                          + b1_ref[0]))
        g = jnp.sin(fr * (jnp.dot(w2_ref[0], g, precision=HI, preferred_element_type=F32) + b2_ref[0]))
        h = jnp.dot(w3_ref[0], g, precision=HI, preferred_element_type=F32)
        h_ref[l] = (h * win_ref[...]).astype(BF16)

    o_ref[0] = lax.dot_general(f_ref[...], h_ref[l], (((1,), (1,)), ((), ())), preferred_element_type=F32)


def _hy_filter_spectrum(lseq, kb, zfeat_t, window_t, fmat, p):
    n = fmat.shape[0]
    lsel = lambda *shape: pl.BlockSpec((1,) + shape, lambda k, l: (l,) + (0,) * len(shape))
    return pl.pallas_call(
        _hy_filter_kernel,
        grid=(n // (2 * kb), DEPTH),
        in_specs=[
            pl.BlockSpec((LANE, lseq), lambda k, l: (0, 0)),
            pl.BlockSpec((D_HY, lseq), lambda k, l: (0, 0)),
            pl.BlockSpec((2 * kb, lseq), lambda k, l: (k, 0)),
            lsel(HY_HID, LANE), lsel(HY_HID, 1), lsel(HY_HID, HY_HID), lsel(HY_HID, 1),
            lsel(HY_HID, 1), lsel(D_HY, HY_HID),
        ],
        out_specs=pl.BlockSpec((1, 2 * kb, D_HY), lambda k, l: (l, k, 0)),
        out_shape=jax.ShapeDtypeStruct((DEPTH, n, D_HY), F32),
        scratch_shapes=[pltpu.VMEM((DEPTH, D_HY, lseq), BF16)],
        compiler_params=_cparams(2),
        name=f"hyena_filter_{lseq}",
    )(zfeat_t, window_t, fmat, p["hy_w1t"], p["hy_b1"], p["hy_w2t"], p["hy_b2"], p["hy_freq"], p["hy_w3t"])


def _hy_spectral_block(f_blk, g_blk, hf, z_bf, is_dc_block):
    kb = f_blk.shape[0] // 2
    zf = jnp.dot(f_blk, z_bf, preferred_element_type=F32)
    rz, iz = zf[:kb], zf[kb:]
    rh, ih = hf[:kb], hf[kb:]
    ii = iz * ih
    re = rz * rh - ii
    im = rz * ih + iz * rh
    if is_dc_block is not None:
        dc = jnp.logical_and(lax.broadcasted_iota(I32, (kb, 1), 0) == 0, is_dc_block)
        re = jnp.where(dc, rz * rh, re)
        im = jnp.where(dc, ii, im)
    pr = jnp.concatenate([re, im], axis=0).astype(BF16)
    return jnp.dot(g_blk, pr, preferred_element_type=F32)


def _hy_prologue(u_ref, cw_ref, cb_ref, lseq, z_ref, zbf_ref, x2_ref):
    rin = _row_in_seq(u_ref.shape[0], lseq)
    uc = _dwconv(u_ref[...], cw_ref, cb_ref, rin, lseq)
    z = uc[:, :D_HY] * uc[:, D_HY:2 * D_HY]
    z_ref[...] = z
    zbf_ref[...] = z.astype(BF16)
    x2_ref[...] = uc[:, 2 * D_HY:]


def _hy_kernel(u_ref, cw_ref, cb_ref, hb_ref, fc_ref, gc_ref, hfc_ref, fl_ref, gl_ref, hfl_ref,
               o_ref, z_ref, zbf_ref, x2_ref, acc_ref):
    b = pl.program_id(0)
    k = pl.program_id(1)
    cw, cb = cw_ref.at[0], cb_ref.at[0]

    @pl.when(jnp.logical_and(b < NB_CTX, k == 0))
    def _():
        _hy_prologue(u_ref, cw, cb, L_CTX, z_ref, zbf_ref, x2_ref)
        for s in range(RB // L_CTX):
            rows = slice(s * L_CTX, (s + 1) * L_CTX)
            acc_ref[rows, :] = _hy_spectral_block(fc_ref[...], gc_ref[...], hfc_ref[0], zbf_ref[rows, :], True)

    @pl.when(b >= NB_CTX)
    def _():
        @pl.when(k == 0)
        def _():
            _hy_prologue(u_ref, cw, cb, L_LAT, z_ref, zbf_ref, x2_ref)
            acc_ref[...] = jnp.zeros_like(acc_ref)

        acc_ref[...] += _hy_spectral_block(fl_ref[...], gl_ref[...], hfl_ref[0], zbf_ref[...], k == 0)

    @pl.when(k == HY_NKB - 1)
    def _():
        o_ref[...] = x2_ref[...] * (acc_ref[...] + hb_ref[0] * z_ref[...])


def _hy_mixer(l, u_hy, p, c):
    lat_k = lambda b, k: jnp.where(b < NB_CTX, 0, k)
    lsel = lambda *shape: pl.BlockSpec((1,) + shape, lambda b, k: (l,) + (0,) * len(shape))
    kbl = 2 * HY_KB_LAT
    return pl.pallas_call(
        _hy_kernel,
        grid=(NB, HY_NKB),
        in_specs=[
            pl.BlockSpec((RB, 3 * D_HY), lambda b, k: (b, 0)),
            lsel(3, 3 * D_HY), lsel(1, 3 * D_HY), lsel(1, D_HY),
            pl.BlockSpec((N_FFT_CTX, L_CTX), lambda b, k: (0, 0)),
            pl.BlockSpec((L_CTX, N_FFT_CTX), lambda b, k: (0, 0)),
            lsel(N_FFT_CTX, D_HY),
            pl.BlockSpec((kbl, L_LAT), lambda b, k: (lat_k(b, k), 0)),
            pl.BlockSpec((L_LAT, kbl), lambda b, k: (0, lat_k(b, k))),
            pl.BlockSpec((1, kbl, D_HY), lambda b, k: (l, lat_k(b, k), 0)),
        ],
        out_specs=pl.BlockSpec((RB, D_HY), lambda b, k: (b, 0)),
        out_shape=jax.ShapeDtypeStruct((T, D_HY), F32),
        scratch_shapes=[pltpu.VMEM((RB, D_HY), F32), pltpu.VMEM((RB, D_HY), BF16),
                        pltpu.VMEM((RB, D_HY), F32), pltpu.VMEM((RB, D_HY), F32)],
        compiler_params=_cparams(2),
        name=f"hyena_mixer_{l}",
    )(u_hy, p["hy_conv_w"], p["hy_conv_b"], p["hy_bias"],
      c["f_ctx"], c["g_ctx"], c["hf_ctx"], c["f_lat"], c["g_lat"], c["hf_lat"])


def _ssd_block(u_z, u_xbc, u_dt, u_dtT, cw, cb, dtb_row, dtb_col, alog_row, alog_col, d_exp, nw,
               e64_ref, mbd_ref, mdiag_ref, h0_ref, o_ref, st_ref,
               x_s, bc_s, yb_s, sf_s, sb_s, cs_s, col_s, row_s, lseq):
    y_acc = (o_ref, yb_s)
    s_dir = (sf_s, sb_s)
    rows = u_z.shape[0]
    nchunk = rows // SSM_Q
    cps = lseq // SSM_Q
    rin = _row_in_seq(rows, lseq)
    for c0 in range(0, D_XBC, LANE):
        cols = slice(c0, c0 + LANE)
        xbc = _dwconv(u_xbc[:, cols], cw.at[:, cols], cb.at[:, cols], rin, lseq)
        xbc = xbc * jax.nn.sigmoid(xbc)
        if c0 < D_SSM:
            x_s[:, cols] = xbc
        else:
            bc_s[:, c0 - D_SSM:c0 - D_SSM + LANE] = xbc

    li = lax.broadcasted_iota(I32, (SSM_Q, SSM_Q), 0)
    si = lax.broadcasted_iota(I32, (SSM_Q, SSM_Q), 1)
    low_half = lax.broadcasted_iota(I32, (SSM_Q, LANE), 1) < SSM_N
    hq = SSM_H // SSM_G
    wq = hq * SSM_P
    hp = SSM_H * SSM_P

    for d in range(2):
        for g in range(SSM_G):
            if h0_ref is not None:
                s_dir[d][g] = jnp.tile(h0_ref[d, g * wq:(g + 1) * wq, :], (1, hq))
            else:
                s_dir[d][g] = jnp.zeros((wq, wq), F32)

    nh2 = 2 * SSM_H
    tri_f = (li >= si).astype(BF16)
    tri_b = (li <= si).astype(BF16)
    fwd_lane = lax.broadcasted_iota(I32, (1, nh2), 1) < SSM_H
    fwd_sub = lax.broadcasted_iota(I32, (nh2, 1), 0) < SSM_H
    a_row = -jnp.exp(alog_row[...])
    a_col = -jnp.exp(alog_col[...])
    def chunk_stats(c, carry):
        rs = pl.ds(pl.multiple_of(c * SSM_Q, SSM_Q), SSM_Q)
        dt_c = jax.nn.softplus(u_dt[rs, :] + dtb_row[...])
        hi, lo = _split_bf16(dt_c * a_row, 2)
        cs_f = jnp.dot(tri_f, hi, preferred_element_type=F32) + jnp.dot(tri_f, lo, preferred_element_type=F32)
        cs_b = jnp.dot(tri_b, hi, preferred_element_type=F32) + jnp.dot(tri_b, lo, preferred_element_type=F32)
        cs_col = jnp.where(fwd_lane, cs_f, cs_b)
        tot = jnp.where(fwd_lane, cs_f[SSM_Q - 1:SSM_Q, :], cs_b[0:1, :])
        cs_s[rs, :] = cs_col
        for k, v in enumerate((jnp.exp(cs_col), dt_c * jnp.exp(tot - cs_col))):
            for m, piece in enumerate(_split_bf16(v, 2)):
                col_s[rs, (2 * k + m) * nh2:(2 * k + m + 1) * nh2] = piece
        dt_r = jax.nn.softplus(u_dtT[c] + dtb_col[...])
        hi, lo = _split_bf16(dt_r * a_col, 2)
        csr_f = jnp.dot(hi, tri_b, preferred_element_type=F32) + jnp.dot(lo, tri_b, preferred_element_type=F32)
        csr_b = jnp.dot(hi, tri_f, preferred_element_type=F32) + jnp.dot(lo, tri_f, preferred_element_type=F32)
        row_s[c, :nh2, :] = jnp.where(fwd_sub, csr_f, csr_b)
        row_s[c, nh2:, :] = dt_r
        return carry

    lax.fori_loop(0, nchunk, chunk_stats, 0)

    def chunk_step(ci, carry, d):
        causal = (li >= si) if d == 0 else (li <= si)
        edge = SSM_Q - 1 if d == 0 else 0
        c = ci if d == 0 else nchunk - 1 - ci
        rsl = pl.ds(pl.multiple_of(c * SSM_Q, SSM_Q), SSM_Q)
        if h0_ref is None and cps < nchunk:
            first = (c % cps == 0) if d == 0 else (c % cps == cps - 1)
            s_dir[d][...] = s_dir[d][...] * jnp.where(first, 0.0, 1.0)

        cs_col = cs_s[rsl, :]
        rows_c = row_s[c]
        spread = jnp.dot(col_s[rsl, :], e64_ref[d], preferred_element_type=F32)
        ecs_x, wdec_x = spread[:, :hp], spread[:, hp:]
        etot_x = ecs_x[edge:edge + 1, :]

        bcm = bc_s[rsl, :]
        bm, cm = bcm[:, :LANE], bcm[:, LANE:]
        bm_r, cm_r = pltpu.roll(bm, SSM_N, axis=1), pltpu.roll(cm, SSM_N, axis=1)
        bmb, cmb = bm.astype(BF16), cm.astype(BF16)
        for g in range(SSM_G):
            gl = slice(g * SSM_N, (g + 1) * SSM_N)
            ql = slice(g * wq, (g + 1) * wq)
            same = low_half if g == 0 else jnp.logical_not(low_half)
            b2 = jnp.where(same, bm, bm_r)
            c2 = jnp.where(same, cm, cm_r)
            gmat = lax.dot_general(cmb[:, gl], bmb[:, gl], (((1,), (1,)), ((), ())),
                                   preferred_element_type=F32)
            sc = []
            for h in range(SSM_H * d + g * hq, SSM_H * d + (g + 1) * hq):
                diff = cs_col[:, h:h + 1] - rows_c[h:h + 1, :]
                decay = jnp.exp(jnp.where(causal, diff, -1e30))
                sc.append((gmat * decay * rows_c[nh2 + h:nh2 + h + 1, :]).astype(BF16))
            sc = jnp.concatenate(sc, axis=1)
            xq = x_s[rsl, ql]
            bd = jnp.tile(xq.astype(BF16), (hq, 1)) * mbd_ref[...]
            y = jnp.dot(sc, bd, preferred_element_type=F32)
            s_old = s_dir[d][g]
            y_off = lax.dot_general(jnp.concatenate([c2, c2], axis=1).astype(BF16),
                                    (s_old * mdiag_ref[...]).astype(BF16),
                                    (((1,), (1,)), ((), ())), preferred_element_type=F32)
            y_acc[d][rsl, ql] = y + y_off * ecs_x[:, ql]
            bx = (jnp.concatenate([b2, b2], axis=1) * wdec_x[:, ql]).astype(BF16)
            s_new = jnp.dot(xq.T.astype(BF16), bx, preferred_element_type=F32)
            s_dir[d][g] = s_old * etot_x[:, ql] + s_new

        if st_ref is not None:
            last = (c % cps == cps - 1) if d == 0 else (c % cps == 0)

            @pl.when(last)
            def _():
                for g in range(SSM_G):
                    for hl in range(hq):
                        blk = slice(hl * SSM_P, (hl + 1) * SSM_P)
                        st_ref[c // cps, 0, d, g * hq + hl] = s_dir[d][g, blk, blk]
        return carry

    for d in range(2):
        lax.fori_loop(0, nchunk, functools.partial(chunk_step, d=d), 0)

    for r0 in range(0, rows, 256):
        rsl = slice(r0, r0 + 256)
        y = o_ref[rsl, :] + yb_s[rsl, :] + d_exp[...] * x_s[rsl, :]
        z = u_z[rsl, :]
        y = y * (z * jax.nn.sigmoid(z))
        o_ref[rsl, :] = _rms(y) * nw[...]


def _ssd_kernel(u_z, u_xbc, u_dt, u_dtT, cw, cb, dtb_row, dtb_col, alog_row, alog_col, d_exp, nw,
                e64_ref, mbd_ref, mdiag_ref, h0_ref, st_in_ref, o_ref, st_ref,
                x_s, bc_s, yb_s, sf_s, sb_s, cs_s, col_s, row_s):
    del st_in_ref
    b = pl.program_id(0)
    args = (u_z, u_xbc, u_dt, u_dtT, cw.at[0], cb.at[0], dtb_row.at[0], dtb_col.at[0], alog_row.at[0],
            alog_col.at[0], d_exp.at[0], nw.at[0], e64_ref, mbd_ref, mdiag_ref)
    scr = (x_s, bc_s, yb_s, sf_s, sb_s, cs_s, col_s, row_s)

    @pl.when(b < NB_CTX)
    def _():
        _ssd_block(*args, None, o_ref, st_ref, *scr, L_CTX)

    @pl.when(b >= NB_CTX)
    def _():
        _ssd_block(*args, h0_ref.at[0, 0], o_ref, None, *scr, L_LAT)


def _ssd_mixer(l, u_z, u_xbc, u_dt, u_dtT, p, c, state_ssm, new_states):
    lsel = lambda *shape: pl.BlockSpec((1,) + shape, lambda b: (l,) + (0,) * len(shape))
    full = lambda a: pl.BlockSpec(a.shape, lambda b: (0,) * a.ndim, pipeline_mode=pl.Buffered(1))
    hp = SSM_H * SSM_P
    nseq_blk = RB // L_CTX
    return pl.pallas_call(
        _ssd_kernel,
        grid=(NB,),
        in_specs=[
            pl.BlockSpec((RB, D_SSM), lambda b: (b, 0), pipeline_mode=pl.Buffered(1)),
            pl.BlockSpec((RB, D_XBC), lambda b: (b, 0)),
            pl.BlockSpec((RB, 2 * SSM_H), lambda b: (b, 0)),
            pl.BlockSpec((RB // SSM_Q, 2 * SSM_H, SSM_Q), lambda b: (b, 0, 0)),
            lsel(4, D_XBC), lsel(1, D_XBC), lsel(1, 2 * SSM_H), lsel(2 * SSM_H, 1),
            lsel(1, 2 * SSM_H), lsel(2 * SSM_H, 1), lsel(1, D_SSM), lsel(1, D_SSM),
            full(c["e64"]), full(c["mbd"]), full(c["mdiag"]),
            pl.BlockSpec((1, 1, 2, hp, SSM_N), lambda b: (jnp.maximum(b - NB_CTX, 0), l, 0, 0, 0)),
            pl.BlockSpec(memory_space=pl.ANY),
        ],
        out_specs=[pl.BlockSpec((RB, D_SSM), lambda b: (b, 0)),
                   pl.BlockSpec((nseq_blk, 1, 2, SSM_H, SSM_P, SSM_N),
                                lambda b: (jnp.minimum(b, NB_CTX - 1), l, 0, 0, 0, 0))],
        out_shape=[jax.ShapeDtypeStruct((T, D_SSM), F32),
                   jax.ShapeDtypeStruct(new_states.shape, F32)],
        input_output_aliases={16: 1},
        scratch_shapes=[pltpu.VMEM((RB, D_SSM), F32), pltpu.VMEM((RB, 2 * SSM_G * SSM_N), F32),
                        pltpu.VMEM((RB, D_SSM), F32),
                        pltpu.VMEM((SSM_G, hp // SSM_G, hp // SSM_G), F32),
                        pltpu.VMEM((SSM_G, hp // SSM_G, hp // SSM_G), F32),
                        pltpu.VMEM((RB, 2 * SSM_H), F32), pltpu.VMEM((RB, 8 * SSM_H), BF16),
                        pltpu.VMEM((RB // SSM_Q, 4 * SSM_H, SSM_Q), F32)],
        compiler_params=_cparams(1),
        name=f"ssd_mixer_{l}",
    )(u_z, u_xbc, u_dt, u_dtT, p["ssm_conv_w"], p["ssm_conv_b"], p["ssm_dtb_row"], p["ssm_dtb_col"],
      p["ssm_alog_row"], p["ssm_alog_col"], p["ssm_d_exp"], p["ssm_norm_w"],
      c["e64"], c["mbd"], c["mdiag"], state_ssm, new_states)


def _k2_kernel(*refs, routed):
    if routed:
        (ol_ref, oh_ref, os_ref, x_ref, mod_ref, nw_ref, w_ref, rt_ref,
         xo_ref, h2_ref, route_ref, cnt_ref, wbf_ref) = refs
    else:
        ol_ref, oh_ref, os_ref, x_ref, mod_ref, nw_ref, w_ref, xo_ref, h2_ref, wbf_ref = refs

    @pl.when(pl.program_id(0) == 0)
    def _():
        wbf_ref[...] = w_ref[0].astype(BF16)

    m = mod_ref[0]
    o = jnp.dot(ol_ref[...].astype(BF16), wbf_ref[0:256, :], preferred_element_type=F32)
    o = o + jnp.dot(oh_ref[...].astype(BF16), wbf_ref[256:512, :], preferred_element_type=F32)
    o = o + jnp.dot(os_ref[...].astype(BF16), wbf_ref[512:1024, :], preferred_element_type=F32)
    x = x_ref[...] + m[2:3] * o
    xo_ref[...] = x
    h2 = _rms(x) * nw_ref[0]
    h2 = h2 * (1.0 + m[4:5]) + m[3:4]
    h2_hi = h2.astype(BF16)
    h2_ref[...] = h2_hi

    if routed:
        h2_lo = (h2 - h2_hi.astype(F32)).astype(BF16)
        r_hi, r_lo = _split_bf16(rt_ref[0], 2)
        logits = (jnp.dot(h2_hi, r_hi, preferred_element_type=F32)
                  + jnp.dot(h2_lo, r_hi, preferred_element_type=F32)
                  + jnp.dot(h2_hi, r_lo, preferred_element_type=F32))
        eid = lax.broadcasted_iota(I32, logits.shape, 1)
        m1 = jnp.max(logits, axis=1, keepdims=True)
        i1 = jnp.min(jnp.where(logits == m1, eid, N_EXP), axis=1, keepdims=True)
        rest = jnp.where(eid == i1, -jnp.inf, logits)
        m2 = jnp.max(rest, axis=1, keepdims=True)
        i2 = jnp.min(jnp.where(rest == m2, eid, N_EXP), axis=1, keepdims=True)
        w1 = 1.0 / (1.0 + jnp.exp(m2 - m1))
        w2 = 1.0 - w1
        oh1 = (eid == i1).astype(F32)
        oh2 = (eid == i2).astype(F32)
        both = oh1 + oh2
        before = (lax.broadcasted_iota(I32, (TM, TM), 0) > lax.broadcasted_iota(I32, (TM, TM), 1))
        ahead = jnp.dot(before.astype(BF16), both.astype(BF16), preferred_element_type=F32)
        r1 = jnp.sum(oh1 * ahead, axis=1, keepdims=True)
        r2 = jnp.sum(oh2 * ahead, axis=1, keepdims=True)
        zero = jnp.zeros_like(w1)
        route_ref[...] = jnp.concatenate(
            [i1.astype(F32), i2.astype(F32), r1, r2, w1, w2, zero, zero], axis=1)
        cnt_ref[0] = jnp.sum(both, axis=0, keepdims=True)


def _k2(l, o_lru, o_hy, o_ssm, x, mod_l, norm2_w, w_out, router=None, j=0):
    routed = router is not None
    tok = lambda w: pl.BlockSpec((TM, w), lambda i: (i, 0))
    in_specs = [
        tok(D_LRU), tok(D_HY), tok(D_SSM), tok(D),
        pl.BlockSpec((1, 6, D), lambda i: (_mod_row(i), 0, 0)),
        pl.BlockSpec((1, 1, D), lambda i: (l, 0, 0)),
        pl.BlockSpec((1, D, D), lambda i: (l, 0, 0)),
    ]
    args = [o_lru, o_hy, o_ssm, x, mod_l, norm2_w.reshape(DEPTH, 1, D), w_out]
    out_specs = [tok(D), tok(D)]
    out_shape = [jax.ShapeDtypeStruct((T, D), F32), jax.ShapeDtypeStruct((T, D), BF16)]
    if routed:
        in_specs.append(pl.BlockSpec((1, D, N_EXP), lambda i: (j, 0, 0)))
        args.append(router)
        out_specs += [tok(8), pl.BlockSpec((1, 1, N_EXP), lambda i: (i, 0, 0))]
        out_shape += [jax.ShapeDtypeStruct((T, 8), F32), jax.ShapeDtypeStruct((NT, 1, N_EXP), F32)]
    return pl.pallas_call(
        functools.partial(_k2_kernel, routed=routed),
        grid=(NT,),
        in_specs=in_specs,
        out_specs=out_specs,
        out_shape=out_shape,
        scratch_shapes=[pltpu.VMEM((D, D), BF16)],
        compiler_params=_cparams(1),
        name=f"k2_outproj_{l}",
    )(*args)


def _ffn_stream_kernel(be_ref, bi_ref, na_ref, x_ref, *rest, j, dense):
    if dense:
        xres_ref, mod_ref, wg_hbm, wu_hbm, wd_hbm, o_ref, wg_s, wu_s, wd_s, stg_g, stg_u, stg_d, sem = rest
    else:
        wg_hbm, wu_hbm, wd_hbm, o_ref, wg_s, wu_s, wd_s, stg_g, stg_u, stg_d, sem = rest
    del bi_ref
    b = pl.program_id(0)
    n_act = na_ref[0]
    e = be_ref[b]
    active = b < n_act
    load = jnp.logical_and(active, jnp.logical_or(b == 0, be_ref[jnp.maximum(b - 1, 0)] != e))
    e_next = be_ref[jnp.minimum(b + 1, pl.num_programs(0) - 1)]
    feed_next = jnp.logical_and(b + 1 < n_act, e_next != e)

    def copies(ee, c):
        slot = c % FFN_STAGES
        cols = slice(c * FF_CHUNK, (c + 1) * FF_CHUNK)
        return (pltpu.make_async_copy(wg_hbm.at[j, ee, :, cols], stg_g.at[slot], sem.at[0, slot]),
                pltpu.make_async_copy(wu_hbm.at[j, ee, :, cols], stg_u.at[slot], sem.at[1, slot]),
                pltpu.make_async_copy(wd_hbm.at[j, ee, cols, :], stg_d.at[slot], sem.at[2, slot]))

    def start(ee, c):
        for cp in copies(ee, c):
            cp.start()

    def chunk_out(c, x, acc):
        g = jnp.dot(x, wg_s[c], preferred_element_type=F32)
        u = jnp.dot(x, wu_s[c], preferred_element_type=F32)
        hmid = (g * jax.nn.sigmoid(g) * u).astype(BF16)
        part = jnp.dot(hmid, wd_s[c], preferred_element_type=F32)
        return part if acc is None else acc + part

    def finish(acc):
        if dense:
            o_ref[...] = xres_ref[...] + mod_ref[0][5:6] * acc
        else:
            o_ref[...] = acc.astype(o_ref.dtype)

    @pl.when(load)
    def _():
        @pl.when(b == 0)
        def _():
            for c in range(FFN_STAGES):
                start(e, c)

        x = x_ref[...]
        acc = None
        for c in range(N_FF_CHUNK):
            slot = c % FFN_STAGES
            for cp in copies(e, c):
                cp.wait()
            wg_s[c] = stg_g[slot].astype(BF16)
            wu_s[c] = stg_u[slot].astype(BF16)
            wd_s[c] = stg_d[slot].astype(BF16)
            if c + FFN_STAGES < N_FF_CHUNK:
                start(e, c + FFN_STAGES)
            acc = chunk_out(c, x, acc)
        finish(acc)

    @pl.when(jnp.logical_and(active, jnp.logical_not(load)))
    def _():
        x = x_ref[...]
        acc = None
        for c in range(N_FF_CHUNK):
            acc = chunk_out(c, x, acc)
        finish(acc)

    if not dense:
        @pl.when(jnp.logical_not(active))
        def _():
            o_ref[...] = jnp.zeros_like(o_ref)

    @pl.when(feed_next)
    def _():
        for c in range(FFN_STAGES):
            start(e_next, c)


def _ffn_scratch():
    return [pltpu.VMEM((N_FF_CHUNK, D, FF_CHUNK), BF16), pltpu.VMEM((N_FF_CHUNK, D, FF_CHUNK), BF16),
            pltpu.VMEM((N_FF_CHUNK, FF_CHUNK, D), BF16),
            pltpu.VMEM((FFN_STAGES, D, FF_CHUNK), F32), pltpu.VMEM((FFN_STAGES, D, FF_CHUNK), F32),
            pltpu.VMEM((FFN_STAGES, FF_CHUNK, D), F32), pltpu.SemaphoreType.DMA((3, FFN_STAGES))]


def _dense_ffn(j, h2, x, mod_l, wg, wu, wd):
    hbm = pl.BlockSpec(memory_space=pl.ANY)
    zeros = jnp.zeros((NT,), I32)
    return pl.pallas_call(
        functools.partial(_ffn_stream_kernel, j=j, dense=True),
        grid_spec=pltpu.PrefetchScalarGridSpec(
            num_scalar_prefetch=3,
            grid=(NT,),
            in_specs=[
                pl.BlockSpec((TM, D), lambda i, *_: (i, 0)),
                pl.BlockSpec((TM, D), lambda i, *_: (i, 0)),
                pl.BlockSpec((1, 6, D), lambda i, *_: (_mod_row(i), 0, 0)),
                hbm, hbm, hbm,
            ],
            out_specs=pl.BlockSpec((TM, D), lambda i, *_: (i, 0)),
            scratch_shapes=_ffn_scratch(),
        ),
        out_shape=jax.ShapeDtypeStruct((T, D), F32),
        compiler_params=_cparams(1),
        name=f"dense_ffn_{j}",
    )(zeros, zeros, jnp.full((1,), NT, I32), h2, x, mod_l, wg[:, None], wu[:, None], wd[:, None])


def _expert_ffn(j, xs, blk_e, blk_i, n_active, wg, wu, wd):
    hbm = pl.BlockSpec(memory_space=pl.ANY)
    return pl.pallas_call(
        functools.partial(_ffn_stream_kernel, j=j, dense=False),
        grid_spec=pltpu.PrefetchScalarGridSpec(
            num_scalar_prefetch=3,
            grid=(NBLK,),
            in_specs=[pl.BlockSpec((TMB, D), lambda b, be, bi, na: (bi[b], 0)), hbm, hbm, hbm],
            out_specs=pl.BlockSpec((TMB, D), lambda b, be, bi, na: (b, 0)),
            scratch_shapes=_ffn_scratch(),
        ),
        out_shape=jax.ShapeDtypeStruct((RMAX, D), BF16),
        compiler_params=_cparams(1),
        name=f"expert_ffn_{j}",
    )(blk_e, blk_i, n_active, xs, wg, wu, wd)


def _chunk_rows(idx):
    return pl.ds(pl.multiple_of(idx * GCH, GCH), GCH)


def _sort_kernel(gch_ref, nused_ref, pad0_ref, npad_ref, h_ref, dl_ref, xs_ref, xl_ref, zero_ref, sem):
    i = pl.program_id(0)
    slot = i % 2

    def copy(tile, q):
        s = tile % 2
        return pltpu.make_async_copy(xl_ref.at[s, _chunk_rows(q)],
                                     xs_ref.at[_chunk_rows(gch_ref[tile * NLC + q])], sem.at[s])

    def wait_tile(tile):
        lax.fori_loop(0, nused_ref[tile], lambda q, c: (copy(tile, q).wait(), c)[1], 0)

    @pl.when(i >= 2)
    def _():
        wait_tile(i - 2)

    dl = dl_ref[0]
    r = lax.broadcasted_iota(I32, (LCAP, TM), 0).astype(F32)
    perm = jnp.logical_or(r == dl[0:1, :], r == dl[1:2, :]).astype(BF16)
    xl_ref[slot] = jnp.dot(perm, h_ref[...], preferred_element_type=F32).astype(BF16)
    lax.fori_loop(0, nused_ref[i], lambda q, c: (copy(i, q).start(), c)[1], 0)

    @pl.when(i == NT - 1)
    def _():
        zero_ref[...] = jnp.zeros_like(zero_ref)
        for e in range(N_EXP + 1):
            def zcopy(q, e=e):
                return pltpu.make_async_copy(zero_ref, xs_ref.at[_chunk_rows(pad0_ref[e] + q)], sem.at[2])
            lax.fori_loop(0, npad_ref[e], lambda q, c, f=zcopy: (f(q).start(), c)[1], 0)
            lax.fori_loop(0, npad_ref[e], lambda q, c, f=zcopy: (f(q).wait(), c)[1], 0)
        wait_tile(i - 1)
        wait_tile(i)


def _sort_tokens(h2, dl_row, gch, nused, pad0, npad):
    return pl.pallas_call(
        _sort_kernel,
        grid_spec=pltpu.PrefetchScalarGridSpec(
            num_scalar_prefetch=4,
            grid=(NT,),
            in_specs=[
                pl.BlockSpec((TM, D), lambda i, *_: (i, 0)),
                pl.BlockSpec((1, 2, TM), lambda i, *_: (i, 0, 0)),
            ],
            out_specs=pl.BlockSpec(memory_space=pl.ANY),
            scratch_shapes=[pltpu.VMEM((2, LCAP, D), BF16), pltpu.VMEM((GCH, D), BF16),
                            pltpu.SemaphoreType.DMA((3,))],
        ),
        out_shape=jax.ShapeDtypeStruct((RMAX, D), BF16),
        compiler_params=_cparams(1),
        name="moe_sort",
    )(gch, nused, pad0, npad, h2, dl_row)


def _combine_kernel(gch_ref, nused_ref, y_ref, info_ref, x_ref, mod_ref, fw_ref, *rest, final):
    if final:
        oc_ref, ol_ref, yl_ref, sem = rest
    else:
        o_ref, yl_ref, sem = rest
    i = pl.program_id(0)
    slot = i % 2

    def copy(tile, q):
        s = tile % 2
        return pltpu.make_async_copy(y_ref.at[_chunk_rows(gch_ref[tile * NLC + q])],
                                     yl_ref.at[s, _chunk_rows(q)], sem.at[s])

    def fetch(tile):
        lax.fori_loop(0, nused_ref[tile], lambda q, c: (copy(tile, q).start(), c)[1], 0)

    @pl.when(i == 0)
    def _():
        fetch(i)

    @pl.when(i + 1 < NT)
    def _():
        fetch(i + 1)

    n = nused_ref[i]

    def clear(q, c):
        yl_ref[slot, _chunk_rows(q), :] = jnp.zeros((GCH, D), BF16)
        return c

    lax.fori_loop(n, NLC, clear, 0)
    lax.fori_loop(0, n, lambda q, c: (copy(i, q).wait(), c)[1], 0)

    info = info_ref[...]
    col = lax.broadcasted_iota(I32, (TM, LCAP), 1).astype(F32)
    yl = yl_ref[slot]
    y1 = jnp.dot((col == info[:, 0:1]).astype(BF16), yl, preferred_element_type=F32)
    y2 = jnp.dot((col == info[:, 1:2]).astype(BF16), yl, preferred_element_type=F32)
    x = x_ref[...] + mod_ref[0][5:6] * (info[:, 2:3] * y1 + info[:, 3:4] * y2)
    if final:
        x = _rms(x) * fw_ref[...]

        @pl.when(i < NT_CTX)
        def _():
            oc_ref[...] = x

        @pl.when(i >= NT_CTX)
        def _():
            ol_ref[...] = x
    else:
        o_ref[...] = x


def _combine(y, info, x, mod_l, final_w, gch, nused, final):
    tile = pl.BlockSpec((TM, D), lambda i, *_: (i, 0))
    if final:
        out_specs = [pl.BlockSpec((TM, D), lambda i, *_: (jnp.minimum(i, NT_CTX - 1), 0)),
                     pl.BlockSpec((TM, D), lambda i, *_: (jnp.maximum(i - NT_CTX, 0), 0))]
        out_shape = [jax.ShapeDtypeStruct((T_CTX, D), F32), jax.ShapeDtypeStruct((T_LAT, D), F32)]
    else:
        out_specs, out_shape = tile, jax.ShapeDtypeStruct((T, D), F32)
    return pl.pallas_call(
        functools.partial(_combine_kernel, final=final),
        grid_spec=pltpu.PrefetchScalarGridSpec(
            num_scalar_prefetch=2,
            grid=(NT,),
            in_specs=[
                pl.BlockSpec(memory_space=pl.ANY),
                pl.BlockSpec((TM, 4), lambda i, *_: (i, 0)),
                tile,
                pl.BlockSpec((1, 6, D), lambda i, *_: (_mod_row(i), 0, 0)),
                pl.BlockSpec((1, D), lambda i, *_: (0, 0)),
            ],
            out_specs=out_specs,
            scratch_shapes=[pltpu.VMEM((2, LCAP, D), BF16), pltpu.SemaphoreType.DMA((2,))],
        ),
        out_shape=out_shape,
        compiler_params=_cparams(1),
        name="moe_combine",
    )(gch, nused, y, info, x, mod_l, final_w.reshape(1, D))


def _moe_plan(route, counts):
    cnt = counts.reshape(NT, N_EXP).astype(I32)
    cpad = (cnt + GCH - 1) // GCH * GCH
    lo = jnp.cumsum(cpad, axis=1) - cpad
    nused = (lo[:, -1] + cpad[:, -1]) // GCH
    tot = jnp.sum(cpad, axis=0)
    gpad = (tot + TMB - 1) // TMB * TMB
    goff = jnp.cumsum(gpad) - gpad
    so = goff[None, :] + jnp.cumsum(cpad, axis=0) - cpad

    e1 = route[:, 0].astype(I32)
    e2 = route[:, 1].astype(I32)
    eid = jnp.arange(N_EXP, dtype=I32)[None, :]
    lo_tok = jnp.repeat(lo, TM, axis=0)
    dl1 = jnp.sum(jnp.where(e1[:, None] == eid, lo_tok, 0), axis=1).astype(F32) + route[:, 2]
    dl2 = jnp.sum(jnp.where(e2[:, None] == eid, lo_tok, 0), axis=1).astype(F32) + route[:, 3]
    dl_row = jnp.stack([dl1.reshape(NT, TM), dl2.reshape(NT, TM)], axis=1)
    info = jnp.stack([dl1, dl2, route[:, 4], route[:, 5]], axis=1)

    q = jnp.arange(NLC, dtype=I32)[None, :, None]
    lo16 = (lo // GCH)[:, None, :]
    c16 = (cpad // GCH)[:, None, :]
    in_seg = jnp.logical_and(q >= lo16, q < lo16 + c16)
    gch = jnp.sum(jnp.where(in_seg, (so // GCH)[:, None, :] + q - lo16, 0), axis=2).reshape(NT * NLC)

    nblk = gpad // TMB
    n_active = jnp.sum(nblk)
    b = jnp.arange(NBLK, dtype=I32)
    blk_i = jnp.maximum(jnp.minimum(b, n_active - 1), 0)
    bend = (goff + gpad) // TMB
    blk_e = jnp.minimum(jnp.sum((blk_i[:, None] >= bend[None, :]).astype(I32), axis=1), N_EXP - 1)
    used = n_active * TMB
    pad0 = jnp.concatenate([goff + tot, used.reshape(1)]) // GCH
    npad = jnp.concatenate([gpad - tot, (RMAX - used).reshape(1)]) // GCH
    return dict(dl_row=dl_row, info=info, gch=gch.astype(I32), nused=nused.astype(I32),
                blk_e=blk_e.astype(I32), blk_i=blk_i.astype(I32),
                n_active=n_active.reshape(1).astype(I32), pad0=pad0.astype(I32), npad=npad.astype(I32))


def _grid_pos_embed(n_tokens):
    rows = n_tokens // GRID_W
    r = jnp.repeat(jnp.arange(rows, dtype=F32), GRID_W)
    col = jnp.tile(jnp.arange(GRID_W, dtype=F32), rows)
    quarter = D // 4
    omega = 1.0 / (10000.0 ** (jnp.arange(quarter, dtype=F32) / quarter))
    ang_r = r[:, None] * omega[None]
    ang_c = col[:, None] * omega[None]
    return jnp.concatenate([jnp.sin(ang_r), jnp.cos(ang_r), jnp.sin(ang_c), jnp.cos(ang_c)], axis=-1)


def _hy_pos_features(n):
    pos = jnp.arange(n, dtype=F32)
    t = pos / (n - 1)
    bands = jnp.linspace(1e-4, HY_BANDS - 1, HY_BANDS, dtype=F32)
    ang = (2.0 * math.pi * pos / n)[:, None] * bands[None]
    z = jnp.concatenate([t[:, None], jnp.cos(ang), -jnp.sin(ang)], axis=-1)
    z = jnp.pad(z, ((0, 0), (0, LANE - HY_POS_DIM)))
    half = n // 2
    dist = jnp.abs(pos - half) / half
    deltas = jnp.abs(jnp.linspace(HY_MIN_DECAY, HY_MAX_DECAY, D_HY, dtype=F32))
    return z, jnp.exp(-dist[:, None] * deltas[None])


def _dft_mats(lseq, n, kb):
    nkb = n // 2 // kb
    t = jnp.arange(lseq, dtype=I32)

    def tables(tt):
        def cs(freq):
            ang = (2.0 * math.pi / n) * ((freq[:, None] * tt[None, :]) % n).astype(F32)
            return jnp.cos(ang), jnp.sin(ang)
        (ca, sa), (cb, sb) = cs(jnp.arange(nkb, dtype=I32) * kb), cs(jnp.arange(kb, dtype=I32))
        ca, sa, cb, sb = ca[:, None, :], sa[:, None, :], cb[None], sb[None]
        re = ca * cb - sa * sb
        im = -(sa * cb + ca * sb)
        dc = jnp.logical_and(jnp.arange(nkb)[:, None, None] == 0, jnp.arange(kb)[None, :, None] == 0)
        alt = (1.0 - 2.0 * (tt % 2).astype(F32))[None, None, :]
        return re, jnp.where(dc, alt, im), dc

    re, im, _ = tables(t)
    f = jnp.concatenate([re, im], axis=1).reshape(n, lseq)
    re, im, dc = tables(t + lseq // 2)
    wk = jnp.where(dc, 1.0, 2.0) / n
    gt = jnp.concatenate([wk * re, jnp.where(dc, 1.0 / n, wk) * im], axis=1).reshape(n, lseq)
    return f.astype(BF16), gt.T.astype(BF16)


def _ssd_constants():
    hp = SSM_H * SSM_P
    hq = SSM_H // SSM_G
    e64 = np.zeros((2, 8 * SSM_H, 2 * hp), np.float32)
    for d in range(2):
        for q in range(4):
            for hh in range(SSM_H):
                e64[d, q * 2 * SSM_H + d * SSM_H + hh, (q // 2) * hp + hh * SSM_P:(q // 2) * hp + (hh + 1) * SSM_P] = 1.0
    hq_of = np.arange(hq * SSM_P) // SSM_P
    mbd = (np.arange(hq * SSM_Q)[:, None] // SSM_Q == hq_of[None, :]).astype(np.float32)
    mdiag = (hq_of[:, None] == hq_of[None, :]).astype(np.float32)
    return dict(e64=jnp.asarray(e64, dtype=BF16), mbd=jnp.asarray(mbd, dtype=BF16), mdiag=jnp.asarray(mdiag))


def _block_diag_heads(w):
    eye = jnp.eye(LRU_HEADS, dtype=w.dtype)
    return jnp.einsum("ldhij,hg->ldhigj", w, eye).reshape(DEPTH, 2, D_LRU, D_LRU)


def kernel(x_prompt, x_sample, state_lru, state_ssm, c, c_ctx, norm1_w, norm2_w, final_norm_w, ada_w, ada_b,
           w_in, w_out, lru_conv_w, lru_conv_b, lru_wa, lru_ba, lru_wi, lru_bi, lru_lambda, hy_conv_w, hy_conv_b,
           hy_w1, hy_b1, hy_w2, hy_b2, hy_freq, hy_w3, hy_bias, ssm_conv_w, ssm_conv_b, ssm_dt_bias, ssm_a_log,
           ssm_d, ssm_norm_w, ffn_w_gate, ffn_w_up, ffn_w_down, moe_router, moe_w_gate, moe_w_up, moe_w_down):
    hp = SSM_H * SSM_P
    wa, wi = _block_diag_heads(lru_wa), _block_diag_heads(lru_wi)
    row = lambda a: a.reshape(DEPTH, 1, -1)
    p = {
        "lru_conv_w": lru_conv_w, "lru_conv_b": row(lru_conv_b), "lru_lambda": lru_lambda,
        "lru_wbig": jnp.concatenate([wa[:, 0], wi[:, 0], wa[:, 1], wi[:, 1]], axis=-1).astype(BF16),
        "lru_bias": jnp.concatenate([lru_ba[:, 0], lru_bi[:, 0], lru_ba[:, 1], lru_bi[:, 1]], axis=-1)[:, None],
        "hy_conv_w": hy_conv_w, "hy_conv_b": row(hy_conv_b), "hy_bias": row(hy_bias),
        "hy_w1t": jnp.swapaxes(jnp.pad(hy_w1, ((0, 0), (0, LANE - HY_POS_DIM), (0, 0))), 1, 2),
        "hy_b1": hy_b1[:, :, None], "hy_w2t": jnp.swapaxes(hy_w2, 1, 2), "hy_b2": hy_b2[:, :, None],
        "hy_freq": hy_freq[:, :, None], "hy_w3t": jnp.swapaxes(hy_w3, 1, 2),
        "ssm_conv_w": ssm_conv_w, "ssm_conv_b": row(ssm_conv_b),
        "ssm_dtb_row": row(ssm_dt_bias), "ssm_dtb_col": ssm_dt_bias.reshape(DEPTH, 2 * SSM_H, 1),
        "ssm_alog_row": row(ssm_a_log), "ssm_alog_col": ssm_a_log.reshape(DEPTH, 2 * SSM_H, 1),
        "ssm_d_exp": jnp.repeat(ssm_d, SSM_P, axis=-1)[:, None], "ssm_norm_w": row(ssm_norm_w),
    }
    w_dtT = jnp.swapaxes(w_in[:, :, D_MAIN:], 1, 2)
    ffn_w = (ffn_w_gate, ffn_w_up, ffn_w_down)
    moe_w = (moe_w_gate, moe_w_up, moe_w_down)

    cst = _ssd_constants()
    cst["lru_rep"] = jnp.asarray(np.arange(RB)[:, None] // SUBLANES == np.arange(RB // SUBLANES)[None, :], dtype=BF16)
    z_ctx, win_ctx = _hy_pos_features(L_CTX)
    z_lat, win_lat = _hy_pos_features(L_LAT)
    cst["f_ctx"], cst["g_ctx"] = _dft_mats(L_CTX, N_FFT_CTX, N_FFT_CTX // 2)
    cst["f_lat"], cst["g_lat"] = _dft_mats(L_LAT, N_FFT_LAT, HY_KB_LAT)
    cst["hf_ctx"] = _hy_filter_spectrum(L_CTX, N_FFT_CTX // 2, z_ctx.T, win_ctx.T, cst["f_ctx"], p)
    cst["hf_lat"] = _hy_filter_spectrum(L_LAT, HY_KB_LAT, z_lat.T, win_lat.T, cst["f_lat"], p)

    cond = jnp.concatenate([c_ctx[None], c], axis=0)
    mod = _mod_table(jnp.broadcast_to(cond[:, :, None], (3, D, LANE)), ada_w, ada_b)
    mod = mod[:, :3].reshape(DEPTH, 3, 6, D)

    x = jnp.concatenate([x_prompt.reshape(T_CTX, D),
                         (x_sample + _grid_pos_embed(L_LAT)[None]).reshape(T_LAT, D)], axis=0)
    st_ssm_in = state_ssm.reshape(N_LAT_SEQ, DEPTH, 2, hp, SSM_N)

    lru_states = []
    new_ssm = jnp.zeros((N_CTX_SEQ, DEPTH, 2, SSM_H, SSM_P, SSM_N), F32)
    for l in range(DEPTH):
        u_lru, u_hy, u_z, u_xbc, u_dt, u_dtT = _k1(l, x, mod[l], norm1_w, w_in, w_dtT)
        o_lru, s_lru = _lru_mixer(l, u_lru, p, cst, state_lru)
        o_hy = _hy_mixer(l, u_hy, p, cst)
        o_ssm, new_ssm = _ssd_mixer(l, u_z, u_xbc, u_dt, u_dtT, p, cst, st_ssm_in, new_ssm)
        lru_states.append(s_lru.reshape(N_CTX_SEQ, 2, D_LRU))
        j = l // 2
        if l % 2 == 0:
            x, h2 = _k2(l, o_lru, o_hy, o_ssm, x, mod[l], norm2_w, w_out)
            x = _dense_ffn(j, h2, x, mod[l], *ffn_w)
        else:
            x, h2, route, counts = _k2(l, o_lru, o_hy, o_ssm, x, mod[l], norm2_w, w_out, moe_router, j)
            plan = _moe_plan(route, counts)
            xs = _sort_tokens(h2, plan["dl_row"], plan["gch"], plan["nused"], plan["pad0"], plan["npad"])
            y = _expert_ffn(j, xs, plan["blk_e"], plan["blk_i"], plan["n_active"], *moe_w)
            x = _combine(y, plan["info"], x, mod[l], final_norm_w, plan["gch"], plan["nused"],
                         final=(l == DEPTH - 1))
    y_prompt = x[0].reshape(N_CTX_SEQ, L_CTX, D)
    y_sample = x[1].reshape(N_LAT_SEQ, L_LAT, D)
    return (y_prompt, y_sample, jnp.stack(lru_states, axis=1), new_ssm)
```

```python
import functools
import math

import numpy as np
import jax
import jax.numpy as jnp
from jax import lax
from jax.experimental import pallas as pl
from jax.experimental.pallas import tpu as pltpu

F32 = jnp.float32
BF16 = jnp.bfloat16
I32 = jnp.int32
HI = lax.Precision.HIGHEST

D = 1024
N_CTX_SEQ, L_CTX = 16, 256
N_LAT_SEQ, L_LAT = 2, 2048
DEPTH = 4
GRID_W = 64
D_LRU = 256
LRU_HEADS, LRU_HD = 4, 64
LRU_C = 8.0
D_HY = 256
HY_BANDS = 16
HY_POS_DIM = 1 + 2 * HY_BANDS
HY_HID = 64
HY_MAX_DECAY = math.log(1e-2) / 0.3
HY_MIN_DECAY = math.log(1e-2) / 1.5
D_SSM = 512
SSM_P = 64
SSM_H = 8
SSM_G = 2
SSM_N = 64
SSM_Q = 128
D_XBC = D_SSM + 2 * SSM_G * SSM_N
D_MAIN = 2 * D_LRU + 3 * D_HY + D_SSM + D_XBC
D_IN = D_MAIN + 2 * SSM_H
D_FF = 2816
N_EXP = 8
EPS = 1e-6

LANE = 128
SUBLANES = 8
BF16_ROWS = 16
T_CTX = N_CTX_SEQ * L_CTX
T_LAT = N_LAT_SEQ * L_LAT
T = T_CTX + T_LAT
TM = 512
NT = T // TM
NT_CTX = T_CTX // TM
NT_PER_LAT = L_LAT // TM
RB = 2048
NB = T // RB
NB_CTX = T_CTX // RB
FF_CHUNK = 256
N_FF_CHUNK = D_FF // FF_CHUNK
FFN_STAGES = 3
VMEM_LIMIT = 56 * 1024 * 1024

GCH = BF16_ROWS
LCAP = 2 * TM + N_EXP * GCH
NLC = LCAP // GCH
TMB = 512
RMAX = -(-(2 * T + NT * N_EXP * (GCH - 1) + N_EXP * (TMB - 1)) // TMB) * TMB
NBLK = RMAX // TMB

HY_KB_LAT = 512
N_FFT_LAT = 3 * L_LAT // 2
N_FFT_CTX = 3 * L_CTX // 2
HY_NKB = N_FFT_LAT // 2 // HY_KB_LAT


def _cparams(n_axes=1, vmem=VMEM_LIMIT):
    return pltpu.CompilerParams(dimension_semantics=("arbitrary",) * n_axes, vmem_limit_bytes=vmem)


def _mod_row(i):
    return jnp.where(i < NT_CTX, 0, 1 + (i - NT_CTX) // NT_PER_LAT)


def _bdot(a, b):
    return jnp.dot(a.astype(BF16), b.astype(BF16), preferred_element_type=F32)


def _rms(x):
    return x * lax.rsqrt(jnp.mean(x * x, axis=-1, keepdims=True) + EPS)


def _split_bf16(v, parts):
    out = []
    for _ in range(parts):
        piece = v.astype(BF16)
        out.append(piece)
        v = v - piece.astype(F32)
    return out


def _mod_kernel(cb_ref, w_ref, b_ref, o_ref):
    tn = w_ref.shape[2]

    def body(kc, accs):
        k0 = pl.multiple_of(kc * 8, 8)
        wk = w_ref[0, pl.ds(k0, 8), :]
        out = []
        for r in range(3):
            c = cb_ref[r, pl.ds(k0, 8), :]
            c = c * jax.nn.sigmoid(c)
            out.append(accs[r] + jnp.tile(c, (1, tn // LANE)) * wk)
        return tuple(out)

    accs = lax.fori_loop(0, D // 8, body, tuple(jnp.zeros((8, tn), F32) for _ in range(3)), unroll=8)
    rows = [jnp.sum(a, axis=0, keepdims=True) + b_ref[0] for a in accs]
    o_ref[0] = jnp.concatenate(rows + [jnp.zeros((5, tn), F32)], axis=0)


def _mod_table(cond_b, ada_w, ada_b):
    tn = 1024
    return pl.pallas_call(
        _mod_kernel,
        grid=(DEPTH, 6 * D // tn),
        in_specs=[
            pl.BlockSpec((3, D, LANE), lambda l, j: (0, 0, 0)),
            pl.BlockSpec((1, D, tn), lambda l, j: (l, 0, j)),
            pl.BlockSpec((1, 1, tn), lambda l, j: (l, 0, j)),
        ],
        out_specs=pl.BlockSpec((1, 8, tn), lambda l, j: (l, 0, j)),
        out_shape=jax.ShapeDtypeStruct((DEPTH, 8, 6 * D), F32),
        compiler_params=_cparams(2),
        name="mod_table",
    )(cond_b, ada_w, ada_b.reshape(DEPTH, 1, 6 * D))


def _k1_kernel(x_ref, mod_ref, nw_ref, w_hbm, wdtT_ref,
               o_lru, o_hy, o_z, o_xbc, o_dt, o_dtT, wbf_ref, stage_ref, sem, *, l):
    @pl.when(pl.program_id(0) == 0)
    def _():
        cp = pltpu.make_async_copy(w_hbm.at[l], stage_ref, sem.at[0])
        cp.start()
        cp.wait()
        wbf_ref[...] = stage_ref[...].astype(BF16)

    m = mod_ref[0]
    h = _rms(x_ref[...]) * nw_ref[0]
    h = h * (1.0 + m[1:2]) + m[0:1]
    hb = h.astype(BF16)

    def proj(lo, hi):
        return jnp.dot(hb, wbf_ref[:, lo:hi], preferred_element_type=F32)

    o_lru[...] = proj(0, 512)
    o_hy[...] = proj(512, 1280)
    o_z[...] = proj(1280, 1792)
    o_xbc[...] = proj(1792, 2560)
    o_dt[...] = proj(D_MAIN, D_IN)
    dtT = lax.dot_general(wdtT_ref[0].astype(BF16), hb, (((1,), (1,)), ((), ())),
                          preferred_element_type=F32)
    for j in range(TM // SSM_Q):
        o_dtT[j] = dtT[:, j * SSM_Q:(j + 1) * SSM_Q]


def _k1(l, x, mod_l, norm1_w, w_in, w_dtT):
    tok = lambda w: pl.BlockSpec((TM, w), lambda i: (i, 0))
    return pl.pallas_call(
        functools.partial(_k1_kernel, l=l),
        grid=(NT,),
        in_specs=[
            tok(D),
            pl.BlockSpec((1, 6, D), lambda i: (_mod_row(i), 0, 0)),
            pl.BlockSpec((1, 1, D), lambda i: (l, 0, 0)),
            pl.BlockSpec(memory_space=pl.ANY),
            pl.BlockSpec((1, 2 * SSM_H, D), lambda i: (l, 0, 0)),
        ],
        out_specs=[tok(512), tok(768), tok(512), tok(768), tok(2 * SSM_H),
                   pl.BlockSpec((TM // SSM_Q, 2 * SSM_H, SSM_Q), lambda i: (i, 0, 0))],
        out_shape=[jax.ShapeDtypeStruct((T, 512), F32), jax.ShapeDtypeStruct((T, 768), F32),
                   jax.ShapeDtypeStruct((T, 512), F32), jax.ShapeDtypeStruct((T, 768), F32),
                   jax.ShapeDtypeStruct((T, 2 * SSM_H), F32),
                   jax.ShapeDtypeStruct((T // SSM_Q, 2 * SSM_H, SSM_Q), F32)],
        scratch_shapes=[pltpu.VMEM((D, D_IN), BF16), pltpu.VMEM((D, D_IN), F32),
                        pltpu.SemaphoreType.DMA((1,))],
        compiler_params=_cparams(1),
        name=f"k1_inproj_{l}",
    )(x, mod_l, norm1_w.reshape(DEPTH, 1, D), w_in, w_dtT)


def _row_in_seq(rows, lseq):
    return lax.broadcasted_iota(I32, (rows, 1), 0) & (lseq - 1)


def _shift_rows(x, s, rin, lseq):
    if s == 0:
        return x
    y = pltpu.roll(x, s % x.shape[0], axis=0)
    valid = (rin >= s) if s > 0 else (rin < lseq + s)
    return jnp.where(valid, y, 0.0)


def _dwconv(x, w_ref, b_ref, rin, lseq):
    k_w = w_ref.shape[0]
    y = b_ref[...]
    for k in range(k_w):
        y = y + w_ref[k:k + 1, :] * _shift_rows(x, k_w // 2 - k, rin, lseq)
    return y


def _lru_scan(a, b, d, rin, lseq, rep_ref, ab_s):
    rows = a.shape[0]
    ngrp, gps = rows // SUBLANES, lseq // SUBLANES
    sub = rin & (SUBLANES - 1)
    for s in (1, 2, 4):
        sh = (s if d == 0 else -s) % rows
        valid = (sub >= s) if d == 0 else (sub < SUBLANES - s)
        b = b + jnp.where(valid, a * pltpu.roll(b, sh, axis=0), 0.0)
        a = jnp.where(valid, a * pltpu.roll(a, sh, axis=0), a)
    edge = SUBLANES - 1 if d == 0 else 0

    def group_edges(k, v):
        for j in range(D_LRU // LANE):
            ab_s[k, j] = v[:, j * LANE:(j + 1) * LANE]
        return jnp.concatenate([ab_s[k, j, pl.ds(edge, ngrp, stride=SUBLANES), :]
                                for j in range(D_LRU // LANE)], axis=1)

    ga, gb = group_edges(0, a), group_edges(1, b)
    gin = lax.broadcasted_iota(I32, (ngrp, 1), 0) & (gps - 1)
    s = 1
    while s < gps:
        sh = (s if d == 0 else -s) % ngrp
        valid = (gin >= s) if d == 0 else (gin < gps - s)
        gb = gb + jnp.where(valid, ga * pltpu.roll(gb, sh, axis=0), 0.0)
        if 2 * s < gps:
            ga = jnp.where(valid, ga * pltpu.roll(ga, sh, axis=0), ga)
        s *= 2
    valid = (gin >= 1) if d == 0 else (gin < gps - 1)
    cin = jnp.where(valid, pltpu.roll(gb, (1 if d == 0 else -1) % ngrp, axis=0), 0.0)
    hi, lo = _split_bf16(cin, 2)
    cin_x = (jnp.dot(rep_ref[...], hi, preferred_element_type=F32)
             + jnp.dot(rep_ref[...], lo, preferred_element_type=F32))
    return a * cin_x + b


def _lru_block(u_ref, cw_ref, cb_ref, wbig_ref, bias_ref, lam_ref, rep_ref, h0_ref, o_ref, st_ref, ab_s, lseq):
    rows = u_ref.shape[0]
    rin = _row_in_seq(rows, lseq)
    u = u_ref[...]
    gate = u[:, D_LRU:]
    x = _dwconv(u[:, :D_LRU], cw_ref, cb_ref, rin, lseq)
    xb = x.astype(BF16)
    y = None
    finals = []
    for d in range(2):
        g = jnp.dot(xb, wbig_ref[:, 512 * d:512 * (d + 1)], preferred_element_type=F32)
        g = g + bias_ref[:, 512 * d:512 * (d + 1)]
        r = jax.nn.sigmoid(g[:, :D_LRU])
        ig = jax.nn.sigmoid(g[:, D_LRU:])
        log_a = -LRU_C * r * jax.nn.softplus(-lam_ref[d:d + 1, :])
        a = jnp.exp(log_a)
        th = jnp.tanh(log_a)
        b = jnp.sqrt(-2.0 * th / (1.0 - th)) * (ig * x)
        if h0_ref is not None:
            edge = (rin == 0) if d == 0 else (rin == lseq - 1)
            b = b + jnp.where(edge, a * h0_ref[d:d + 1, :], 0.0)
        b = _lru_scan(a, b, d, rin, lseq, rep_ref, ab_s)
        y = b if y is None else y + b
        if st_ref is not None:
            last = lseq - 1 if d == 0 else 0
            finals.append(jnp.concatenate(
                [b[j * lseq + last:j * lseq + last + 1, :] for j in range(rows // lseq)], axis=0))
    o_ref[...] = y * jax.nn.gelu(gate)
    if st_ref is not None:
        st_ref[...] = jnp.concatenate(finals, axis=1)


def _lru_kernel(u_ref, cw_ref, cb_ref, wbig_ref, bias_ref, lam_ref, rep_ref, h0_ref, o_ref, st_ref, ab_s):
    b = pl.program_id(0)
    args = (u_ref, cw_ref.at[0], cb_ref.at[0], wbig_ref.at[0], bias_ref.at[0], lam_ref.at[0], rep_ref)

    @pl.when(b < NB_CTX)
    def _():
        _lru_block(*args, None, o_ref, st_ref, ab_s, L_CTX)

    @pl.when(b >= NB_CTX)
    def _():
        _lru_block(*args, h0_ref.at[0, 0], o_ref, None, ab_s, L_LAT)


def _lru_mixer(l, u_lru, p, c, state_lru):
    lsel = lambda *shape: pl.BlockSpec((1,) + shape, lambda b: (l,) + (0,) * len(shape))
    return pl.pallas_call(
        _lru_kernel,
        grid=(NB,),
        in_specs=[
            pl.BlockSpec((RB, 512), lambda b: (b, 0)),
            lsel(4, D_LRU), lsel(1, D_LRU), lsel(D_LRU, 1024), lsel(1, 1024), lsel(2, D_LRU),
            pl.BlockSpec((RB, RB // SUBLANES), lambda b: (0, 0), pipeline_mode=pl.Buffered(1)),
            pl.BlockSpec((1, 1, 2, D_LRU), lambda b: (jnp.maximum(b - NB_CTX, 0), l, 0, 0)),
        ],
        out_specs=[pl.BlockSpec((RB, D_LRU), lambda b: (b, 0)),
                   pl.BlockSpec((RB // L_CTX, 2 * D_LRU), lambda b: (jnp.minimum(b, NB_CTX - 1), 0))],
        out_shape=[jax.ShapeDtypeStruct((T, D_LRU), F32),
                   jax.ShapeDtypeStruct((N_CTX_SEQ, 2 * D_LRU), F32)],
        scratch_shapes=[pltpu.VMEM((2, D_LRU // LANE, RB, LANE), F32)],
        compiler_params=_cparams(1),
        name=f"lru_mixer_{l}",
    )(u_lru, p["lru_conv_w"], p["lru_conv_b"], p["lru_wbig"], p["lru_bias"], p["lru_lambda"],
      c["lru_rep"], state_lru)


def _hy_filter_kernel(z_ref, win_ref, f_ref, w1_ref, b1_ref, w2_ref, b2_ref, fr_ref, w3_ref,
                      o_ref, h_ref):
    l = pl.program_id(1)

    @pl.when(pl.program_id(0) == 0)
    def _():
        fr = fr_ref[0]
        g = jnp.sin(fr * (jnp.dot(w1_ref[0], z_ref[...], precision=HI, preferred_element_type=F32)
                          + b1_ref[0]))
        g = jnp.sin(fr * (jnp.dot(w2_ref[0], g, precision=HI, preferred_element_type=F32) + b2_ref[0]))
        h = jnp.dot(w3_ref[0], g, precision=HI, preferred_element_type=F32)
        h_ref[l] = (h * win_ref[...]).astype(BF16)

    o_ref[0] = lax.dot_general(f_ref[...], h_ref[l], (((1,), (1,)), ((), ())), preferred_element_type=F32)


def _hy_filter_spectrum(lseq, kb, zfeat_t, window_t, fmat, p):
    n = fmat.shape[0]
    lsel = lambda *shape: pl.BlockSpec((1,) + shape, lambda k, l: (l,) + (0,) * len(shape))
    return pl.pallas_call(
        _hy_filter_kernel,
        grid=(n // (2 * kb), DEPTH),
        in_specs=[
            pl.BlockSpec((LANE, lseq), lambda k, l: (0, 0)),
            pl.BlockSpec((D_HY, lseq), lambda k, l: (0, 0)),
            pl.BlockSpec((2 * kb, lseq), lambda k, l: (k, 0)),
            lsel(HY_HID, LANE), lsel(HY_HID, 1), lsel(HY_HID, HY_HID), lsel(HY_HID, 1),
            lsel(HY_HID, 1), lsel(D_HY, HY_HID),
        ],
        out_specs=pl.BlockSpec((1, 2 * kb, D_HY), lambda k, l: (l, k, 0)),
        out_shape=jax.ShapeDtypeStruct((DEPTH, n, D_HY), F32),
        scratch_shapes=[pltpu.VMEM((DEPTH, D_HY, lseq), BF16)],
        compiler_params=_cparams(2),
        name=f"hyena_filter_{lseq}",
    )(zfeat_t, window_t, fmat, p["hy_w1t"], p["hy_b1"], p["hy_w2t"], p["hy_b2"], p["hy_freq"], p["hy_w3t"])


def _hy_spectral_block(f_blk, g_blk, hf, z_bf, is_dc_block):
    kb = f_blk.shape[0] // 2
    zf = jnp.dot(f_blk, z_bf, preferred_element_type=F32)
    rz, iz = zf[:kb], zf[kb:]
    rh, ih = hf[:kb], hf[kb:]
    ii = iz * ih
    re = rz * rh - ii
    im = rz * ih + iz * rh
    if is_dc_block is not None:
        dc = jnp.logical_and(lax.broadcasted_iota(I32, (kb, 1), 0) == 0, is_dc_block)
        re = jnp.where(dc, rz * rh, re)
        im = jnp.where(dc, ii, im)
    pr = jnp.concatenate([re, im], axis=0).astype(BF16)
    return jnp.dot(g_blk, pr, preferred_element_type=F32)


def _hy_prologue(u_ref, cw_ref, cb_ref, lseq, z_ref, zbf_ref, x2_ref):
    rin = _row_in_seq(u_ref.shape[0], lseq)
    uc = _dwconv(u_ref[...], cw_ref, cb_ref, rin, lseq)
    z = uc[:, :D_HY] * uc[:, D_HY:2 * D_HY]
    z_ref[...] = z
    zbf_ref[...] = z.astype(BF16)
    x2_ref[...] = uc[:, 2 * D_HY:]


def _hy_kernel(u_ref, cw_ref, cb_ref, hb_ref, fc_ref, gc_ref, hfc_ref, fl_ref, gl_ref, hfl_ref,
               o_ref, z_ref, zbf_ref, x2_ref, acc_ref):
    b = pl.program_id(0)
    k = pl.program_id(1)
    cw, cb = cw_ref.at[0], cb_ref.at[0]

    @pl.when(jnp.logical_and(b < NB_CTX, k == 0))
    def _():
        _hy_prologue(u_ref, cw, cb, L_CTX, z_ref, zbf_ref, x2_ref)
        for s in range(RB // L_CTX):
            rows = slice(s * L_CTX, (s + 1) * L_CTX)
            acc_ref[rows, :] = _hy_spectral_block(fc_ref[...], gc_ref[...], hfc_ref[0], zbf_ref[rows, :], True)

    @pl.when(b >= NB_CTX)
    def _():
        @pl.when(k == 0)
        def _():
            _hy_prologue(u_ref, cw, cb, L_LAT, z_ref, zbf_ref, x2_ref)
            acc_ref[...] = jnp.zeros_like(acc_ref)

        acc_ref[...] += _hy_spectral_block(fl_ref[...], gl_ref[...], hfl_ref[0], zbf_ref[...], k == 0)

    @pl.when(k == HY_NKB - 1)
    def _():
        o_ref[...] = x2_ref[...] * (acc_ref[...] + hb_ref[0] * z_ref[...])


def _hy_mixer(l, u_hy, p, c):
    lat_k = lambda b, k: jnp.where(b < NB_CTX, 0, k)
    lsel = lambda *shape: pl.BlockSpec((1,) + shape, lambda b, k: (l,) + (0,) * len(shape))
    kbl = 2 * HY_KB_LAT
    return pl.pallas_call(
        _hy_kernel,
        grid=(NB, HY_NKB),
        in_specs=[
            pl.BlockSpec((RB, 3 * D_HY), lambda b, k: (b, 0)),
            lsel(3, 3 * D_HY), lsel(1, 3 * D_HY), lsel(1, D_HY),
            pl.BlockSpec((N_FFT_CTX, L_CTX), lambda b, k: (0, 0)),
            pl.BlockSpec((L_CTX, N_FFT_CTX), lambda b, k: (0, 0)),
            lsel(N_FFT_CTX, D_HY),
            pl.BlockSpec((kbl, L_LAT), lambda b, k: (lat_k(b, k), 0)),
            pl.BlockSpec((L_LAT, kbl), lambda b, k: (0, lat_k(b, k))),
            pl.BlockSpec((1, kbl, D_HY), lambda b, k: (l, lat_k(b, k), 0)),
        ],
        out_specs=pl.BlockSpec((RB, D_HY), lambda b, k: (b, 0)),
        out_shape=jax.ShapeDtypeStruct((T, D_HY), F32),
        scratch_shapes=[pltpu.VMEM((RB, D_HY), F32), pltpu.VMEM((RB, D_HY), BF16),
                        pltpu.VMEM((RB, D_HY), F32), pltpu.VMEM((RB, D_HY), F32)],
        compiler_params=_cparams(2),
        name=f"hyena_mixer_{l}",
    )(u_hy, p["hy_conv_w"], p["hy_conv_b"], p["hy_bias"],
      c["f_ctx"], c["g_ctx"], c["hf_ctx"], c["f_lat"], c["g_lat"], c["hf_lat"])


def _ssd_block(u_z, u_xbc, u_dt, u_dtT, cw, cb, dtb_row, dtb_col, alog_row, alog_col, d_exp, nw,
               e64_ref, mbd_ref, mdiag_ref, h0_ref, o_ref, st_ref,
               x_s, bc_s, yb_s, sf_s, sb_s, cs_s, col_s, row_s, lseq):
    y_acc = (o_ref, yb_s)
    s_dir = (sf_s, sb_s)
    rows = u_z.shape[0]
    nchunk = rows // SSM_Q
    cps = lseq // SSM_Q
    rin = _row_in_seq(rows, lseq)
    for c0 in range(0, D_XBC, LANE):
        cols = slice(c0, c0 + LANE)
        xbc = _dwconv(u_xbc[:, cols], cw.at[:, cols], cb.at[:, cols], rin, lseq)
        xbc = xbc * jax.nn.sigmoid(xbc)
        if c0 < D_SSM:
            x_s[:, cols] = xbc
        else:
            bc_s[:, c0 - D_SSM:c0 - D_SSM + LANE] = xbc

    li = lax.broadcasted_iota(I32, (SSM_Q, SSM_Q), 0)
    si = lax.broadcasted_iota(I32, (SSM_Q, SSM_Q), 1)
    low_half = lax.broadcasted_iota(I32, (SSM_Q, LANE), 1) < SSM_N
    hq = SSM_H // SSM_G
    wq = hq * SSM_P
    hp = SSM_H * SSM_P

    for d in range(2):
        for g in range(SSM_G):
            if h0_ref is not None:
                s_dir[d][g] = jnp.tile(h0_ref[d, g * wq:(g + 1) * wq, :], (1, hq))
            else:
                s_dir[d][g] = jnp.zeros((wq, wq), F32)

    nh2 = 2 * SSM_H
    tri_f = (li >= si).astype(BF16)
    tri_b = (li <= si).astype(BF16)
    fwd_lane = lax.broadcasted_iota(I32, (1, nh2), 1) < SSM_H
    fwd_sub = lax.broadcasted_iota(I32, (nh2, 1), 0) < SSM_H
    a_row = -jnp.exp(alog_row[...])
    a_col = -jnp.exp(alog_col[...])
    def chunk_stats(c, carry):
        rs = pl.ds(pl.multiple_of(c * SSM_Q, SSM_Q), SSM_Q)
        dt_c = jax.nn.softplus(u_dt[rs, :] + dtb_row[...])
        hi, lo = _split_bf16(dt_c * a_row, 2)
        cs_f = jnp.dot(tri_f, hi, preferred_element_type=F32) + jnp.dot(tri_f, lo, preferred_element_type=F32)
        cs_b = jnp.dot(tri_b, hi, preferred_element_type=F32) + jnp.dot(tri_b, lo, preferred_element_type=F32)
        cs_col = jnp.where(fwd_lane, cs_f, cs_b)
        tot = jnp.where(fwd_lane, cs_f[SSM_Q - 1:SSM_Q, :], cs_b[0:1, :])
        cs_s[rs, :] = cs_col
        for k, v in enumerate((jnp.exp(cs_col), dt_c * jnp.exp(tot - cs_col))):
            for m, piece in enumerate(_split_bf16(v, 2)):
                col_s[rs, (2 * k + m) * nh2:(2 * k + m + 1) * nh2] = piece
        dt_r = jax.nn.softplus(u_dtT[c] + dtb_col[...])
        hi, lo = _split_bf16(dt_r * a_col, 2)
        csr_f = jnp.dot(hi, tri_b, preferred_element_type=F32) + jnp.dot(lo, tri_b, preferred_element_type=F32)
        csr_b = jnp.dot(hi, tri_f, preferred_element_type=F32) + jnp.dot(lo, tri_f, preferred_element_type=F32)
        row_s[c, :nh2, :] = jnp.where(fwd_sub, csr_f, csr_b)
        row_s[c, nh2:, :] = dt_r
        return carry

    lax.fori_loop(0, nchunk, chunk_stats, 0, unroll=4)

    def chunk_step(ci, carry, d):
        causal = (li >= si) if d == 0 else (li <= si)
        edge = SSM_Q - 1 if d == 0 else 0
        c = ci if d == 0 else nchunk - 1 - ci
        rsl = pl.ds(pl.multiple_of(c * SSM_Q, SSM_Q), SSM_Q)
        if h0_ref is None and cps < nchunk:
            first = (c % cps == 0) if d == 0 else (c % cps == cps - 1)
            s_dir[d][...] = s_dir[d][...] * jnp.where(first, 0.0, 1.0)

        cs_col = cs_s[rsl, :]
        rows_c = row_s[c]
        spread = jnp.dot(col_s[rsl, :], e64_ref[d], preferred_element_type=F32)
        ecs_x, wdec_x = spread[:, :hp], spread[:, hp:]
        etot_x = ecs_x[edge:edge + 1, :]

        bcm = bc_s[rsl, :]
        bm, cm = bcm[:, :LANE], bcm[:, LANE:]
        bm_r, cm_r = pltpu.roll(bm, SSM_N, axis=1), pltpu.roll(cm, SSM_N, axis=1)
        bmb, cmb = bm.astype(BF16), cm.astype(BF16)
        for g in range(SSM_G):
            gl = slice(g * SSM_N, (g + 1) * SSM_N)
            ql = slice(g * wq, (g + 1) * wq)
            same = low_half if g == 0 else jnp.logical_not(low_half)
            b2 = jnp.where(same, bm, bm_r)
            c2 = jnp.where(same, cm, cm_r)
            gmat = lax.dot_general(cmb[:, gl], bmb[:, gl], (((1,), (1,)), ((), ())),
                                   preferred_element_type=F32)
            sc = []
            for h in range(SSM_H * d + g * hq, SSM_H * d + (g + 1) * hq):
                diff = cs_col[:, h:h + 1] - rows_c[h:h + 1, :]
                decay = jnp.exp(jnp.where(causal, diff, -1e30))
                sc.append((gmat * decay * rows_c[nh2 + h:nh2 + h + 1, :]).astype(BF16))
            sc = jnp.concatenate(sc, axis=1)
            xq = x_s[rsl, ql]
            bd = jnp.tile(xq.astype(BF16), (hq, 1)) * mbd_ref[...]
            y = jnp.dot(sc, bd, preferred_element_type=F32)
            s_old = s_dir[d][g]
            y_off = lax.dot_general(jnp.concatenate([c2, c2], axis=1).astype(BF16),
                                    (s_old * mdiag_ref[...]).astype(BF16),
                                    (((1,), (1,)), ((), ())), preferred_element_type=F32)
            y_acc[d][rsl, ql] = y + y_off * ecs_x[:, ql]
            bx = (jnp.concatenate([b2, b2], axis=1) * wdec_x[:, ql]).astype(BF16)
            s_new = jnp.dot(xq.T.astype(BF16), bx, preferred_element_type=F32)
            s_dir[d][g] = s_old * etot_x[:, ql] + s_new

        if st_ref is not None:
            last = (c % cps == cps - 1) if d == 0 else (c % cps == 0)

            @pl.when(last)
            def _():
                for g in range(SSM_G):
                    for hl in range(hq):
                        blk = slice(hl * SSM_P, (hl + 1) * SSM_P)
                        st_ref[c // cps, 0, d, g * hq + hl] = s_dir[d][g, blk, blk]
        return carry

    for d in range(2):
        lax.fori_loop(0, nchunk, functools.partial(chunk_step, d=d), 0, unroll=2)

    for r0 in range(0, rows, 256):
        rsl = slice(r0, r0 + 256)
        y = o_ref[rsl, :] + yb_s[rsl, :] + d_exp[...] * x_s[rsl, :]
        z = u_z[rsl, :]
        y = y * (z * jax.nn.sigmoid(z))
        o_ref[rsl, :] = _rms(y) * nw[...]


def _ssd_kernel(u_z, u_xbc, u_dt, u_dtT, cw, cb, dtb_row, dtb_col, alog_row, alog_col, d_exp, nw,
                e64_ref, mbd_ref, mdiag_ref, h0_ref, st_in_ref, o_ref, st_ref,
                x_s, bc_s, yb_s, sf_s, sb_s, cs_s, col_s, row_s):
    del st_in_ref
    b = pl.program_id(0)
    args = (u_z, u_xbc, u_dt, u_dtT, cw.at[0], cb.at[0], dtb_row.at[0], dtb_col.at[0], alog_row.at[0],
            alog_col.at[0], d_exp.at[0], nw.at[0], e64_ref, mbd_ref, mdiag_ref)
    scr = (x_s, bc_s, yb_s, sf_s, sb_s, cs_s, col_s, row_s)

    @pl.when(b < NB_CTX)
    def _():
        _ssd_block(*args, None, o_ref, st_ref, *scr, L_CTX)

    @pl.when(b >= NB_CTX)
    def _():
        _ssd_block(*args, h0_ref.at[0, 0], o_ref, None, *scr, L_LAT)


def _ssd_mixer(l, u_z, u_xbc, u_dt, u_dtT, p, c, state_ssm, new_states):
    lsel = lambda *shape: pl.BlockSpec((1,) + shape, lambda b: (l,) + (0,) * len(shape))
    full = lambda a: pl.BlockSpec(a.shape, lambda b: (0,) * a.ndim, pipeline_mode=pl.Buffered(1))
    hp = SSM_H * SSM_P
    nseq_blk = RB // L_CTX
    return pl.pallas_call(
        _ssd_kernel,
        grid=(NB,),
        in_specs=[
            pl.BlockSpec((RB, D_SSM), lambda b: (b, 0), pipeline_mode=pl.Buffered(1)),
            pl.BlockSpec((RB, D_XBC), lambda b: (b, 0)),
            pl.BlockSpec((RB, 2 * SSM_H), lambda b: (b, 0)),
            pl.BlockSpec((RB // SSM_Q, 2 * SSM_H, SSM_Q), lambda b: (b, 0, 0)),
            lsel(4, D_XBC), lsel(1, D_XBC), lsel(1, 2 * SSM_H), lsel(2 * SSM_H, 1),
            lsel(1, 2 * SSM_H), lsel(2 * SSM_H, 1), lsel(1, D_SSM), lsel(1, D_SSM),
            full(c["e64"]), full(c["mbd"]), full(c["mdiag"]),
            pl.BlockSpec((1, 1, 2, hp, SSM_N), lambda b: (jnp.maximum(b - NB_CTX, 0), l, 0, 0, 0)),
            pl.BlockSpec(memory_space=pl.ANY),
        ],
        out_specs=[pl.BlockSpec((RB, D_SSM), lambda b: (b, 0)),
                   pl.BlockSpec((nseq_blk, 1, 2, SSM_H, SSM_P, SSM_N),
                                lambda b: (jnp.minimum(b, NB_CTX - 1), l, 0, 0, 0, 0))],
        out_shape=[jax.ShapeDtypeStruct((T, D_SSM), F32),
                   jax.ShapeDtypeStruct(new_states.shape, F32)],
        input_output_aliases={16: 1},
        scratch_shapes=[pltpu.VMEM((RB, D_SSM), F32), pltpu.VMEM((RB, 2 * SSM_G * SSM_N), F32),
                        pltpu.VMEM((RB, D_SSM), F32),
                        pltpu.VMEM((SSM_G, hp // SSM_G, hp // SSM_G), F32),
                        pltpu.VMEM((SSM_G, hp // SSM_G, hp // SSM_G), F32),
                        pltpu.VMEM((RB, 2 * SSM_H), F32), pltpu.VMEM((RB, 8 * SSM_H), BF16),
                        pltpu.VMEM((RB // SSM_Q, 4 * SSM_H, SSM_Q), F32)],
        compiler_params=_cparams(1),
        name=f"ssd_mixer_{l}",
    )(u_z, u_xbc, u_dt, u_dtT, p["ssm_conv_w"], p["ssm_conv_b"], p["ssm_dtb_row"], p["ssm_dtb_col"],
      p["ssm_alog_row"], p["ssm_alog_col"], p["ssm_d_exp"], p["ssm_norm_w"],
      c["e64"], c["mbd"], c["mdiag"], state_ssm, new_states)


def _k2_kernel(*refs, routed):
    if routed:
        (ol_ref, oh_ref, os_ref, x_ref, mod_ref, nw_ref, w_ref, rt_ref,
         xo_ref, h2_ref, route_ref, cnt_ref, wbf_ref) = refs
    else:
        ol_ref, oh_ref, os_ref, x_ref, mod_ref, nw_ref, w_ref, xo_ref, h2_ref, wbf_ref = refs

    @pl.when(pl.program_id(0) == 0)
    def _():
        wbf_ref[...] = w_ref[0].astype(BF16)

    m = mod_ref[0]
    o = jnp.dot(ol_ref[...].astype(BF16), wbf_ref[0:256, :], preferred_element_type=F32)
    o = o + jnp.dot(oh_ref[...].astype(BF16), wbf_ref[256:512, :], preferred_element_type=F32)
    o = o + jnp.dot(os_ref[...].astype(BF16), wbf_ref[512:1024, :], preferred_element_type=F32)
    x = x_ref[...] + m[2:3] * o
    xo_ref[...] = x
    h2 = _rms(x) * nw_ref[0]
    h2 = h2 * (1.0 + m[4:5]) + m[3:4]
    h2_hi = h2.astype(BF16)
    h2_ref[...] = h2_hi

    if routed:
        h2_lo = (h2 - h2_hi.astype(F32)).astype(BF16)
        r_hi, r_lo = _split_bf16(rt_ref[0], 2)
        logits = (jnp.dot(h2_hi, r_hi, preferred_element_type=F32)
                  + jnp.dot(h2_lo, r_hi, preferred_element_type=F32)
                  + jnp.dot(h2_hi, r_lo, preferred_element_type=F32))
        eid = lax.broadcasted_iota(I32, logits.shape, 1)
        m1 = jnp.max(logits, axis=1, keepdims=True)
        i1 = jnp.min(jnp.where(logits == m1, eid, N_EXP), axis=1, keepdims=True)
        rest = jnp.where(eid == i1, -jnp.inf, logits)
        m2 = jnp.max(rest, axis=1, keepdims=True)
        i2 = jnp.min(jnp.where(rest == m2, eid, N_EXP), axis=1, keepdims=True)
        w1 = 1.0 / (1.0 + jnp.exp(m2 - m1))
        w2 = 1.0 - w1
        oh1 = (eid == i1).astype(F32)
        oh2 = (eid == i2).astype(F32)
        both = oh1 + oh2
        before = (lax.broadcasted_iota(I32, (TM, TM), 0) > lax.broadcasted_iota(I32, (TM, TM), 1))
        ahead = jnp.dot(before.astype(BF16), both.astype(BF16), preferred_element_type=F32)
        r1 = jnp.sum(oh1 * ahead, axis=1, keepdims=True)
        r2 = jnp.sum(oh2 * ahead, axis=1, keepdims=True)
        zero = jnp.zeros_like(w1)
        route_ref[...] = jnp.concatenate(
            [i1.astype(F32), i2.astype(F32), r1, r2, w1, w2, zero, zero], axis=1)
        cnt_ref[0] = jnp.sum(both, axis=0, keepdims=True)


def _k2(l, o_lru, o_hy, o_ssm, x, mod_l, norm2_w, w_out, router=None, j=0):
    routed = router is not None
    tok = lambda w: pl.BlockSpec((TM, w), lambda i: (i, 0))
    in_specs = [
        tok(D_LRU), tok(D_HY), tok(D_SSM), tok(D),
        pl.BlockSpec((1, 6, D), lambda i: (_mod_row(i), 0, 0)),
        pl.BlockSpec((1, 1, D), lambda i: (l, 0, 0)),
        pl.BlockSpec((1, D, D), lambda i: (l, 0, 0)),
    ]
    args = [o_lru, o_hy, o_ssm, x, mod_l, norm2_w.reshape(DEPTH, 1, D), w_out]
    out_specs = [tok(D), tok(D)]
    out_shape = [jax.ShapeDtypeStruct((T, D), F32), jax.ShapeDtypeStruct((T, D), BF16)]
    if routed:
        in_specs.append(pl.BlockSpec((1, D, N_EXP), lambda i: (j, 0, 0)))
        args.append(router)
        out_specs += [tok(8), pl.BlockSpec((1, 1, N_EXP), lambda i: (i, 0, 0))]
        out_shape += [jax.ShapeDtypeStruct((T, 8), F32), jax.ShapeDtypeStruct((NT, 1, N_EXP), F32)]
    return pl.pallas_call(
        functools.partial(_k2_kernel, routed=routed),
        grid=(NT,),
        in_specs=in_specs,
        out_specs=out_specs,
        out_shape=out_shape,
        scratch_shapes=[pltpu.VMEM((D, D), BF16)],
        compiler_params=_cparams(1),
        name=f"k2_outproj_{l}",
    )(*args)


def _ffn_stream_kernel(be_ref, bi_ref, na_ref, x_ref, *rest, j, dense):
    if dense:
        xres_ref, mod_ref, wg_hbm, wu_hbm, wd_hbm, o_ref, wg_s, wu_s, wd_s, stg_g, stg_u, stg_d, sem = rest
    else:
        wg_hbm, wu_hbm, wd_hbm, o_ref, wg_s, wu_s, wd_s, stg_g, stg_u, stg_d, sem = rest
    del bi_ref
    b = pl.program_id(0)
    n_act = na_ref[0]
    e = be_ref[b]
    active = b < n_act
    load = jnp.logical_and(active, jnp.logical_or(b == 0, be_ref[jnp.maximum(b - 1, 0)] != e))
    e_next = be_ref[jnp.minimum(b + 1, pl.num_programs(0) - 1)]
    feed_next = jnp.logical_and(b + 1 < n_act, e_next != e)

    def copies(ee, c):
        slot = c % FFN_STAGES
        cols = slice(c * FF_CHUNK, (c + 1) * FF_CHUNK)
        return (pltpu.make_async_copy(wg_hbm.at[j, ee, :, cols], stg_g.at[slot], sem.at[0, slot]),
                pltpu.make_async_copy(wu_hbm.at[j, ee, :, cols], stg_u.at[slot], sem.at[1, slot]),
                pltpu.make_async_copy(wd_hbm.at[j, ee, cols, :], stg_d.at[slot], sem.at[2, slot]))

    def start(ee, c):
        for cp in copies(ee, c):
            cp.start()

    def chunk_out(c, x, acc):
        g = jnp.dot(x, wg_s[c], preferred_element_type=F32)
        u = jnp.dot(x, wu_s[c], preferred_element_type=F32)
        hmid = (g * jax.nn.sigmoid(g) * u).astype(BF16)
        part = jnp.dot(hmid, wd_s[c], preferred_element_type=F32)
        return part if acc is None else acc + part

    def finish(acc):
        if dense:
            o_ref[...] = xres_ref[...] + mod_ref[0][5:6] * acc
        else:
            o_ref[...] = acc.astype(o_ref.dtype)

    @pl.when(load)
    def _():
        @pl.when(b == 0)
        def _():
            for c in range(FFN_STAGES):
                start(e, c)

        x = x_ref[...]
        acc = None
        for c in range(N_FF_CHUNK):
            slot = c % FFN_STAGES
            for cp in copies(e, c):
                cp.wait()
            wg_s[c] = stg_g[slot].astype(BF16)
            wu_s[c] = stg_u[slot].astype(BF16)
            wd_s[c] = stg_d[slot].astype(BF16)
            if c + FFN_STAGES < N_FF_CHUNK:
                start(e, c + FFN_STAGES)
            acc = chunk_out(c, x, acc)
        finish(acc)

    @pl.when(jnp.logical_and(active, jnp.logical_not(load)))
    def _():
        x = x_ref[...]
        acc = None
        for c in range(N_FF_CHUNK):
            acc = chunk_out(c, x, acc)
        finish(acc)

    if not dense:
        @pl.when(jnp.logical_not(active))
        def _():
            o_ref[...] = jnp.zeros_like(o_ref)

    @pl.when(feed_next)
    def _():
        for c in range(FFN_STAGES):
            start(e_next, c)


def _ffn_scratch():
    return [pltpu.VMEM((N_FF_CHUNK, D, FF_CHUNK), BF16), pltpu.VMEM((N_FF_CHUNK, D, FF_CHUNK), BF16),
            pltpu.VMEM((N_FF_CHUNK, FF_CHUNK, D), BF16),
            pltpu.VMEM((FFN_STAGES, D, FF_CHUNK), F32), pltpu.VMEM((FFN_STAGES, D, FF_CHUNK), F32),
            pltpu.VMEM((FFN_STAGES, FF_CHUNK, D), F32), pltpu.SemaphoreType.DMA((3, FFN_STAGES))]


def _dense_ffn(j, h2, x, mod_l, wg, wu, wd):
    hbm = pl.BlockSpec(memory_space=pl.ANY)
    zeros = jnp.zeros((NT,), I32)
    return pl.pallas_call(
        functools.partial(_ffn_stream_kernel, j=j, dense=True),
        grid_spec=pltpu.PrefetchScalarGridSpec(
            num_scalar_prefetch=3,
            grid=(NT,),
            in_specs=[
                pl.BlockSpec((TM, D), lambda i, *_: (i, 0)),
                pl.BlockSpec((TM, D), lambda i, *_: (i, 0)),
                pl.BlockSpec((1, 6, D), lambda i, *_: (_mod_row(i), 0, 0)),
                hbm, hbm, hbm,
            ],
            out_specs=pl.BlockSpec((TM, D), lambda i, *_: (i, 0)),
            scratch_shapes=_ffn_scratch(),
        ),
        out_shape=jax.ShapeDtypeStruct((T, D), F32),
        compiler_params=_cparams(1),
        name=f"dense_ffn_{j}",
    )(zeros, zeros, jnp.full((1,), NT, I32), h2, x, mod_l, wg[:, None], wu[:, None], wd[:, None])


def _expert_ffn(j, xs, blk_e, blk_i, n_active, wg, wu, wd):
    hbm = pl.BlockSpec(memory_space=pl.ANY)
    return pl.pallas_call(
        functools.partial(_ffn_stream_kernel, j=j, dense=False),
        grid_spec=pltpu.PrefetchScalarGridSpec(
            num_scalar_prefetch=3,
            grid=(NBLK,),
            in_specs=[pl.BlockSpec((TMB, D), lambda b, be, bi, na: (bi[b], 0)), hbm, hbm, hbm],
            out_specs=pl.BlockSpec((TMB, D), lambda b, be, bi, na: (b, 0)),
            scratch_shapes=_ffn_scratch(),
        ),
        out_shape=jax.ShapeDtypeStruct((RMAX, D), BF16),
        compiler_params=_cparams(1),
        name=f"expert_ffn_{j}",
    )(blk_e, blk_i, n_active, xs, wg, wu, wd)


def _chunk_rows(idx):
    return pl.ds(pl.multiple_of(idx * GCH, GCH), GCH)


def _sort_kernel(gch_ref, nused_ref, pad0_ref, npad_ref, h_ref, dl_ref, xs_ref, xl_ref, zero_ref, sem):
    i = pl.program_id(0)
    slot = i % 2

    def copy(tile, q):
        s = tile % 2
        return pltpu.make_async_copy(xl_ref.at[s, _chunk_rows(q)],
                                     xs_ref.at[_chunk_rows(gch_ref[tile * NLC + q])], sem.at[s])

    def wait_tile(tile):
        lax.fori_loop(0, nused_ref[tile], lambda q, c: (copy(tile, q).wait(), c)[1], 0)

    @pl.when(i >= 2)
    def _():
        wait_tile(i - 2)

    dl = dl_ref[0]
    r = lax.broadcasted_iota(I32, (LCAP, TM), 0).astype(F32)
    perm = jnp.logical_or(r == dl[0:1, :], r == dl[1:2, :]).astype(BF16)
    xl_ref[slot] = jnp.dot(perm, h_ref[...], preferred_element_type=F32).astype(BF16)
    lax.fori_loop(0, nused_ref[i], lambda q, c: (copy(i, q).start(), c)[1], 0)

    @pl.when(i == NT - 1)
    def _():
        zero_ref[...] = jnp.zeros_like(zero_ref)
        for e in range(N_EXP + 1):
            def zcopy(q, e=e):
                return pltpu.make_async_copy(zero_ref, xs_ref.at[_chunk_rows(pad0_ref[e] + q)], sem.at[2])
            lax.fori_loop(0, npad_ref[e], lambda q, c, f=zcopy: (f(q).start(), c)[1], 0)
            lax.fori_loop(0, npad_ref[e], lambda q, c, f=zcopy: (f(q).wait(), c)[1], 0)
        wait_tile(i - 1)
        wait_tile(i)


def _sort_tokens(h2, dl_row, gch, nused, pad0, npad):
    return pl.pallas_call(
        _sort_kernel,
        grid_spec=pltpu.PrefetchScalarGridSpec(
            num_scalar_prefetch=4,
            grid=(NT,),
            in_specs=[
                pl.BlockSpec((TM, D), lambda i, *_: (i, 0)),
                pl.BlockSpec((1, 2, TM), lambda i, *_: (i, 0, 0)),
            ],
            out_specs=pl.BlockSpec(memory_space=pl.ANY),
            scratch_shapes=[pltpu.VMEM((2, LCAP, D), BF16), pltpu.VMEM((GCH, D), BF16),
                            pltpu.SemaphoreType.DMA((3,))],
        ),
        out_shape=jax.ShapeDtypeStruct((RMAX, D), BF16),
        compiler_params=_cparams(1),
        name="moe_sort",
    )(gch, nused, pad0, npad, h2, dl_row)


def _combine_kernel(gch_ref, nused_ref, y_ref, info_ref, x_ref, mod_ref, fw_ref, *rest, final):
    if final:
        oc_ref, ol_ref, yl_ref, sem = rest
    else:
        o_ref, yl_ref, sem = rest
    i = pl.program_id(0)
    slot = i % 2

    def copy(tile, q):
        s = tile % 2
        return pltpu.make_async_copy(y_ref.at[_chunk_rows(gch_ref[tile * NLC + q])],
                                     yl_ref.at[s, _chunk_rows(q)], sem.at[s])

    def fetch(tile):
        lax.fori_loop(0, nused_ref[tile], lambda q, c: (copy(tile, q).start(), c)[1], 0)

    @pl.when(i == 0)
    def _():
        fetch(i)

    @pl.when(i + 1 < NT)
    def _():
        fetch(i + 1)

    n = nused_ref[i]

    def clear(q, c):
        yl_ref[slot, _chunk_rows(q), :] = jnp.zeros((GCH, D), BF16)
        return c

    lax.fori_loop(n, NLC, clear, 0)
    lax.fori_loop(0, n, lambda q, c: (copy(i, q).wait(), c)[1], 0)

    info = info_ref[...]
    col = lax.broadcasted_iota(I32, (TM, LCAP), 1).astype(F32)
    yl = yl_ref[slot]
    y1 = jnp.dot((col == info[:, 0:1]).astype(BF16), yl, preferred_element_type=F32)
    y2 = jnp.dot((col == info[:, 1:2]).astype(BF16), yl, preferred_element_type=F32)
    x = x_ref[...] + mod_ref[0][5:6] * (info[:, 2:3] * y1 + info[:, 3:4] * y2)
    if final:
        x = _rms(x) * fw_ref[...]

        @pl.when(i < NT_CTX)
        def _():
            oc_ref[...] = x

        @pl.when(i >= NT_CTX)
        def _():
            ol_ref[...] = x
    else:
        o_ref[...] = x


def _combine(y, info, x, mod_l, final_w, gch, nused, final):
    tile = pl.BlockSpec((TM, D), lambda i, *_: (i, 0))
    if final:
        out_specs = [pl.BlockSpec((TM, D), lambda i, *_: (jnp.minimum(i, NT_CTX - 1), 0)),
                     pl.BlockSpec((TM, D), lambda i, *_: (jnp.maximum(i - NT_CTX, 0), 0))]
        out_shape = [jax.ShapeDtypeStruct((T_CTX, D), F32), jax.ShapeDtypeStruct((T_LAT, D), F32)]
    else:
        out_specs, out_shape = tile, jax.ShapeDtypeStruct((T, D), F32)
    return pl.pallas_call(
        functools.partial(_combine_kernel, final=final),
        grid_spec=pltpu.PrefetchScalarGridSpec(
            num_scalar_prefetch=2,
            grid=(NT,),
            in_specs=[
                pl.BlockSpec(memory_space=pl.ANY),
                pl.BlockSpec((TM, 4), lambda i, *_: (i, 0)),
                tile,
                pl.BlockSpec((1, 6, D), lambda i, *_: (_mod_row(i), 0, 0)),
                pl.BlockSpec((1, D), lambda i, *_: (0, 0)),
            ],
            out_specs=out_specs,
            scratch_shapes=[pltpu.VMEM((2, LCAP, D), BF16), pltpu.SemaphoreType.DMA((2,))],
        ),
        out_shape=out_shape,
        compiler_params=_cparams(1),
        name="moe_combine",
    )(gch, nused, y, info, x, mod_l, final_w.reshape(1, D))


def _moe_plan(route, counts):
    cnt = counts.reshape(NT, N_EXP).astype(I32)
    cpad = (cnt + GCH - 1) // GCH * GCH
    lo = jnp.cumsum(cpad, axis=1) - cpad
    nused = (lo[:, -1] + cpad[:, -1]) // GCH
    tot = jnp.sum(cpad, axis=0)
    gpad = (tot + TMB - 1) // TMB * TMB
    goff = jnp.cumsum(gpad) - gpad
    so = goff[None, :] + jnp.cumsum(cpad, axis=0) - cpad

    e1 = route[:, 0].astype(I32)
    e2 = route[:, 1].astype(I32)
    eid = jnp.arange(N_EXP, dtype=I32)[None, :]
    lo_tok = jnp.repeat(lo, TM, axis=0)
    dl1 = jnp.sum(jnp.where(e1[:, None] == eid, lo_tok, 0), axis=1).astype(F32) + route[:, 2]
    dl2 = jnp.sum(jnp.where(e2[:, None] == eid, lo_tok, 0), axis=1).astype(F32) + route[:, 3]
    dl_row = jnp.stack([dl1.reshape(NT, TM), dl2.reshape(NT, TM)], axis=1)
    info = jnp.stack([dl1, dl2, route[:, 4], route[:, 5]], axis=1)

    q = jnp.arange(NLC, dtype=I32)[None, :, None]
    lo16 = (lo // GCH)[:, None, :]
    c16 = (cpad // GCH)[:, None, :]
    in_seg = jnp.logical_and(q >= lo16, q < lo16 + c16)
    gch = jnp.sum(jnp.where(in_seg, (so // GCH)[:, None, :] + q - lo16, 0), axis=2).reshape(NT * NLC)

    nblk = gpad // TMB
    n_active = jnp.sum(nblk)
    b = jnp.arange(NBLK, dtype=I32)
    blk_i = jnp.maximum(jnp.minimum(b, n_active - 1), 0)
    bend = (goff + gpad) // TMB
    blk_e = jnp.minimum(jnp.sum((blk_i[:, None] >= bend[None, :]).astype(I32), axis=1), N_EXP - 1)
    used = n_active * TMB
    pad0 = jnp.concatenate([goff + tot, used.reshape(1)]) // GCH
    npad = jnp.concatenate([gpad - tot, (RMAX - used).reshape(1)]) // GCH
    return dict(dl_row=dl_row, info=info, gch=gch.astype(I32), nused=nused.astype(I32),
                blk_e=blk_e.astype(I32), blk_i=blk_i.astype(I32),
                n_active=n_active.reshape(1).astype(I32), pad0=pad0.astype(I32), npad=npad.astype(I32))


def _grid_pos_embed(n_tokens):
    rows = n_tokens // GRID_W
    r = jnp.repeat(jnp.arange(rows, dtype=F32), GRID_W)
    col = jnp.tile(jnp.arange(GRID_W, dtype=F32), rows)
    quarter = D // 4
    omega = 1.0 / (10000.0 ** (jnp.arange(quarter, dtype=F32) / quarter))
    ang_r = r[:, None] * omega[None]
    ang_c = col[:, None] * omega[None]
    return jnp.concatenate([jnp.sin(ang_r), jnp.cos(ang_r), jnp.sin(ang_c), jnp.cos(ang_c)], axis=-1)


def _hy_pos_features(n):
    pos = jnp.arange(n, dtype=F32)
    t = pos / (n - 1)
    bands = jnp.linspace(1e-4, HY_BANDS - 1, HY_BANDS, dtype=F32)
    ang = (2.0 * math.pi * pos / n)[:, None] * bands[None]
    z = jnp.concatenate([t[:, None], jnp.cos(ang), -jnp.sin(ang)], axis=-1)
    z = jnp.pad(z, ((0, 0), (0, LANE - HY_POS_DIM)))
    half = n // 2
    dist = jnp.abs(pos - half) / half
    deltas = jnp.abs(jnp.linspace(HY_MIN_DECAY, HY_MAX_DECAY, D_HY, dtype=F32))
    return z, jnp.exp(-dist[:, None] * deltas[None])


def _dft_mats(lseq, n, kb):
    nkb = n // 2 // kb
    t = jnp.arange(lseq, dtype=I32)

    def tables(tt):
        def cs(freq):
            ang = (2.0 * math.pi / n) * ((freq[:, None] * tt[None, :]) % n).astype(F32)
            return jnp.cos(ang), jnp.sin(ang)
        (ca, sa), (cb, sb) = cs(jnp.arange(nkb, dtype=I32) * kb), cs(jnp.arange(kb, dtype=I32))
        ca, sa, cb, sb = ca[:, None, :], sa[:, None, :], cb[None], sb[None]
        re = ca * cb - sa * sb
        im = -(sa * cb + ca * sb)
        dc = jnp.logical_and(jnp.arange(nkb)[:, None, None] == 0, jnp.arange(kb)[None, :, None] == 0)
        alt = (1.0 - 2.0 * (tt % 2).astype(F32))[None, None, :]
        return re, jnp.where(dc, alt, im), dc

    re, im, _ = tables(t)
    f = jnp.concatenate([re, im], axis=1).reshape(n, lseq)
    re, im, dc = tables(t + lseq // 2)
    wk = jnp.where(dc, 1.0, 2.0) / n
    gt = jnp.concatenate([wk * re, jnp.where(dc, 1.0 / n, wk) * im], axis=1).reshape(n, lseq)
    return f.astype(BF16), gt.T.astype(BF16)


def _ssd_constants():
    hp = SSM_H * SSM_P
    hq = SSM_H // SSM_G
    e64 = np.zeros((2, 8 * SSM_H, 2 * hp), np.float32)
    for d in range(2):
        for q in range(4):
            for hh in range(SSM_H):
                e64[d, q * 2 * SSM_H + d * SSM_H + hh, (q // 2) * hp + hh * SSM_P:(q // 2) * hp + (hh + 1) * SSM_P] = 1.0
    hq_of = np.arange(hq * SSM_P) // SSM_P
    mbd = (np.arange(hq * SSM_Q)[:, None] // SSM_Q == hq_of[None, :]).astype(np.float32)
    mdiag = (hq_of[:, None] == hq_of[None, :]).astype(np.float32)
    return dict(e64=jnp.asarray(e64, dtype=BF16), mbd=jnp.asarray(mbd, dtype=BF16), mdiag=jnp.asarray(mdiag))


def _block_diag_heads(w):
    eye = jnp.eye(LRU_HEADS, dtype=w.dtype)
    return jnp.einsum("ldhij,hg->ldhigj", w, eye).reshape(DEPTH, 2, D_LRU, D_LRU)


def kernel(x_prompt, x_sample, state_lru, state_ssm, c, c_ctx, norm1_w, norm2_w, final_norm_w, ada_w, ada_b,
           w_in, w_out, lru_conv_w, lru_conv_b, lru_wa, lru_ba, lru_wi, lru_bi, lru_lambda, hy_conv_w, hy_conv_b,
           hy_w1, hy_b1, hy_w2, hy_b2, hy_freq, hy_w3, hy_bias, ssm_conv_w, ssm_conv_b, ssm_dt_bias, ssm_a_log,
           ssm_d, ssm_norm_w, ffn_w_gate, ffn_w_up, ffn_w_down, moe_router, moe_w_gate, moe_w_up, moe_w_down):
    hp = SSM_H * SSM_P
    wa, wi = _block_diag_heads(lru_wa), _block_diag_heads(lru_wi)
    row = lambda a: a.reshape(DEPTH, 1, -1)
    p = {
        "lru_conv_w": lru_conv_w, "lru_conv_b": row(lru_conv_b), "lru_lambda": lru_lambda,
        "lru_wbig": jnp.concatenate([wa[:, 0], wi[:, 0], wa[:, 1], wi[:, 1]], axis=-1).astype(BF16),
        "lru_bias": jnp.concatenate([lru_ba[:, 0], lru_bi[:, 0], lru_ba[:, 1], lru_bi[:, 1]], axis=-1)[:, None],
        "hy_conv_w": hy_conv_w, "hy_conv_b": row(hy_conv_b), "hy_bias": row(hy_bias),
        "hy_w1t": jnp.swapaxes(jnp.pad(hy_w1, ((0, 0), (0, LANE - HY_POS_DIM), (0, 0))), 1, 2),
        "hy_b1": hy_b1[:, :, None], "hy_w2t": jnp.swapaxes(hy_w2, 1, 2), "hy_b2": hy_b2[:, :, None],
        "hy_freq": hy_freq[:, :, None], "hy_w3t": jnp.swapaxes(hy_w3, 1, 2),
        "ssm_conv_w": ssm_conv_w, "ssm_conv_b": row(ssm_conv_b),
        "ssm_dtb_row": row(ssm_dt_bias), "ssm_dtb_col": ssm_dt_bias.reshape(DEPTH, 2 * SSM_H, 1),
        "ssm_alog_row": row(ssm_a_log), "ssm_alog_col": ssm_a_log.reshape(DEPTH, 2 * SSM_H, 1),
        "ssm_d_exp": jnp.repeat(ssm_d, SSM_P, axis=-1)[:, None], "ssm_norm_w": row(ssm_norm_w),
    }
    w_dtT = jnp.swapaxes(w_in[:, :, D_MAIN:], 1, 2)
    ffn_w = (ffn_w_gate, ffn_w_up, ffn_w_down)
    moe_w = (moe_w_gate, moe_w_up, moe_w_down)

    cst = _ssd_constants()
    cst["lru_rep"] = jnp.asarray(np.arange(RB)[:, None] // SUBLANES == np.arange(RB // SUBLANES)[None, :], dtype=BF16)
    z_ctx, win_ctx = _hy_pos_features(L_CTX)
    z_lat, win_lat = _hy_pos_features(L_LAT)
    cst["f_ctx"], cst["g_ctx"] = _dft_mats(L_CTX, N_FFT_CTX, N_FFT_CTX // 2)
    cst["f_lat"], cst["g_lat"] = _dft_mats(L_LAT, N_FFT_LAT, HY_KB_LAT)
    cst["hf_ctx"] = _hy_filter_spectrum(L_CTX, N_FFT_CTX // 2, z_ctx.T, win_ctx.T, cst["f_ctx"], p)
    cst["hf_lat"] = _hy_filter_spectrum(L_LAT, HY_KB_LAT, z_lat.T, win_lat.T, cst["f_lat"], p)

    cond = jnp.concatenate([c_ctx[None], c], axis=0)
    mod = _mod_table(jnp.broadcast_to(cond[:, :, None], (3, D, LANE)), ada_w, ada_b)
    mod = mod[:, :3].reshape(DEPTH, 3, 6, D)

    x = jnp.concatenate([x_prompt.reshape(T_CTX, D),
                         (x_sample + _grid_pos_embed(L_LAT)[None]).reshape(T_LAT, D)], axis=0)
    st_ssm_in = state_ssm.reshape(N_LAT_SEQ, DEPTH, 2, hp, SSM_N)

    lru_states = []
    new_ssm = jnp.zeros((N_CTX_SEQ, DEPTH, 2, SSM_H, SSM_P, SSM_N), F32)
    for l in range(DEPTH):
        u_lru, u_hy, u_z, u_xbc, u_dt, u_dtT = _k1(l, x, mod[l], norm1_w, w_in, w_dtT)
        o_lru, s_lru = _lru_mixer(l, u_lru, p, cst, state_lru)
        o_hy = _hy_mixer(l, u_hy, p, cst)
        o_ssm, new_ssm = _ssd_mixer(l, u_z, u_xbc, u_dt, u_dtT, p, cst, st_ssm_in, new_ssm)
        lru_states.append(s_lru.reshape(N_CTX_SEQ, 2, D_LRU))
        j = l // 2
        if l % 2 == 0:
            x, h2 = _k2(l, o_lru, o_hy, o_ssm, x, mod[l], norm2_w, w_out)
            x = _dense_ffn(j, h2, x, mod[l], *ffn_w)
        else:
            x, h2, route, counts = _k2(l, o_lru, o_hy, o_ssm, x, mod[l], norm2_w, w_out, moe_router, j)
            plan = _moe_plan(route, counts)
            xs = _sort_tokens(h2, plan["dl_row"], plan["gch"], plan["nused"], plan["pad0"], plan["npad"])
            y = _expert_ffn(j, xs, plan["blk_e"], plan["blk_i"], plan["n_active"], *moe_w)
            x = _combine(y, plan["info"], x, mod[l], final_norm_w, plan["gch"], plan["nused"],
                         final=(l == DEPTH - 1))
    y_prompt = x[0].reshape(N_CTX_SEQ, L_CTX, D)
    y_sample = x[1].reshape(N_LAT_SEQ, L_LAT, D)
    return (y_prompt, y_sample, jnp.stack(lru_states, axis=1), new_ssm)
```

```python
import functools
import math

import numpy as np
import jax
import jax.numpy as jnp
from jax import lax
from jax.experimental import pallas as pl
from jax.experimental.pallas import tpu as pltpu

F32 = jnp.float32
BF16 = jnp.bfloat16
I32 = jnp.int32
HI = lax.Precision.HIGHEST

D = 1024
N_CTX_SEQ, L_CTX = 16, 256
N_LAT_SEQ, L_LAT = 2, 2048
DEPTH = 4
GRID_W = 64
D_LRU = 256
LRU_HEADS, LRU_HD = 4, 64
LRU_C = 8.0
D_HY = 256
HY_BANDS = 16
HY_POS_DIM = 1 + 2 * HY_BANDS
HY_HID = 64
HY_MAX_DECAY = math.log(1e-2) / 0.3
HY_MIN_DECAY = math.log(1e-2) / 1.5
D_SSM = 512
SSM_P = 64
SSM_H = 8
SSM_G = 2
SSM_N = 64
SSM_Q = 128
D_XBC = D_SSM + 2 * SSM_G * SSM_N
D_MAIN = 2 * D_LRU + 3 * D_HY + D_SSM + D_XBC
D_IN = D_MAIN + 2 * SSM_H
D_FF = 2816
N_EXP = 8
EPS = 1e-6

LANE = 128
SUBLANES = 8
BF16_ROWS = 16
T_CTX = N_CTX_SEQ * L_CTX
T_LAT = N_LAT_SEQ * L_LAT
T = T_CTX + T_LAT
TM = 512
NT = T // TM
NT_CTX = T_CTX // TM
NT_PER_LAT = L_LAT // TM
RB = 2048
NB = T // RB
NB_CTX = T_CTX // RB
FF_CHUNK = 256
N_FF_CHUNK = D_FF // FF_CHUNK
FFN_STAGES = 3
VMEM_LIMIT = 56 * 1024 * 1024

GCH = BF16_ROWS
LCAP = 2 * TM + N_EXP * GCH
NLC = LCAP // GCH
TMB = 512
RMAX = -(-(2 * T + NT * N_EXP * (GCH - 1) + N_EXP * (TMB - 1)) // TMB) * TMB
NBLK = RMAX // TMB

HY_KB_LAT = 512
N_FFT_LAT = 3 * L_LAT // 2
N_FFT_CTX = 3 * L_CTX // 2
HY_NKB = N_FFT_LAT // 2 // HY_KB_LAT


def _cparams(n_axes=1, vmem=VMEM_LIMIT):
    return pltpu.CompilerParams(dimension_semantics=("arbitrary",) * n_axes, vmem_limit_bytes=vmem)


def _mod_row(i):
    return jnp.where(i < NT_CTX, 0, 1 + (i - NT_CTX) // NT_PER_LAT)


def _bdot(a, b):
    return jnp.dot(a.astype(BF16), b.astype(BF16), preferred_element_type=F32)


def _rms(x):
    return x * lax.rsqrt(jnp.mean(x * x, axis=-1, keepdims=True) + EPS)


def _split_bf16(v, parts):
    out = []
    for _ in range(parts):
        piece = v.astype(BF16)
        out.append(piece)
        v = v - piece.astype(F32)
    return out


def _mod_kernel(cb_ref, w_ref, b_ref, o_ref):
    tn = w_ref.shape[2]

    def body(kc, accs):
        k0 = pl.multiple_of(kc * 8, 8)
        wk = w_ref[0, pl.ds(k0, 8), :]
        out = []
        for r in range(3):
            c = cb_ref[r, pl.ds(k0, 8), :]
            c = c * jax.nn.sigmoid(c)
            out.append(accs[r] + jnp.tile(c, (1, tn // LANE)) * wk)
        return tuple(out)

    accs = lax.fori_loop(0, D // 8, body, tuple(jnp.zeros((8, tn), F32) for _ in range(3)), unroll=8)
    rows = [jnp.sum(a, axis=0, keepdims=True) + b_ref[0] for a in accs]
    o_ref[0] = jnp.concatenate(rows + [jnp.zeros((5, tn), F32)], axis=0)


def _mod_table(cond_b, ada_w, ada_b):
    tn = 1024
    return pl.pallas_call(
        _mod_kernel,
        grid=(DEPTH, 6 * D // tn),
        in_specs=[
            pl.BlockSpec((3, D, LANE), lambda l, j: (0, 0, 0)),
            pl.BlockSpec((1, D, tn), lambda l, j: (l, 0, j)),
            pl.BlockSpec((1, 1, tn), lambda l, j: (l, 0, j)),
        ],
        out_specs=pl.BlockSpec((1, 8, tn), lambda l, j: (l, 0, j)),
        out_shape=jax.ShapeDtypeStruct((DEPTH, 8, 6 * D), F32),
        compiler_params=_cparams(2),
        name="mod_table",
    )(cond_b, ada_w, ada_b.reshape(DEPTH, 1, 6 * D))


def _k1_kernel(x_ref, mod_ref, nw_ref, w_hbm, wdtT_ref,
               o_lru, o_hy, o_z, o_xbc, o_dt, o_dtT, wbf_ref, stage_ref, sem, *, l):
    @pl.when(pl.program_id(0) == 0)
    def _():
        cp = pltpu.make_async_copy(w_hbm.at[l], stage_ref, sem.at[0])
        cp.start()
        cp.wait()
        wbf_ref[...] = stage_ref[...].astype(BF16)

    m = mod_ref[0]
    h = _rms(x_ref[...]) * nw_ref[0]
    h = h * (1.0 + m[1:2]) + m[0:1]
    hb = h.astype(BF16)

    def proj(lo, hi):
        return jnp.dot(hb, wbf_ref[:, lo:hi], preferred_element_type=F32)

    o_lru[...] = proj(0, 512)
    o_hy[...] = proj(512, 1280)
    o_z[...] = proj(1280, 1792)
    o_xbc[...] = proj(1792, 2560)
    o_dt[...] = proj(D_MAIN, D_IN)
    dtT = lax.dot_general(wdtT_ref[0].astype(BF16), hb, (((1,), (1,)), ((), ())),
                          preferred_element_type=F32)
    for j in range(TM // SSM_Q):
        o_dtT[j] = dtT[:, j * SSM_Q:(j + 1) * SSM_Q]


def _k1(l, x, mod_l, norm1_w, w_in, w_dtT):
    tok = lambda w: pl.BlockSpec((TM, w), lambda i: (i, 0))
    return pl.pallas_call(
        functools.partial(_k1_kernel, l=l),
        grid=(NT,),
        in_specs=[
            tok(D),
            pl.BlockSpec((1, 6, D), lambda i: (_mod_row(i), 0, 0)),
            pl.BlockSpec((1, 1, D), lambda i: (l, 0, 0)),
            pl.BlockSpec(memory_space=pl.ANY),
            pl.BlockSpec((1, 2 * SSM_H, D), lambda i: (l, 0, 0)),
        ],
        out_specs=[tok(512), tok(768), tok(512), tok(768), tok(2 * SSM_H),
                   pl.BlockSpec((TM // SSM_Q, 2 * SSM_H, SSM_Q), lambda i: (i, 0, 0))],
        out_shape=[jax.ShapeDtypeStruct((T, 512), F32), jax.ShapeDtypeStruct((T, 768), F32),
                   jax.ShapeDtypeStruct((T, 512), F32), jax.ShapeDtypeStruct((T, 768), F32),
                   jax.ShapeDtypeStruct((T, 2 * SSM_H), F32),
                   jax.ShapeDtypeStruct((T // SSM_Q, 2 * SSM_H, SSM_Q), F32)],
        scratch_shapes=[pltpu.VMEM((D, D_IN), BF16), pltpu.VMEM((D, D_IN), F32),
                        pltpu.SemaphoreType.DMA((1,))],
        compiler_params=_cparams(1),
        name=f"k1_inproj_{l}",
    )(x, mod_l, norm1_w.reshape(DEPTH, 1, D), w_in, w_dtT)


def _row_in_seq(rows, lseq):
    return lax.broadcasted_iota(I32, (rows, 1), 0) & (lseq - 1)


def _shift_rows(x, s, rin, lseq):
    if s == 0:
        return x
    y = pltpu.roll(x, s % x.shape[0], axis=0)
    valid = (rin >= s) if s > 0 else (rin < lseq + s)
    return jnp.where(valid, y, 0.0)


def _dwconv(x, w_ref, b_ref, rin, lseq):
    k_w = w_ref.shape[0]
    y = b_ref[...]
    for k in range(k_w):
        y = y + w_ref[k:k + 1, :] * _shift_rows(x, k_w // 2 - k, rin, lseq)
    return y


def _lru_scan(a, b, d, rin, lseq, rep_ref, ab_s):
    rows = a.shape[0]
    ngrp, gps = rows // SUBLANES, lseq // SUBLANES
    sub = rin & (SUBLANES - 1)
    for s in (1, 2, 4):
        sh = (s if d == 0 else -s) % rows
        valid = (sub >= s) if d == 0 else (sub < SUBLANES - s)
        b = b + jnp.where(valid, a * pltpu.roll(b, sh, axis=0), 0.0)
        a = jnp.where(valid, a * pltpu.roll(a, sh, axis=0), a)
    edge = SUBLANES - 1 if d == 0 else 0

    def group_edges(k, v):
        for j in range(D_LRU // LANE):
            ab_s[k, j] = v[:, j * LANE:(j + 1) * LANE]
        return jnp.concatenate([ab_s[k, j, pl.ds(edge, ngrp, stride=SUBLANES), :]
                                for j in range(D_LRU // LANE)], axis=1)

    ga, gb = group_edges(0, a), group_edges(1, b)
    gin = lax.broadcasted_iota(I32, (ngrp, 1), 0) & (gps - 1)
    s = 1
    while s < gps:
        sh = (s if d == 0 else -s) % ngrp
        valid = (gin >= s) if d == 0 else (gin < gps - s)
        gb = gb + jnp.where(valid, ga * pltpu.roll(gb, sh, axis=0), 0.0)
        if 2 * s < gps:
            ga = jnp.where(valid, ga * pltpu.roll(ga, sh, axis=0), ga)
        s *= 2
    valid = (gin >= 1) if d == 0 else (gin < gps - 1)
    cin = jnp.where(valid, pltpu.roll(gb, (1 if d == 0 else -1) % ngrp, axis=0), 0.0)
    hi, lo = _split_bf16(cin, 2)
    cin_x = (jnp.dot(rep_ref[...], hi, preferred_element_type=F32)
             + jnp.dot(rep_ref[...], lo, preferred_element_type=F32))
    return a * cin_x + b


def _lru_block(u_ref, cw_ref, cb_ref, wbig_ref, bias_ref, lam_ref, rep_ref, h0_ref, o_ref, st_ref, ab_s, lseq):
    rows = u_ref.shape[0]
    rin = _row_in_seq(rows, lseq)
    u = u_ref[...]
    gate = u[:, D_LRU:]
    x = _dwconv(u[:, :D_LRU], cw_ref, cb_ref, rin, lseq)
    xb = x.astype(BF16)
    y = None
    finals = []
    for d in range(2):
        g = jnp.dot(xb, wbig_ref[:, 512 * d:512 * (d + 1)], preferred_element_type=F32)
        g = g + bias_ref[:, 512 * d:512 * (d + 1)]
        r = jax.nn.sigmoid(g[:, :D_LRU])
        ig = jax.nn.sigmoid(g[:, D_LRU:])
        log_a = -LRU_C * r * jax.nn.softplus(-lam_ref[d:d + 1, :])
        a = jnp.exp(log_a)
        th = jnp.tanh(log_a)
        b = jnp.sqrt(-2.0 * th / (1.0 - th)) * (ig * x)
        if h0_ref is not None:
            edge = (rin == 0) if d == 0 else (rin == lseq - 1)
            b = b + jnp.where(edge, a * h0_ref[d:d + 1, :], 0.0)
        b = _lru_scan(a, b, d, rin, lseq, rep_ref, ab_s)
        y = b if y is None else y + b
        if st_ref is not None:
            last = lseq - 1 if d == 0 else 0
            finals.append(jnp.concatenate(
                [b[j * lseq + last:j * lseq + last + 1, :] for j in range(rows // lseq)], axis=0))
    o_ref[...] = y * jax.nn.gelu(gate)
    if st_ref is not None:
        st_ref[...] = jnp.concatenate(finals, axis=1)


def _lru_kernel(u_ref, cw_ref, cb_ref, wbig_ref, bias_ref, lam_ref, rep_ref, h0_ref, o_ref, st_ref, ab_s):
    b = pl.program_id(0)
    args = (u_ref, cw_ref.at[0], cb_ref.at[0], wbig_ref.at[0], bias_ref.at[0], lam_ref.at[0], rep_ref)

    @pl.when(b < NB_CTX)
    def _():
        _lru_block(*args, None, o_ref, st_ref, ab_s, L_CTX)

    @pl.when(b >= NB_CTX)
    def _():
        _lru_block(*args, h0_ref.at[0, 0], o_ref, None, ab_s, L_LAT)


def _lru_mixer(l, u_lru, p, c, state_lru):
    lsel = lambda *shape: pl.BlockSpec((1,) + shape, lambda b: (l,) + (0,) * len(shape))
    return pl.pallas_call(
        _lru_kernel,
        grid=(NB,),
        in_specs=[
            pl.BlockSpec((RB, 512), lambda b: (b, 0)),
            lsel(4, D_LRU), lsel(1, D_LRU), lsel(D_LRU, 1024), lsel(1, 1024), lsel(2, D_LRU),
            pl.BlockSpec((RB, RB // SUBLANES), lambda b: (0, 0), pipeline_mode=pl.Buffered(1)),
            pl.BlockSpec((1, 1, 2, D_LRU), lambda b: (jnp.maximum(b - NB_CTX, 0), l, 0, 0)),
        ],
        out_specs=[pl.BlockSpec((RB, D_LRU), lambda b: (b, 0)),
                   pl.BlockSpec((RB // L_CTX, 2 * D_LRU), lambda b: (jnp.minimum(b, NB_CTX - 1), 0))],
        out_shape=[jax.ShapeDtypeStruct((T, D_LRU), F32),
                   jax.ShapeDtypeStruct((N_CTX_SEQ, 2 * D_LRU), F32)],
        scratch_shapes=[pltpu.VMEM((2, D_LRU // LANE, RB, LANE), F32)],
        compiler_params=_cparams(1),
        name=f"lru_mixer_{l}",
    )(u_lru, p["lru_conv_w"], p["lru_conv_b"], p["lru_wbig"], p["lru_bias"], p["lru_lambda"],
      c["lru_rep"], state_lru)


def _hy_filter_kernel(z_ref, win_ref, f_ref, w1_ref, b1_ref, w2_ref, b2_ref, fr_ref, w3_ref,
                      o_ref, h_ref):
    l = pl.program_id(1)

    @pl.when(pl.program_id(0) == 0)
    def _():
        fr = fr_ref[0]
        g = jnp.sin(fr * (jnp.dot(w1_ref[0], z_ref[...], precision=HI, preferred_element_type=F32)
                          + b1_ref[0]))
        g = jnp.sin(fr * (jnp.dot(w2_ref[0], g, precision=HI, preferred_element_type=F32) + b2_ref[0]))
        h = jnp.dot(w3_ref[0], g, precision=HI, preferred_element_type=F32)
        h_ref[l] = (h * win_ref[...]).astype(BF16)

    o_ref[0] = lax.dot_general(f_ref[...], h_ref[l], (((1,), (1,)), ((), ())), preferred_element_type=F32)


def _hy_filter_spectrum(lseq, kb, zfeat_t, window_t, fmat, p):
    n = fmat.shape[0]
    lsel = lambda *shape: pl.BlockSpec((1,) + shape, lambda k, l: (l,) + (0,) * len(shape))
    return pl.pallas_call(
        _hy_filter_kernel,
        grid=(n // (2 * kb), DEPTH),
        in_specs=[
            pl.BlockSpec((LANE, lseq), lambda k, l: (0, 0)),
            pl.BlockSpec((D_HY, lseq), lambda k, l: (0, 0)),
            pl.BlockSpec((2 * kb, lseq), lambda k, l: (k, 0)),
            lsel(HY_HID, LANE), lsel(HY_HID, 1), lsel(HY_HID, HY_HID), lsel(HY_HID, 1),
            lsel(HY_HID, 1), lsel(D_HY, HY_HID),
        ],
        out_specs=pl.BlockSpec((1, 2 * kb, D_HY), lambda k, l: (l, k, 0)),
        out_shape=jax.ShapeDtypeStruct((DEPTH, n, D_HY), F32),
        scratch_shapes=[pltpu.VMEM((DEPTH, D_HY, lseq), BF16)],
        compiler_params=_cparams(2),
        name=f"hyena_filter_{lseq}",
    )(zfeat_t, window_t, fmat, p["hy_w1t"], p["hy_b1"], p["hy_w2t"], p["hy_b2"], p["hy_freq"], p["hy_w3t"])


def _hy_spectral_block(f_blk, g_blk, hf, z_bf, is_dc_block):
    kb = f_blk.shape[0] // 2
    zf = jnp.dot(f_blk, z_bf, preferred_element_type=F32)
    rz, iz = zf[:kb], zf[kb:]
    rh, ih = hf[:kb], hf[kb:]
    ii = iz * ih
    re = rz * rh - ii
    im = rz * ih + iz * rh
    if is_dc_block is not None:
        dc = jnp.logical_and(lax.broadcasted_iota(I32, (kb, 1), 0) == 0, is_dc_block)
        re = jnp.where(dc, rz * rh, re)
        im = jnp.where(dc, ii, im)
    pr = jnp.concatenate([re, im], axis=0).astype(BF16)
    return jnp.dot(g_blk, pr, preferred_element_type=F32)


def _hy_prologue(u_ref, cw_ref, cb_ref, lseq, z_ref, zbf_ref, x2_ref):
    rin = _row_in_seq(u_ref.shape[0], lseq)
    uc = _dwconv(u_ref[...], cw_ref, cb_ref, rin, lseq)
    z = uc[:, :D_HY] * uc[:, D_HY:2 * D_HY]
    z_ref[...] = z
    zbf_ref[...] = z.astype(BF16)
    x2_ref[...] = uc[:, 2 * D_HY:]


def _hy_kernel(u_ref, cw_ref, cb_ref, hb_ref, fc_ref, gc_ref, hfc_ref, fl_ref, gl_ref, hfl_ref,
               o_ref, z_ref, zbf_ref, x2_ref, acc_ref):
    b = pl.program_id(0)
    k = pl.program_id(1)
    cw, cb = cw_ref.at[0], cb_ref.at[0]

    @pl.when(jnp.logical_and(b < NB_CTX, k == 0))
    def _():
        _hy_prologue(u_ref, cw, cb, L_CTX, z_ref, zbf_ref, x2_ref)
        for s in range(RB // L_CTX):
            rows = slice(s * L_CTX, (s + 1) * L_CTX)
            acc_ref[rows, :] = _hy_spectral_block(fc_ref[...], gc_ref[...], hfc_ref[0], zbf_ref[rows, :], True)

    @pl.when(b >= NB_CTX)
    def _():
        @pl.when(k == 0)
        def _():
            _hy_prologue(u_ref, cw, cb, L_LAT, z_ref, zbf_ref, x2_ref)
            acc_ref[...] = jnp.zeros_like(acc_ref)

        acc_ref[...] += _hy_spectral_block(fl_ref[...], gl_ref[...], hfl_ref[0], zbf_ref[...], k == 0)

    @pl.when(k == HY_NKB - 1)
    def _():
        o_ref[...] = x2_ref[...] * (acc_ref[...] + hb_ref[0] * z_ref[...])


def _hy_mixer(l, u_hy, p, c):
    lat_k = lambda b, k: jnp.where(b < NB_CTX, 0, k)
    lsel = lambda *shape: pl.BlockSpec((1,) + shape, lambda b, k: (l,) + (0,) * len(shape))
    kbl = 2 * HY_KB_LAT
    return pl.pallas_call(
        _hy_kernel,
        grid=(NB, HY_NKB),
        in_specs=[
            pl.BlockSpec((RB, 3 * D_HY), lambda b, k: (b, 0)),
            lsel(3, 3 * D_HY), lsel(1, 3 * D_HY), lsel(1, D_HY),
            pl.BlockSpec((N_FFT_CTX, L_CTX), lambda b, k: (0, 0)),
            pl.BlockSpec((L_CTX, N_FFT_CTX), lambda b, k: (0, 0)),
            lsel(N_FFT_CTX, D_HY),
            pl.BlockSpec((kbl, L_LAT), lambda b, k: (lat_k(b, k), 0)),
            pl.BlockSpec((L_LAT, kbl), lambda b, k: (0, lat_k(b, k))),
            pl.BlockSpec((1, kbl, D_HY), lambda b, k: (l, lat_k(b, k), 0)),
        ],
        out_specs=pl.BlockSpec((RB, D_HY), lambda b, k: (b, 0)),
        out_shape=jax.ShapeDtypeStruct((T, D_HY), F32),
        scratch_shapes=[pltpu.VMEM((RB, D_HY), F32), pltpu.VMEM((RB, D_HY), BF16),
                        pltpu.VMEM((RB, D_HY), F32), pltpu.VMEM((RB, D_HY), F32)],
        compiler_params=_cparams(2),
        name=f"hyena_mixer_{l}",
    )(u_hy, p["hy_conv_w"], p["hy_conv_b"], p["hy_bias"],
      c["f_ctx"], c["g_ctx"], c["hf_ctx"], c["f_lat"], c["g_lat"], c["hf_lat"])


def _ssd_block(u_z, u_xbc, u_dt, u_dtT, cw, cb, dtb_row, dtb_col, alog_row, alog_col, d_exp, nw,
               e64_ref, mbd_ref, mdiag_ref, h0_ref, o_ref, st_ref,
               x_s, bc_s, yb_s, sf_s, sb_s, cs_s, col_s, row_s, lseq):
    y_acc = (o_ref, yb_s)
    s_dir = (sf_s, sb_s)
    rows = u_z.shape[0]
    nchunk = rows // SSM_Q
    cps = lseq // SSM_Q
    rin = _row_in_seq(rows, lseq)
    for c0 in range(0, D_XBC, LANE):
        cols = slice(c0, c0 + LANE)
        xbc = _dwconv(u_xbc[:, cols], cw.at[:, cols], cb.at[:, cols], rin, lseq)
        xbc = xbc * jax.nn.sigmoid(xbc)
        if c0 < D_SSM:
            x_s[:, cols] = xbc
        else:
            bc_s[:, c0 - D_SSM:c0 - D_SSM + LANE] = xbc

    li = lax.broadcasted_iota(I32, (SSM_Q, SSM_Q), 0)
    si = lax.broadcasted_iota(I32, (SSM_Q, SSM_Q), 1)
    low_half = lax.broadcasted_iota(I32, (SSM_Q, LANE), 1) < SSM_N
    hq = SSM_H // SSM_G
    wq = hq * SSM_P
    hp = SSM_H * SSM_P

    for d in range(2):
        for g in range(SSM_G):
            if h0_ref is not None:
                s_dir[d][g] = jnp.tile(h0_ref[d, g * wq:(g + 1) * wq, :], (1, hq))
            else:
                s_dir[d][g] = jnp.zeros((wq, wq), F32)

    nh2 = 2 * SSM_H
    tri_f = (li >= si).astype(BF16)
    tri_b = (li <= si).astype(BF16)
    fwd_lane = lax.broadcasted_iota(I32, (1, nh2), 1) < SSM_H
    fwd_sub = lax.broadcasted_iota(I32, (nh2, 1), 0) < SSM_H
    a_row = -jnp.exp(alog_row[...])
    a_col = -jnp.exp(alog_col[...])
    def chunk_stats(c, carry):
        rs = pl.ds(pl.multiple_of(c * SSM_Q, SSM_Q), SSM_Q)
        dt_c = jax.nn.softplus(u_dt[rs, :] + dtb_row[...])
        hi, lo = _split_bf16(dt_c * a_row, 2)
        cs_f = jnp.dot(tri_f, hi, preferred_element_type=F32) + jnp.dot(tri_f, lo, preferred_element_type=F32)
        cs_b = jnp.dot(tri_b, hi, preferred_element_type=F32) + jnp.dot(tri_b, lo, preferred_element_type=F32)
        cs_col = jnp.where(fwd_lane, cs_f, cs_b)
        tot = jnp.where(fwd_lane, cs_f[SSM_Q - 1:SSM_Q, :], cs_b[0:1, :])
        cs_s[rs, :] = cs_col
        for k, v in enumerate((jnp.exp(cs_col), dt_c * jnp.exp(tot - cs_col))):
            for m, piece in enumerate(_split_bf16(v, 2)):
                col_s[rs, (2 * k + m) * nh2:(2 * k + m + 1) * nh2] = piece
        dt_r = jax.nn.softplus(u_dtT[c] + dtb_col[...])
        hi, lo = _split_bf16(dt_r * a_col, 2)
        csr_f = jnp.dot(hi, tri_b, preferred_element_type=F32) + jnp.dot(lo, tri_b, preferred_element_type=F32)
        csr_b = jnp.dot(hi, tri_f, preferred_element_type=F32) + jnp.dot(lo, tri_f, preferred_element_type=F32)
        row_s[c, :nh2, :] = jnp.where(fwd_sub, csr_f, csr_b)
        row_s[c, nh2:, :] = dt_r
        return carry

    lax.fori_loop(0, nchunk, chunk_stats, 0, unroll=8)

    def chunk_step(ci, carry, d):
        causal = (li >= si) if d == 0 else (li <= si)
        edge = SSM_Q - 1 if d == 0 else 0
        c = ci if d == 0 else nchunk - 1 - ci
        rsl = pl.ds(pl.multiple_of(c * SSM_Q, SSM_Q), SSM_Q)
        if h0_ref is None and cps < nchunk:
            first = (c % cps == 0) if d == 0 else (c % cps == cps - 1)
            s_dir[d][...] = s_dir[d][...] * jnp.where(first, 0.0, 1.0)

        cs_col = cs_s[rsl, :]
        rows_c = row_s[c]
        spread = jnp.dot(col_s[rsl, :], e64_ref[d], preferred_element_type=F32)
        ecs_x, wdec_x = spread[:, :hp], spread[:, hp:]
        etot_x = ecs_x[edge:edge + 1, :]

        bcm = bc_s[rsl, :]
        bm, cm = bcm[:, :LANE], bcm[:, LANE:]
        bm_r, cm_r = pltpu.roll(bm, SSM_N, axis=1), pltpu.roll(cm, SSM_N, axis=1)
        bmb, cmb = bm.astype(BF16), cm.astype(BF16)
        for g in range(SSM_G):
            gl = slice(g * SSM_N, (g + 1) * SSM_N)
            ql = slice(g * wq, (g + 1) * wq)
            same = low_half if g == 0 else jnp.logical_not(low_half)
            b2 = jnp.where(same, bm, bm_r)
            c2 = jnp.where(same, cm, cm_r)
            gmat = lax.dot_general(cmb[:, gl], bmb[:, gl], (((1,), (1,)), ((), ())),
                                   preferred_element_type=F32)
            sc = []
            for h in range(SSM_H * d + g * hq, SSM_H * d + (g + 1) * hq):
                diff = cs_col[:, h:h + 1] - rows_c[h:h + 1, :]
                decay = jnp.exp(jnp.where(causal, diff, -1e30))
                sc.append((gmat * decay * rows_c[nh2 + h:nh2 + h + 1, :]).astype(BF16))
            sc = jnp.concatenate(sc, axis=1)
            xq = x_s[rsl, ql]
            bd = jnp.tile(xq.astype(BF16), (hq, 1)) * mbd_ref[...]
            y = jnp.dot(sc, bd, preferred_element_type=F32)
            s_old = s_dir[d][g]
            y_off = lax.dot_general(jnp.concatenate([c2, c2], axis=1).astype(BF16),
                                    (s_old * mdiag_ref[...]).astype(BF16),
                                    (((1,), (1,)), ((), ())), preferred_element_type=F32)
            y_acc[d][rsl, ql] = y + y_off * ecs_x[:, ql]
            bx = (jnp.concatenate([b2, b2], axis=1) * wdec_x[:, ql]).astype(BF16)
            s_new = jnp.dot(xq.T.astype(BF16), bx, preferred_element_type=F32)
            s_dir[d][g] = s_old * etot_x[:, ql] + s_new

        if st_ref is not None:
            last = (c % cps == cps - 1) if d == 0 else (c % cps == 0)

            @pl.when(last)
            def _():
                for g in range(SSM_G):
                    for hl in range(hq):
                        blk = slice(hl * SSM_P, (hl + 1) * SSM_P)
                        st_ref[c // cps, 0, d, g * hq + hl] = s_dir[d][g, blk, blk]
        return carry

    for d in range(2):
        lax.fori_loop(0, nchunk, functools.partial(chunk_step, d=d), 0, unroll=4)

    for r0 in range(0, rows, 256):
        rsl = slice(r0, r0 + 256)
        y = o_ref[rsl, :] + yb_s[rsl, :] + d_exp[...] * x_s[rsl, :]
        z = u_z[rsl, :]
        y = y * (z * jax.nn.sigmoid(z))
        o_ref[rsl, :] = _rms(y) * nw[...]


def _ssd_kernel(u_z, u_xbc, u_dt, u_dtT, cw, cb, dtb_row, dtb_col, alog_row, alog_col, d_exp, nw,
                e64_ref, mbd_ref, mdiag_ref, h0_ref, st_in_ref, o_ref, st_ref,
                x_s, bc_s, yb_s, sf_s, sb_s, cs_s, col_s, row_s):
    del st_in_ref
    b = pl.program_id(0)
    args = (u_z, u_xbc, u_dt, u_dtT, cw.at[0], cb.at[0], dtb_row.at[0], dtb_col.at[0], alog_row.at[0],
            alog_col.at[0], d_exp.at[0], nw.at[0], e64_ref, mbd_ref, mdiag_ref)
    scr = (x_s, bc_s, yb_s, sf_s, sb_s, cs_s, col_s, row_s)

    @pl.when(b < NB_CTX)
    def _():
        _ssd_block(*args, None, o_ref, st_ref, *scr, L_CTX)

    @pl.when(b >= NB_CTX)
    def _():
        _ssd_block(*args, h0_ref.at[0, 0], o_ref, None, *scr, L_LAT)


def _ssd_mixer(l, u_z, u_xbc, u_dt, u_dtT, p, c, state_ssm, new_states):
    lsel = lambda *shape: pl.BlockSpec((1,) + shape, lambda b: (l,) + (0,) * len(shape))
    full = lambda a: pl.BlockSpec(a.shape, lambda b: (0,) * a.ndim, pipeline_mode=pl.Buffered(1))
    hp = SSM_H * SSM_P
    nseq_blk = RB // L_CTX
    return pl.pallas_call(
        _ssd_kernel,
        grid=(NB,),
        in_specs=[
            pl.BlockSpec((RB, D_SSM), lambda b: (b, 0), pipeline_mode=pl.Buffered(1)),
            pl.BlockSpec((RB, D_XBC), lambda b: (b, 0)),
            pl.BlockSpec((RB, 2 * SSM_H), lambda b: (b, 0)),
            pl.BlockSpec((RB // SSM_Q, 2 * SSM_H, SSM_Q), lambda b: (b, 0, 0)),
            lsel(4, D_XBC), lsel(1, D_XBC), lsel(1, 2 * SSM_H), lsel(2 * SSM_H, 1),
            lsel(1, 2 * SSM_H), lsel(2 * SSM_H, 1), lsel(1, D_SSM), lsel(1, D_SSM),
            full(c["e64"]), full(c["mbd"]), full(c["mdiag"]),
            pl.BlockSpec((1, 1, 2, hp, SSM_N), lambda b: (jnp.maximum(b - NB_CTX, 0), l, 0, 0, 0)),
            pl.BlockSpec(memory_space=pl.ANY),
        ],
        out_specs=[pl.BlockSpec((RB, D_SSM), lambda b: (b, 0)),
                   pl.BlockSpec((nseq_blk, 1, 2, SSM_H, SSM_P, SSM_N),
                                lambda b: (jnp.minimum(b, NB_CTX - 1), l, 0, 0, 0, 0))],
        out_shape=[jax.ShapeDtypeStruct((T, D_SSM), F32),
                   jax.ShapeDtypeStruct(new_states.shape, F32)],
        input_output_aliases={16: 1},
        scratch_shapes=[pltpu.VMEM((RB, D_SSM), F32), pltpu.VMEM((RB, 2 * SSM_G * SSM_N), F32),
                        pltpu.VMEM((RB, D_SSM), F32),
                        pltpu.VMEM((SSM_G, hp // SSM_G, hp // SSM_G), F32),
                        pltpu.VMEM((SSM_G, hp // SSM_G, hp // SSM_G), F32),
                        pltpu.VMEM((RB, 2 * SSM_H), F32), pltpu.VMEM((RB, 8 * SSM_H), BF16),
                        pltpu.VMEM((RB // SSM_Q, 4 * SSM_H, SSM_Q), F32)],
        compiler_params=_cparams(1),
        name=f"ssd_mixer_{l}",
    )(u_z, u_xbc, u_dt, u_dtT, p["ssm_conv_w"], p["ssm_conv_b"], p["ssm_dtb_row"], p["ssm_dtb_col"],
      p["ssm_alog_row"], p["ssm_alog_col"], p["ssm_d_exp"], p["ssm_norm_w"],
      c["e64"], c["mbd"], c["mdiag"], state_ssm, new_states)


def _k2_kernel(*refs, routed):
    if routed:
        (ol_ref, oh_ref, os_ref, x_ref, mod_ref, nw_ref, w_ref, rt_ref,
         xo_ref, h2_ref, route_ref, cnt_ref, wbf_ref) = refs
    else:
        ol_ref, oh_ref, os_ref, x_ref, mod_ref, nw_ref, w_ref, xo_ref, h2_ref, wbf_ref = refs

    @pl.when(pl.program_id(0) == 0)
    def _():
        wbf_ref[...] = w_ref[0].astype(BF16)

    m = mod_ref[0]
    o = jnp.dot(ol_ref[...].astype(BF16), wbf_ref[0:256, :], preferred_element_type=F32)
    o = o + jnp.dot(oh_ref[...].astype(BF16), wbf_ref[256:512, :], preferred_element_type=F32)
    o = o + jnp.dot(os_ref[...].astype(BF16), wbf_ref[512:1024, :], preferred_element_type=F32)
    x = x_ref[...] + m[2:3] * o
    xo_ref[...] = x
    h2 = _rms(x) * nw_ref[0]
    h2 = h2 * (1.0 + m[4:5]) + m[3:4]
    h2_hi = h2.astype(BF16)
    h2_ref[...] = h2_hi

    if routed:
        h2_lo = (h2 - h2_hi.astype(F32)).astype(BF16)
        r_hi, r_lo = _split_bf16(rt_ref[0], 2)
        logits = (jnp.dot(h2_hi, r_hi, preferred_element_type=F32)
                  + jnp.dot(h2_lo, r_hi, preferred_element_type=F32)
                  + jnp.dot(h2_hi, r_lo, preferred_element_type=F32))
        eid = lax.broadcasted_iota(I32, logits.shape, 1)
        m1 = jnp.max(logits, axis=1, keepdims=True)
        i1 = jnp.min(jnp.where(logits == m1, eid, N_EXP), axis=1, keepdims=True)
        rest = jnp.where(eid == i1, -jnp.inf, logits)
        m2 = jnp.max(rest, axis=1, keepdims=True)
        i2 = jnp.min(jnp.where(rest == m2, eid, N_EXP), axis=1, keepdims=True)
        w1 = 1.0 / (1.0 + jnp.exp(m2 - m1))
        w2 = 1.0 - w1
        oh1 = (eid == i1).astype(F32)
        oh2 = (eid == i2).astype(F32)
        both = oh1 + oh2
        before = (lax.broadcasted_iota(I32, (TM, TM), 0) > lax.broadcasted_iota(I32, (TM, TM), 1))
        ahead = jnp.dot(before.astype(BF16), both.astype(BF16), preferred_element_type=F32)
        r1 = jnp.sum(oh1 * ahead, axis=1, keepdims=True)
        r2 = jnp.sum(oh2 * ahead, axis=1, keepdims=True)
        zero = jnp.zeros_like(w1)
        route_ref[...] = jnp.concatenate(
            [i1.astype(F32), i2.astype(F32), r1, r2, w1, w2, zero, zero], axis=1)
        cnt_ref[0] = jnp.sum(both, axis=0, keepdims=True)


def _k2(l, o_lru, o_hy, o_ssm, x, mod_l, norm2_w, w_out, router=None, j=0):
    routed = router is not None
    tok = lambda w: pl.BlockSpec((TM, w), lambda i: (i, 0))
    in_specs = [
        tok(D_LRU), tok(D_HY), tok(D_SSM), tok(D),
        pl.BlockSpec((1, 6, D), lambda i: (_mod_row(i), 0, 0)),
        pl.BlockSpec((1, 1, D), lambda i: (l, 0, 0)),
        pl.BlockSpec((1, D, D), lambda i: (l, 0, 0)),
    ]
    args = [o_lru, o_hy, o_ssm, x, mod_l, norm2_w.reshape(DEPTH, 1, D), w_out]
    out_specs = [tok(D), tok(D)]
    out_shape = [jax.ShapeDtypeStruct((T, D), F32), jax.ShapeDtypeStruct((T, D), BF16)]
    if routed:
        in_specs.append(pl.BlockSpec((1, D, N_EXP), lambda i: (j, 0, 0)))
        args.append(router)
        out_specs += [tok(8), pl.BlockSpec((1, 1, N_EXP), lambda i: (i, 0, 0))]
        out_shape += [jax.ShapeDtypeStruct((T, 8), F32), jax.ShapeDtypeStruct((NT, 1, N_EXP), F32)]
    return pl.pallas_call(
        functools.partial(_k2_kernel, routed=routed),
        grid=(NT,),
        in_specs=in_specs,
        out_specs=out_specs,
        out_shape=out_shape,
        scratch_shapes=[pltpu.VMEM((D, D), BF16)],
        compiler_params=_cparams(1),
        name=f"k2_outproj_{l}",
    )(*args)


def _ffn_stream_kernel(be_ref, bi_ref, na_ref, x_ref, *rest, j, dense):
    if dense:
        xres_ref, mod_ref, wg_hbm, wu_hbm, wd_hbm, o_ref, wg_s, wu_s, wd_s, stg_g, stg_u, stg_d, sem = rest
    else:
        wg_hbm, wu_hbm, wd_hbm, o_ref, wg_s, wu_s, wd_s, stg_g, stg_u, stg_d, sem = rest
    del bi_ref
    b = pl.program_id(0)
    n_act = na_ref[0]
    e = be_ref[b]
    active = b < n_act
    load = jnp.logical_and(active, jnp.logical_or(b == 0, be_ref[jnp.maximum(b - 1, 0)] != e))
    e_next = be_ref[jnp.minimum(b + 1, pl.num_programs(0) - 1)]
    feed_next = jnp.logical_and(b + 1 < n_act, e_next != e)

    def copies(ee, c):
        slot = c % FFN_STAGES
        cols = slice(c * FF_CHUNK, (c + 1) * FF_CHUNK)
        return (pltpu.make_async_copy(wg_hbm.at[j, ee, :, cols], stg_g.at[slot], sem.at[0, slot]),
                pltpu.make_async_copy(wu_hbm.at[j, ee, :, cols], stg_u.at[slot], sem.at[1, slot]),
                pltpu.make_async_copy(wd_hbm.at[j, ee, cols, :], stg_d.at[slot], sem.at[2, slot]))

    def start(ee, c):
        for cp in copies(ee, c):
            cp.start()

    def chunk_out(c, x, acc):
        g = jnp.dot(x, wg_s[c], preferred_element_type=F32)
        u = jnp.dot(x, wu_s[c], preferred_element_type=F32)
        hmid = (g * jax.nn.sigmoid(g) * u).astype(BF16)
        part = jnp.dot(hmid, wd_s[c], preferred_element_type=F32)
        return part if acc is None else acc + part

    def finish(acc):
        if dense:
            o_ref[...] = xres_ref[...] + mod_ref[0][5:6] * acc
        else:
            o_ref[...] = acc.astype(o_ref.dtype)

    @pl.when(load)
    def _():
        @pl.when(b == 0)
        def _():
            for c in range(FFN_STAGES):
                start(e, c)

        x = x_ref[...]
        acc = None
        for c in range(N_FF_CHUNK):
            slot = c % FFN_STAGES
            for cp in copies(e, c):
                cp.wait()
            wg_s[c] = stg_g[slot].astype(BF16)
            wu_s[c] = stg_u[slot].astype(BF16)
            wd_s[c] = stg_d[slot].astype(BF16)
            if c + FFN_STAGES < N_FF_CHUNK:
                start(e, c + FFN_STAGES)
            acc = chunk_out(c, x, acc)
        finish(acc)

    @pl.when(jnp.logical_and(active, jnp.logical_not(load)))
    def _():
        x = x_ref[...]
        acc = None
        for c in range(N_FF_CHUNK):
            acc = chunk_out(c, x, acc)
        finish(acc)

    if not dense:
        @pl.when(jnp.logical_not(active))
        def _():
            o_ref[...] = jnp.zeros_like(o_ref)

    @pl.when(feed_next)
    def _():
        for c in range(FFN_STAGES):
            start(e_next, c)


def _ffn_scratch():
    return [pltpu.VMEM((N_FF_CHUNK, D, FF_CHUNK), BF16), pltpu.VMEM((N_FF_CHUNK, D, FF_CHUNK), BF16),
            pltpu.VMEM((N_FF_CHUNK, FF_CHUNK, D), BF16),
            pltpu.VMEM((FFN_STAGES, D, FF_CHUNK), F32), pltpu.VMEM((FFN_STAGES, D, FF_CHUNK), F32),
            pltpu.VMEM((FFN_STAGES, FF_CHUNK, D), F32), pltpu.SemaphoreType.DMA((3, FFN_STAGES))]


def _dense_ffn(j, h2, x, mod_l, wg, wu, wd):
    hbm = pl.BlockSpec(memory_space=pl.ANY)
    zeros = jnp.zeros((NT,), I32)
    return pl.pallas_call(
        functools.partial(_ffn_stream_kernel, j=j, dense=True),
        grid_spec=pltpu.PrefetchScalarGridSpec(
            num_scalar_prefetch=3,
            grid=(NT,),
            in_specs=[
                pl.BlockSpec((TM, D), lambda i, *_: (i, 0)),
                pl.BlockSpec((TM, D), lambda i, *_: (i, 0)),
                pl.BlockSpec((1, 6, D), lambda i, *_: (_mod_row(i), 0, 0)),
                hbm, hbm, hbm,
            ],
            out_specs=pl.BlockSpec((TM, D), lambda i, *_: (i, 0)),
            scratch_shapes=_ffn_scratch(),
        ),
        out_shape=jax.ShapeDtypeStruct((T, D), F32),
        compiler_params=_cparams(1),
        name=f"dense_ffn_{j}",
    )(zeros, zeros, jnp.full((1,), NT, I32), h2, x, mod_l, wg[:, None], wu[:, None], wd[:, None])


def _expert_ffn(j, xs, blk_e, blk_i, n_active, wg, wu, wd):
    hbm = pl.BlockSpec(memory_space=pl.ANY)
    return pl.pallas_call(
        functools.partial(_ffn_stream_kernel, j=j, dense=False),
        grid_spec=pltpu.PrefetchScalarGridSpec(
            num_scalar_prefetch=3,
            grid=(NBLK,),
            in_specs=[pl.BlockSpec((TMB, D), lambda b, be, bi, na: (bi[b], 0)), hbm, hbm, hbm],
            out_specs=pl.BlockSpec((TMB, D), lambda b, be, bi, na: (b, 0)),
            scratch_shapes=_ffn_scratch(),
        ),
        out_shape=jax.ShapeDtypeStruct((RMAX, D), BF16),
        compiler_params=_cparams(1),
        name=f"expert_ffn_{j}",
    )(blk_e, blk_i, n_active, xs, wg, wu, wd)


def _chunk_rows(idx):
    return pl.ds(pl.multiple_of(idx * GCH, GCH), GCH)


def _sort_kernel(gch_ref, nused_ref, pad0_ref, npad_ref, h_ref, dl_ref, xs_ref, xl_ref, zero_ref, sem):
    i = pl.program_id(0)
    slot = i % 2

    def copy(tile, q):
        s = tile % 2
        return pltpu.make_async_copy(xl_ref.at[s, _chunk_rows(q)],
                                     xs_ref.at[_chunk_rows(gch_ref[tile * NLC + q])], sem.at[s])

    def wait_tile(tile):
        lax.fori_loop(0, nused_ref[tile], lambda q, c: (copy(tile, q).wait(), c)[1], 0)

    @pl.when(i >= 2)
    def _():
        wait_tile(i - 2)

    dl = dl_ref[0]
    r = lax.broadcasted_iota(I32, (LCAP, TM), 0).astype(F32)
    perm = jnp.logical_or(r == dl[0:1, :], r == dl[1:2, :]).astype(BF16)
    xl_ref[slot] = jnp.dot(perm, h_ref[...], preferred_element_type=F32).astype(BF16)
    lax.fori_loop(0, nused_ref[i], lambda q, c: (copy(i, q).start(), c)[1], 0)

    @pl.when(i == NT - 1)
    def _():
        zero_ref[...] = jnp.zeros_like(zero_ref)
        for e in range(N_EXP + 1):
            def zcopy(q, e=e):
                return pltpu.make_async_copy(zero_ref, xs_ref.at[_chunk_rows(pad0_ref[e] + q)], sem.at[2])
            lax.fori_loop(0, npad_ref[e], lambda q, c, f=zcopy: (f(q).start(), c)[1], 0)
            lax.fori_loop(0, npad_ref[e], lambda q, c, f=zcopy: (f(q).wait(), c)[1], 0)
        wait_tile(i - 1)
        wait_tile(i)


def _sort_tokens(h2, dl_row, gch, nused, pad0, npad):
    return pl.pallas_call(
        _sort_kernel,
        grid_spec=pltpu.PrefetchScalarGridSpec(
            num_scalar_prefetch=4,
            grid=(NT,),
            in_specs=[
                pl.BlockSpec((TM, D), lambda i, *_: (i, 0)),
                pl.BlockSpec((1, 2, TM), lambda i, *_: (i, 0, 0)),
            ],
            out_specs=pl.BlockSpec(memory_space=pl.ANY),
            scratch_shapes=[pltpu.VMEM((2, LCAP, D), BF16), pltpu.VMEM((GCH, D), BF16),
                            pltpu.SemaphoreType.DMA((3,))],
        ),
        out_shape=jax.ShapeDtypeStruct((RMAX, D), BF16),
        compiler_params=_cparams(1),
        name="moe_sort",
    )(gch, nused, pad0, npad, h2, dl_row)


def _combine_kernel(gch_ref, nused_ref, y_ref, info_ref, x_ref, mod_ref, fw_ref, *rest, final):
    if final:
        oc_ref, ol_ref, yl_ref, sem = rest
    else:
        o_ref, yl_ref, sem = rest
    i = pl.program_id(0)
    slot = i % 2

    def copy(tile, q):
        s = tile % 2
        return pltpu.make_async_copy(y_ref.at[_chunk_rows(gch_ref[tile * NLC + q])],
                                     yl_ref.at[s, _chunk_rows(q)], sem.at[s])

    def fetch(tile):
        lax.fori_loop(0, nused_ref[tile], lambda q, c: (copy(tile, q).start(), c)[1], 0)

    @pl.when(i == 0)
    def _():
        fetch(i)

    @pl.when(i + 1 < NT)
    def _():
        fetch(i + 1)

    n = nused_ref[i]

    def clear(q, c):
        yl_ref[slot, _chunk_rows(q), :] = jnp.zeros((GCH, D), BF16)
        return c

    lax.fori_loop(n, NLC, clear, 0)
    lax.fori_loop(0, n, lambda q, c: (copy(i, q).wait(), c)[1], 0)

    info = info_ref[...]
    col = lax.broadcasted_iota(I32, (TM, LCAP), 1).astype(F32)
    yl = yl_ref[slot]
    y1 = jnp.dot((col == info[:, 0:1]).astype(BF16), yl, preferred_element_type=F32)
    y2 = jnp.dot((col == info[:, 1:2]).astype(BF16), yl, preferred_element_type=F32)
    x = x_ref[...] + mod_ref[0][5:6] * (info[:, 2:3] * y1 + info[:, 3:4] * y2)
    if final:
        x = _rms(x) * fw_ref[...]

        @pl.when(i < NT_CTX)
        def _():
            oc_ref[...] = x

        @pl.when(i >= NT_CTX)
        def _():
            ol_ref[...] = x
    else:
        o_ref[...] = x


def _combine(y, info, x, mod_l, final_w, gch, nused, final):
    tile = pl.BlockSpec((TM, D), lambda i, *_: (i, 0))
    if final:
        out_specs = [pl.BlockSpec((TM, D), lambda i, *_: (jnp.minimum(i, NT_CTX - 1), 0)),
                     pl.BlockSpec((TM, D), lambda i, *_: (jnp.maximum(i - NT_CTX, 0), 0))]
        out_shape = [jax.ShapeDtypeStruct((T_CTX, D), F32), jax.ShapeDtypeStruct((T_LAT, D), F32)]
    else:
        out_specs, out_shape = tile, jax.ShapeDtypeStruct((T, D), F32)
    return pl.pallas_call(
        functools.partial(_combine_kernel, final=final),
        grid_spec=pltpu.PrefetchScalarGridSpec(
            num_scalar_prefetch=2,
            grid=(NT,),
            in_specs=[
                pl.BlockSpec(memory_space=pl.ANY),
                pl.BlockSpec((TM, 4), lambda i, *_: (i, 0)),
                tile,
                pl.BlockSpec((1, 6, D), lambda i, *_: (_mod_row(i), 0, 0)),
                pl.BlockSpec((1, D), lambda i, *_: (0, 0)),
            ],
            out_specs=out_specs,
            scratch_shapes=[pltpu.VMEM((2, LCAP, D), BF16), pltpu.SemaphoreType.DMA((2,))],
        ),
        out_shape=out_shape,
        compiler_params=_cparams(1),
        name="moe_combine",
    )(gch, nused, y, info, x, mod_l, final_w.reshape(1, D))


def _moe_plan(route, counts):
    cnt = counts.reshape(NT, N_EXP).astype(I32)
    cpad = (cnt + GCH - 1) // GCH * GCH
    lo = jnp.cumsum(cpad, axis=1) - cpad
    nused = (lo[:, -1] + cpad[:, -1]) // GCH
    tot = jnp.sum(cpad, axis=0)
    gpad = (tot + TMB - 1) // TMB * TMB
    goff = jnp.cumsum(gpad) - gpad
    so = goff[None, :] + jnp.cumsum(cpad, axis=0) - cpad

    e1 = route[:, 0].astype(I32)
    e2 = route[:, 1].astype(I32)
    eid = jnp.arange(N_EXP, dtype=I32)[None, :]
    lo_tok = jnp.repeat(lo, TM, axis=0)
    dl1 = jnp.sum(jnp.where(e1[:, None] == eid, lo_tok, 0), axis=1).astype(F32) + route[:, 2]
    dl2 = jnp.sum(jnp.where(e2[:, None] == eid, lo_tok, 0), axis=1).astype(F32) + route[:, 3]
    dl_row = jnp.stack([dl1.reshape(NT, TM), dl2.reshape(NT, TM)], axis=1)
    info = jnp.stack([dl1, dl2, route[:, 4], route[:, 5]], axis=1)

    q = jnp.arange(NLC, dtype=I32)[None, :, None]
    lo16 = (lo // GCH)[:, None, :]
    c16 = (cpad // GCH)[:, None, :]
    in_seg = jnp.logical_and(q >= lo16, q < lo16 + c16)
    gch = jnp.sum(jnp.where(in_seg, (so // GCH)[:, None, :] + q - lo16, 0), axis=2).reshape(NT * NLC)

    nblk = gpad // TMB
    n_active = jnp.sum(nblk)
    b = jnp.arange(NBLK, dtype=I32)
    blk_i = jnp.maximum(jnp.minimum(b, n_active - 1), 0)
    bend = (goff + gpad) // TMB
    blk_e = jnp.minimum(jnp.sum((blk_i[:, None] >= bend[None, :]).astype(I32), axis=1), N_EXP - 1)
    used = n_active * TMB
    pad0 = jnp.concatenate([goff + tot, used.reshape(1)]) // GCH
    npad = jnp.concatenate([gpad - tot, (RMAX - used).reshape(1)]) // GCH
    return dict(dl_row=dl_row, info=info, gch=gch.astype(I32), nused=nused.astype(I32),
                blk_e=blk_e.astype(I32), blk_i=blk_i.astype(I32),
                n_active=n_active.reshape(1).astype(I32), pad0=pad0.astype(I32), npad=npad.astype(I32))


def _grid_pos_embed(n_tokens):
    rows = n_tokens // GRID_W
    r = jnp.repeat(jnp.arange(rows, dtype=F32), GRID_W)
    col = jnp.tile(jnp.arange(GRID_W, dtype=F32), rows)
    quarter = D // 4
    omega = 1.0 / (10000.0 ** (jnp.arange(quarter, dtype=F32) / quarter))
    ang_r = r[:, None] * omega[None]
    ang_c = col[:, None] * omega[None]
    return jnp.concatenate([jnp.sin(ang_r), jnp.cos(ang_r), jnp.sin(ang_c), jnp.cos(ang_c)], axis=-1)


def _hy_pos_features(n):
    pos = jnp.arange(n, dtype=F32)
    t = pos / (n - 1)
    bands = jnp.linspace(1e-4, HY_BANDS - 1, HY_BANDS, dtype=F32)
    ang = (2.0 * math.pi * pos / n)[:, None] * bands[None]
    z = jnp.concatenate([t[:, None], jnp.cos(ang), -jnp.sin(ang)], axis=-1)
    z = jnp.pad(z, ((0, 0), (0, LANE - HY_POS_DIM)))
    half = n // 2
    dist = jnp.abs(pos - half) / half
    deltas = jnp.abs(jnp.linspace(HY_MIN_DECAY, HY_MAX_DECAY, D_HY, dtype=F32))
    return z, jnp.exp(-dist[:, None] * deltas[None])


def _dft_mats(lseq, n, kb):
    nkb = n // 2 // kb
    t = jnp.arange(lseq, dtype=I32)

    def tables(tt):
        def cs(freq):
            ang = (2.0 * math.pi / n) * ((freq[:, None] * tt[None, :]) % n).astype(F32)
            return jnp.cos(ang), jnp.sin(ang)
        (ca, sa), (cb, sb) = cs(jnp.arange(nkb, dtype=I32) * kb), cs(jnp.arange(kb, dtype=I32))
        ca, sa, cb, sb = ca[:, None, :], sa[:, None, :], cb[None], sb[None]
        re = ca * cb - sa * sb
        im = -(sa * cb + ca * sb)
        dc = jnp.logical_and(jnp.arange(nkb)[:, None, None] == 0, jnp.arange(kb)[None, :, None] == 0)
        alt = (1.0 - 2.0 * (tt % 2).astype(F32))[None, None, :]
        return re, jnp.where(dc, alt, im), dc

    re, im, _ = tables(t)
    f = jnp.concatenate([re, im], axis=1).reshape(n, lseq)
    re, im, dc = tables(t + lseq // 2)
    wk = jnp.where(dc, 1.0, 2.0) / n
    gt = jnp.concatenate([wk * re, jnp.where(dc, 1.0 / n, wk) * im], axis=1).reshape(n, lseq)
    return f.astype(BF16), gt.T.astype(BF16)


def _ssd_constants():
    hp = SSM_H * SSM_P
    hq = SSM_H // SSM_G
    e64 = np.zeros((2, 8 * SSM_H, 2 * hp), np.float32)
    for d in range(2):
        for q in range(4):
            for hh in range(SSM_H):
                e64[d, q * 2 * SSM_H + d * SSM_H + hh, (q // 2) * hp + hh * SSM_P:(q // 2) * hp + (hh + 1) * SSM_P] = 1.0
    hq_of = np.arange(hq * SSM_P) // SSM_P
    mbd = (np.arange(hq * SSM_Q)[:, None] // SSM_Q == hq_of[None, :]).astype(np.float32)
    mdiag = (hq_of[:, None] == hq_of[None, :]).astype(np.float32)
    return dict(e64=jnp.asarray(e64, dtype=BF16), mbd=jnp.asarray(mbd, dtype=BF16), mdiag=jnp.asarray(mdiag))


def _block_diag_heads(w):
    eye = jnp.eye(LRU_HEADS, dtype=w.dtype)
    return jnp.einsum("ldhij,hg->ldhigj", w, eye).reshape(DEPTH, 2, D_LRU, D_LRU)


def kernel(x_prompt, x_sample, state_lru, state_ssm, c, c_ctx, norm1_w, norm2_w, final_norm_w, ada_w, ada_b,
           w_in, w_out, lru_conv_w, lru_conv_b, lru_wa, lru_ba, lru_wi, lru_bi, lru_lambda, hy_conv_w, hy_conv_b,
           hy_w1, hy_b1, hy_w2, hy_b2, hy_freq, hy_w3, hy_bias, ssm_conv_w, ssm_conv_b, ssm_dt_bias, ssm_a_log,
           ssm_d, ssm_norm_w, ffn_w_gate, ffn_w_up, ffn_w_down, moe_router, moe_w_gate, moe_w_up, moe_w_down):
    hp = SSM_H * SSM_P
    wa, wi = _block_diag_heads(lru_wa), _block_diag_heads(lru_wi)
    row = lambda a: a.reshape(DEPTH, 1, -1)
    p = {
        "lru_conv_w": lru_conv_w, "lru_conv_b": row(lru_conv_b), "lru_lambda": lru_lambda,
        "lru_wbig": jnp.concatenate([wa[:, 0], wi[:, 0], wa[:, 1], wi[:, 1]], axis=-1).astype(BF16),
        "lru_bias": jnp.concatenate([lru_ba[:, 0], lru_bi[:, 0], lru_ba[:, 1], lru_bi[:, 1]], axis=-1)[:, None],
        "hy_conv_w": hy_conv_w, "hy_conv_b": row(hy_conv_b), "hy_bias": row(hy_bias),
        "hy_w1t": jnp.swapaxes(jnp.pad(hy_w1, ((0, 0), (0, LANE - HY_POS_DIM), (0, 0))), 1, 2),
        "hy_b1": hy_b1[:, :, None], "hy_w2t": jnp.swapaxes(hy_w2, 1, 2), "hy_b2": hy_b2[:, :, None],
        "hy_freq": hy_freq[:, :, None], "hy_w3t": jnp.swapaxes(hy_w3, 1, 2),
        "ssm_conv_w": ssm_conv_w, "ssm_conv_b": row(ssm_conv_b),
        "ssm_dtb_row": row(ssm_dt_bias), "ssm_dtb_col": ssm_dt_bias.reshape(DEPTH, 2 * SSM_H, 1),
        "ssm_alog_row": row(ssm_a_log), "ssm_alog_col": ssm_a_log.reshape(DEPTH, 2 * SSM_H, 1),
        "ssm_d_exp": jnp.repeat(ssm_d, SSM_P, axis=-1)[:, None], "ssm_norm_w": row(ssm_norm_w),
    }
    w_dtT = jnp.swapaxes(w_in[:, :, D_MAIN:], 1, 2)
    ffn_w = (ffn_w_gate, ffn_w_up, ffn_w_down)
    moe_w = (moe_w_gate, moe_w_up, moe_w_down)

    cst = _ssd_constants()
    cst["lru_rep"] = jnp.asarray(np.arange(RB)[:, None] // SUBLANES == np.arange(RB // SUBLANES)[None, :], dtype=BF16)
    z_ctx, win_ctx = _hy_pos_features(L_CTX)
    z_lat, win_lat = _hy_pos_features(L_LAT)
    cst["f_ctx"], cst["g_ctx"] = _dft_mats(L_CTX, N_FFT_CTX, N_FFT_CTX // 2)
    cst["f_lat"], cst["g_lat"] = _dft_mats(L_LAT, N_FFT_LAT, HY_KB_LAT)
    cst["hf_ctx"] = _hy_filter_spectrum(L_CTX, N_FFT_CTX // 2, z_ctx.T, win_ctx.T, cst["f_ctx"], p)
    cst["hf_lat"] = _hy_filter_spectrum(L_LAT, HY_KB_LAT, z_lat.T, win_lat.T, cst["f_lat"], p)

    cond = jnp.concatenate([c_ctx[None], c], axis=0)
    mod = _mod_table(jnp.broadcast_to(cond[:, :, None], (3, D, LANE)), ada_w, ada_b)
    mod = mod[:, :3].reshape(DEPTH, 3, 6, D)

    x = jnp.concatenate([x_prompt.reshape(T_CTX, D),
                         (x_sample + _grid_pos_embed(L_LAT)[None]).reshape(T_LAT, D)], axis=0)
    st_ssm_in = state_ssm.reshape(N_LAT_SEQ, DEPTH, 2, hp, SSM_N)

    lru_states = []
    new_ssm = jnp.zeros((N_CTX_SEQ, DEPTH, 2, SSM_H, SSM_P, SSM_N), F32)
    for l in range(DEPTH):
        u_lru, u_hy, u_z, u_xbc, u_dt, u_dtT = _k1(l, x, mod[l], norm1_w, w_in, w_dtT)
        o_lru, s_lru = _lru_mixer(l, u_lru, p, cst, state_lru)
        o_hy = _hy_mixer(l, u_hy, p, cst)
        o_ssm, new_ssm = _ssd_mixer(l, u_z, u_xbc, u_dt, u_dtT, p, cst, st_ssm_in, new_ssm)
        lru_states.append(s_lru.reshape(N_CTX_SEQ, 2, D_LRU))
        j = l // 2
        if l % 2 == 0:
            x, h2 = _k2(l, o_lru, o_hy, o_ssm, x, mod[l], norm2_w, w_out)
            x = _dense_ffn(j, h2, x, mod[l], *ffn_w)
        else:
            x, h2, route, counts = _k2(l, o_lru, o_hy, o_ssm, x, mod[l], norm2_w, w_out, moe_router, j)
            plan = _moe_plan(route, counts)
            xs = _sort_tokens(h2, plan["dl_row"], plan["gch"], plan["nused"], plan["pad0"], plan["npad"])
            y = _expert_ffn(j, xs, plan["blk_e"], plan["blk_i"], plan["n_active"], *moe_w)
            x = _combine(y, plan["info"], x, mod[l], final_norm_w, plan["gch"], plan["nused"],
                         final=(l == DEPTH - 1))
    y_prompt = x[0].reshape(N_CTX_SEQ, L_CTX, D)
    y_sample = x[1].reshape(N_LAT_SEQ, L_LAT, D)
    return (y_prompt, y_sample, jnp.stack(lru_states, axis=1), new_ssm)
```

```python
import functools
import math

import numpy as np
import jax
import jax.numpy as jnp
from jax import lax
from jax.experimental import pallas as pl
from jax.experimental.pallas import tpu as pltpu

F32 = jnp.float32
BF16 = jnp.bfloat16
I32 = jnp.int32
HI = lax.Precision.HIGHEST

D = 1024
N_CTX_SEQ, L_CTX = 16, 256
N_LAT_SEQ, L_LAT = 2, 2048
DEPTH = 4
GRID_W = 64
D_LRU = 256
LRU_HEADS, LRU_HD = 4, 64
LRU_C = 8.0
D_HY = 256
HY_BANDS = 16
HY_POS_DIM = 1 + 2 * HY_BANDS
HY_HID = 64
HY_MAX_DECAY = math.log(1e-2) / 0.3
HY_MIN_DECAY = math.log(1e-2) / 1.5
D_SSM = 512
SSM_P = 64
SSM_H = 8
SSM_G = 2
SSM_N = 64
SSM_Q = 128
D_XBC = D_SSM + 2 * SSM_G * SSM_N
D_MAIN = 2 * D_LRU + 3 * D_HY + D_SSM + D_XBC
D_IN = D_MAIN + 2 * SSM_H
D_FF = 2816
N_EXP = 8
EPS = 1e-6

LANE = 128
SUBLANES = 8
BF16_ROWS = 16
T_CTX = N_CTX_SEQ * L_CTX
T_LAT = N_LAT_SEQ * L_LAT
T = T_CTX + T_LAT
TM = 512
NT = T // TM
NT_CTX = T_CTX // TM
NT_PER_LAT = L_LAT // TM
RB = 2048
NB = T // RB
NB_CTX = T_CTX // RB
FF_CHUNK = 256
N_FF_CHUNK = D_FF // FF_CHUNK
FFN_STAGES = 3
VMEM_LIMIT = 56 * 1024 * 1024

GCH = BF16_ROWS
LCAP = 2 * TM + N_EXP * GCH
NLC = LCAP // GCH
TMB = 512
RMAX = -(-(2 * T + NT * N_EXP * (GCH - 1) + N_EXP * (TMB - 1)) // TMB) * TMB
NBLK = RMAX // TMB

HY_KB_LAT = 512
N_FFT_LAT = 3 * L_LAT // 2
N_FFT_CTX = 3 * L_CTX // 2
HY_NKB = N_FFT_LAT // 2 // HY_KB_LAT


def _cparams(n_axes=1, vmem=VMEM_LIMIT):
    return pltpu.CompilerParams(dimension_semantics=("arbitrary",) * n_axes, vmem_limit_bytes=vmem)


def _mod_row(i):
    return jnp.where(i < NT_CTX, 0, 1 + (i - NT_CTX) // NT_PER_LAT)


def _bdot(a, b):
    return jnp.dot(a.astype(BF16), b.astype(BF16), preferred_element_type=F32)


def _rms(x):
    return x * lax.rsqrt(jnp.mean(x * x, axis=-1, keepdims=True) + EPS)


def _split_bf16(v, parts):
    out = []
    for _ in range(parts):
        piece = v.astype(BF16)
        out.append(piece)
        v = v - piece.astype(F32)
    return out


def _mod_kernel(cb_ref, w_ref, b_ref, o_ref):
    tn = w_ref.shape[2]

    def body(kc, accs):
        k0 = pl.multiple_of(kc * 8, 8)
        wk = w_ref[0, pl.ds(k0, 8), :]
        out = []
        for r in range(3):
            c = cb_ref[r, pl.ds(k0, 8), :]
            c = c * jax.nn.sigmoid(c)
            out.append(accs[r] + jnp.tile(c, (1, tn // LANE)) * wk)
        return tuple(out)

    accs = lax.fori_loop(0, D // 8, body, tuple(jnp.zeros((8, tn), F32) for _ in range(3)), unroll=8)
    rows = [jnp.sum(a, axis=0, keepdims=True) + b_ref[0] for a in accs]
    o_ref[0] = jnp.concatenate(rows + [jnp.zeros((5, tn), F32)], axis=0)


def _mod_table(cond_b, ada_w, ada_b):
    tn = 1024
    return pl.pallas_call(
        _mod_kernel,
        grid=(DEPTH, 6 * D // tn),
        in_specs=[
            pl.BlockSpec((3, D, LANE), lambda l, j: (0, 0, 0)),
            pl.BlockSpec((1, D, tn), lambda l, j: (l, 0, j)),
            pl.BlockSpec((1, 1, tn), lambda l, j: (l, 0, j)),
        ],
        out_specs=pl.BlockSpec((1, 8, tn), lambda l, j: (l, 0, j)),
        out_shape=jax.ShapeDtypeStruct((DEPTH, 8, 6 * D), F32),
        compiler_params=_cparams(2),
        name="mod_table",
    )(cond_b, ada_w, ada_b.reshape(DEPTH, 1, 6 * D))


def _k1_kernel(x_ref, mod_ref, nw_ref, w_hbm, wdtT_ref,
               o_lru, o_hy, o_z, o_xbc, o_dt, o_dtT, wbf_ref, stage_ref, sem, *, l):
    @pl.when(pl.program_id(0) == 0)
    def _():
        cp = pltpu.make_async_copy(w_hbm.at[l], stage_ref, sem.at[0])
        cp.start()
        cp.wait()
        wbf_ref[...] = stage_ref[...].astype(BF16)

    m = mod_ref[0]
    h = _rms(x_ref[...]) * nw_ref[0]
    h = h * (1.0 + m[1:2]) + m[0:1]
    hb = h.astype(BF16)

    def proj(lo, hi):
        return jnp.dot(hb, wbf_ref[:, lo:hi], preferred_element_type=F32)

    o_lru[...] = proj(0, 512)
    o_hy[...] = proj(512, 1280)
    o_z[...] = proj(1280, 1792)
    o_xbc[...] = proj(1792, 2560)
    o_dt[...] = proj(D_MAIN, D_IN)
    dtT = lax.dot_general(wdtT_ref[0].astype(BF16), hb, (((1,), (1,)), ((), ())),
                          preferred_element_type=F32)
    for j in range(TM // SSM_Q):
        o_dtT[j] = dtT[:, j * SSM_Q:(j + 1) * SSM_Q]


def _k1(l, x, mod_l, norm1_w, w_in, w_dtT):
    tok = lambda w: pl.BlockSpec((TM, w), lambda i: (i, 0))
    return pl.pallas_call(
        functools.partial(_k1_kernel, l=l),
        grid=(NT,),
        in_specs=[
            tok(D),
            pl.BlockSpec((1, 6, D), lambda i: (_mod_row(i), 0, 0)),
            pl.BlockSpec((1, 1, D), lambda i: (l, 0, 0)),
            pl.BlockSpec(memory_space=pl.ANY),
            pl.BlockSpec((1, 2 * SSM_H, D), lambda i: (l, 0, 0)),
        ],
        out_specs=[tok(512), tok(768), tok(512), tok(768), tok(2 * SSM_H),
                   pl.BlockSpec((TM // SSM_Q, 2 * SSM_H, SSM_Q), lambda i: (i, 0, 0))],
        out_shape=[jax.ShapeDtypeStruct((T, 512), F32), jax.ShapeDtypeStruct((T, 768), F32),
                   jax.ShapeDtypeStruct((T, 512), F32), jax.ShapeDtypeStruct((T, 768), F32),
                   jax.ShapeDtypeStruct((T, 2 * SSM_H), F32),
                   jax.ShapeDtypeStruct((T // SSM_Q, 2 * SSM_H, SSM_Q), F32)],
        scratch_shapes=[pltpu.VMEM((D, D_IN), BF16), pltpu.VMEM((D, D_IN), F32),
                        pltpu.SemaphoreType.DMA((1,))],
        compiler_params=_cparams(1),
        name=f"k1_inproj_{l}",
    )(x, mod_l, norm1_w.reshape(DEPTH, 1, D), w_in, w_dtT)


def _row_in_seq(rows, lseq):
    return lax.broadcasted_iota(I32, (rows, 1), 0) & (lseq - 1)


def _shift_rows(x, s, rin, lseq):
    if s == 0:
        return x
    y = pltpu.roll(x, s % x.shape[0], axis=0)
    valid = (rin >= s) if s > 0 else (rin < lseq + s)
    return jnp.where(valid, y, 0.0)


def _dwconv(x, w_ref, b_ref, rin, lseq):
    k_w = w_ref.shape[0]
    y = b_ref[...]
    for k in range(k_w):
        y = y + w_ref[k:k + 1, :] * _shift_rows(x, k_w // 2 - k, rin, lseq)
    return y


def _lru_scan(a, b, d, rin, lseq, rep_ref, ab_s):
    rows = a.shape[0]
    ngrp, gps = rows // SUBLANES, lseq // SUBLANES
    sub = rin & (SUBLANES - 1)
    for s in (1, 2, 4):
        sh = (s if d == 0 else -s) % rows
        valid = (sub >= s) if d == 0 else (sub < SUBLANES - s)
        b = b + jnp.where(valid, a * pltpu.roll(b, sh, axis=0), 0.0)
        a = jnp.where(valid, a * pltpu.roll(a, sh, axis=0), a)
    edge = SUBLANES - 1 if d == 0 else 0

    def group_edges(k, v):
        for j in range(D_LRU // LANE):
            ab_s[k, j] = v[:, j * LANE:(j + 1) * LANE]
        return jnp.concatenate([ab_s[k, j, pl.ds(edge, ngrp, stride=SUBLANES), :]
                                for j in range(D_LRU // LANE)], axis=1)

    ga, gb = group_edges(0, a), group_edges(1, b)
    gin = lax.broadcasted_iota(I32, (ngrp, 1), 0) & (gps - 1)
    s = 1
    while s < gps:
        sh = (s if d == 0 else -s) % ngrp
        valid = (gin >= s) if d == 0 else (gin < gps - s)
        gb = gb + jnp.where(valid, ga * pltpu.roll(gb, sh, axis=0), 0.0)
        if 2 * s < gps:
            ga = jnp.where(valid, ga * pltpu.roll(ga, sh, axis=0), ga)
        s *= 2
    valid = (gin >= 1) if d == 0 else (gin < gps - 1)
    cin = jnp.where(valid, pltpu.roll(gb, (1 if d == 0 else -1) % ngrp, axis=0), 0.0)
    hi, lo = _split_bf16(cin, 2)
    cin_x = (jnp.dot(rep_ref[...], hi, preferred_element_type=F32)
             + jnp.dot(rep_ref[...], lo, preferred_element_type=F32))
    return a * cin_x + b


def _lru_block(u_ref, cw_ref, cb_ref, wbig_ref, bias_ref, lam_ref, rep_ref, h0_ref, o_ref, st_ref, ab_s, lseq):
    rows = u_ref.shape[0]
    rin = _row_in_seq(rows, lseq)
    u = u_ref[...]
    gate = u[:, D_LRU:]
    x = _dwconv(u[:, :D_LRU], cw_ref, cb_ref, rin, lseq)
    xb = x.astype(BF16)
    y = None
    finals = []
    for d in range(2):
        g = jnp.dot(xb, wbig_ref[:, 512 * d:512 * (d + 1)], preferred_element_type=F32)
        g = g + bias_ref[:, 512 * d:512 * (d + 1)]
        r = jax.nn.sigmoid(g[:, :D_LRU])
        ig = jax.nn.sigmoid(g[:, D_LRU:])
        log_a = -LRU_C * r * jax.nn.softplus(-lam_ref[d:d + 1, :])
        a = jnp.exp(log_a)
        th = jnp.tanh(log_a)
        b = jnp.sqrt(-2.0 * th / (1.0 - th)) * (ig * x)
        if h0_ref is not None:
            edge = (rin == 0) if d == 0 else (rin == lseq - 1)
            b = b + jnp.where(edge, a * h0_ref[d:d + 1, :], 0.0)
        b = _lru_scan(a, b, d, rin, lseq, rep_ref, ab_s)
        y = b if y is None else y + b
        if st_ref is not None:
            last = lseq - 1 if d == 0 else 0
            finals.append(jnp.concatenate(
                [b[j * lseq + last:j * lseq + last + 1, :] for j in range(rows // lseq)], axis=0))
    o_ref[...] = y * jax.nn.gelu(gate)
    if st_ref is not None:
        st_ref[...] = jnp.concatenate(finals, axis=1)


def _lru_kernel(u_ref, cw_ref, cb_ref, wbig_ref, bias_ref, lam_ref, rep_ref, h0_ref, o_ref, st_ref, ab_s):
    b = pl.program_id(0)
    args = (u_ref, cw_ref.at[0], cb_ref.at[0], wbig_ref.at[0], bias_ref.at[0], lam_ref.at[0], rep_ref)

    @pl.when(b < NB_CTX)
    def _():
        _lru_block(*args, None, o_ref, st_ref, ab_s, L_CTX)

    @pl.when(b >= NB_CTX)
    def _():
        _lru_block(*args, h0_ref.at[0, 0], o_ref, None, ab_s, L_LAT)


def _lru_mixer(l, u_lru, p, c, state_lru):
    lsel = lambda *shape: pl.BlockSpec((1,) + shape, lambda b: (l,) + (0,) * len(shape))
    return pl.pallas_call(
        _lru_kernel,
        grid=(NB,),
        in_specs=[
            pl.BlockSpec((RB, 512), lambda b: (b, 0)),
            lsel(4, D_LRU), lsel(1, D_LRU), lsel(D_LRU, 1024), lsel(1, 1024), lsel(2, D_LRU),
            pl.BlockSpec((RB, RB // SUBLANES), lambda b: (0, 0), pipeline_mode=pl.Buffered(1)),
            pl.BlockSpec((1, 1, 2, D_LRU), lambda b: (jnp.maximum(b - NB_CTX, 0), l, 0, 0)),
        ],
        out_specs=[pl.BlockSpec((RB, D_LRU), lambda b: (b, 0)),
                   pl.BlockSpec((RB // L_CTX, 2 * D_LRU), lambda b: (jnp.minimum(b, NB_CTX - 1), 0))],
        out_shape=[jax.ShapeDtypeStruct((T, D_LRU), F32),
                   jax.ShapeDtypeStruct((N_CTX_SEQ, 2 * D_LRU), F32)],
        scratch_shapes=[pltpu.VMEM((2, D_LRU // LANE, RB, LANE), F32)],
        compiler_params=_cparams(1),
        name=f"lru_mixer_{l}",
    )(u_lru, p["lru_conv_w"], p["lru_conv_b"], p["lru_wbig"], p["lru_bias"], p["lru_lambda"],
      c["lru_rep"], state_lru)


def _hy_filter_kernel(z_ref, win_ref, f_ref, w1_ref, b1_ref, w2_ref, b2_ref, fr_ref, w3_ref,
                      o_ref, h_ref):
    l = pl.program_id(1)

    @pl.when(pl.program_id(0) == 0)
    def _():
        fr = fr_ref[0]
        g = jnp.sin(fr * (jnp.dot(w1_ref[0], z_ref[...], precision=HI, preferred_element_type=F32)
                          + b1_ref[0]))
        g = jnp.sin(fr * (jnp.dot(w2_ref[0], g, precision=HI, preferred_element_type=F32) + b2_ref[0]))
        h = jnp.dot(w3_ref[0], g, precision=HI, preferred_element_type=F32)
        h_ref[l] = (h * win_ref[...]).astype(BF16)

    o_ref[0] = lax.dot_general(f_ref[...], h_ref[l], (((1,), (1,)), ((), ())), preferred_element_type=F32)


def _hy_filter_spectrum(lseq, kb, zfeat_t, window_t, fmat, p):
    n = fmat.shape[0]
    lsel = lambda *shape: pl.BlockSpec((1,) + shape, lambda k, l: (l,) + (0,) * len(shape))
    return pl.pallas_call(
        _hy_filter_kernel,
        grid=(n // (2 * kb), DEPTH),
        in_specs=[
            pl.BlockSpec((LANE, lseq), lambda k, l: (0, 0)),
            pl.BlockSpec((D_HY, lseq), lambda k, l: (0, 0)),
            pl.BlockSpec((2 * kb, lseq), lambda k, l: (k, 0)),
            lsel(HY_HID, LANE), lsel(HY_HID, 1), lsel(HY_HID, HY_HID), lsel(HY_HID, 1),
            lsel(HY_HID, 1), lsel(D_HY, HY_HID),
        ],
        out_specs=pl.BlockSpec((1, 2 * kb, D_HY), lambda k, l: (l, k, 0)),
        out_shape=jax.ShapeDtypeStruct((DEPTH, n, D_HY), F32),
        scratch_shapes=[pltpu.VMEM((DEPTH, D_HY, lseq), BF16)],
        compiler_params=_cparams(2),
        name=f"hyena_filter_{lseq}",
    )(zfeat_t, window_t, fmat, p["hy_w1t"], p["hy_b1"], p["hy_w2t"], p["hy_b2"], p["hy_freq"], p["hy_w3t"])


def _hy_spectral_block(f_blk, g_blk, hf, z_bf, is_dc_block):
    kb = f_blk.shape[0] // 2
    zf = jnp.dot(f_blk, z_bf, preferred_element_type=F32)
    rz, iz = zf[:kb], zf[kb:]
    rh, ih = hf[:kb], hf[kb:]
    ii = iz * ih
    re = rz * rh - ii
    im = rz * ih + iz * rh
    if is_dc_block is not None:
        dc = jnp.logical_and(lax.broadcasted_iota(I32, (kb, 1), 0) == 0, is_dc_block)
        re = jnp.where(dc, rz * rh, re)
        im = jnp.where(dc, ii, im)
    pr = jnp.concatenate([re, im], axis=0).astype(BF16)
    return jnp.dot(g_blk, pr, preferred_element_type=F32)


def _hy_prologue(u_ref, cw_ref, cb_ref, lseq, z_ref, zbf_ref, x2_ref):
    rin = _row_in_seq(u_ref.shape[0], lseq)
    uc = _dwconv(u_ref[...], cw_ref, cb_ref, rin, lseq)
    z = uc[:, :D_HY] * uc[:, D_HY:2 * D_HY]
    z_ref[...] = z
    zbf_ref[...] = z.astype(BF16)
    x2_ref[...] = uc[:, 2 * D_HY:]


def _hy_kernel(u_ref, cw_ref, cb_ref, hb_ref, fc_ref, gc_ref, hfc_ref, fl_ref, gl_ref, hfl_ref,
               o_ref, z_ref, zbf_ref, x2_ref, acc_ref):
    b = pl.program_id(0)
    k = pl.program_id(1)
    cw, cb = cw_ref.at[0], cb_ref.at[0]

    @pl.when(jnp.logical_and(b < NB_CTX, k == 0))
    def _():
        _hy_prologue(u_ref, cw, cb, L_CTX, z_ref, zbf_ref, x2_ref)
        for s in range(RB // L_CTX):
            rows = slice(s * L_CTX, (s + 1) * L_CTX)
            acc_ref[rows, :] = _hy_spectral_block(fc_ref[...], gc_ref[...], hfc_ref[0], zbf_ref[rows, :], True)

    @pl.when(b >= NB_CTX)
    def _():
        @pl.when(k == 0)
        def _():
            _hy_prologue(u_ref, cw, cb, L_LAT, z_ref, zbf_ref, x2_ref)
            acc_ref[...] = jnp.zeros_like(acc_ref)

        acc_ref[...] += _hy_spectral_block(fl_ref[...], gl_ref[...], hfl_ref[0], zbf_ref[...], k == 0)

    @pl.when(k == HY_NKB - 1)
    def _():
        o_ref[...] = x2_ref[...] * (acc_ref[...] + hb_ref[0] * z_ref[...])


def _hy_mixer(l, u_hy, p, c):
    lat_k = lambda b, k: jnp.where(b < NB_CTX, 0, k)
    lsel = lambda *shape: pl.BlockSpec((1,) + shape, lambda b, k: (l,) + (0,) * len(shape))
    kbl = 2 * HY_KB_LAT
    return pl.pallas_call(
        _hy_kernel,
        grid=(NB, HY_NKB),
        in_specs=[
            pl.BlockSpec((RB, 3 * D_HY), lambda b, k: (b, 0)),
            lsel(3, 3 * D_HY), lsel(1, 3 * D_HY), lsel(1, D_HY),
            pl.BlockSpec((N_FFT_CTX, L_CTX), lambda b, k: (0, 0)),
            pl.BlockSpec((L_CTX, N_FFT_CTX), lambda b, k: (0, 0)),
            lsel(N_FFT_CTX, D_HY),
            pl.BlockSpec((kbl, L_LAT), lambda b, k: (lat_k(b, k), 0)),
            pl.BlockSpec((L_LAT, kbl), lambda b, k: (0, lat_k(b, k))),
            pl.BlockSpec((1, kbl, D_HY), lambda b, k: (l, lat_k(b, k), 0)),
        ],
        out_specs=pl.BlockSpec((RB, D_HY), lambda b, k: (b, 0)),
        out_shape=jax.ShapeDtypeStruct((T, D_HY), F32),
        scratch_shapes=[pltpu.VMEM((RB, D_HY), F32), pltpu.VMEM((RB, D_HY), BF16),
                        pltpu.VMEM((RB, D_HY), F32), pltpu.VMEM((RB, D_HY), F32)],
        compiler_params=_cparams(2),
        name=f"hyena_mixer_{l}",
    )(u_hy, p["hy_conv_w"], p["hy_conv_b"], p["hy_bias"],
      c["f_ctx"], c["g_ctx"], c["hf_ctx"], c["f_lat"], c["g_lat"], c["hf_lat"])


def _ssd_block(u_z, u_xbc, u_dt, u_dtT, cw, cb, dtb_row, dtb_col, alog_row, alog_col, d_exp, nw,
               e64_ref, mbd_ref, mdiag_ref, h0_ref, o_ref, st_ref,
               x_s, bc_s, yb_s, sf_s, sb_s, cs_s, col_s, row_s, lseq):
    y_acc = (o_ref, yb_s)
    s_dir = (sf_s, sb_s)
    rows = u_z.shape[0]
    nchunk = rows // SSM_Q
    cps = lseq // SSM_Q
    rin = _row_in_seq(rows, lseq)
    for c0 in range(0, D_XBC, LANE):
        cols = slice(c0, c0 + LANE)
        xbc = _dwconv(u_xbc[:, cols], cw.at[:, cols], cb.at[:, cols], rin, lseq)
        xbc = xbc * jax.nn.sigmoid(xbc)
        if c0 < D_SSM:
            x_s[:, cols] = xbc
        else:
            bc_s[:, c0 - D_SSM:c0 - D_SSM + LANE] = xbc

    li = lax.broadcasted_iota(I32, (SSM_Q, SSM_Q), 0)
    si = lax.broadcasted_iota(I32, (SSM_Q, SSM_Q), 1)
    low_half = lax.broadcasted_iota(I32, (SSM_Q, LANE), 1) < SSM_N
    hq = SSM_H // SSM_G
    wq = hq * SSM_P
    hp = SSM_H * SSM_P

    for d in range(2):
        for g in range(SSM_G):
            if h0_ref is not None:
                s_dir[d][g] = jnp.tile(h0_ref[d, g * wq:(g + 1) * wq, :], (1, hq))
            else:
                s_dir[d][g] = jnp.zeros((wq, wq), F32)

    nh2 = 2 * SSM_H
    tri_f = (li >= si).astype(BF16)
    tri_b = (li <= si).astype(BF16)
    fwd_lane = lax.broadcasted_iota(I32, (1, nh2), 1) < SSM_H
    fwd_sub = lax.broadcasted_iota(I32, (nh2, 1), 0) < SSM_H
    a_row = -jnp.exp(alog_row[...])
    a_col = -jnp.exp(alog_col[...])
    def chunk_stats(c, carry):
        rs = pl.ds(pl.multiple_of(c * SSM_Q, SSM_Q), SSM_Q)
        dt_c = jax.nn.softplus(u_dt[rs, :] + dtb_row[...])
        hi, lo = _split_bf16(dt_c * a_row, 2)
        cs_f = jnp.dot(tri_f, hi, preferred_element_type=F32) + jnp.dot(tri_f, lo, preferred_element_type=F32)
        cs_b = jnp.dot(tri_b, hi, preferred_element_type=F32) + jnp.dot(tri_b, lo, preferred_element_type=F32)
        cs_col = jnp.where(fwd_lane, cs_f, cs_b)
        tot = jnp.where(fwd_lane, cs_f[SSM_Q - 1:SSM_Q, :], cs_b[0:1, :])
        cs_s[rs, :] = cs_col
        for k, v in enumerate((jnp.exp(cs_col), dt_c * jnp.exp(tot - cs_col))):
            for m, piece in enumerate(_split_bf16(v, 2)):
                col_s[rs, (2 * k + m) * nh2:(2 * k + m + 1) * nh2] = piece
        dt_r = jax.nn.softplus(u_dtT[c] + dtb_col[...])
        hi, lo = _split_bf16(dt_r * a_col, 2)
        csr_f = jnp.dot(hi, tri_b, preferred_element_type=F32) + jnp.dot(lo, tri_b, preferred_element_type=F32)
        csr_b = jnp.dot(hi, tri_f, preferred_element_type=F32) + jnp.dot(lo, tri_f, preferred_element_type=F32)
        row_s[c, :nh2, :] = jnp.where(fwd_sub, csr_f, csr_b)
        row_s[c, nh2:, :] = dt_r
        return carry

    lax.fori_loop(0, nchunk, chunk_stats, 0, unroll=8)

    def chunk_step(ci, carry, d):
        causal = (li >= si) if d == 0 else (li <= si)
        edge = SSM_Q - 1 if d == 0 else 0
        c = ci if d == 0 else nchunk - 1 - ci
        rsl = pl.ds(pl.multiple_of(c * SSM_Q, SSM_Q), SSM_Q)
        if h0_ref is None and cps < nchunk:
            first = (c % cps == 0) if d == 0 else (c % cps == cps - 1)
            s_dir[d][...] = s_dir[d][...] * jnp.where(first, 0.0, 1.0)

        cs_col = cs_s[rsl, :]
        rows_c = row_s[c]
        spread = jnp.dot(col_s[rsl, :], e64_ref[d], preferred_element_type=F32)
        ecs_x, wdec_x = spread[:, :hp], spread[:, hp:]
        etot_x = ecs_x[edge:edge + 1, :]

        bcm = bc_s[rsl, :]
        bm, cm = bcm[:, :LANE], bcm[:, LANE:]
        bm_r, cm_r = pltpu.roll(bm, SSM_N, axis=1), pltpu.roll(cm, SSM_N, axis=1)
        bmb, cmb = bm.astype(BF16), cm.astype(BF16)
        for g in range(SSM_G):
            gl = slice(g * SSM_N, (g + 1) * SSM_N)
            ql = slice(g * wq, (g + 1) * wq)
            same = low_half if g == 0 else jnp.logical_not(low_half)
            b2 = jnp.where(same, bm, bm_r)
            c2 = jnp.where(same, cm, cm_r)
            gmat = lax.dot_general(cmb[:, gl], bmb[:, gl], (((1,), (1,)), ((), ())),
                                   preferred_element_type=F32)
            sc = []
            for h in range(SSM_H * d + g * hq, SSM_H * d + (g + 1) * hq):
                diff = cs_col[:, h:h + 1] - rows_c[h:h + 1, :]
                decay = jnp.exp(jnp.where(causal, diff, -1e30))
                sc.append((gmat * decay * rows_c[nh2 + h:nh2 + h + 1, :]).astype(BF16))
            sc = jnp.concatenate(sc, axis=1)
            xq = x_s[rsl, ql]
            bd = jnp.tile(xq.astype(BF16), (hq, 1)) * mbd_ref[...]
            y = jnp.dot(sc, bd, preferred_element_type=F32)
            s_old = s_dir[d][g]
            y_off = lax.dot_general(jnp.concatenate([c2, c2], axis=1).astype(BF16),
                                    (s_old * mdiag_ref[...]).astype(BF16),
                                    (((1,), (1,)), ((), ())), preferred_element_type=F32)
            y_acc[d][rsl, ql] = y + y_off * ecs_x[:, ql]
            bx = (jnp.concatenate([b2, b2], axis=1) * wdec_x[:, ql]).astype(BF16)
            s_new = jnp.dot(xq.T.astype(BF16), bx, preferred_element_type=F32)
            s_dir[d][g] = s_old * etot_x[:, ql] + s_new

        if st_ref is not None:
            last = (c % cps == cps - 1) if d == 0 else (c % cps == 0)

            @pl.when(last)
            def _():
                for g in range(SSM_G):
                    for hl in range(hq):
                        blk = slice(hl * SSM_P, (hl + 1) * SSM_P)
                        st_ref[c // cps, 0, d, g * hq + hl] = s_dir[d][g, blk, blk]
        return carry

    for d in range(2):
        lax.fori_loop(0, nchunk, functools.partial(chunk_step, d=d), 0, unroll=2)

    for r0 in range(0, rows, 256):
        rsl = slice(r0, r0 + 256)
        y = o_ref[rsl, :] + yb_s[rsl, :] + d_exp[...] * x_s[rsl, :]
        z = u_z[rsl, :]
        y = y * (z * jax.nn.sigmoid(z))
        o_ref[rsl, :] = _rms(y) * nw[...]


def _ssd_kernel(u_z, u_xbc, u_dt, u_dtT, cw, cb, dtb_row, dtb_col, alog_row, alog_col, d_exp, nw,
                e64_ref, mbd_ref, mdiag_ref, h0_ref, st_in_ref, o_ref, st_ref,
                x_s, bc_s, yb_s, sf_s, sb_s, cs_s, col_s, row_s):
    del st_in_ref
    b = pl.program_id(0)
    args = (u_z, u_xbc, u_dt, u_dtT, cw.at[0], cb.at[0], dtb_row.at[0], dtb_col.at[0], alog_row.at[0],
            alog_col.at[0], d_exp.at[0], nw.at[0], e64_ref, mbd_ref, mdiag_ref)
    scr = (x_s, bc_s, yb_s, sf_s, sb_s, cs_s, col_s, row_s)

    @pl.when(b < NB_CTX)
    def _():
        _ssd_block(*args, None, o_ref, st_ref, *scr, L_CTX)

    @pl.when(b >= NB_CTX)
    def _():
        _ssd_block(*args, h0_ref.at[0, 0], o_ref, None, *scr, L_LAT)


def _ssd_mixer(l, u_z, u_xbc, u_dt, u_dtT, p, c, state_ssm, new_states):
    lsel = lambda *shape: pl.BlockSpec((1,) + shape, lambda b: (l,) + (0,) * len(shape))
    full = lambda a: pl.BlockSpec(a.shape, lambda b: (0,) * a.ndim, pipeline_mode=pl.Buffered(1))
    hp = SSM_H * SSM_P
    nseq_blk = RB // L_CTX
    return pl.pallas_call(
        _ssd_kernel,
        grid=(NB,),
        in_specs=[
            pl.BlockSpec((RB, D_SSM), lambda b: (b, 0), pipeline_mode=pl.Buffered(1)),
            pl.BlockSpec((RB, D_XBC), lambda b: (b, 0)),
            pl.BlockSpec((RB, 2 * SSM_H), lambda b: (b, 0)),
            pl.BlockSpec((RB // SSM_Q, 2 * SSM_H, SSM_Q), lambda b: (b, 0, 0)),
            lsel(4, D_XBC), lsel(1, D_XBC), lsel(1, 2 * SSM_H), lsel(2 * SSM_H, 1),
            lsel(1, 2 * SSM_H), lsel(2 * SSM_H, 1), lsel(1, D_SSM), lsel(1, D_SSM),
            full(c["e64"]), full(c["mbd"]), full(c["mdiag"]),
            pl.BlockSpec((1, 1, 2, hp, SSM_N), lambda b: (jnp.maximum(b - NB_CTX, 0), l, 0, 0, 0)),
            pl.BlockSpec(memory_space=pl.ANY),
        ],
        out_specs=[pl.BlockSpec((RB, D_SSM), lambda b: (b, 0)),
                   pl.BlockSpec((nseq_blk, 1, 2, SSM_H, SSM_P, SSM_N),
                                lambda b: (jnp.minimum(b, NB_CTX - 1), l, 0, 0, 0, 0))],
        out_shape=[jax.ShapeDtypeStruct((T, D_SSM), F32),
                   jax.ShapeDtypeStruct(new_states.shape, F32)],
        input_output_aliases={16: 1},
        scratch_shapes=[pltpu.VMEM((RB, D_SSM), F32), pltpu.VMEM((RB, 2 * SSM_G * SSM_N), F32),
                        pltpu.VMEM((RB, D_SSM), F32),
                        pltpu.VMEM((SSM_G, hp // SSM_G, hp // SSM_G), F32),
                        pltpu.VMEM((SSM_G, hp // SSM_G, hp // SSM_G), F32),
                        pltpu.VMEM((RB, 2 * SSM_H), F32), pltpu.VMEM((RB, 8 * SSM_H), BF16),
                        pltpu.VMEM((RB // SSM_Q, 4 * SSM_H, SSM_Q), F32)],
        compiler_params=_cparams(1),
        name=f"ssd_mixer_{l}",
    )(u_z, u_xbc, u_dt, u_dtT, p["ssm_conv_w"], p["ssm_conv_b"], p["ssm_dtb_row"], p["ssm_dtb_col"],
      p["ssm_alog_row"], p["ssm_alog_col"], p["ssm_d_exp"], p["ssm_norm_w"],
      c["e64"], c["mbd"], c["mdiag"], state_ssm, new_states)


def _k2_kernel(*refs, routed):
    if routed:
        (ol_ref, oh_ref, os_ref, x_ref, mod_ref, nw_ref, w_ref, rt_ref,
         xo_ref, h2_ref, route_ref, cnt_ref, wbf_ref) = refs
    else:
        ol_ref, oh_ref, os_ref, x_ref, mod_ref, nw_ref, w_ref, xo_ref, h2_ref, wbf_ref = refs

    @pl.when(pl.program_id(0) == 0)
    def _():
        wbf_ref[...] = w_ref[0].astype(BF16)

    m = mod_ref[0]
    o = jnp.dot(ol_ref[...].astype(BF16), wbf_ref[0:256, :], preferred_element_type=F32)
    o = o + jnp.dot(oh_ref[...].astype(BF16), wbf_ref[256:512, :], preferred_element_type=F32)
    o = o + jnp.dot(os_ref[...].astype(BF16), wbf_ref[512:1024, :], preferred_element_type=F32)
    x = x_ref[...] + m[2:3] * o
    xo_ref[...] = x
    h2 = _rms(x) * nw_ref[0]
    h2 = h2 * (1.0 + m[4:5]) + m[3:4]
    h2_hi = h2.astype(BF16)
    h2_ref[...] = h2_hi

    if routed:
        h2_lo = (h2 - h2_hi.astype(F32)).astype(BF16)
        r_hi, r_lo = _split_bf16(rt_ref[0], 2)
        logits = (jnp.dot(h2_hi, r_hi, preferred_element_type=F32)
                  + jnp.dot(h2_lo, r_hi, preferred_element_type=F32)
                  + jnp.dot(h2_hi, r_lo, preferred_element_type=F32))
        eid = lax.broadcasted_iota(I32, logits.shape, 1)
        m1 = jnp.max(logits, axis=1, keepdims=True)
        i1 = jnp.min(jnp.where(logits == m1, eid, N_EXP), axis=1, keepdims=True)
        rest = jnp.where(eid == i1, -jnp.inf, logits)
        m2 = jnp.max(rest, axis=1, keepdims=True)
        i2 = jnp.min(jnp.where(rest == m2, eid, N_EXP), axis=1, keepdims=True)
        w1 = 1.0 / (1.0 + jnp.exp(m2 - m1))
        w2 = 1.0 - w1
        oh1 = (eid == i1).astype(F32)
        oh2 = (eid == i2).astype(F32)
        both = oh1 + oh2
        before = (lax.broadcasted_iota(I32, (TM, TM), 0) > lax.broadcasted_iota(I32, (TM, TM), 1))
        ahead = jnp.dot(before.astype(BF16), both.astype(BF16), preferred_element_type=F32)
        r1 = jnp.sum(oh1 * ahead, axis=1, keepdims=True)
        r2 = jnp.sum(oh2 * ahead, axis=1, keepdims=True)
        zero = jnp.zeros_like(w1)
        route_ref[...] = jnp.concatenate(
            [i1.astype(F32), i2.astype(F32), r1, r2, w1, w2, zero, zero], axis=1)
        cnt_ref[0] = jnp.sum(both, axis=0, keepdims=True)


def _k2(l, o_lru, o_hy, o_ssm, x, mod_l, norm2_w, w_out, router=None, j=0):
    routed = router is not None
    tok = lambda w: pl.BlockSpec((TM, w), lambda i: (i, 0))
    in_specs = [
        tok(D_LRU), tok(D_HY), tok(D_SSM), tok(D),
        pl.BlockSpec((1, 6, D), lambda i: (_mod_row(i), 0, 0)),
        pl.BlockSpec((1, 1, D), lambda i: (l, 0, 0)),
        pl.BlockSpec((1, D, D), lambda i: (l, 0, 0)),
    ]
    args = [o_lru, o_hy, o_ssm, x, mod_l, norm2_w.reshape(DEPTH, 1, D), w_out]
    out_specs = [tok(D), tok(D)]
    out_shape = [jax.ShapeDtypeStruct((T, D), F32), jax.ShapeDtypeStruct((T, D), BF16)]
    if routed:
        in_specs.append(pl.BlockSpec((1, D, N_EXP), lambda i: (j, 0, 0)))
        args.append(router)
        out_specs += [tok(8), pl.BlockSpec((1, 1, N_EXP), lambda i: (i, 0, 0))]
        out_shape += [jax.ShapeDtypeStruct((T, 8), F32), jax.ShapeDtypeStruct((NT, 1, N_EXP), F32)]
    return pl.pallas_call(
        functools.partial(_k2_kernel, routed=routed),
        grid=(NT,),
        in_specs=in_specs,
        out_specs=out_specs,
        out_shape=out_shape,
        scratch_shapes=[pltpu.VMEM((D, D), BF16)],
        compiler_params=_cparams(1),
        name=f"k2_outproj_{l}",
    )(*args)


def _ffn_stream_kernel(be_ref, bi_ref, na_ref, x_ref, *rest, j, dense):
    if dense:
        xres_ref, mod_ref, wg_hbm, wu_hbm, wd_hbm, o_ref, wg_s, wu_s, wd_s, stg_g, stg_u, stg_d, sem = rest
    else:
        wg_hbm, wu_hbm, wd_hbm, o_ref, wg_s, wu_s, wd_s, stg_g, stg_u, stg_d, sem = rest
    del bi_ref
    b = pl.program_id(0)
    n_act = na_ref[0]
    e = be_ref[b]
    active = b < n_act
    load = jnp.logical_and(active, jnp.logical_or(b == 0, be_ref[jnp.maximum(b - 1, 0)] != e))
    e_next = be_ref[jnp.minimum(b + 1, pl.num_programs(0) - 1)]
    feed_next = jnp.logical_and(b + 1 < n_act, e_next != e)

    def copies(ee, c):
        slot = c % FFN_STAGES
        cols = slice(c * FF_CHUNK, (c + 1) * FF_CHUNK)
        return (pltpu.make_async_copy(wg_hbm.at[j, ee, :, cols], stg_g.at[slot], sem.at[0, slot]),
                pltpu.make_async_copy(wu_hbm.at[j, ee, :, cols], stg_u.at[slot], sem.at[1, slot]),
                pltpu.make_async_copy(wd_hbm.at[j, ee, cols, :], stg_d.at[slot], sem.at[2, slot]))

    def start(ee, c):
        for cp in copies(ee, c):
            cp.start()

    def chunk_out(c, x, acc):
        g = jnp.dot(x, wg_s[c], preferred_element_type=F32)
        u = jnp.dot(x, wu_s[c], preferred_element_type=F32)
        hmid = (g * jax.nn.sigmoid(g) * u).astype(BF16)
        part = jnp.dot(hmid, wd_s[c], preferred_element_type=F32)
        return part if acc is None else acc + part

    def finish(acc):
        if dense:
            o_ref[...] = xres_ref[...] + mod_ref[0][5:6] * acc
        else:
            o_ref[...] = acc.astype(o_ref.dtype)

    @pl.when(load)
    def _():
        @pl.when(b == 0)
        def _():
            for c in range(FFN_STAGES):
                start(e, c)

        x = x_ref[...]
        acc = None
        for c in range(N_FF_CHUNK):
            slot = c % FFN_STAGES
            for cp in copies(e, c):
                cp.wait()
            wg_s[c] = stg_g[slot].astype(BF16)
            wu_s[c] = stg_u[slot].astype(BF16)
            wd_s[c] = stg_d[slot].astype(BF16)
            if c + FFN_STAGES < N_FF_CHUNK:
                start(e, c + FFN_STAGES)
            acc = chunk_out(c, x, acc)
        finish(acc)

    @pl.when(jnp.logical_and(active, jnp.logical_not(load)))
    def _():
        x = x_ref[...]
        acc = None
        for c in range(N_FF_CHUNK):
            acc = chunk_out(c, x, acc)
        finish(acc)

    if not dense:
        @pl.when(jnp.logical_not(active))
        def _():
            o_ref[...] = jnp.zeros_like(o_ref)

    @pl.when(feed_next)
    def _():
        for c in range(FFN_STAGES):
            start(e_next, c)


def _ffn_scratch():
    return [pltpu.VMEM((N_FF_CHUNK, D, FF_CHUNK), BF16), pltpu.VMEM((N_FF_CHUNK, D, FF_CHUNK), BF16),
            pltpu.VMEM((N_FF_CHUNK, FF_CHUNK, D), BF16),
            pltpu.VMEM((FFN_STAGES, D, FF_CHUNK), F32), pltpu.VMEM((FFN_STAGES, D, FF_CHUNK), F32),
            pltpu.VMEM((FFN_STAGES, FF_CHUNK, D), F32), pltpu.SemaphoreType.DMA((3, FFN_STAGES))]


def _dense_ffn(j, h2, x, mod_l, wg, wu, wd):
    hbm = pl.BlockSpec(memory_space=pl.ANY)
    zeros = jnp.zeros((NT,), I32)
    return pl.pallas_call(
        functools.partial(_ffn_stream_kernel, j=j, dense=True),
        grid_spec=pltpu.PrefetchScalarGridSpec(
            num_scalar_prefetch=3,
            grid=(NT,),
            in_specs=[
                pl.BlockSpec((TM, D), lambda i, *_: (i, 0)),
                pl.BlockSpec((TM, D), lambda i, *_: (i, 0)),
                pl.BlockSpec((1, 6, D), lambda i, *_: (_mod_row(i), 0, 0)),
                hbm, hbm, hbm,
            ],
            out_specs=pl.BlockSpec((TM, D), lambda i, *_: (i, 0)),
            scratch_shapes=_ffn_scratch(),
        ),
        out_shape=jax.ShapeDtypeStruct((T, D), F32),
        compiler_params=_cparams(1),
        name=f"dense_ffn_{j}",
    )(zeros, zeros, jnp.full((1,), NT, I32), h2, x, mod_l, wg[:, None], wu[:, None], wd[:, None])


def _expert_ffn(j, xs, blk_e, blk_i, n_active, wg, wu, wd):
    hbm = pl.BlockSpec(memory_space=pl.ANY)
    return pl.pallas_call(
        functools.partial(_ffn_stream_kernel, j=j, dense=False),
        grid_spec=pltpu.PrefetchScalarGridSpec(
            num_scalar_prefetch=3,
            grid=(NBLK,),
            in_specs=[pl.BlockSpec((TMB, D), lambda b, be, bi, na: (bi[b], 0)), hbm, hbm, hbm],
            out_specs=pl.BlockSpec((TMB, D), lambda b, be, bi, na: (b, 0)),
            scratch_shapes=_ffn_scratch(),
        ),
        out_shape=jax.ShapeDtypeStruct((RMAX, D), BF16),
        compiler_params=_cparams(1),
        name=f"expert_ffn_{j}",
    )(blk_e, blk_i, n_active, xs, wg, wu, wd)


def _chunk_rows(idx):
    return pl.ds(pl.multiple_of(idx * GCH, GCH), GCH)


def _sort_kernel(gch_ref, nused_ref, pad0_ref, npad_ref, h_ref, dl_ref, xs_ref, xl_ref, zero_ref, sem):
    i = pl.program_id(0)
    slot = i % 2

    def copy(tile, q):
        s = tile % 2
        return pltpu.make_async_copy(xl_ref.at[s, _chunk_rows(q)],
                                     xs_ref.at[_chunk_rows(gch_ref[tile * NLC + q])], sem.at[s])

    def wait_tile(tile):
        lax.fori_loop(0, nused_ref[tile], lambda q, c: (copy(tile, q).wait(), c)[1], 0)

    @pl.when(i >= 2)
    def _():
        wait_tile(i - 2)

    dl = dl_ref[0]
    r = lax.broadcasted_iota(I32, (LCAP, TM), 0).astype(F32)
    perm = jnp.logical_or(r == dl[0:1, :], r == dl[1:2, :]).astype(BF16)
    xl_ref[slot] = jnp.dot(perm, h_ref[...], preferred_element_type=F32).astype(BF16)
    lax.fori_loop(0, nused_ref[i], lambda q, c: (copy(i, q).start(), c)[1], 0)

    @pl.when(i == NT - 1)
    def _():
        zero_ref[...] = jnp.zeros_like(zero_ref)
        for e in range(N_EXP + 1):
            def zcopy(q, e=e):
                return pltpu.make_async_copy(zero_ref, xs_ref.at[_chunk_rows(pad0_ref[e] + q)], sem.at[2])
            lax.fori_loop(0, npad_ref[e], lambda q, c, f=zcopy: (f(q).start(), c)[1], 0)
            lax.fori_loop(0, npad_ref[e], lambda q, c, f=zcopy: (f(q).wait(), c)[1], 0)
        wait_tile(i - 1)
        wait_tile(i)


def _sort_tokens(h2, dl_row, gch, nused, pad0, npad):
    return pl.pallas_call(
        _sort_kernel,
        grid_spec=pltpu.PrefetchScalarGridSpec(
            num_scalar_prefetch=4,
            grid=(NT,),
            in_specs=[
                pl.BlockSpec((TM, D), lambda i, *_: (i, 0)),
                pl.BlockSpec((1, 2, TM), lambda i, *_: (i, 0, 0)),
            ],
            out_specs=pl.BlockSpec(memory_space=pl.ANY),
            scratch_shapes=[pltpu.VMEM((2, LCAP, D), BF16), pltpu.VMEM((GCH, D), BF16),
                            pltpu.SemaphoreType.DMA((3,))],
        ),
        out_shape=jax.ShapeDtypeStruct((RMAX, D), BF16),
        compiler_params=_cparams(1),
        name="moe_sort",
    )(gch, nused, pad0, npad, h2, dl_row)


def _combine_kernel(gch_ref, nused_ref, y_ref, info_ref, x_ref, mod_ref, fw_ref, *rest, final):
    if final:
        oc_ref, ol_ref, yl_ref, sem = rest
    else:
        o_ref, yl_ref, sem = rest
    i = pl.program_id(0)
    slot = i % 2

    def copy(tile, q):
        s = tile % 2
        return pltpu.make_async_copy(y_ref.at[_chunk_rows(gch_ref[tile * NLC + q])],
                                     yl_ref.at[s, _chunk_rows(q)], sem.at[s])

    def fetch(tile):
        lax.fori_loop(0, nused_ref[tile], lambda q, c: (copy(tile, q).start(), c)[1], 0)

    @pl.when(i == 0)
    def _():
        fetch(i)

    @pl.when(i + 1 < NT)
    def _():
        fetch(i + 1)

    n = nused_ref[i]

    def clear(q, c):
        yl_ref[slot, _chunk_rows(q), :] = jnp.zeros((GCH, D), BF16)
        return c

    lax.fori_loop(n, NLC, clear, 0)
    lax.fori_loop(0, n, lambda q, c: (copy(i, q).wait(), c)[1], 0)

    info = info_ref[...]
    col = lax.broadcasted_iota(I32, (TM, LCAP), 1).astype(F32)
    yl = yl_ref[slot]
    y1 = jnp.dot((col == info[:, 0:1]).astype(BF16), yl, preferred_element_type=F32)
    y2 = jnp.dot((col == info[:, 1:2]).astype(BF16), yl, preferred_element_type=F32)
    x = x_ref[...] + mod_ref[0][5:6] * (info[:, 2:3] * y1 + info[:, 3:4] * y2)
    if final:
        x = _rms(x) * fw_ref[...]

        @pl.when(i < NT_CTX)
        def _():
            oc_ref[...] = x

        @pl.when(i >= NT_CTX)
        def _():
            ol_ref[...] = x
    else:
        o_ref[...] = x


def _combine(y, info, x, mod_l, final_w, gch, nused, final):
    tile = pl.BlockSpec((TM, D), lambda i, *_: (i, 0))
    if final:
        out_specs = [pl.BlockSpec((TM, D), lambda i, *_: (jnp.minimum(i, NT_CTX - 1), 0)),
                     pl.BlockSpec((TM, D), lambda i, *_: (jnp.maximum(i - NT_CTX, 0), 0))]
        out_shape = [jax.ShapeDtypeStruct((T_CTX, D), F32), jax.ShapeDtypeStruct((T_LAT, D), F32)]
    else:
        out_specs, out_shape = tile, jax.ShapeDtypeStruct((T, D), F32)
    return pl.pallas_call(
        functools.partial(_combine_kernel, final=final),
        grid_spec=pltpu.PrefetchScalarGridSpec(
            num_scalar_prefetch=2,
            grid=(NT,),
            in_specs=[
                pl.BlockSpec(memory_space=pl.ANY),
                pl.BlockSpec((TM, 4), lambda i, *_: (i, 0)),
                tile,
                pl.BlockSpec((1, 6, D), lambda i, *_: (_mod_row(i), 0, 0)),
                pl.BlockSpec((1, D), lambda i, *_: (0, 0)),
            ],
            out_specs=out_specs,
            scratch_shapes=[pltpu.VMEM((2, LCAP, D), BF16), pltpu.SemaphoreType.DMA((2,))],
        ),
        out_shape=out_shape,
        compiler_params=_cparams(1),
        name="moe_combine",
    )(gch, nused, y, info, x, mod_l, final_w.reshape(1, D))


def _moe_plan(route, counts):
    cnt = counts.reshape(NT, N_EXP).astype(I32)
    cpad = (cnt + GCH - 1) // GCH * GCH
    lo = jnp.cumsum(cpad, axis=1) - cpad
    nused = (lo[:, -1] + cpad[:, -1]) // GCH
    tot = jnp.sum(cpad, axis=0)
    gpad = (tot + TMB - 1) // TMB * TMB
    goff = jnp.cumsum(gpad) - gpad
    so = goff[None, :] + jnp.cumsum(cpad, axis=0) - cpad

    e1 = route[:, 0].astype(I32)
    e2 = route[:, 1].astype(I32)
    eid = jnp.arange(N_EXP, dtype=I32)[None, :]
    lo_tok = jnp.repeat(lo, TM, axis=0)
    dl1 = jnp.sum(jnp.where(e1[:, None] == eid, lo_tok, 0), axis=1).astype(F32) + route[:, 2]
    dl2 = jnp.sum(jnp.where(e2[:, None] == eid, lo_tok, 0), axis=1).astype(F32) + route[:, 3]
    dl_row = jnp.stack([dl1.reshape(NT, TM), dl2.reshape(NT, TM)], axis=1)
    info = jnp.stack([dl1, dl2, route[:, 4], route[:, 5]], axis=1)

    q = jnp.arange(NLC, dtype=I32)[None, :, None]
    lo16 = (lo // GCH)[:, None, :]
    c16 = (cpad // GCH)[:, None, :]
    in_seg = jnp.logical_and(q >= lo16, q < lo16 + c16)
    gch = jnp.sum(jnp.where(in_seg, (so // GCH)[:, None, :] + q - lo16, 0), axis=2).reshape(NT * NLC)

    nblk = gpad // TMB
    n_active = jnp.sum(nblk)
    b = jnp.arange(NBLK, dtype=I32)
    blk_i = jnp.maximum(jnp.minimum(b, n_active - 1), 0)
    bend = (goff + gpad) // TMB
    blk_e = jnp.minimum(jnp.sum((blk_i[:, None] >= bend[None, :]).astype(I32), axis=1), N_EXP - 1)
    used = n_active * TMB
    pad0 = jnp.concatenate([goff + tot, used.reshape(1)]) // GCH
    npad = jnp.concatenate([gpad - tot, (RMAX - used).reshape(1)]) // GCH
    return dict(dl_row=dl_row, info=info, gch=gch.astype(I32), nused=nused.astype(I32),
                blk_e=blk_e.astype(I32), blk_i=blk_i.astype(I32),
                n_active=n_active.reshape(1).astype(I32), pad0=pad0.astype(I32), npad=npad.astype(I32))


def _grid_pos_embed(n_tokens):
    rows = n_tokens // GRID_W
    r = jnp.repeat(jnp.arange(rows, dtype=F32), GRID_W)
    col = jnp.tile(jnp.arange(GRID_W, dtype=F32), rows)
    quarter = D // 4
    omega = 1.0 / (10000.0 ** (jnp.arange(quarter, dtype=F32) / quarter))
    ang_r = r[:, None] * omega[None]
    ang_c = col[:, None] * omega[None]
    return jnp.concatenate([jnp.sin(ang_r), jnp.cos(ang_r), jnp.sin(ang_c), jnp.cos(ang_c)], axis=-1)


def _hy_pos_features(n):
    pos = jnp.arange(n, dtype=F32)
    t = pos / (n - 1)
    bands = jnp.linspace(1e-4, HY_BANDS - 1, HY_BANDS, dtype=F32)
    ang = (2.0 * math.pi * pos / n)[:, None] * bands[None]
    z = jnp.concatenate([t[:, None], jnp.cos(ang), -jnp.sin(ang)], axis=-1)
    z = jnp.pad(z, ((0, 0), (0, LANE - HY_POS_DIM)))
    half = n // 2
    dist = jnp.abs(pos - half) / half
    deltas = jnp.abs(jnp.linspace(HY_MIN_DECAY, HY_MAX_DECAY, D_HY, dtype=F32))
    return z, jnp.exp(-dist[:, None] * deltas[None])


def _dft_mats(lseq, n, kb):
    nkb = n // 2 // kb
    t = jnp.arange(lseq, dtype=I32)

    def tables(tt):
        def cs(freq):
            ang = (2.0 * math.pi / n) * ((freq[:, None] * tt[None, :]) % n).astype(F32)
            return jnp.cos(ang), jnp.sin(ang)
        (ca, sa), (cb, sb) = cs(jnp.arange(nkb, dtype=I32) * kb), cs(jnp.arange(kb, dtype=I32))
        ca, sa, cb, sb = ca[:, None, :], sa[:, None, :], cb[None], sb[None]
        re = ca * cb - sa * sb
        im = -(sa * cb + ca * sb)
        dc = jnp.logical_and(jnp.arange(nkb)[:, None, None] == 0, jnp.arange(kb)[None, :, None] == 0)
        alt = (1.0 - 2.0 * (tt % 2).astype(F32))[None, None, :]
        return re, jnp.where(dc, alt, im), dc

    re, im, _ = tables(t)
    f = jnp.concatenate([re, im], axis=1).reshape(n, lseq)
    re, im, dc = tables(t + lseq // 2)
    wk = jnp.where(dc, 1.0, 2.0) / n
    gt = jnp.concatenate([wk * re, jnp.where(dc, 1.0 / n, wk) * im], axis=1).reshape(n, lseq)
    return f.astype(BF16), gt.T.astype(BF16)


def _ssd_constants():
    hp = SSM_H * SSM_P
    hq = SSM_H // SSM_G
    e64 = np.zeros((2, 8 * SSM_H, 2 * hp), np.float32)
    for d in range(2):
        for q in range(4):
            for hh in range(SSM_H):
                e64[d, q * 2 * SSM_H + d * SSM_H + hh, (q // 2) * hp + hh * SSM_P:(q // 2) * hp + (hh + 1) * SSM_P] = 1.0
    hq_of = np.arange(hq * SSM_P) // SSM_P
    mbd = (np.arange(hq * SSM_Q)[:, None] // SSM_Q == hq_of[None, :]).astype(np.float32)
    mdiag = (hq_of[:, None] == hq_of[None, :]).astype(np.float32)
    return dict(e64=jnp.asarray(e64, dtype=BF16), mbd=jnp.asarray(mbd, dtype=BF16), mdiag=jnp.asarray(mdiag))


def _block_diag_heads(w):
    eye = jnp.eye(LRU_HEADS, dtype=w.dtype)
    return jnp.einsum("ldhij,hg->ldhigj", w, eye).reshape(DEPTH, 2, D_LRU, D_LRU)


def kernel(x_prompt, x_sample, state_lru, state_ssm, c, c_ctx, norm1_w, norm2_w, final_norm_w, ada_w, ada_b,
           w_in, w_out, lru_conv_w, lru_conv_b, lru_wa, lru_ba, lru_wi, lru_bi, lru_lambda, hy_conv_w, hy_conv_b,
           hy_w1, hy_b1, hy_w2, hy_b2, hy_freq, hy_w3, hy_bias, ssm_conv_w, ssm_conv_b, ssm_dt_bias, ssm_a_log,
           ssm_d, ssm_norm_w, ffn_w_gate, ffn_w_up, ffn_w_down, moe_router, moe_w_gate, moe_w_up, moe_w_down):
    hp = SSM_H * SSM_P
    wa, wi = _block_diag_heads(lru_wa), _block_diag_heads(lru_wi)
    row = lambda a: a.reshape(DEPTH, 1, -1)
    p = {
        "lru_conv_w": lru_conv_w, "lru_conv_b": row(lru_conv_b), "lru_lambda": lru_lambda,
        "lru_wbig": jnp.concatenate([wa[:, 0], wi[:, 0], wa[:, 1], wi[:, 1]], axis=-1).astype(BF16),
        "lru_bias": jnp.concatenate([lru_ba[:, 0], lru_bi[:, 0], lru_ba[:, 1], lru_bi[:, 1]], axis=-1)[:, None],
        "hy_conv_w": hy_conv_w, "hy_conv_b": row(hy_conv_b), "hy_bias": row(hy_bias),
        "hy_w1t": jnp.swapaxes(jnp.pad(hy_w1, ((0, 0), (0, LANE - HY_POS_DIM), (0, 0))), 1, 2),
        "hy_b1": hy_b1[:, :, None], "hy_w2t": jnp.swapaxes(hy_w2, 1, 2), "hy_b2": hy_b2[:, :, None],
        "hy_freq": hy_freq[:, :, None], "hy_w3t": jnp.swapaxes(hy_w3, 1, 2),
        "ssm_conv_w": ssm_conv_w, "ssm_conv_b": row(ssm_conv_b),
        "ssm_dtb_row": row(ssm_dt_bias), "ssm_dtb_col": ssm_dt_bias.reshape(DEPTH, 2 * SSM_H, 1),
        "ssm_alog_row": row(ssm_a_log), "ssm_alog_col": ssm_a_log.reshape(DEPTH, 2 * SSM_H, 1),
        "ssm_d_exp": jnp.repeat(ssm_d, SSM_P, axis=-1)[:, None], "ssm_norm_w": row(ssm_norm_w),
    }
    w_dtT = jnp.swapaxes(w_in[:, :, D_MAIN:], 1, 2)
    ffn_w = (ffn_w_gate, ffn_w_up, ffn_w_down)
    moe_w = (moe_w_gate, moe_w_up, moe_w_down)

    cst = _ssd_constants()
    cst["lru_rep"] = jnp.asarray(np.arange(RB)[:, None] // SUBLANES == np.arange(RB // SUBLANES)[None, :], dtype=BF16)
    z_ctx, win_ctx = _hy_pos_features(L_CTX)
    z_lat, win_lat = _hy_pos_features(L_LAT)
    cst["f_ctx"], cst["g_ctx"] = _dft_mats(L_CTX, N_FFT_CTX, N_FFT_CTX // 2)
    cst["f_lat"], cst["g_lat"] = _dft_mats(L_LAT, N_FFT_LAT, HY_KB_LAT)
    cst["hf_ctx"] = _hy_filter_spectrum(L_CTX, N_FFT_CTX // 2, z_ctx.T, win_ctx.T, cst["f_ctx"], p)
    cst["hf_lat"] = _hy_filter_spectrum(L_LAT, HY_KB_LAT, z_lat.T, win_lat.T, cst["f_lat"], p)

    cond = jnp.concatenate([c_ctx[None], c], axis=0)
    mod = _mod_table(jnp.broadcast_to(cond[:, :, None], (3, D, LANE)), ada_w, ada_b)
    mod = mod[:, :3].reshape(DEPTH, 3, 6, D)

    x = jnp.concatenate([x_prompt.reshape(T_CTX, D),
                         (x_sample + _grid_pos_embed(L_LAT)[None]).reshape(T_LAT, D)], axis=0)
    st_ssm_in = state_ssm.reshape(N_LAT_SEQ, DEPTH, 2, hp, SSM_N)

    lru_states = []
    new_ssm = jnp.zeros((N_CTX_SEQ, DEPTH, 2, SSM_H, SSM_P, SSM_N), F32)
    for l in range(DEPTH):
        u_lru, u_hy, u_z, u_xbc, u_dt, u_dtT = _k1(l, x, mod[l], norm1_w, w_in, w_dtT)
        o_lru, s_lru = _lru_mixer(l, u_lru, p, cst, state_lru)
        o_hy = _hy_mixer(l, u_hy, p, cst)
        o_ssm, new_ssm = _ssd_mixer(l, u_z, u_xbc, u_dt, u_dtT, p, cst, st_ssm_in, new_ssm)
        lru_states.append(s_lru.reshape(N_CTX_SEQ, 2, D_LRU))
        j = l // 2
        if l % 2 == 0:
            x, h2 = _k2(l, o_lru, o_hy, o_ssm, x, mod[l], norm2_w, w_out)
            x = _dense_ffn(j, h2, x, mod[l], *ffn_w)
        else:
            x, h2, route, counts = _k2(l, o_lru, o_hy, o_ssm, x, mod[l], norm2_w, w_out, moe_router, j)
            plan = _moe_plan(route, counts)
            xs = _sort_tokens(h2, plan["dl_row"], plan["gch"], plan["nused"], plan["pad0"], plan["npad"])
            y = _expert_ffn(j, xs, plan["blk_e"], plan["blk_i"], plan["n_active"], *moe_w)
            x = _combine(y, plan["info"], x, mod[l], final_norm_w, plan["gch"], plan["nused"],
                         final=(l == DEPTH - 1))
    y_prompt = x[0].reshape(N_CTX_SEQ, L_CTX, D)
    y_sample = x[1].reshape(N_LAT_SEQ, L_LAT, D)
    return (y_prompt, y_sample, jnp.stack(lru_states, axis=1), new_ssm)
```

```python
import functools
import math

import numpy as np
import jax
import jax.numpy as jnp
from jax import lax
from jax.experimental import pallas as pl
from jax.experimental.pallas import tpu as pltpu

F32 = jnp.float32
BF16 = jnp.bfloat16
I32 = jnp.int32
HI = lax.Precision.HIGHEST

D = 1024
N_CTX_SEQ, L_CTX = 16, 256
N_LAT_SEQ, L_LAT = 2, 2048
DEPTH = 4
GRID_W = 64
D_LRU = 256
LRU_HEADS, LRU_HD = 4, 64
LRU_C = 8.0
D_HY = 256
HY_BANDS = 16
HY_POS_DIM = 1 + 2 * HY_BANDS
HY_HID = 64
HY_MAX_DECAY = math.log(1e-2) / 0.3
HY_MIN_DECAY = math.log(1e-2) / 1.5
D_SSM = 512
SSM_P = 64
SSM_H = 8
SSM_G = 2
SSM_N = 64
SSM_Q = 128
D_XBC = D_SSM + 2 * SSM_G * SSM_N
D_MAIN = 2 * D_LRU + 3 * D_HY + D_SSM + D_XBC
D_IN = D_MAIN + 2 * SSM_H
D_FF = 2816
N_EXP = 8
EPS = 1e-6

LANE = 128
SUBLANES = 8
BF16_ROWS = 16
T_CTX = N_CTX_SEQ * L_CTX
T_LAT = N_LAT_SEQ * L_LAT
T = T_CTX + T_LAT
TM = 512
NT = T // TM
NT_CTX = T_CTX // TM
NT_PER_LAT = L_LAT // TM
RB = 2048
NB = T // RB
NB_CTX = T_CTX // RB
FF_CHUNK = 256
N_FF_CHUNK = D_FF // FF_CHUNK
FFN_STAGES = 3
VMEM_LIMIT = 56 * 1024 * 1024

GCH = BF16_ROWS
LCAP = 2 * TM + N_EXP * GCH
NLC = LCAP // GCH
TMB = 256
RMAX = -(-(2 * T + NT * N_EXP * (GCH - 1) + N_EXP * (TMB - 1)) // TMB) * TMB
NBLK = RMAX // TMB

HY_KB_LAT = 512
N_FFT_LAT = 3 * L_LAT // 2
N_FFT_CTX = 3 * L_CTX // 2
HY_NKB = N_FFT_LAT // 2 // HY_KB_LAT


def _cparams(n_axes=1, vmem=VMEM_LIMIT):
    return pltpu.CompilerParams(dimension_semantics=("arbitrary",) * n_axes, vmem_limit_bytes=vmem)


def _mod_row(i):
    return jnp.where(i < NT_CTX, 0, 1 + (i - NT_CTX) // NT_PER_LAT)


def _bdot(a, b):
    return jnp.dot(a.astype(BF16), b.astype(BF16), preferred_element_type=F32)


def _rms(x):
    return x * lax.rsqrt(jnp.mean(x * x, axis=-1, keepdims=True) + EPS)


def _split_bf16(v, parts):
    out = []
    for _ in range(parts):
        piece = v.astype(BF16)
        out.append(piece)
        v = v - piece.astype(F32)
    return out


def _mod_kernel(cb_ref, w_ref, b_ref, o_ref):
    tn = w_ref.shape[2]

    def body(kc, accs):
        k0 = pl.multiple_of(kc * 8, 8)
        wk = w_ref[0, pl.ds(k0, 8), :]
        out = []
        for r in range(3):
            c = cb_ref[r, pl.ds(k0, 8), :]
            c = c * jax.nn.sigmoid(c)
            out.append(accs[r] + jnp.tile(c, (1, tn // LANE)) * wk)
        return tuple(out)

    accs = lax.fori_loop(0, D // 8, body, tuple(jnp.zeros((8, tn), F32) for _ in range(3)), unroll=8)
    rows = [jnp.sum(a, axis=0, keepdims=True) + b_ref[0] for a in accs]
    o_ref[0] = jnp.concatenate(rows + [jnp.zeros((5, tn), F32)], axis=0)


def _mod_table(cond_b, ada_w, ada_b):
    tn = 1024
    return pl.pallas_call(
        _mod_kernel,
        grid=(DEPTH, 6 * D // tn),
        in_specs=[
            pl.BlockSpec((3, D, LANE), lambda l, j: (0, 0, 0)),
            pl.BlockSpec((1, D, tn), lambda l, j: (l, 0, j)),
            pl.BlockSpec((1, 1, tn), lambda l, j: (l, 0, j)),
        ],
        out_specs=pl.BlockSpec((1, 8, tn), lambda l, j: (l, 0, j)),
        out_shape=jax.ShapeDtypeStruct((DEPTH, 8, 6 * D), F32),
        compiler_params=_cparams(2),
        name="mod_table",
    )(cond_b, ada_w, ada_b.reshape(DEPTH, 1, 6 * D))


def _k1_kernel(x_ref, mod_ref, nw_ref, w_hbm, wdtT_ref,
               o_lru, o_hy, o_z, o_xbc, o_dt, o_dtT, wbf_ref, stage_ref, sem, *, l):
    @pl.when(pl.program_id(0) == 0)
    def _():
        cp = pltpu.make_async_copy(w_hbm.at[l], stage_ref, sem.at[0])
        cp.start()
        cp.wait()
        wbf_ref[...] = stage_ref[...].astype(BF16)

    m = mod_ref[0]
    h = _rms(x_ref[...]) * nw_ref[0]
    h = h * (1.0 + m[1:2]) + m[0:1]
    hb = h.astype(BF16)

    def proj(lo, hi):
        return jnp.dot(hb, wbf_ref[:, lo:hi], preferred_element_type=F32)

    o_lru[...] = proj(0, 512)
    o_hy[...] = proj(512, 1280)
    o_z[...] = proj(1280, 1792)
    o_xbc[...] = proj(1792, 2560)
    o_dt[...] = proj(D_MAIN, D_IN)
    dtT = lax.dot_general(wdtT_ref[0].astype(BF16), hb, (((1,), (1,)), ((), ())),
                          preferred_element_type=F32)
    for j in range(TM // SSM_Q):
        o_dtT[j] = dtT[:, j * SSM_Q:(j + 1) * SSM_Q]


def _k1(l, x, mod_l, norm1_w, w_in, w_dtT):
    tok = lambda w: pl.BlockSpec((TM, w), lambda i: (i, 0))
    return pl.pallas_call(
        functools.partial(_k1_kernel, l=l),
        grid=(NT,),
        in_specs=[
            tok(D),
            pl.BlockSpec((1, 6, D), lambda i: (_mod_row(i), 0, 0)),
            pl.BlockSpec((1, 1, D), lambda i: (l, 0, 0)),
            pl.BlockSpec(memory_space=pl.ANY),
            pl.BlockSpec((1, 2 * SSM_H, D), lambda i: (l, 0, 0)),
        ],
        out_specs=[tok(512), tok(768), tok(512), tok(768), tok(2 * SSM_H),
                   pl.BlockSpec((TM // SSM_Q, 2 * SSM_H, SSM_Q), lambda i: (i, 0, 0))],
        out_shape=[jax.ShapeDtypeStruct((T, 512), F32), jax.ShapeDtypeStruct((T, 768), F32),
                   jax.ShapeDtypeStruct((T, 512), F32), jax.ShapeDtypeStruct((T, 768), F32),
                   jax.ShapeDtypeStruct((T, 2 * SSM_H), F32),
                   jax.ShapeDtypeStruct((T // SSM_Q, 2 * SSM_H, SSM_Q), F32)],
        scratch_shapes=[pltpu.VMEM((D, D_IN), BF16), pltpu.VMEM((D, D_IN), F32),
                        pltpu.SemaphoreType.DMA((1,))],
        compiler_params=_cparams(1),
        name=f"k1_inproj_{l}",
    )(x, mod_l, norm1_w.reshape(DEPTH, 1, D), w_in, w_dtT)


def _row_in_seq(rows, lseq):
    return lax.broadcasted_iota(I32, (rows, 1), 0) & (lseq - 1)


def _shift_rows(x, s, rin, lseq):
    if s == 0:
        return x
    y = pltpu.roll(x, s % x.shape[0], axis=0)
    valid = (rin >= s) if s > 0 else (rin < lseq + s)
    return jnp.where(valid, y, 0.0)


def _dwconv(x, w_ref, b_ref, rin, lseq):
    k_w = w_ref.shape[0]
    y = b_ref[...]
    for k in range(k_w):
        y = y + w_ref[k:k + 1, :] * _shift_rows(x, k_w // 2 - k, rin, lseq)
    return y


def _lru_scan(a, b, d, rin, lseq, rep_ref, ab_s):
    rows = a.shape[0]
    ngrp, gps = rows // SUBLANES, lseq // SUBLANES
    sub = rin & (SUBLANES - 1)
    for s in (1, 2, 4):
        sh = (s if d == 0 else -s) % rows
        valid = (sub >= s) if d == 0 else (sub < SUBLANES - s)
        b = b + jnp.where(valid, a * pltpu.roll(b, sh, axis=0), 0.0)
        a = jnp.where(valid, a * pltpu.roll(a, sh, axis=0), a)
    edge = SUBLANES - 1 if d == 0 else 0

    def group_edges(k, v):
        for j in range(D_LRU // LANE):
            ab_s[k, j] = v[:, j * LANE:(j + 1) * LANE]
        return jnp.concatenate([ab_s[k, j, pl.ds(edge, ngrp, stride=SUBLANES), :]
                                for j in range(D_LRU // LANE)], axis=1)

    ga, gb = group_edges(0, a), group_edges(1, b)
    gin = lax.broadcasted_iota(I32, (ngrp, 1), 0) & (gps - 1)
    s = 1
    while s < gps:
        sh = (s if d == 0 else -s) % ngrp
        valid = (gin >= s) if d == 0 else (gin < gps - s)
        gb = gb + jnp.where(valid, ga * pltpu.roll(gb, sh, axis=0), 0.0)
        if 2 * s < gps:
            ga = jnp.where(valid, ga * pltpu.roll(ga, sh, axis=0), ga)
        s *= 2
    valid = (gin >= 1) if d == 0 else (gin < gps - 1)
    cin = jnp.where(valid, pltpu.roll(gb, (1 if d == 0 else -1) % ngrp, axis=0), 0.0)
    hi, lo = _split_bf16(cin, 2)
    cin_x = (jnp.dot(rep_ref[...], hi, preferred_element_type=F32)
             + jnp.dot(rep_ref[...], lo, preferred_element_type=F32))
    return a * cin_x + b


def _lru_block(u_ref, cw_ref, cb_ref, wbig_ref, bias_ref, lam_ref, rep_ref, h0_ref, o_ref, st_ref, ab_s, lseq):
    rows = u_ref.shape[0]
    rin = _row_in_seq(rows, lseq)
    u = u_ref[...]
    gate = u[:, D_LRU:]
    x = _dwconv(u[:, :D_LRU], cw_ref, cb_ref, rin, lseq)
    xb = x.astype(BF16)
    y = None
    finals = []
    for d in range(2):
        g = jnp.dot(xb, wbig_ref[:, 512 * d:512 * (d + 1)], preferred_element_type=F32)
        g = g + bias_ref[:, 512 * d:512 * (d + 1)]
        r = jax.nn.sigmoid(g[:, :D_LRU])
        ig = jax.nn.sigmoid(g[:, D_LRU:])
        log_a = -LRU_C * r * jax.nn.softplus(-lam_ref[d:d + 1, :])
        a = jnp.exp(log_a)
        th = jnp.tanh(log_a)
        b = jnp.sqrt(-2.0 * th / (1.0 - th)) * (ig * x)
        if h0_ref is not None:
            edge = (rin == 0) if d == 0 else (rin == lseq - 1)
            b = b + jnp.where(edge, a * h0_ref[d:d + 1, :], 0.0)
        b = _lru_scan(a, b, d, rin, lseq, rep_ref, ab_s)
        y = b if y is None else y + b
        if st_ref is not None:
            last = lseq - 1 if d == 0 else 0
            finals.append(jnp.concatenate(
                [b[j * lseq + last:j * lseq + last + 1, :] for j in range(rows // lseq)], axis=0))
    o_ref[...] = y * jax.nn.gelu(gate)
    if st_ref is not None:
        st_ref[...] = jnp.concatenate(finals, axis=1)


def _lru_kernel(u_ref, cw_ref, cb_ref, wbig_ref, bias_ref, lam_ref, rep_ref, h0_ref, o_ref, st_ref, ab_s):
    b = pl.program_id(0)
    args = (u_ref, cw_ref.at[0], cb_ref.at[0], wbig_ref.at[0], bias_ref.at[0], lam_ref.at[0], rep_ref)

    @pl.when(b < NB_CTX)
    def _():
        _lru_block(*args, None, o_ref, st_ref, ab_s, L_CTX)

    @pl.when(b >= NB_CTX)
    def _():
        _lru_block(*args, h0_ref.at[0, 0], o_ref, None, ab_s, L_LAT)


def _lru_mixer(l, u_lru, p, c, state_lru):
    lsel = lambda *shape: pl.BlockSpec((1,) + shape, lambda b: (l,) + (0,) * len(shape))
    return pl.pallas_call(
        _lru_kernel,
        grid=(NB,),
        in_specs=[
            pl.BlockSpec((RB, 512), lambda b: (b, 0)),
            lsel(4, D_LRU), lsel(1, D_LRU), lsel(D_LRU, 1024), lsel(1, 1024), lsel(2, D_LRU),
            pl.BlockSpec((RB, RB // SUBLANES), lambda b: (0, 0), pipeline_mode=pl.Buffered(1)),
            pl.BlockSpec((1, 1, 2, D_LRU), lambda b: (jnp.maximum(b - NB_CTX, 0), l, 0, 0)),
        ],
        out_specs=[pl.BlockSpec((RB, D_LRU), lambda b: (b, 0)),
                   pl.BlockSpec((RB // L_CTX, 2 * D_LRU), lambda b: (jnp.minimum(b, NB_CTX - 1), 0))],
        out_shape=[jax.ShapeDtypeStruct((T, D_LRU), F32),
                   jax.ShapeDtypeStruct((N_CTX_SEQ, 2 * D_LRU), F32)],
        scratch_shapes=[pltpu.VMEM((2, D_LRU // LANE, RB, LANE), F32)],
        compiler_params=_cparams(1),
        name=f"lru_mixer_{l}",
    )(u_lru, p["lru_conv_w"], p["lru_conv_b"], p["lru_wbig"], p["lru_bias"], p["lru_lambda"],
      c["lru_rep"], state_lru)


def _hy_filter_kernel(z_ref, win_ref, f_ref, w1_ref, b1_ref, w2_ref, b2_ref, fr_ref, w3_ref,
                      o_ref, h_ref):
    l = pl.program_id(1)

    @pl.when(pl.program_id(0) == 0)
    def _():
        fr = fr_ref[0]
        g = jnp.sin(fr * (jnp.dot(w1_ref[0], z_ref[...], precision=HI, preferred_element_type=F32)
                          + b1_ref[0]))
        g = jnp.sin(fr * (jnp.dot(w2_ref[0], g, precision=HI, preferred_element_type=F32) + b2_ref[0]))
        h = jnp.dot(w3_ref[0], g, precision=HI, preferred_element_type=F32)
        h_ref[l] = (h * win_ref[...]).astype(BF16)

    o_ref[0] = lax.dot_general(f_ref[...], h_ref[l], (((1,), (1,)), ((), ())), preferred_element_type=F32)


def _hy_filter_spectrum(lseq, kb, zfeat_t, window_t, fmat, p):
    n = fmat.shape[0]
    lsel = lambda *shape: pl.BlockSpec((1,) + shape, lambda k, l: (l,) + (0,) * len(shape))
    return pl.pallas_call(
        _hy_filter_kernel,
        grid=(n // (2 * kb), DEPTH),
        in_specs=[
            pl.BlockSpec((LANE, lseq), lambda k, l: (0, 0)),
            pl.BlockSpec((D_HY, lseq), lambda k, l: (0, 0)),
            pl.BlockSpec((2 * kb, lseq), lambda k, l: (k, 0)),
            lsel(HY_HID, LANE), lsel(HY_HID, 1), lsel(HY_HID, HY_HID), lsel(HY_HID, 1),
            lsel(HY_HID, 1), lsel(D_HY, HY_HID),
        ],
        out_specs=pl.BlockSpec((1, 2 * kb, D_HY), lambda k, l: (l, k, 0)),
        out_shape=jax.ShapeDtypeStruct((DEPTH, n, D_HY), F32),
        scratch_shapes=[pltpu.VMEM((DEPTH, D_HY, lseq), BF16)],
        compiler_params=_cparams(2),
        name=f"hyena_filter_{lseq}",
    )(zfeat_t, window_t, fmat, p["hy_w1t"], p["hy_b1"], p["hy_w2t"], p["hy_b2"], p["hy_freq"], p["hy_w3t"])


def _hy_spectral_block(f_blk, g_blk, hf, z_bf, is_dc_block):
    kb = f_blk.shape[0] // 2
    zf = jnp.dot(f_blk, z_bf, preferred_element_type=F32)
    rz, iz = zf[:kb], zf[kb:]
    rh, ih = hf[:kb], hf[kb:]
    ii = iz * ih
    re = rz * rh - ii
    im = rz * ih + iz * rh
    if is_dc_block is not None:
        dc = jnp.logical_and(lax.broadcasted_iota(I32, (kb, 1), 0) == 0, is_dc_block)
        re = jnp.where(dc, rz * rh, re)
        im = jnp.where(dc, ii, im)
    pr = jnp.concatenate([re, im], axis=0).astype(BF16)
    return jnp.dot(g_blk, pr, preferred_element_type=F32)


def _hy_prologue(u_ref, cw_ref, cb_ref, lseq, z_ref, zbf_ref, x2_ref):
    rin = _row_in_seq(u_ref.shape[0], lseq)
    uc = _dwconv(u_ref[...], cw_ref, cb_ref, rin, lseq)
    z = uc[:, :D_HY] * uc[:, D_HY:2 * D_HY]
    z_ref[...] = z
    zbf_ref[...] = z.astype(BF16)
    x2_ref[...] = uc[:, 2 * D_HY:]


def _hy_kernel(u_ref, cw_ref, cb_ref, hb_ref, fc_ref, gc_ref, hfc_ref, fl_ref, gl_ref, hfl_ref,
               o_ref, z_ref, zbf_ref, x2_ref, acc_ref):
    b = pl.program_id(0)
    k = pl.program_id(1)
    cw, cb = cw_ref.at[0], cb_ref.at[0]

    @pl.when(jnp.logical_and(b < NB_CTX, k == 0))
    def _():
        _hy_prologue(u_ref, cw, cb, L_CTX, z_ref, zbf_ref, x2_ref)
        for s in range(RB // L_CTX):
            rows = slice(s * L_CTX, (s + 1) * L_CTX)
            acc_ref[rows, :] = _hy_spectral_block(fc_ref[...], gc_ref[...], hfc_ref[0], zbf_ref[rows, :], True)

    @pl.when(b >= NB_CTX)
    def _():
        @pl.when(k == 0)
        def _():
            _hy_prologue(u_ref, cw, cb, L_LAT, z_ref, zbf_ref, x2_ref)
            acc_ref[...] = jnp.zeros_like(acc_ref)

        acc_ref[...] += _hy_spectral_block(fl_ref[...], gl_ref[...], hfl_ref[0], zbf_ref[...], k == 0)

    @pl.when(k == HY_NKB - 1)
    def _():
        o_ref[...] = x2_ref[...] * (acc_ref[...] + hb_ref[0] * z_ref[...])


def _hy_mixer(l, u_hy, p, c):
    lat_k = lambda b, k: jnp.where(b < NB_CTX, 0, k)
    lsel = lambda *shape: pl.BlockSpec((1,) + shape, lambda b, k: (l,) + (0,) * len(shape))
    kbl = 2 * HY_KB_LAT
    return pl.pallas_call(
        _hy_kernel,
        grid=(NB, HY_NKB),
        in_specs=[
            pl.BlockSpec((RB, 3 * D_HY), lambda b, k: (b, 0)),
            lsel(3, 3 * D_HY), lsel(1, 3 * D_HY), lsel(1, D_HY),
            pl.BlockSpec((N_FFT_CTX, L_CTX), lambda b, k: (0, 0)),
            pl.BlockSpec((L_CTX, N_FFT_CTX), lambda b, k: (0, 0)),
            lsel(N_FFT_CTX, D_HY),
            pl.BlockSpec((kbl, L_LAT), lambda b, k: (lat_k(b, k), 0)),
            pl.BlockSpec((L_LAT, kbl), lambda b, k: (0, lat_k(b, k))),
            pl.BlockSpec((1, kbl, D_HY), lambda b, k: (l, lat_k(b, k), 0)),
        ],
        out_specs=pl.BlockSpec((RB, D_HY), lambda b, k: (b, 0)),
        out_shape=jax.ShapeDtypeStruct((T, D_HY), F32),
        scratch_shapes=[pltpu.VMEM((RB, D_HY), F32), pltpu.VMEM((RB, D_HY), BF16),
                        pltpu.VMEM((RB, D_HY), F32), pltpu.VMEM((RB, D_HY), F32)],
        compiler_params=_cparams(2),
        name=f"hyena_mixer_{l}",
    )(u_hy, p["hy_conv_w"], p["hy_conv_b"], p["hy_bias"],
      c["f_ctx"], c["g_ctx"], c["hf_ctx"], c["f_lat"], c["g_lat"], c["hf_lat"])


def _ssd_block(u_z, u_xbc, u_dt, u_dtT, cw, cb, dtb_row, dtb_col, alog_row, alog_col, d_exp, nw,
               e64_ref, mbd_ref, mdiag_ref, h0_ref, o_ref, st_ref,
               x_s, bc_s, yb_s, sf_s, sb_s, cs_s, col_s, row_s, lseq):
    y_acc = (o_ref, yb_s)
    s_dir = (sf_s, sb_s)
    rows = u_z.shape[0]
    nchunk = rows // SSM_Q
    cps = lseq // SSM_Q
    rin = _row_in_seq(rows, lseq)
    for c0 in range(0, D_XBC, LANE):
        cols = slice(c0, c0 + LANE)
        xbc = _dwconv(u_xbc[:, cols], cw.at[:, cols], cb.at[:, cols], rin, lseq)
        xbc = xbc * jax.nn.sigmoid(xbc)
        if c0 < D_SSM:
            x_s[:, cols] = xbc
        else:
            bc_s[:, c0 - D_SSM:c0 - D_SSM + LANE] = xbc

    li = lax.broadcasted_iota(I32, (SSM_Q, SSM_Q), 0)
    si = lax.broadcasted_iota(I32, (SSM_Q, SSM_Q), 1)
    low_half = lax.broadcasted_iota(I32, (SSM_Q, LANE), 1) < SSM_N
    hq = SSM_H // SSM_G
    wq = hq * SSM_P
    hp = SSM_H * SSM_P

    for d in range(2):
        for g in range(SSM_G):
            if h0_ref is not None:
                s_dir[d][g] = jnp.tile(h0_ref[d, g * wq:(g + 1) * wq, :], (1, hq))
            else:
                s_dir[d][g] = jnp.zeros((wq, wq), F32)

    nh2 = 2 * SSM_H
    tri_f = (li >= si).astype(BF16)
    tri_b = (li <= si).astype(BF16)
    fwd_lane = lax.broadcasted_iota(I32, (1, nh2), 1) < SSM_H
    fwd_sub = lax.broadcasted_iota(I32, (nh2, 1), 0) < SSM_H
    a_row = -jnp.exp(alog_row[...])
    a_col = -jnp.exp(alog_col[...])
    def chunk_stats(c, carry):
        rs = pl.ds(pl.multiple_of(c * SSM_Q, SSM_Q), SSM_Q)
        dt_c = jax.nn.softplus(u_dt[rs, :] + dtb_row[...])
        hi, lo = _split_bf16(dt_c * a_row, 2)
        cs_f = jnp.dot(tri_f, hi, preferred_element_type=F32) + jnp.dot(tri_f, lo, preferred_element_type=F32)
        cs_b = jnp.dot(tri_b, hi, preferred_element_type=F32) + jnp.dot(tri_b, lo, preferred_element_type=F32)
        cs_col = jnp.where(fwd_lane, cs_f, cs_b)
        tot = jnp.where(fwd_lane, cs_f[SSM_Q - 1:SSM_Q, :], cs_b[0:1, :])
        cs_s[rs, :] = cs_col
        for k, v in enumerate((jnp.exp(cs_col), dt_c * jnp.exp(tot - cs_col))):
            for m, piece in enumerate(_split_bf16(v, 2)):
                col_s[rs, (2 * k + m) * nh2:(2 * k + m + 1) * nh2] = piece
        dt_r = jax.nn.softplus(u_dtT[c] + dtb_col[...])
        hi, lo = _split_bf16(dt_r * a_col, 2)
        csr_f = jnp.dot(hi, tri_b, preferred_element_type=F32) + jnp.dot(lo, tri_b, preferred_element_type=F32)
        csr_b = jnp.dot(hi, tri_f, preferred_element_type=F32) + jnp.dot(lo, tri_f, preferred_element_type=F32)
        row_s[c, :nh2, :] = jnp.where(fwd_sub, csr_f, csr_b)
        row_s[c, nh2:, :] = dt_r
        return carry

    lax.fori_loop(0, nchunk, chunk_stats, 0, unroll=8)

    def chunk_step(ci, carry, d):
        causal = (li >= si) if d == 0 else (li <= si)
        edge = SSM_Q - 1 if d == 0 else 0
        c = ci if d == 0 else nchunk - 1 - ci
        rsl = pl.ds(pl.multiple_of(c * SSM_Q, SSM_Q), SSM_Q)
        if h0_ref is None and cps < nchunk:
            first = (c % cps == 0) if d == 0 else (c % cps == cps - 1)
            s_dir[d][...] = s_dir[d][...] * jnp.where(first, 0.0, 1.0)

        cs_col = cs_s[rsl, :]
        rows_c = row_s[c]
        spread = jnp.dot(col_s[rsl, :], e64_ref[d], preferred_element_type=F32)
        ecs_x, wdec_x = spread[:, :hp], spread[:, hp:]
        etot_x = ecs_x[edge:edge + 1, :]

        bcm = bc_s[rsl, :]
        bm, cm = bcm[:, :LANE], bcm[:, LANE:]
        bm_r, cm_r = pltpu.roll(bm, SSM_N, axis=1), pltpu.roll(cm, SSM_N, axis=1)
        bmb, cmb = bm.astype(BF16), cm.astype(BF16)
        for g in range(SSM_G):
            gl = slice(g * SSM_N, (g + 1) * SSM_N)
            ql = slice(g * wq, (g + 1) * wq)
            same = low_half if g == 0 else jnp.logical_not(low_half)
            b2 = jnp.where(same, bm, bm_r)
            c2 = jnp.where(same, cm, cm_r)
            gmat = lax.dot_general(cmb[:, gl], bmb[:, gl], (((1,), (1,)), ((), ())),
                                   preferred_element_type=F32)
            sc = []
            for h in range(SSM_H * d + g * hq, SSM_H * d + (g + 1) * hq):
                diff = cs_col[:, h:h + 1] - rows_c[h:h + 1, :]
                decay = jnp.exp(jnp.where(causal, diff, -1e30))
                sc.append((gmat * decay * rows_c[nh2 + h:nh2 + h + 1, :]).astype(BF16))
            sc = jnp.concatenate(sc, axis=1)
            xq = x_s[rsl, ql]
            bd = jnp.tile(xq.astype(BF16), (hq, 1)) * mbd_ref[...]
            y = jnp.dot(sc, bd, preferred_element_type=F32)
            s_old = s_dir[d][g]
            y_off = lax.dot_general(jnp.concatenate([c2, c2], axis=1).astype(BF16),
                                    (s_old * mdiag_ref[...]).astype(BF16),
                                    (((1,), (1,)), ((), ())), preferred_element_type=F32)
            y_acc[d][rsl, ql] = y + y_off * ecs_x[:, ql]
            bx = (jnp.concatenate([b2, b2], axis=1) * wdec_x[:, ql]).astype(BF16)
            s_new = jnp.dot(xq.T.astype(BF16), bx, preferred_element_type=F32)
            s_dir[d][g] = s_old * etot_x[:, ql] + s_new

        if st_ref is not None:
            last = (c % cps == cps - 1) if d == 0 else (c % cps == 0)

            @pl.when(last)
            def _():
                for g in range(SSM_G):
                    for hl in range(hq):
                        blk = slice(hl * SSM_P, (hl + 1) * SSM_P)
                        st_ref[c // cps, 0, d, g * hq + hl] = s_dir[d][g, blk, blk]
        return carry

    for d in range(2):
        lax.fori_loop(0, nchunk, functools.partial(chunk_step, d=d), 0, unroll=2)

    for r0 in range(0, rows, 256):
        rsl = slice(r0, r0 + 256)
        y = o_ref[rsl, :] + yb_s[rsl, :] + d_exp[...] * x_s[rsl, :]
        z = u_z[rsl, :]
        y = y * (z * jax.nn.sigmoid(z))
        o_ref[rsl, :] = _rms(y) * nw[...]


def _ssd_kernel(u_z, u_xbc, u_dt, u_dtT, cw, cb, dtb_row, dtb_col, alog_row, alog_col, d_exp, nw,
                e64_ref, mbd_ref, mdiag_ref, h0_ref, st_in_ref, o_ref, st_ref,
                x_s, bc_s, yb_s, sf_s, sb_s, cs_s, col_s, row_s):
    del st_in_ref
    b = pl.program_id(0)
    args = (u_z, u_xbc, u_dt, u_dtT, cw.at[0], cb.at[0], dtb_row.at[0], dtb_col.at[0], alog_row.at[0],
            alog_col.at[0], d_exp.at[0], nw.at[0], e64_ref, mbd_ref, mdiag_ref)
    scr = (x_s, bc_s, yb_s, sf_s, sb_s, cs_s, col_s, row_s)

    @pl.when(b < NB_CTX)
    def _():
        _ssd_block(*args, None, o_ref, st_ref, *scr, L_CTX)

    @pl.when(b >= NB_CTX)
    def _():
        _ssd_block(*args, h0_ref.at[0, 0], o_ref, None, *scr, L_LAT)


def _ssd_mixer(l, u_z, u_xbc, u_dt, u_dtT, p, c, state_ssm, new_states):
    lsel = lambda *shape: pl.BlockSpec((1,) + shape, lambda b: (l,) + (0,) * len(shape))
    full = lambda a: pl.BlockSpec(a.shape, lambda b: (0,) * a.ndim, pipeline_mode=pl.Buffered(1))
    hp = SSM_H * SSM_P
    nseq_blk = RB // L_CTX
    return pl.pallas_call(
        _ssd_kernel,
        grid=(NB,),
        in_specs=[
            pl.BlockSpec((RB, D_SSM), lambda b: (b, 0), pipeline_mode=pl.Buffered(1)),
            pl.BlockSpec((RB, D_XBC), lambda b: (b, 0)),
            pl.BlockSpec((RB, 2 * SSM_H), lambda b: (b, 0)),
            pl.BlockSpec((RB // SSM_Q, 2 * SSM_H, SSM_Q), lambda b: (b, 0, 0)),
            lsel(4, D_XBC), lsel(1, D_XBC), lsel(1, 2 * SSM_H), lsel(2 * SSM_H, 1),
            lsel(1, 2 * SSM_H), lsel(2 * SSM_H, 1), lsel(1, D_SSM), lsel(1, D_SSM),
            full(c["e64"]), full(c["mbd"]), full(c["mdiag"]),
            pl.BlockSpec((1, 1, 2, hp, SSM_N), lambda b: (jnp.maximum(b - NB_CTX, 0), l, 0, 0, 0)),
            pl.BlockSpec(memory_space=pl.ANY),
        ],
        out_specs=[pl.BlockSpec((RB, D_SSM), lambda b: (b, 0)),
                   pl.BlockSpec((nseq_blk, 1, 2, SSM_H, SSM_P, SSM_N),
                                lambda b: (jnp.minimum(b, NB_CTX - 1), l, 0, 0, 0, 0))],
        out_shape=[jax.ShapeDtypeStruct((T, D_SSM), F32),
                   jax.ShapeDtypeStruct(new_states.shape, F32)],
        input_output_aliases={16: 1},
        scratch_shapes=[pltpu.VMEM((RB, D_SSM), F32), pltpu.VMEM((RB, 2 * SSM_G * SSM_N), F32),
                        pltpu.VMEM((RB, D_SSM), F32),
                        pltpu.VMEM((SSM_G, hp // SSM_G, hp // SSM_G), F32),
                        pltpu.VMEM((SSM_G, hp // SSM_G, hp // SSM_G), F32),
                        pltpu.VMEM((RB, 2 * SSM_H), F32), pltpu.VMEM((RB, 8 * SSM_H), BF16),
                        pltpu.VMEM((RB // SSM_Q, 4 * SSM_H, SSM_Q), F32)],
        compiler_params=_cparams(1),
        name=f"ssd_mixer_{l}",
    )(u_z, u_xbc, u_dt, u_dtT, p["ssm_conv_w"], p["ssm_conv_b"], p["ssm_dtb_row"], p["ssm_dtb_col"],
      p["ssm_alog_row"], p["ssm_alog_col"], p["ssm_d_exp"], p["ssm_norm_w"],
      c["e64"], c["mbd"], c["mdiag"], state_ssm, new_states)


def _k2_kernel(*refs, routed):
    if routed:
        (ol_ref, oh_ref, os_ref, x_ref, mod_ref, nw_ref, w_ref, rt_ref,
         xo_ref, h2_ref, route_ref, cnt_ref, wbf_ref) = refs
    else:
        ol_ref, oh_ref, os_ref, x_ref, mod_ref, nw_ref, w_ref, xo_ref, h2_ref, wbf_ref = refs

    @pl.when(pl.program_id(0) == 0)
    def _():
        wbf_ref[...] = w_ref[0].astype(BF16)

    m = mod_ref[0]
    o = jnp.dot(ol_ref[...].astype(BF16), wbf_ref[0:256, :], preferred_element_type=F32)
    o = o + jnp.dot(oh_ref[...].astype(BF16), wbf_ref[256:512, :], preferred_element_type=F32)
    o = o + jnp.dot(os_ref[...].astype(BF16), wbf_ref[512:1024, :], preferred_element_type=F32)
    x = x_ref[...] + m[2:3] * o
    xo_ref[...] = x
    h2 = _rms(x) * nw_ref[0]
    h2 = h2 * (1.0 + m[4:5]) + m[3:4]
    h2_hi = h2.astype(BF16)
    h2_ref[...] = h2_hi

    if routed:
        h2_lo = (h2 - h2_hi.astype(F32)).astype(BF16)
        r_hi, r_lo = _split_bf16(rt_ref[0], 2)
        logits = (jnp.dot(h2_hi, r_hi, preferred_element_type=F32)
                  + jnp.dot(h2_lo, r_hi, preferred_element_type=F32)
                  + jnp.dot(h2_hi, r_lo, preferred_element_type=F32))
        eid = lax.broadcasted_iota(I32, logits.shape, 1)
        m1 = jnp.max(logits, axis=1, keepdims=True)
        i1 = jnp.min(jnp.where(logits == m1, eid, N_EXP), axis=1, keepdims=True)
        rest = jnp.where(eid == i1, -jnp.inf, logits)
        m2 = jnp.max(rest, axis=1, keepdims=True)
        i2 = jnp.min(jnp.where(rest == m2, eid, N_EXP), axis=1, keepdims=True)
        w1 = 1.0 / (1.0 + jnp.exp(m2 - m1))
        w2 = 1.0 - w1
        oh1 = (eid == i1).astype(F32)
        oh2 = (eid == i2).astype(F32)
        both = oh1 + oh2
        before = (lax.broadcasted_iota(I32, (TM, TM), 0) > lax.broadcasted_iota(I32, (TM, TM), 1))
        ahead = jnp.dot(before.astype(BF16), both.astype(BF16), preferred_element_type=F32)
        r1 = jnp.sum(oh1 * ahead, axis=1, keepdims=True)
        r2 = jnp.sum(oh2 * ahead, axis=1, keepdims=True)
        zero = jnp.zeros_like(w1)
        route_ref[...] = jnp.concatenate(
            [i1.astype(F32), i2.astype(F32), r1, r2, w1, w2, zero, zero], axis=1)
        cnt_ref[0] = jnp.sum(both, axis=0, keepdims=True)


def _k2(l, o_lru, o_hy, o_ssm, x, mod_l, norm2_w, w_out, router=None, j=0):
    routed = router is not None
    tok = lambda w: pl.BlockSpec((TM, w), lambda i: (i, 0))
    in_specs = [
        tok(D_LRU), tok(D_HY), tok(D_SSM), tok(D),
        pl.BlockSpec((1, 6, D), lambda i: (_mod_row(i), 0, 0)),
        pl.BlockSpec((1, 1, D), lambda i: (l, 0, 0)),
        pl.BlockSpec((1, D, D), lambda i: (l, 0, 0)),
    ]
    args = [o_lru, o_hy, o_ssm, x, mod_l, norm2_w.reshape(DEPTH, 1, D), w_out]
    out_specs = [tok(D), tok(D)]
    out_shape = [jax.ShapeDtypeStruct((T, D), F32), jax.ShapeDtypeStruct((T, D), BF16)]
    if routed:
        in_specs.append(pl.BlockSpec((1, D, N_EXP), lambda i: (j, 0, 0)))
        args.append(router)
        out_specs += [tok(8), pl.BlockSpec((1, 1, N_EXP), lambda i: (i, 0, 0))]
        out_shape += [jax.ShapeDtypeStruct((T, 8), F32), jax.ShapeDtypeStruct((NT, 1, N_EXP), F32)]
    return pl.pallas_call(
        functools.partial(_k2_kernel, routed=routed),
        grid=(NT,),
        in_specs=in_specs,
        out_specs=out_specs,
        out_shape=out_shape,
        scratch_shapes=[pltpu.VMEM((D, D), BF16)],
        compiler_params=_cparams(1),
        name=f"k2_outproj_{l}",
    )(*args)


def _ffn_stream_kernel(be_ref, bi_ref, na_ref, x_ref, *rest, j, dense):
    if dense:
        xres_ref, mod_ref, wg_hbm, wu_hbm, wd_hbm, o_ref, wg_s, wu_s, wd_s, stg_g, stg_u, stg_d, sem = rest
    else:
        wg_hbm, wu_hbm, wd_hbm, o_ref, wg_s, wu_s, wd_s, stg_g, stg_u, stg_d, sem = rest
    del bi_ref
    b = pl.program_id(0)
    n_act = na_ref[0]
    e = be_ref[b]
    active = b < n_act
    load = jnp.logical_and(active, jnp.logical_or(b == 0, be_ref[jnp.maximum(b - 1, 0)] != e))
    e_next = be_ref[jnp.minimum(b + 1, pl.num_programs(0) - 1)]
    feed_next = jnp.logical_and(b + 1 < n_act, e_next != e)

    def copies(ee, c):
        slot = c % FFN_STAGES
        cols = slice(c * FF_CHUNK, (c + 1) * FF_CHUNK)
        return (pltpu.make_async_copy(wg_hbm.at[j, ee, :, cols], stg_g.at[slot], sem.at[0, slot]),
                pltpu.make_async_copy(wu_hbm.at[j, ee, :, cols], stg_u.at[slot], sem.at[1, slot]),
                pltpu.make_async_copy(wd_hbm.at[j, ee, cols, :], stg_d.at[slot], sem.at[2, slot]))

    def start(ee, c):
        for cp in copies(ee, c):
            cp.start()

    def chunk_out(c, x, acc):
        g = jnp.dot(x, wg_s[c], preferred_element_type=F32)
        u = jnp.dot(x, wu_s[c], preferred_element_type=F32)
        hmid = (g * jax.nn.sigmoid(g) * u).astype(BF16)
        part = jnp.dot(hmid, wd_s[c], preferred_element_type=F32)
        return part if acc is None else acc + part

    def finish(acc):
        if dense:
            o_ref[...] = xres_ref[...] + mod_ref[0][5:6] * acc
        else:
            o_ref[...] = acc.astype(o_ref.dtype)

    @pl.when(load)
    def _():
        @pl.when(b == 0)
        def _():
            for c in range(FFN_STAGES):
                start(e, c)

        x = x_ref[...]
        acc = None
        for c in range(N_FF_CHUNK):
            slot = c % FFN_STAGES
            for cp in copies(e, c):
                cp.wait()
            wg_s[c] = stg_g[slot].astype(BF16)
            wu_s[c] = stg_u[slot].astype(BF16)
            wd_s[c] = stg_d[slot].astype(BF16)
            if c + FFN_STAGES < N_FF_CHUNK:
                start(e, c + FFN_STAGES)
            acc = chunk_out(c, x, acc)
        finish(acc)

    @pl.when(jnp.logical_and(active, jnp.logical_not(load)))
    def _():
        x = x_ref[...]
        acc = None
        for c in range(N_FF_CHUNK):
            acc = chunk_out(c, x, acc)
        finish(acc)

    if not dense:
        @pl.when(jnp.logical_not(active))
        def _():
            o_ref[...] = jnp.zeros_like(o_ref)

    @pl.when(feed_next)
    def _():
        for c in range(FFN_STAGES):
            start(e_next, c)


def _ffn_scratch():
    return [pltpu.VMEM((N_FF_CHUNK, D, FF_CHUNK), BF16), pltpu.VMEM((N_FF_CHUNK, D, FF_CHUNK), BF16),
            pltpu.VMEM((N_FF_CHUNK, FF_CHUNK, D), BF16),
            pltpu.VMEM((FFN_STAGES, D, FF_CHUNK), F32), pltpu.VMEM((FFN_STAGES, D, FF_CHUNK), F32),
            pltpu.VMEM((FFN_STAGES, FF_CHUNK, D), F32), pltpu.SemaphoreType.DMA((3, FFN_STAGES))]


def _dense_ffn(j, h2, x, mod_l, wg, wu, wd):
    hbm = pl.BlockSpec(memory_space=pl.ANY)
    zeros = jnp.zeros((NT,), I32)
    return pl.pallas_call(
        functools.partial(_ffn_stream_kernel, j=j, dense=True),
        grid_spec=pltpu.PrefetchScalarGridSpec(
            num_scalar_prefetch=3,
            grid=(NT,),
            in_specs=[
                pl.BlockSpec((TM, D), lambda i, *_: (i, 0)),
                pl.BlockSpec((TM, D), lambda i, *_: (i, 0)),
                pl.BlockSpec((1, 6, D), lambda i, *_: (_mod_row(i), 0, 0)),
                hbm, hbm, hbm,
            ],
            out_specs=pl.BlockSpec((TM, D), lambda i, *_: (i, 0)),
            scratch_shapes=_ffn_scratch(),
        ),
        out_shape=jax.ShapeDtypeStruct((T, D), F32),
        compiler_params=_cparams(1),
        name=f"dense_ffn_{j}",
    )(zeros, zeros, jnp.full((1,), NT, I32), h2, x, mod_l, wg[:, None], wu[:, None], wd[:, None])


def _expert_ffn(j, xs, blk_e, blk_i, n_active, wg, wu, wd):
    hbm = pl.BlockSpec(memory_space=pl.ANY)
    return pl.pallas_call(
        functools.partial(_ffn_stream_kernel, j=j, dense=False),
        grid_spec=pltpu.PrefetchScalarGridSpec(
            num_scalar_prefetch=3,
            grid=(NBLK,),
            in_specs=[pl.BlockSpec((TMB, D), lambda b, be, bi, na: (bi[b], 0)), hbm, hbm, hbm],
            out_specs=pl.BlockSpec((TMB, D), lambda b, be, bi, na: (b, 0)),
            scratch_shapes=_ffn_scratch(),
        ),
        out_shape=jax.ShapeDtypeStruct((RMAX, D), BF16),
        compiler_params=_cparams(1),
        name=f"expert_ffn_{j}",
    )(blk_e, blk_i, n_active, xs, wg, wu, wd)


def _chunk_rows(idx):
    return pl.ds(pl.multiple_of(idx * GCH, GCH), GCH)


def _sort_kernel(gch_ref, nused_ref, pad0_ref, npad_ref, h_ref, dl_ref, xs_ref, xl_ref, zero_ref, sem):
    i = pl.program_id(0)
    slot = i % 2

    def copy(tile, q):
        s = tile % 2
        return pltpu.make_async_copy(xl_ref.at[s, _chunk_rows(q)],
                                     xs_ref.at[_chunk_rows(gch_ref[tile * NLC + q])], sem.at[s])

    def wait_tile(tile):
        lax.fori_loop(0, nused_ref[tile], lambda q, c: (copy(tile, q).wait(), c)[1], 0)

    @pl.when(i >= 2)
    def _():
        wait_tile(i - 2)

    dl = dl_ref[0]
    r = lax.broadcasted_iota(I32, (LCAP, TM), 0).astype(F32)
    perm = jnp.logical_or(r == dl[0:1, :], r == dl[1:2, :]).astype(BF16)
    xl_ref[slot] = jnp.dot(perm, h_ref[...], preferred_element_type=F32).astype(BF16)
    lax.fori_loop(0, nused_ref[i], lambda q, c: (copy(i, q).start(), c)[1], 0)

    @pl.when(i == NT - 1)
    def _():
        zero_ref[...] = jnp.zeros_like(zero_ref)
        for e in range(N_EXP + 1):
            def zcopy(q, e=e):
                return pltpu.make_async_copy(zero_ref, xs_ref.at[_chunk_rows(pad0_ref[e] + q)], sem.at[2])
            lax.fori_loop(0, npad_ref[e], lambda q, c, f=zcopy: (f(q).start(), c)[1], 0)
            lax.fori_loop(0, npad_ref[e], lambda q, c, f=zcopy: (f(q).wait(), c)[1], 0)
        wait_tile(i - 1)
        wait_tile(i)


def _sort_tokens(h2, dl_row, gch, nused, pad0, npad):
    return pl.pallas_call(
        _sort_kernel,
        grid_spec=pltpu.PrefetchScalarGridSpec(
            num_scalar_prefetch=4,
            grid=(NT,),
            in_specs=[
                pl.BlockSpec((TM, D), lambda i, *_: (i, 0)),
                pl.BlockSpec((1, 2, TM), lambda i, *_: (i, 0, 0)),
            ],
            out_specs=pl.BlockSpec(memory_space=pl.ANY),
            scratch_shapes=[pltpu.VMEM((2, LCAP, D), BF16), pltpu.VMEM((GCH, D), BF16),
                            pltpu.SemaphoreType.DMA((3,))],
        ),
        out_shape=jax.ShapeDtypeStruct((RMAX, D), BF16),
        compiler_params=_cparams(1),
        name="moe_sort",
    )(gch, nused, pad0, npad, h2, dl_row)


def _combine_kernel(gch_ref, nused_ref, y_ref, info_ref, x_ref, mod_ref, fw_ref, *rest, final):
    if final:
        oc_ref, ol_ref, yl_ref, sem = rest
    else:
        o_ref, yl_ref, sem = rest
    i = pl.program_id(0)
    slot = i % 2

    def copy(tile, q):
        s = tile % 2
        return pltpu.make_async_copy(y_ref.at[_chunk_rows(gch_ref[tile * NLC + q])],
                                     yl_ref.at[s, _chunk_rows(q)], sem.at[s])

    def fetch(tile):
        lax.fori_loop(0, nused_ref[tile], lambda q, c: (copy(tile, q).start(), c)[1], 0)

    @pl.when(i == 0)
    def _():
        fetch(i)

    @pl.when(i + 1 < NT)
    def _():
        fetch(i + 1)

    n = nused_ref[i]

    def clear(q, c):
        yl_ref[slot, _chunk_rows(q), :] = jnp.zeros((GCH, D), BF16)
        return c

    lax.fori_loop(n, NLC, clear, 0)
    lax.fori_loop(0, n, lambda q, c: (copy(i, q).wait(), c)[1], 0)

    info = info_ref[...]
    col = lax.broadcasted_iota(I32, (TM, LCAP), 1).astype(F32)
    yl = yl_ref[slot]
    y1 = jnp.dot((col == info[:, 0:1]).astype(BF16), yl, preferred_element_type=F32)
    y2 = jnp.dot((col == info[:, 1:2]).astype(BF16), yl, preferred_element_type=F32)
    x = x_ref[...] + mod_ref[0][5:6] * (info[:, 2:3] * y1 + info[:, 3:4] * y2)
    if final:
        x = _rms(x) * fw_ref[...]

        @pl.when(i < NT_CTX)
        def _():
            oc_ref[...] = x

        @pl.when(i >= NT_CTX)
        def _():
            ol_ref[...] = x
    else:
        o_ref[...] = x


def _combine(y, info, x, mod_l, final_w, gch, nused, final):
    tile = pl.BlockSpec((TM, D), lambda i, *_: (i, 0))
    if final:
        out_specs = [pl.BlockSpec((TM, D), lambda i, *_: (jnp.minimum(i, NT_CTX - 1), 0)),
                     pl.BlockSpec((TM, D), lambda i, *_: (jnp.maximum(i - NT_CTX, 0), 0))]
        out_shape = [jax.ShapeDtypeStruct((T_CTX, D), F32), jax.ShapeDtypeStruct((T_LAT, D), F32)]
    else:
        out_specs, out_shape = tile, jax.ShapeDtypeStruct((T, D), F32)
    return pl.pallas_call(
        functools.partial(_combine_kernel, final=final),
        grid_spec=pltpu.PrefetchScalarGridSpec(
            num_scalar_prefetch=2,
            grid=(NT,),
            in_specs=[
                pl.BlockSpec(memory_space=pl.ANY),
                pl.BlockSpec((TM, 4), lambda i, *_: (i, 0)),
                tile,
                pl.BlockSpec((1, 6, D), lambda i, *_: (_mod_row(i), 0, 0)),
                pl.BlockSpec((1, D), lambda i, *_: (0, 0)),
            ],
            out_specs=out_specs,
            scratch_shapes=[pltpu.VMEM((2, LCAP, D), BF16), pltpu.SemaphoreType.DMA((2,))],
        ),
        out_shape=out_shape,
        compiler_params=_cparams(1),
        name="moe_combine",
    )(gch, nused, y, info, x, mod_l, final_w.reshape(1, D))


def _moe_plan(route, counts):
    cnt = counts.reshape(NT, N_EXP).astype(I32)
    cpad = (cnt + GCH - 1) // GCH * GCH
    lo = jnp.cumsum(cpad, axis=1) - cpad
    nused = (lo[:, -1] + cpad[:, -1]) // GCH
    tot = jnp.sum(cpad, axis=0)
    gpad = (tot + TMB - 1) // TMB * TMB
    goff = jnp.cumsum(gpad) - gpad
    so = goff[None, :] + jnp.cumsum(cpad, axis=0) - cpad

    e1 = route[:, 0].astype(I32)
    e2 = route[:, 1].astype(I32)
    eid = jnp.arange(N_EXP, dtype=I32)[None, :]
    lo_tok = jnp.repeat(lo, TM, axis=0)
    dl1 = jnp.sum(jnp.where(e1[:, None] == eid, lo_tok, 0), axis=1).astype(F32) + route[:, 2]
    dl2 = jnp.sum(jnp.where(e2[:, None] == eid, lo_tok, 0), axis=1).astype(F32) + route[:, 3]
    dl_row = jnp.stack([dl1.reshape(NT, TM), dl2.reshape(NT, TM)], axis=1)
    info = jnp.stack([dl1, dl2, route[:, 4], route[:, 5]], axis=1)

    q = jnp.arange(NLC, dtype=I32)[None, :, None]
    lo16 = (lo // GCH)[:, None, :]
    c16 = (cpad // GCH)[:, None, :]
    in_seg = jnp.logical_and(q >= lo16, q < lo16 + c16)
    gch = jnp.sum(jnp.where(in_seg, (so // GCH)[:, None, :] + q - lo16, 0), axis=2).reshape(NT * NLC)

    nblk = gpad // TMB
    n_active = jnp.sum(nblk)
    b = jnp.arange(NBLK, dtype=I32)
    blk_i = jnp.maximum(jnp.minimum(b, n_active - 1), 0)
    bend = (goff + gpad) // TMB
    blk_e = jnp.minimum(jnp.sum((blk_i[:, None] >= bend[None, :]).astype(I32), axis=1), N_EXP - 1)
    used = n_active * TMB
    pad0 = jnp.concatenate([goff + tot, used.reshape(1)]) // GCH
    npad = jnp.concatenate([gpad - tot, (RMAX - used).reshape(1)]) // GCH
    return dict(dl_row=dl_row, info=info, gch=gch.astype(I32), nused=nused.astype(I32),
                blk_e=blk_e.astype(I32), blk_i=blk_i.astype(I32),
                n_active=n_active.reshape(1).astype(I32), pad0=pad0.astype(I32), npad=npad.astype(I32))


def _grid_pos_embed(n_tokens):
    rows = n_tokens // GRID_W
    r = jnp.repeat(jnp.arange(rows, dtype=F32), GRID_W)
    col = jnp.tile(jnp.arange(GRID_W, dtype=F32), rows)
    quarter = D // 4
    omega = 1.0 / (10000.0 ** (jnp.arange(quarter, dtype=F32) / quarter))
    ang_r = r[:, None] * omega[None]
    ang_c = col[:, None] * omega[None]
    return jnp.concatenate([jnp.sin(ang_r), jnp.cos(ang_r), jnp.sin(ang_c), jnp.cos(ang_c)], axis=-1)


def _hy_pos_features(n):
    pos = jnp.arange(n, dtype=F32)
    t = pos / (n - 1)
    bands = jnp.linspace(1e-4, HY_BANDS - 1, HY_BANDS, dtype=F32)
    ang = (2.0 * math.pi * pos / n)[:, None] * bands[None]
    z = jnp.concatenate([t[:, None], jnp.cos(ang), -jnp.sin(ang)], axis=-1)
    z = jnp.pad(z, ((0, 0), (0, LANE - HY_POS_DIM)))
    half = n // 2
    dist = jnp.abs(pos - half) / half
    deltas = jnp.abs(jnp.linspace(HY_MIN_DECAY, HY_MAX_DECAY, D_HY, dtype=F32))
    return z, jnp.exp(-dist[:, None] * deltas[None])


def _dft_mats(lseq, n, kb):
    nkb = n // 2 // kb
    t = jnp.arange(lseq, dtype=I32)

    def tables(tt):
        def cs(freq):
            ang = (2.0 * math.pi / n) * ((freq[:, None] * tt[None, :]) % n).astype(F32)
            return jnp.cos(ang), jnp.sin(ang)
        (ca, sa), (cb, sb) = cs(jnp.arange(nkb, dtype=I32) * kb), cs(jnp.arange(kb, dtype=I32))
        ca, sa, cb, sb = ca[:, None, :], sa[:, None, :], cb[None], sb[None]
        re = ca * cb - sa * sb
        im = -(sa * cb + ca * sb)
        dc = jnp.logical_and(jnp.arange(nkb)[:, None, None] == 0, jnp.arange(kb)[None, :, None] == 0)
        alt = (1.0 - 2.0 * (tt % 2).astype(F32))[None, None, :]
        return re, jnp.where(dc, alt, im), dc

    re, im, _ = tables(t)
    f = jnp.concatenate([re, im], axis=1).reshape(n, lseq)
    re, im, dc = tables(t + lseq // 2)
    wk = jnp.where(dc, 1.0, 2.0) / n
    gt = jnp.concatenate([wk * re, jnp.where(dc, 1.0 / n, wk) * im], axis=1).reshape(n, lseq)
    return f.astype(BF16), gt.T.astype(BF16)


def _ssd_constants():
    hp = SSM_H * SSM_P
    hq = SSM_H // SSM_G
    e64 = np.zeros((2, 8 * SSM_H, 2 * hp), np.float32)
    for d in range(2):
        for q in range(4):
            for hh in range(SSM_H):
                e64[d, q * 2 * SSM_H + d * SSM_H + hh, (q // 2) * hp + hh * SSM_P:(q // 2) * hp + (hh + 1) * SSM_P] = 1.0
    hq_of = np.arange(hq * SSM_P) // SSM_P
    mbd = (np.arange(hq * SSM_Q)[:, None] // SSM_Q == hq_of[None, :]).astype(np.float32)
    mdiag = (hq_of[:, None] == hq_of[None, :]).astype(np.float32)
    return dict(e64=jnp.asarray(e64, dtype=BF16), mbd=jnp.asarray(mbd, dtype=BF16), mdiag=jnp.asarray(mdiag))


def _block_diag_heads(w):
    eye = jnp.eye(LRU_HEADS, dtype=w.dtype)
    return jnp.einsum("ldhij,hg->ldhigj", w, eye).reshape(DEPTH, 2, D_LRU, D_LRU)


def kernel(x_prompt, x_sample, state_lru, state_ssm, c, c_ctx, norm1_w, norm2_w, final_norm_w, ada_w, ada_b,
           w_in, w_out, lru_conv_w, lru_conv_b, lru_wa, lru_ba, lru_wi, lru_bi, lru_lambda, hy_conv_w, hy_conv_b,
           hy_w1, hy_b1, hy_w2, hy_b2, hy_freq, hy_w3, hy_bias, ssm_conv_w, ssm_conv_b, ssm_dt_bias, ssm_a_log,
           ssm_d, ssm_norm_w, ffn_w_gate, ffn_w_up, ffn_w_down, moe_router, moe_w_gate, moe_w_up, moe_w_down):
    hp = SSM_H * SSM_P
    wa, wi = _block_diag_heads(lru_wa), _block_diag_heads(lru_wi)
    row = lambda a: a.reshape(DEPTH, 1, -1)
    p = {
        "lru_conv_w": lru_conv_w, "lru_conv_b": row(lru_conv_b), "lru_lambda": lru_lambda,
        "lru_wbig": jnp.concatenate([wa[:, 0], wi[:, 0], wa[:, 1], wi[:, 1]], axis=-1).astype(BF16),
        "lru_bias": jnp.concatenate([lru_ba[:, 0], lru_bi[:, 0], lru_ba[:, 1], lru_bi[:, 1]], axis=-1)[:, None],
        "hy_conv_w": hy_conv_w, "hy_conv_b": row(hy_conv_b), "hy_bias": row(hy_bias),
        "hy_w1t": jnp.swapaxes(jnp.pad(hy_w1, ((0, 0), (0, LANE - HY_POS_DIM), (0, 0))), 1, 2),
        "hy_b1": hy_b1[:, :, None], "hy_w2t": jnp.swapaxes(hy_w2, 1, 2), "hy_b2": hy_b2[:, :, None],
        "hy_freq": hy_freq[:, :, None], "hy_w3t": jnp.swapaxes(hy_w3, 1, 2),
        "ssm_conv_w": ssm_conv_w, "ssm_conv_b": row(ssm_conv_b),
        "ssm_dtb_row": row(ssm_dt_bias), "ssm_dtb_col": ssm_dt_bias.reshape(DEPTH, 2 * SSM_H, 1),
        "ssm_alog_row": row(ssm_a_log), "ssm_alog_col": ssm_a_log.reshape(DEPTH, 2 * SSM_H, 1),
        "ssm_d_exp": jnp.repeat(ssm_d, SSM_P, axis=-1)[:, None], "ssm_norm_w": row(ssm_norm_w),
    }
    w_dtT = jnp.swapaxes(w_in[:, :, D_MAIN:], 1, 2)
    ffn_w = (ffn_w_gate, ffn_w_up, ffn_w_down)
    moe_w = (moe_w_gate, moe_w_up, moe_w_down)

    cst = _ssd_constants()
    cst["lru_rep"] = jnp.asarray(np.arange(RB)[:, None] // SUBLANES == np.arange(RB // SUBLANES)[None, :], dtype=BF16)
    z_ctx, win_ctx = _hy_pos_features(L_CTX)
    z_lat, win_lat = _hy_pos_features(L_LAT)
    cst["f_ctx"], cst["g_ctx"] = _dft_mats(L_CTX, N_FFT_CTX, N_FFT_CTX // 2)
    cst["f_lat"], cst["g_lat"] = _dft_mats(L_LAT, N_FFT_LAT, HY_KB_LAT)
    cst["hf_ctx"] = _hy_filter_spectrum(L_CTX, N_FFT_CTX // 2, z_ctx.T, win_ctx.T, cst["f_ctx"], p)
    cst["hf_lat"] = _hy_filter_spectrum(L_LAT, HY_KB_LAT, z_lat.T, win_lat.T, cst["f_lat"], p)

    cond = jnp.concatenate([c_ctx[None], c], axis=0)
    mod = _mod_table(jnp.broadcast_to(cond[:, :, None], (3, D, LANE)), ada_w, ada_b)
    mod = mod[:, :3].reshape(DEPTH, 3, 6, D)

    x = jnp.concatenate([x_prompt.reshape(T_CTX, D),
                         (x_sample + _grid_pos_embed(L_LAT)[None]).reshape(T_LAT, D)], axis=0)
    st_ssm_in = state_ssm.reshape(N_LAT_SEQ, DEPTH, 2, hp, SSM_N)

    lru_states = []
    new_ssm = jnp.zeros((N_CTX_SEQ, DEPTH, 2, SSM_H, SSM_P, SSM_N), F32)
    for l in range(DEPTH):
        u_lru, u_hy, u_z, u_xbc, u_dt, u_dtT = _k1(l, x, mod[l], norm1_w, w_in, w_dtT)
        o_lru, s_lru = _lru_mixer(l, u_lru, p, cst, state_lru)
        o_hy = _hy_mixer(l, u_hy, p, cst)
        o_ssm, new_ssm = _ssd_mixer(l, u_z, u_xbc, u_dt, u_dtT, p, cst, st_ssm_in, new_ssm)
        lru_states.append(s_lru.reshape(N_CTX_SEQ, 2, D_LRU))
        j = l // 2
        if l % 2 == 0:
            x, h2 = _k2(l, o_lru, o_hy, o_ssm, x, mod[l], norm2_w, w_out)
            x = _dense_ffn(j, h2, x, mod[l], *ffn_w)
        else:
            x, h2, route, counts = _k2(l, o_lru, o_hy, o_ssm, x, mod[l], norm2_w, w_out, moe_router, j)
            plan = _moe_plan(route, counts)
            xs = _sort_tokens(h2, plan["dl_row"], plan["gch"], plan["nused"], plan["pad0"], plan["npad"])
            y = _expert_ffn(j, xs, plan["blk_e"], plan["blk_i"], plan["n_active"], *moe_w)
            x = _combine(y, plan["info"], x, mod[l], final_norm_w, plan["gch"], plan["nused"],
                         final=(l == DEPTH - 1))
    y_prompt = x[0].reshape(N_CTX_SEQ, L_CTX, D)
    y_sample = x[1].reshape(N_LAT_SEQ, L_LAT, D)
    return (y_prompt, y_sample, jnp.stack(lru_states, axis=1), new_ssm)
```
